```python
import jax
import jax.numpy as jnp
from jax import lax
import numpy as np

D_MODEL = 1024
BATCH = 8
SEQ = 2048
DEPTH = 2

GRID_W = 64
CTX_LEN = 256

RET_HEADS = 4
RET_HD = 128
RET_W = RET_HEADS * RET_HD
RET_CHUNK = 128
RET_GN_EPS = 1e-5

NA_HEADS = 8
NA_HD = 64
NA_W = NA_HEADS * NA_HD
NA_KR_MAX = 8
NA_KC = 16

MIX_W = RET_W + NA_W
IN_W = 4 * RET_W + 3 * NA_W

RWKV_HD = 64
RWKV_HEADS = D_MODEL // RWKV_HD
DECAY_LORA = 64
AAA_LORA = 64
GATE_LORA = 160
RWKV_GN_EPS = 64e-5

FFN_HIDDEN = 2816
N_EXPERTS = 8
TOP_K = 2
EXPERT_HIDDEN = 3584

N_EVEN = (DEPTH + 1) // 2
N_ODD = DEPTH // 2

ROPE_BASE = 10000.0
NORM_EPS = 1e-6
F32 = jnp.float32

kernel_name = 'hybrid_retention_natten_rwkv7_moe_dit'


def rms_norm(x, w):
    x32 = x.astype(F32)
    y = x32 * lax.rsqrt(jnp.mean(x32 * x32, axis=-1, keepdims=True) + NORM_EPS)
    return (y * w.astype(F32)).astype(x.dtype)


def modulate(x, w, shift, scale):
    return rms_norm(x, w) * (1.0 + scale) + shift


def head_group_norm(y, eps):
    mu = jnp.mean(y, axis=-1, keepdims=True)
    yc = y - mu
    return yc * lax.rsqrt(jnp.mean(yc * yc, axis=-1, keepdims=True) + eps)


def to_heads(t, n_heads):
    b, l, _ = t.shape
    return t.reshape(b, l, n_heads, -1).transpose(0, 2, 1, 3)


def from_heads(t):
    b, h, l, d = t.shape
    return t.transpose(0, 2, 1, 3).reshape(b, l, h * d)


def rope_1d(x, pos):
    half = x.shape[-1] // 2
    freqs = ROPE_BASE ** (-jnp.arange(half, dtype=F32) / half)
    ang = pos.astype(F32)[:, None] * freqs[None, :]
    cos, sin = jnp.cos(ang), jnp.sin(ang)
    x1, x2 = x[..., :half], x[..., half:]
    return jnp.concatenate([x1 * cos - x2 * sin, x1 * sin + x2 * cos], axis=-1)


def rope_2d(x, rpos, cpos):
    h = x.shape[-1] // 2
    xf = x.astype(F32)
    return jnp.concatenate([rope_1d(xf[..., :h], rpos), rope_1d(xf[..., h:], cpos)], axis=-1).astype(x.dtype)


def retention_chunked(q, k, v, log_g, s0, return_out):
    b, h, l, dk = q.shape
    dv = v.shape[-1]
    n = l // RET_CHUNK
    qc = q.reshape(b, h, n, RET_CHUNK, dk)
    kc = k.reshape(b, h, n, RET_CHUNK, dk)
    vc = v.reshape(b, h, n, RET_CHUNK, dv)
    idx = jnp.arange(RET_CHUNK, dtype=F32)
    k_dec = jnp.exp(log_g[:, None] * (RET_CHUNK - 1.0 - idx)[None, :])
    kv = jnp.einsum('bhncd,hc,bhnce->nbhde', kc, k_dec, vc)
    chunk_dec = jnp.exp(log_g * RET_CHUNK)[None, :, None, None]

    def step(s, kv_n):
        return s * chunk_dec + kv_n, s

    s_fin, s_prev = lax.scan(step, s0, kv)
    if not return_out:
        return None, s_fin
    diff = idx[:, None] - idx[None, :]
    dmat = jnp.where(diff >= 0, jnp.exp(log_g[:, None, None] * jnp.maximum(diff, 0.0)[None]), 0.0)
    scores = jnp.einsum('bhncd,bhnsd->bhncs', qc, kc) * dmat[None, :, None]
    intra = jnp.einsum('bhncs,bhnse->bhnce', scores, vc)
    q_dec = jnp.exp(log_g[:, None] * (idx + 1.0)[None, :])
    cross = jnp.einsum('bhncd,nbhde->bhnce', qc, s_prev) * q_dec[None, :, None, :, None]
    return (intra + cross).reshape(b, h, l, dv), s_fin


def retention_mix(q_c, k_c, v_c, q_l, k_l, v_l, log_g_f, log_g_b, ctx_out):
    b, h, _, dk = q_c.shape
    s0 = jnp.zeros((b, h, dk, v_c.shape[-1]), F32)

    def flip(t):
        return jnp.flip(t, axis=2)

    oc_f, s_f = retention_chunked(q_c, k_c, v_c, log_g_f, s0, ctx_out)
    ol_f, _ = retention_chunked(q_l, k_l, v_l, log_g_f, s_f, True)
    oc_b, s_b = retention_chunked(flip(q_c), flip(k_c), flip(v_c), log_g_b, s0, ctx_out)
    ol_b, _ = retention_chunked(flip(q_l), flip(k_l), flip(v_l), log_g_b, s_b, True)
    y_l = ol_f + flip(ol_b)
    y_c = oc_f + flip(oc_b) if ctx_out else None
    return y_c, y_l


def retention_readout(y, g, gn_w):
    yn = from_heads(head_group_norm(y, RET_GN_EPS)) * gn_w.astype(F32)
    return (yn * jax.nn.silu(g.astype(F32))).astype(g.dtype)


def neighborhood_attention(q, k, v, k_c, v_c, rpb, rows):
    b, h, l, d = q.shape
    kr = min(NA_KR_MAX, rows)
    nwin = kr * NA_KC
    scale = d ** -0.5
    qg = q.reshape(b, h, rows, GRID_W, d)
    kg = k.reshape(b, h, rows, GRID_W, d)
    vg = v.reshape(b, h, rows, GRID_W, d)
    cols = np.arange(GRID_W)
    col_start = np.clip(cols - NA_KC // 2, 0, GRID_W - NA_KC)
    col_idx = col_start[:, None] + np.arange(NA_KC)[None, :]
    col_off = col_idx - cols[:, None] + (NA_KC - 1)
    rpb_c = rpb[:, :, col_off]

    def row_fn(r):
        r0 = jnp.clip(r - kr // 2, 0, rows - kr)
        q_row = lax.dynamic_index_in_dim(qg, r, axis=2, keepdims=False)
        k_win = lax.dynamic_slice_in_dim(kg, r0, kr, axis=2)[:, :, :, col_idx]
        v_win = lax.dynamic_slice_in_dim(vg, r0, kr, axis=2)[:, :, :, col_idx]
        row_off = r0 + jnp.arange(kr) - r + (NA_KR_MAX - 1)
        bias = jnp.transpose(rpb_c[:, row_off], (0, 2, 1, 3))
        s_win = jnp.einsum('bhqd,bhrqcd->bhqrc', q_row, k_win) * scale + bias[None]
        s_ctx = jnp.einsum('bhqd,bhkd->bhqk', q_row, k_c) * scale
        s = jnp.concatenate([s_win.reshape(b, h, GRID_W, nwin), s_ctx], axis=-1).astype(F32)
        p = jax.nn.softmax(s, axis=-1).astype(v.dtype)
        p_win = p[..., :nwin].reshape(b, h, GRID_W, kr, NA_KC)
        return (jnp.einsum('bhqrc,bhrqcd->bhqd', p_win, v_win)
                + jnp.einsum('bhqk,bhkd->bhqd', p[..., nwin:], v_c))

    out = lax.map(row_fn, jnp.arange(rows))
    return jnp.transpose(out, (1, 2, 0, 3, 4)).reshape(b, h, l, d)


def context_attention(q, k, v):
    s = jnp.einsum('bhqd,bhkd->bhqk', q, k) * (q.shape[-1] ** -0.5)
    p = jax.nn.softmax(s.astype(F32), axis=-1).astype(v.dtype)
    return jnp.einsum('bhqk,bhkd->bhqd', p, v)


def even_mixer(h_c, h_l, w_in, dec_f, dec_b, gn_w, qn_w, kn_w, rpb, w_out, rows, ctx_out):
    l = h_l.shape[1]
    t = jnp.arange(l)
    rpos, cpos = t // GRID_W, t % GRID_W
    cuts = [int(u) for u in np.cumsum([RET_W] * 4 + [NA_W] * 3)[:-1]]
    rq_l, rk_l, rv_l, rg_l, nq_l, nk_l, nv_l = jnp.split(h_l @ w_in, cuts, axis=-1)
    rq_c, rk_c, rv_c, rg_c, nq_c, nk_c, nv_c = jnp.split(h_c @ w_in, cuts, axis=-1)
    kscale = RET_HD ** -0.5
    q_l = rope_2d(to_heads(rq_l, RET_HEADS), rpos, cpos).astype(F32)
    k_l = rope_2d(to_heads(rk_l, RET_HEADS), rpos, cpos).astype(F32) * kscale
    v_l = to_heads(rv_l, RET_HEADS).astype(F32)
    q_c = to_heads(rq_c, RET_HEADS).astype(F32)
    k_c = to_heads(rk_c, RET_HEADS).astype(F32) * kscale
    v_c = to_heads(rv_c, RET_HEADS).astype(F32)
    log_g_f = -jnp.exp(dec_f.astype(F32))
    log_g_b = -jnp.exp(dec_b.astype(F32))
    y_c, y_l = retention_mix(q_c, k_c, v_c, q_l, k_l, v_l, log_g_f, log_g_b, ctx_out)
    ret_l = retention_readout(y_l, rg_l, gn_w)
    aq_l = rms_norm(to_heads(nq_l, NA_HEADS), qn_w)
    ak_l = rms_norm(to_heads(nk_l, NA_HEADS), kn_w)
    av_l = to_heads(nv_l, NA_HEADS)
    aq_c = rms_norm(to_heads(nq_c, NA_HEADS), qn_w)
    ak_c = rms_norm(to_heads(nk_c, NA_HEADS), kn_w)
    av_c = to_heads(nv_c, NA_HEADS)
    na_l = from_heads(neighborhood_attention(aq_l, ak_l, av_l, ak_c, av_c, rpb, rows))
    out_l = jnp.concatenate([ret_l, na_l], axis=-1) @ w_out
    out_c = None
    if ctx_out:
        na_c = from_heads(context_attention(aq_c, ak_c, av_c))
        out_c = jnp.concatenate([retention_readout(y_c, rg_c, gn_w), na_c], axis=-1) @ w_out
    return out_c, out_l


def centred_shift(x):
    xp = jnp.pad(x, ((0, 0), (1, 1), (0, 0)))
    return 0.5 * (xp[:, :-2] + xp[:, 2:])


def rwkv_features(h, mu, w_rkv, w0, w1, w2, a0, a1, a2, k_k, k_a, full):
    def hs(t):
        return t.astype(F32).reshape(t.shape[:-1] + (RWKV_HEADS, RWKV_HD))

    xx = centred_shift(h) - h
    xm = h[None] + xx[None] * mu[:, None, None, :]
    lo = 0 if full else 1
    proj = jnp.einsum('sbld,sde->sble', xm[lo:3], w_rkv[lo:3])
    r = hs(proj[0]) if full else None
    k, v = proj[-2], proj[-1]
    w_lora = jnp.einsum('zblr,zrd->zbld', jnp.tanh(jnp.einsum('bld,zdr->zblr', xm[3], w1)), w2)
    log_w = -jax.nn.softplus(-(w0[:, None, None, :] + w_lora).astype(F32)) - 0.5
    decay = jnp.exp(-jnp.exp(log_w))
    a = jax.nn.sigmoid((a0[:, None, None, :] + jnp.einsum('zblr,zrd->zbld', jnp.einsum('bld,zdr->zblr', xm[4], a1), a2)).astype(F32))
    kk = hs(k * k_k)
    kk = kk / jnp.maximum(jnp.sqrt(jnp.sum(kk * kk, axis=-1, keepdims=True)), 1e-12)
    k_dir = hs(k.astype(F32)[None] * (1.0 + (a - 1.0) * k_a.astype(F32)))
    return r, k_dir, hs(v), kk, hs(a), hs(decay), xm[5]


def rwkv7_scan(seqs, s0):
    with_y = len(seqs) == 6

    def step(S, inp):
        w_t, k_t, v_t, a_t, b_t = inp[:5]
        S = (S * w_t[:, :, None, :]
             + jnp.einsum('bhij,bhj->bhi', S, a_t)[..., None] * b_t[:, :, None, :]
             + v_t[..., None] * k_t[:, :, None, :])
        y_t = jnp.einsum('bhij,bhj->bhi', S, inp[5]) if with_y else None
        return S, y_t

    xs = tuple(jnp.moveaxis(t, 1, 0) for t in seqs)
    s_fin, ys = lax.scan(step, s0, xs)
    return (jnp.moveaxis(ys, 0, 1) if with_y else None), s_fin


def rwkv_direction(feat, z, s0, reverse, with_y):
    r, k_dir, v, kk, a, decay, _ = feat
    seqs = [decay[z], k_dir[z], v, -kk, kk * a[z]]
    if with_y:
        seqs.append(r)
    if reverse:
        seqs = [jnp.flip(t, axis=1) for t in seqs]
    y, s = rwkv7_scan(seqs, s0)
    if reverse and with_y:
        y = jnp.flip(y, axis=1)
    return y, s


def rwkv_readout(y, feat, g1, g2, r_k, ln_w, ln_b, w_o):
    r, k_dir, v, _, _, _, xg = feat
    b, l, hh, n = y.shape
    yn = head_group_norm(y, RWKV_GN_EPS).reshape(b, l, hh * n) * ln_w.astype(F32) + ln_b.astype(F32)
    coeff = jnp.sum(r[None] * k_dir * r_k.astype(F32).reshape(hh, n), axis=-1, keepdims=True).sum(axis=0)
    bonus = (coeff * v).reshape(b, l, hh * n)
    g = jax.nn.sigmoid(xg @ g1) @ g2
    return ((yn + bonus).astype(xg.dtype) * g) @ w_o


def odd_mixer(h_c, h_l, mu, w_rkv, w0, w1, w2, a0, a1, a2, g1, g2, k_k, k_a, r_k, ln_w, ln_b, w_o, ctx_out):
    feat_c = rwkv_features(h_c, mu, w_rkv, w0, w1, w2, a0, a1, a2, k_k, k_a, ctx_out)
    feat_l = rwkv_features(h_l, mu, w_rkv, w0, w1, w2, a0, a1, a2, k_k, k_a, True)
    s0 = jnp.zeros((h_c.shape[0], RWKV_HEADS, RWKV_HD, RWKV_HD), F32)
    yc_f, sc_f = rwkv_direction(feat_c, 0, s0, False, ctx_out)
    yc_b, sc_b = rwkv_direction(feat_c, 1, s0, True, ctx_out)
    yl_f, _ = rwkv_direction(feat_l, 0, sc_f, False, True)
    yl_b, _ = rwkv_direction(feat_l, 1, sc_b, True, True)
    out_l = rwkv_readout(yl_f + yl_b, feat_l, g1, g2, r_k, ln_w, ln_b, w_o)
    out_c = rwkv_readout(yc_f + yc_b, feat_c, g1, g2, r_k, ln_w, ln_b, w_o) if ctx_out else None
    return out_c, out_l


def swiglu(x, w13, w2):
    gate, up = jnp.split(x @ w13, 2, axis=-1)
    return (jax.nn.silu(gate) * up) @ w2


def moe_swiglu(h, router, w13, w2):
    b, l, d = h.shape
    t = h.reshape(-1, d)
    logits = (t @ router).astype(F32)
    top_v, top_i = lax.top_k(logits, TOP_K)
    top_w = jax.nn.softmax(top_v, axis=-1)
    gates = jnp.sum(jax.nn.one_hot(top_i, N_EXPERTS, dtype=F32) * top_w[..., None], axis=1)
    out = jnp.zeros(t.shape, F32)
    for e in range(N_EXPERTS):
        out = out + gates[:, e:e + 1] * swiglu(t, w13[e], w2[e]).astype(F32)
    return out.astype(h.dtype).reshape(b, l, d)


def setup_inputs(seed: int = 0) -> dict:
    key = jax.random.key(seed)
    ks = iter(jax.random.split(key, 48))

    def nrm(shape, scale):
        return jax.random.normal(next(ks), shape, F32) * scale

    def unif(shape, lo, hi):
        return jax.random.uniform(next(ks), shape, F32, lo, hi)

    d = D_MODEL
    ret_dec = jnp.log(-jnp.log1p(-(2.0 ** (-5.0 - jnp.arange(RET_HEADS, dtype=F32)))))
    return {
        'x': nrm((BATCH, SEQ, d), 1.0),
        'c': nrm((BATCH, d), 1.0),
        'ctx': nrm((BATCH, CTX_LEN, d), 1.0),
        'c_ctx': nrm((d,), 1.0),
        'ada_w': nrm((DEPTH, d, 6 * d), 0.5 * d ** -0.5),
        'ada_b': nrm((DEPTH, 6 * d), 0.02),
        'norm_mix_w': 1.0 + nrm((DEPTH, d), 0.02),
        'norm_ffn_w': 1.0 + nrm((DEPTH, d), 0.02),
        'ev_w_in': nrm((N_EVEN, d, IN_W), d ** -0.5),
        'ev_ret_decay_f': ret_dec + nrm((N_EVEN, RET_HEADS), 0.05),
        'ev_ret_decay_b': ret_dec + nrm((N_EVEN, RET_HEADS), 0.05),
        'ev_ret_gn_w': 1.0 + nrm((N_EVEN, RET_W), 0.02),
        'ev_na_qn_w': 1.0 + nrm((N_EVEN, NA_HD), 0.02),
        'ev_na_kn_w': 1.0 + nrm((N_EVEN, NA_HD), 0.02),
        'ev_na_rpb': nrm((N_EVEN, NA_HEADS, 2 * NA_KR_MAX - 1, 2 * NA_KC - 1), 0.1),
        'ev_w_out': nrm((N_EVEN, MIX_W, d), MIX_W ** -0.5),
        'ev_ffn_w13': nrm((N_EVEN, d, 2 * FFN_HIDDEN), d ** -0.5),
        'ev_ffn_w2': nrm((N_EVEN, FFN_HIDDEN, d), FFN_HIDDEN ** -0.5),
        'od_mu': unif((N_ODD, 6, d), 0.0, 1.0),
        'od_w_rkv': nrm((N_ODD, 3, d, d), d ** -0.5),
        'od_w0': unif((N_ODD, 2, d), -6.5, -1.5),
        'od_w1': nrm((N_ODD, 2, d, DECAY_LORA), d ** -0.5),
        'od_w2': nrm((N_ODD, 2, DECAY_LORA, d), 0.1 * DECAY_LORA ** -0.5),
        'od_a0': nrm((N_ODD, 2, d), 0.1),
        'od_a1': nrm((N_ODD, 2, d, AAA_LORA), d ** -0.5),
        'od_a2': nrm((N_ODD, 2, AAA_LORA, d), 0.1 * AAA_LORA ** -0.5),
        'od_g1': nrm((N_ODD, d, GATE_LORA), d ** -0.5),
        'od_g2': nrm((N_ODD, GATE_LORA, d), GATE_LORA ** -0.5),
        'od_k_k': 0.85 + nrm((N_ODD, d), 0.05),
        'od_k_a': 1.0 + nrm((N_ODD, d), 0.05),
        'od_r_k': nrm((N_ODD, d), 0.1),
        'od_ln_w': 1.0 + nrm((N_ODD, d), 0.02),
        'od_ln_b': nrm((N_ODD, d), 0.02),
        'od_w_o': nrm((N_ODD, d, d), d ** -0.5),
        'od_router': nrm((N_ODD, d, N_EXPERTS), d ** -0.5),
        'od_moe_w13': nrm((N_ODD, N_EXPERTS, d, 2 * EXPERT_HIDDEN), d ** -0.5),
        'od_moe_w2': nrm((N_ODD, N_EXPERTS, EXPERT_HIDDEN, d), EXPERT_HIDDEN ** -0.5),
    }


def reference(x, c, ctx, c_ctx, ada_w, ada_b, norm_mix_w, norm_ffn_w, ev_w_in, ev_ret_decay_f, ev_ret_decay_b,
              ev_ret_gn_w, ev_na_qn_w, ev_na_kn_w, ev_na_rpb, ev_w_out, ev_ffn_w13, ev_ffn_w2, od_mu, od_w_rkv,
              od_w0, od_w1, od_w2, od_a0, od_a1, od_a2, od_g1, od_g2, od_k_k, od_k_a, od_r_k, od_ln_w, od_ln_b,
              od_w_o, od_router, od_moe_w13, od_moe_w2):
    rows = x.shape[1] // GRID_W
    s_lat = jax.nn.silu(c)
    s_ctx = jax.nn.silu(c_ctx)
    x_l, x_c = x, ctx
    for i in range(DEPTH):
        j = i // 2
        ctx_out = i < DEPTH - 1
        m_l = jnp.split((s_lat @ ada_w[i] + ada_b[i])[:, None, :], 6, axis=-1)
        m_c = jnp.split(s_ctx @ ada_w[i] + ada_b[i], 6, axis=-1)
        h_l = modulate(x_l, norm_mix_w[i], m_l[0], m_l[1])
        h_c = modulate(x_c, norm_mix_w[i], m_c[0], m_c[1])
        if i % 2 == 0:
            o_c, o_l = even_mixer(h_c, h_l, ev_w_in[j], ev_ret_decay_f[j], ev_ret_decay_b[j], ev_ret_gn_w[j],
                                  ev_na_qn_w[j], ev_na_kn_w[j], ev_na_rpb[j], ev_w_out[j], rows, ctx_out)

            def ffn(hh, j=j):
                return swiglu(hh, ev_ffn_w13[j], ev_ffn_w2[j])
        else:
            o_c, o_l = odd_mixer(h_c, h_l, od_mu[j], od_w_rkv[j], od_w0[j], od_w1[j], od_w2[j], od_a0[j], od_a1[j],
                                 od_a2[j], od_g1[j], od_g2[j], od_k_k[j], od_k_a[j], od_r_k[j], od_ln_w[j],
                                 od_ln_b[j], od_w_o[j], ctx_out)

            def ffn(hh, j=j):
                return moe_swiglu(hh, od_router[j], od_moe_w13[j], od_moe_w2[j])
        x_l = x_l + m_l[2] * o_l
        x_l = x_l + m_l[5] * ffn(modulate(x_l, norm_ffn_w[i], m_l[3], m_l[4]))
        if ctx_out:
            x_c = x_c + m_c[2] * o_c
            x_c = x_c + m_c[5] * ffn(modulate(x_c, norm_ffn_w[i], m_c[3], m_c[4]))
    return x_l
```

```python
import functools

import jax
import jax.numpy as jnp
import numpy as np
from jax import lax
from jax.experimental import pallas as pl
from jax.experimental.pallas import tpu as pltpu

F32 = jnp.float32
BF16 = jnp.bfloat16

LANES = 128
GRID_W = 64
RET_HEADS = 4
RET_HD = 128
RET_W = RET_HEADS * RET_HD
RET_CHUNK = 128
RET_GN_EPS = 1e-5
NA_HEADS = 8
NA_HD = 64
NA_W = NA_HEADS * NA_HD
NA_KR = 8
NA_KC = 16
RWKV_HD = 64
RWKV_GN_EPS = 64e-5
RWKV_CHUNK = 64
TOP_K = 2
ROPE_BASE = 10000.0
NORM_EPS = 1e-6
NEG_BIG = -1e30
VMEM_LIMIT = 56 * 1024 * 1024


def _cparams(n_axes):
    return pltpu.CompilerParams(dimension_semantics=("arbitrary",) * n_axes, vmem_limit_bytes=VMEM_LIMIT)


def _dot(a, b):
    return jnp.dot(a.astype(BF16), b.astype(BF16), preferred_element_type=F32)


def _dot_nt(a, b):
    return lax.dot_general(a.astype(BF16), b.astype(BF16), (((1,), (1,)), ((), ())), preferred_element_type=F32)


def _dot_tn(a, b):
    return lax.dot_general(a.astype(BF16), b.astype(BF16), (((0,), (0,)), ((), ())), preferred_element_type=F32)


def _split3(x):
    hi = x.astype(BF16)
    r1 = x - hi.astype(F32)
    mid = r1.astype(BF16)
    lo = (r1 - mid.astype(F32)).astype(BF16)
    return hi, mid, lo


def _dot_hi(a, b):
    ah, am, al = _split3(a)
    bh, bm, bl = _split3(b)
    d = functools.partial(jnp.dot, preferred_element_type=F32)
    return (d(ah, bh) + (d(ah, bm) + d(am, bh)) + (d(am, bm) + d(ah, bl) + d(al, bh)))


def _dot_exact_rhs(a, b_bf16):
    ah, am, al = _split3(a)
    d = functools.partial(jnp.dot, preferred_element_type=F32)
    return d(ah, b_bf16) + d(am, b_bf16) + d(al, b_bf16)


def _mod(v, n):
    return (v & (n - 1)) if n & (n - 1) == 0 else v % n


def _sigmoid(x):
    return 1.0 / (1.0 + jnp.exp(-x))


def _silu(x):
    return x * _sigmoid(x)


def _softplus(x):
    return jnp.maximum(x, 0.0) + jnp.log(1.0 + jnp.exp(-jnp.abs(x)))


def _group_mean_mat(width, group):
    r = lax.broadcasted_iota(jnp.int32, (width, width), 0) // group
    c = lax.broadcasted_iota(jnp.int32, (width, width), 1) // group
    return jnp.where(r == c, 1.0 / group, 0.0).astype(BF16)


def _group_mean(x, group):
    g = _group_mean_mat(LANES, group)
    cols = [_dot_exact_rhs(x[:, c:c + LANES], g) for c in range(0, x.shape[1], LANES)]
    return cols[0] if len(cols) == 1 else jnp.concatenate(cols, axis=1)


def _modnorm(x, nw, shift, scale):
    ms = jnp.mean(x * x, axis=-1, keepdims=True)
    y = x * lax.rsqrt(ms + NORM_EPS) * nw
    return y * (1.0 + scale) + shift


def _ada_kernel(c_ref, w_ref, b_ref, o_ref):
    s = _silu(c_ref[...])
    o_ref[0] = _dot_hi(s, w_ref[0]) + b_ref[0]


def _ada(c_rows, ada_w, ada_b):
    depth, d, n = ada_w.shape
    rows = c_rows.shape[0]
    tn = 1536
    return pl.pallas_call(
        _ada_kernel,
        out_shape=jax.ShapeDtypeStruct((depth, rows, n), F32),
        grid=(depth, n // tn),
        in_specs=[
            pl.BlockSpec((rows, d), lambda l, j: (0, 0)),
            pl.BlockSpec((1, d, tn), lambda l, j: (l, 0, j)),
            pl.BlockSpec((1, 1, tn), lambda l, j: (l, 0, j)),
        ],
        out_specs=pl.BlockSpec((1, rows, tn), lambda l, j: (l, 0, j)),
        compiler_params=_cparams(2),
        name="ada_mod",
    )(c_rows, ada_w, ada_b.reshape(depth, 1, n))


def _modnorm_mm_kernel(*refs, swiglu, gated):
    x_ref, nw_ref, sh_ref, sc_ref = refs[:4]
    pos = 4
    w_refs = refs[pos:pos + (2 if swiglu else 1)]
    pos += len(w_refs)
    gates_ref = refs[pos] if gated else None
    pos += 1 if gated else 0
    o_ref, h_ref = refs[pos], refs[pos + 1]

    first = pl.program_id(1) == 0
    if gated:
        first = jnp.logical_and(first, pl.program_id(2) == 0)

    @pl.when(first)
    def _():
        h_ref[...] = _modnorm(x_ref[...], nw_ref[...], sh_ref[0], sc_ref[0]).astype(BF16)

    h = h_ref[...]
    w0 = w_refs[0][0] if gated else w_refs[0][...]
    acc = jnp.dot(h, w0, preferred_element_type=F32)
    if swiglu:
        w1 = w_refs[1][0] if gated else w_refs[1][...]
        acc = _silu(acc) * jnp.dot(h, w1, preferred_element_type=F32)
    if gated:
        e = pl.program_id(1)
        lane = lax.broadcasted_iota(jnp.int32, gates_ref.shape, 1)
        g = jnp.sum(jnp.where(lane == e, gates_ref[...], 0.0), axis=-1, keepdims=True)
        acc = acc * g
    o_ref[...] = acc.astype(o_ref.dtype)


def _modnorm_mm(x, nw, shift, scale, w, *, group_of_tile, tm, tn, out_dtype, swiglu=False, gates=None):
    m, k = x.shape
    gated = gates is not None
    n_total = w.shape[-1]
    n_out = n_total // 2 if swiglu else n_total
    nj = n_out // tn
    if gated:
        n_e = w.shape[0]
        grid = (m // tm, n_e, nj)
        xmap = lambda i, e, j: (i, 0)
        cmap = lambda i, e, j: (0, 0)
        gmap = lambda i, e, j: (group_of_tile(i), 0, 0)
        wspecs = [pl.BlockSpec((1, k, tn), lambda i, e, j: (e, 0, j))]
        if swiglu:
            wspecs.append(pl.BlockSpec((1, k, tn), lambda i, e, j: (e, 0, j + nj)))
        extra = [pl.BlockSpec((tm, LANES), lambda i, e, j: (i, 0))]
        omap = lambda i, e, j: (i, e * nj + j)
        out_cols = n_e * n_out
    else:
        grid = (m // tm, nj)
        xmap = lambda i, j: (i, 0)
        cmap = lambda i, j: (0, 0)
        gmap = lambda i, j: (group_of_tile(i), 0, 0)
        wspecs = [pl.BlockSpec((k, tn), lambda i, j: (0, j))]
        if swiglu:
            wspecs.append(pl.BlockSpec((k, tn), lambda i, j: (0, j + nj)))
        extra = []
        omap = lambda i, j: (i, j)
        out_cols = n_out
    args = [x, nw.reshape(1, k), shift, scale] + [w] * len(wspecs) + ([gates] if gated else [])
    return pl.pallas_call(
        functools.partial(_modnorm_mm_kernel, swiglu=swiglu, gated=gated),
        out_shape=jax.ShapeDtypeStruct((m, out_cols), out_dtype),
        grid=grid,
        in_specs=[pl.BlockSpec((tm, k), xmap), pl.BlockSpec((1, k), cmap),
                  pl.BlockSpec((1, 1, k), gmap), pl.BlockSpec((1, 1, k), gmap)] + wspecs + extra,
        out_specs=pl.BlockSpec((tm, tn), omap),
        scratch_shapes=[pltpu.VMEM((tm, k), BF16)],
        compiler_params=_cparams(len(grid)),
        name="modnorm_mm",
    )(*args)


def _mm_kernel(*refs, residual, nk):
    if residual:
        a_ref, w_ref, res_ref, gate_ref, o_ref, acc_ref = refs
    else:
        a_ref, w_ref, o_ref, acc_ref = refs
    kk = pl.program_id(2)
    part = jnp.dot(a_ref[...], w_ref[...], preferred_element_type=F32)

    def finish(acc):
        if residual:
            acc = res_ref[...] + gate_ref[0] * acc
        o_ref[...] = acc.astype(o_ref.dtype)

    if nk == 1:
        finish(part)
    else:
        @pl.when(kk == 0)
        def _():
            acc_ref[...] = part

        @pl.when(jnp.logical_and(kk > 0, kk < nk - 1))
        def _():
            acc_ref[...] += part

        @pl.when(kk == nk - 1)
        def _():
            finish(acc_ref[...] + part)


def _mm(a, w, *, tm, tn, tk, out_dtype=F32, res=None, gate=None, group_of_tile=None):
    m, k = a.shape
    n = w.shape[1]
    nk = k // tk
    residual = res is not None
    in_specs = [pl.BlockSpec((tm, tk), lambda i, j, q: (i, q)), pl.BlockSpec((tk, tn), lambda i, j, q: (q, j))]
    args = [a, w]
    if residual:
        in_specs += [pl.BlockSpec((tm, tn), lambda i, j, q: (i, j)),
                     pl.BlockSpec((1, 1, tn), lambda i, j, q: (group_of_tile(i), 0, j))]
        args += [res, gate]
    return pl.pallas_call(
        functools.partial(_mm_kernel, residual=residual, nk=nk),
        out_shape=jax.ShapeDtypeStruct((m, n), out_dtype),
        grid=(m // tm, n // tn, nk),
        in_specs=in_specs,
        out_specs=pl.BlockSpec((tm, tn), lambda i, j, q: (i, j)),
        scratch_shapes=[pltpu.VMEM((tm, tn), F32)],
        compiler_params=_cparams(3),
        name="mm_res" if residual else "mm",
    )(*args)


def _rope(x, cos, sin_signed):
    lane = lax.broadcasted_iota(jnp.int32, x.shape, 1)
    half = RET_HD // 4
    swapped = jnp.where(_mod(lane, 2 * half) < half, pltpu.roll(x, LANES - half, 1), pltpu.roll(x, half, 1))
    return x * cos + swapped * sin_signed


def _ret_direction(q, k, v, s_ref, log_g, backward):
    c = RET_CHUNK
    row = lax.broadcasted_iota(jnp.int32, (c, c), 0)
    col = lax.broadcasted_iota(jnp.int32, (c, c), 1)
    if backward:
        dist = (col - row).astype(F32)
        q_steps = (c - row).astype(F32)
        k_steps = row.astype(F32)
    else:
        dist = (row - col).astype(F32)
        q_steps = (row + 1).astype(F32)
        k_steps = (c - 1 - row).astype(F32)
    dmat = jnp.where(dist >= 0, jnp.exp(log_g * jnp.maximum(dist, 0.0)), 0.0)
    q_dec = jnp.exp(log_g * q_steps)
    k_dec = jnp.exp(log_g * k_steps)
    s_prev = s_ref[...]
    scores = _dot_nt(q, k) * dmat
    out = _dot(scores, v) + _dot(q * q_dec, s_prev)
    s_ref[...] = s_prev * jnp.exp(log_g * float(c)) + _dot_tn(k * k_dec, v)
    return out


def _retention_kernel(qf_ref, kf_ref, vf_ref, cf_ref, sf_ref, qb_ref, kb_ref, vb_ref, cb_ref, sb_ref,
                      decf_ref, decb_ref, of_ref, ob_ref, stf_ref, stb_ref):
    @pl.when(pl.program_id(2) == 0)
    def _():
        stf_ref[...] = jnp.zeros_like(stf_ref)
        stb_ref[...] = jnp.zeros_like(stb_ref)

    kscale = RET_HD ** -0.5
    lgf = -jnp.exp(decf_ref[0])
    lgb = -jnp.exp(decb_ref[0])
    qf = _rope(qf_ref[0, 0], cf_ref[...], sf_ref[...])
    kf = _rope(kf_ref[0, 0], cf_ref[...], sf_ref[...]) * kscale
    of_ref[0, 0] = _ret_direction(qf, kf, vf_ref[0, 0], stf_ref, lgf, False)
    qb = _rope(qb_ref[0, 0], cb_ref[...], sb_ref[...])
    kb = _rope(kb_ref[0, 0], cb_ref[...], sb_ref[...]) * kscale
    ob_ref[0, 0] = _ret_direction(qb, kb, vb_ref[0, 0], stb_ref, lgb, True)


def _retention(q, k, v, cos, sin_signed, dec_f, dec_b, n_ctx_chunks):
    b, h, s, d = q.shape
    c = RET_CHUNK
    n = s // c

    def fwd(bi, hi, t):
        return (bi, hi, t, 0)

    def bwd_chunk(t):
        return jnp.where(t < n_ctx_chunks, n_ctx_chunks - 1 - t, n + n_ctx_chunks - 1 - t)

    def bwd(bi, hi, t):
        return (bi, hi, bwd_chunk(t), 0)

    blk = (1, 1, c, d)
    tab = (c, d)
    dec = lambda a: jnp.broadcast_to(a.astype(F32)[:, None, None], (h, 1, LANES))
    return pl.pallas_call(
        _retention_kernel,
        out_shape=[jax.ShapeDtypeStruct(q.shape, F32)] * 2,
        grid=(b, h, n),
        in_specs=[pl.BlockSpec(blk, fwd)] * 3
        + [pl.BlockSpec(tab, lambda bi, hi, t: (t, 0))] * 2
        + [pl.BlockSpec(blk, bwd)] * 3
        + [pl.BlockSpec(tab, lambda bi, hi, t: (bwd_chunk(t), 0))] * 2
        + [pl.BlockSpec((1, 1, LANES), lambda bi, hi, t: (hi, 0, 0))] * 2,
        out_specs=[pl.BlockSpec(blk, fwd), pl.BlockSpec(blk, bwd)],
        scratch_shapes=[pltpu.VMEM((d, d), F32)] * 2,
        compiler_params=_cparams(3),
        name="retention",
    )(q, k, v, cos, sin_signed, q, k, v, cos, sin_signed, dec(dec_f), dec(dec_b))


def _na_qk_norm(x, w):
    ms = _group_mean(x * x, NA_HD)
    return x * lax.rsqrt(ms + NORM_EPS) * w


def _softmax_pv(parts):
    m = functools.reduce(jnp.maximum, [jnp.max(s, axis=-1, keepdims=True) for s, _ in parts])
    ps = [jnp.exp(s - m) for s, _ in parts]
    den = functools.reduce(jnp.add, [jnp.sum(p, axis=-1, keepdims=True) for p in ps])
    num = functools.reduce(jnp.add, [_dot(p, v) for p, (_, v) in zip(ps, parts)])
    return num / den


def _na_kernel(q_ref, k_ref, v_ref, kc_ref, vc_ref, qn_ref, kn_ref, bias_ref, o_ref, qs_ref, ks_ref, kcs_ref,
               *, rows):
    scale = NA_HD ** -0.5
    qs_ref[...] = (_na_qk_norm(q_ref[...], qn_ref[...]) * scale).astype(BF16)
    ks_ref[...] = _na_qk_norm(k_ref[...], kn_ref[...]).astype(BF16)
    kcs_ref[...] = _na_qk_norm(kc_ref[...], kn_ref[...]).astype(BF16)
    lane = lax.broadcasted_iota(jnp.int32, (GRID_W, LANES), 1)
    first = lane < NA_HD
    vc = vc_ref[...]
    kc = kcs_ref[...]

    def body(r, carry):
        r0 = jnp.clip(r - NA_KR // 2, 0, rows - NA_KR)
        cls = r0 - r + (NA_KR - 1)
        q_r = qs_ref[pl.ds(pl.multiple_of(r * GRID_W, GRID_W), GRID_W), :]
        win = pl.ds(pl.multiple_of(r0 * GRID_W, GRID_W), NA_KR * GRID_W)
        k_w = ks_ref[win, :]
        v_w = v_ref[win, :]
        outs = []
        for hh in range(2):
            qh = jnp.where(first if hh == 0 else jnp.logical_not(first), q_r, jnp.zeros_like(q_r))
            s_w = _dot_nt(qh, k_w) + bias_ref[hh, cls]
            s_c = _dot_nt(qh, kc)
            outs.append(_softmax_pv([(s_w, v_w), (s_c, vc)]))
        o_ref[pl.ds(pl.multiple_of(r * GRID_W, GRID_W), GRID_W), :] = jnp.where(first, outs[0], outs[1])
        return carry

    lax.fori_loop(0, rows, body, 0)


def _na_bias_table(rpb):
    cols = np.arange(GRID_W)
    start = np.clip(cols - NA_KC // 2, 0, GRID_W - NA_KC)
    kcol = np.arange(GRID_W)
    inside = (kcol[None, :] >= start[:, None]) & (kcol[None, :] < start[:, None] + NA_KC)
    col_off = np.clip(kcol[None, :] - cols[:, None] + (NA_KC - 1), 0, 2 * NA_KC - 2)
    row_off = np.arange(NA_KR)[:, None] + np.arange(NA_KR)[None, :]
    tab = rpb[:, row_off][:, :, :, col_off]
    tab = jnp.where(inside[None, None, None], tab, NEG_BIG)
    tab = jnp.transpose(tab, (0, 1, 3, 2, 4))
    return tab.reshape(rpb.shape[0], NA_KR, GRID_W, NA_KR * GRID_W).astype(F32)


def _na_latent(proj_l, proj_c, qn_w, kn_w, bias, batch, seq, ctx_len, col0):
    rows = seq // GRID_W
    pairs = NA_W // LANES
    cb = col0 // LANES
    tile2 = lambda a: jnp.tile(a.astype(F32), 2).reshape(1, LANES)
    lat = lambda off: pl.BlockSpec((seq, LANES), lambda bi, p: (bi, cb + off * pairs + p))
    ctx = lambda off: pl.BlockSpec((ctx_len, LANES), lambda bi, p: (bi, cb + off * pairs + p))
    return pl.pallas_call(
        functools.partial(_na_kernel, rows=rows),
        out_shape=jax.ShapeDtypeStruct((batch * seq, NA_W), F32),
        grid=(batch, pairs),
        in_specs=[lat(0), lat(1), lat(2), ctx(1), ctx(2),
                  pl.BlockSpec((1, LANES), lambda bi, p: (0, 0)), pl.BlockSpec((1, LANES), lambda bi, p: (0, 0)),
                  pl.BlockSpec((2, NA_KR, GRID_W, NA_KR * GRID_W), lambda bi, p: (p, 0, 0, 0))],
        out_specs=pl.BlockSpec((seq, LANES), lambda bi, p: (bi, p)),
        scratch_shapes=[pltpu.VMEM((seq, LANES), BF16), pltpu.VMEM((seq, LANES), BF16),
                        pltpu.VMEM((ctx_len, LANES), BF16)],
        compiler_params=_cparams(2),
        name="na_latent",
    )(proj_l, proj_l, proj_l, proj_c, proj_c, tile2(qn_w), tile2(kn_w), bias)


def _ctx_attn_kernel(q_ref, k_ref, v_ref, qn_ref, kn_ref, o_ref):
    scale = NA_HD ** -0.5
    q = (_na_qk_norm(q_ref[...], qn_ref[...]) * scale).astype(BF16)
    k = _na_qk_norm(k_ref[...], kn_ref[...]).astype(BF16)
    v = v_ref[...]
    lane = lax.broadcasted_iota(jnp.int32, q.shape, 1)
    first = lane < NA_HD
    outs = []
    for hh in range(2):
        qh = jnp.where(first if hh == 0 else jnp.logical_not(first), q, jnp.zeros_like(q))
        outs.append(_softmax_pv([(_dot_nt(qh, k), v)]))
    o_ref[...] = jnp.where(first, outs[0], outs[1])


def _ctx_attention(proj_c, qn_w, kn_w, batch, ctx_len, col0):
    pairs = NA_W // LANES
    cb = col0 // LANES
    tile2 = lambda a: jnp.tile(a.astype(F32), 2).reshape(1, LANES)
    blk = lambda off: pl.BlockSpec((ctx_len, LANES), lambda bi, p: (bi, cb + off * pairs + p))
    return pl.pallas_call(
        _ctx_attn_kernel,
        out_shape=jax.ShapeDtypeStruct((batch * ctx_len, NA_W), F32),
        grid=(batch, pairs),
        in_specs=[blk(0), blk(1), blk(2),
                  pl.BlockSpec((1, LANES), lambda bi, p: (0, 0)), pl.BlockSpec((1, LANES), lambda bi, p: (0, 0))],
        out_specs=pl.BlockSpec((ctx_len, LANES), lambda bi, p: (bi, p)),
        compiler_params=_cparams(2),
        name="ctx_attention",
    )(proj_c, proj_c, proj_c, tile2(qn_w), tile2(kn_w))


def _ret_readout_kernel(yf_ref, yb_ref, g_ref, na_ref, gnw_ref, o_ref):
    y = yf_ref[...] + yb_ref[...]
    for hh in range(RET_HEADS):
        sl = slice(hh * RET_HD, (hh + 1) * RET_HD)
        yh = y[:, sl]
        yc = yh - jnp.mean(yh, axis=-1, keepdims=True)
        yn = yc * lax.rsqrt(jnp.mean(yc * yc, axis=-1, keepdims=True) + RET_GN_EPS)
        o_ref[:, sl] = (yn * gnw_ref[:, sl] * _silu(g_ref[:, sl])).astype(o_ref.dtype)
    o_ref[:, RET_W:] = na_ref[...].astype(o_ref.dtype)


def _ret_readout(yf, yb, proj, na, gn_w, tm):
    m = yf.shape[0]
    gate_block = (3 * RET_W) // RET_W
    return pl.pallas_call(
        _ret_readout_kernel,
        out_shape=jax.ShapeDtypeStruct((m, RET_W + NA_W), BF16),
        grid=(m // tm,),
        in_specs=[pl.BlockSpec((tm, RET_W), lambda i: (i, 0)), pl.BlockSpec((tm, RET_W), lambda i: (i, 0)),
                  pl.BlockSpec((tm, RET_W), lambda i: (i, gate_block)), pl.BlockSpec((tm, NA_W), lambda i: (i, 0)),
                  pl.BlockSpec((1, RET_W), lambda i: (0, 0))],
        out_specs=pl.BlockSpec((tm, RET_W + NA_W), lambda i: (i, 0)),
        compiler_params=_cparams(1),
        name="ret_readout",
    )(yf, yb, proj, na, gn_w.reshape(1, RET_W).astype(F32))


def _rwkv_mix_kernel(x_ref, xp_ref, xn_ref, nw_ref, sh_ref, sc_ref, mu_ref, *o_refs, seq, tm):
    nw, sh, sc = nw_ref[...], sh_ref[0], sc_ref[0]
    h = _modnorm(x_ref[...], nw, sh, sc)
    hp = _modnorm(xp_ref[...], nw, sh, sc)[7:8]
    hn = _modnorm(xn_ref[...], nw, sh, sc)[0:1]
    row = lax.broadcasted_iota(jnp.int32, h.shape, 0)
    pos = _mod(pl.program_id(0) * tm + row, seq)
    up = jnp.where(row == 0, hp, pltpu.roll(h, 1, 0))
    up = jnp.where(pos == 0, 0.0, up)
    dn = jnp.where(row == tm - 1, hn, pltpu.roll(h, tm - 1, 0))
    dn = jnp.where(pos == seq - 1, 0.0, dn)
    xx = 0.5 * (up + dn) - h
    for s_i, o_ref in enumerate(o_refs):
        o_ref[...] = (h + xx * mu_ref[s_i:s_i + 1, :]).astype(o_ref.dtype)


def _rwkv_mix(x, nw, shift, scale, mu, *, seq, tm, group_of_tile):
    m, d = x.shape
    n_slab = m // 8
    per = tm // 8
    gmap = lambda i: (group_of_tile(i), 0, 0)
    return pl.pallas_call(
        functools.partial(_rwkv_mix_kernel, seq=seq, tm=tm),
        out_shape=[jax.ShapeDtypeStruct((m, d), BF16)] * 6,
        grid=(m // tm,),
        in_specs=[pl.BlockSpec((tm, d), lambda i: (i, 0)),
                  pl.BlockSpec((8, d), lambda i: (jnp.maximum(i * per - 1, 0), 0)),
                  pl.BlockSpec((8, d), lambda i: (jnp.minimum((i + 1) * per, n_slab - 1), 0)),
                  pl.BlockSpec((1, d), lambda i: (0, 0)),
                  pl.BlockSpec((1, 1, d), gmap), pl.BlockSpec((1, 1, d), gmap),
                  pl.BlockSpec((6, d), lambda i: (0, 0))],
        out_specs=[pl.BlockSpec((tm, d), lambda i: (i, 0))] * 6,
        compiler_params=_cparams(1),
        name="rwkv_mix",
    )(x, x, x, nw.reshape(1, d), shift, scale, mu)


def _rwkv_post_kernel(r_ref, k_ref, v_ref, xw_ref, xa_ref, xg_ref, w1_ref, w2_ref, a1_ref, a2_ref, g1_ref, g2_ref,
                      w0_ref, a0_ref, kk_ref, ka_ref, rk_ref,
                      rt0, kt0, bt0, at0, wc0, rt1, kt1, bt1, at1, wc1, vb_ref, g_ref, bonus_ref, *, tm):
    c = RWKV_CHUNK
    r, k, v = r_ref[...], k_ref[...], v_ref[...]
    lw_low = jnp.tanh(jnp.dot(xw_ref[...], w1_ref[...], preferred_element_type=F32))
    a_low = jnp.dot(xa_ref[...], a1_ref[...], preferred_element_type=F32)
    g_low = _sigmoid(jnp.dot(xg_ref[...], g1_ref[...], preferred_element_type=F32))
    g_ref[...] = _dot(g_low, g2_ref[...]).astype(g_ref.dtype)
    vb_ref[...] = v.astype(vb_ref.dtype)

    kk = k * kk_ref[...]
    nrm = jnp.sqrt(_group_mean(kk * kk, RWKV_HD) * float(RWKV_HD))
    kk = kk / jnp.maximum(nrm, 1e-12)

    tr = lax.broadcasted_iota(jnp.int32, (c, c), 0)
    tc = lax.broadcasted_iota(jnp.int32, (c, c), 1)
    tri = [(tr >= tc).astype(BF16), (tr <= tc).astype(BF16)]
    lora = w2_ref.shape[1]
    coeff_src = jnp.zeros_like(r)
    outs = [(rt0, kt0, bt0, at0, wc0), (rt1, kt1, bt1, at1, wc1)]
    for z in range(2):
        rt_ref, kt_ref, bt_ref, at_ref, wc_ref = outs[z]
        w_lora = _dot(lw_low[:, z * lora:(z + 1) * lora], w2_ref[z])
        log_w = -_softplus(-(w0_ref[z:z + 1, :] + w_lora)) - 0.5
        lw = -jnp.exp(log_w)
        a_z = _sigmoid(a0_ref[z:z + 1, :] + _dot(a_low[:, z * lora:(z + 1) * lora], a2_ref[z]))
        k_dir = k * (1.0 + (a_z - 1.0) * ka_ref[...])
        coeff_src = coeff_src + r * k_dir * rk_ref[...]
        b_z = kk * a_z
        for ci in range(tm // c):
            sl = slice(ci * c, (ci + 1) * c)
            lw_c = lw[sl]
            cum = _dot_exact_rhs_left(tri[z], lw_c)
            e_pos = jnp.exp(cum)
            e_neg = jnp.exp(-cum)
            rt_ref[sl, :] = (r[sl] * e_pos).astype(rt_ref.dtype)
            kt_ref[sl, :] = (k_dir[sl] * e_neg).astype(kt_ref.dtype)
            bt_ref[sl, :] = (b_z[sl] * e_neg).astype(bt_ref.dtype)
            at_ref[sl, :] = (-kk[sl] * jnp.exp(cum - lw_c)).astype(at_ref.dtype)
            last = cum[c - 1:c, :] if z == 0 else cum[0:1, :]
            wc_ref[ci:ci + 1, :] = jnp.exp(last)
    coeff = _group_mean(coeff_src, RWKV_HD) * float(RWKV_HD)
    bonus_ref[...] = coeff * v


def _dot_exact_rhs_left(tri_bf16, x):
    xh, xm, xl = _split3(x)
    d = functools.partial(jnp.dot, preferred_element_type=F32)
    return d(tri_bf16, xh) + d(tri_bf16, xm) + d(tri_bf16, xl)


def _rwkv_post(r, k, v, xw, xa, xg, p, *, tm):
    m, d = r.shape
    c = RWKV_CHUNK
    lora = p["w2"].shape[1]
    glora = p["g1"].shape[1]
    tok = lambda dt: jax.ShapeDtypeStruct((m, d), dt)
    chunk = jax.ShapeDtypeStruct((m // c, d), F32)
    row = lambda i: (i, 0)
    full2 = lambda i: (0, 0)
    full3 = lambda i: (0, 0, 0)
    tspec = pl.BlockSpec((tm, d), row)
    cspec = pl.BlockSpec((tm // c, d), row)
    return pl.pallas_call(
        functools.partial(_rwkv_post_kernel, tm=tm),
        out_shape=[tok(BF16)] * 4 + [chunk] + [tok(BF16)] * 4 + [chunk] + [tok(BF16), tok(BF16), tok(F32)],
        grid=(m // tm,),
        in_specs=[tspec] * 6
        + [pl.BlockSpec((d, 2 * lora), full2), pl.BlockSpec((2, lora, d), full3),
           pl.BlockSpec((d, 2 * lora), full2), pl.BlockSpec((2, lora, d), full3),
           pl.BlockSpec((d, glora), full2), pl.BlockSpec((glora, d), full2),
           pl.BlockSpec((2, d), full2), pl.BlockSpec((2, d), full2)]
        + [pl.BlockSpec((1, d), full2)] * 3,
        out_specs=[tspec] * 4 + [cspec] + [tspec] * 4 + [cspec] + [tspec] * 3,
        compiler_params=_cparams(1),
        name="rwkv_post",
    )(r, k, v, xw, xa, xg, p["w1"], p["w2"], p["a1"], p["a2"], p["g1"], p["g2"], p["w0"], p["a0"],
      p["k_k"], p["k_a"], p["r_k"])


def _bmm(a, b):
    return jnp.einsum("hcd,hde->hce", a.astype(BF16), b.astype(BF16), preferred_element_type=F32)


def _bmm_solve(a, b):
    return jnp.einsum("hcd,hde->hce", a, b, preferred_element_type=F32, precision=lax.Precision.HIGHEST)


def _bmm_nt(a, b):
    return jnp.einsum("hcd,hsd->hcs", a.astype(BF16), b.astype(BF16), preferred_element_type=F32)


def _bmm_tn(a, b):
    return jnp.einsum("hce,hcd->hed", a.astype(BF16), b.astype(BF16), preferred_element_type=F32)


def _rwkv_scan_kernel(rt_ref, kt_ref, bt_ref, at_ref, v_ref, wc_ref, y_ref, s_ref, *, backward):
    @pl.when(pl.program_id(1) == 0)
    def _():
        s_ref[...] = jnp.zeros_like(s_ref)

    c = RWKV_CHUNK
    rt, kt, bt, at, v = rt_ref[0], kt_ref[0], bt_ref[0], at_ref[0], v_ref[0]
    s0 = s_ref[...]
    row = lax.broadcasted_iota(jnp.int32, (1, c, c), 1)
    col = lax.broadcasted_iota(jnp.int32, (1, c, c), 2)
    strict = (row < col) if backward else (row > col)
    incl = (row <= col) if backward else (row >= col)
    a_ab = jnp.where(strict, _bmm_nt(at, bt), 0.0)
    a_ak = jnp.where(strict, _bmm_nt(at, kt), 0.0)
    a_rb = jnp.where(incl, _bmm_nt(rt, bt), 0.0)
    a_rk = jnp.where(incl, _bmm_nt(rt, kt), 0.0)
    u = _bmm_nt(at, s0) + _bmm(a_ak, v)
    pw = a_ab
    n_sq = int(np.log2(c))
    for it in range(n_sq):
        u = u + _bmm_solve(pw, u)
        if it + 1 < n_sq:
            pw = _bmm_solve(pw, pw)
    y_ref[0] = _bmm_nt(rt, s0) + _bmm(a_rb, u) + _bmm(a_rk, v)
    s_ref[...] = (s0 + _bmm_tn(u, bt) + _bmm_tn(v, kt)) * wc_ref[0]


def _rwkv_scan(rt, kt, bt, at, v, wc, *, n_ctx_chunks, backward):
    b, h, s, n = rt.shape
    c = RWKV_CHUNK
    nchunks = s // c

    def chunk(t):
        if not backward:
            return t
        return jnp.where(t < n_ctx_chunks, n_ctx_chunks - 1 - t, nchunks + n_ctx_chunks - 1 - t)

    tok = pl.BlockSpec((1, h, c, n), lambda bi, t: (bi, 0, chunk(t), 0))
    return pl.pallas_call(
        functools.partial(_rwkv_scan_kernel, backward=backward),
        out_shape=jax.ShapeDtypeStruct((b, h, s, n), F32),
        grid=(b, nchunks),
        in_specs=[tok] * 5 + [pl.BlockSpec((1, h, 1, n), lambda bi, t: (bi * nchunks + chunk(t), 0, 0, 0))],
        out_specs=tok,
        scratch_shapes=[pltpu.VMEM((h, n, n), F32)],
        compiler_params=_cparams(2),
        name="rwkv_scan_b" if backward else "rwkv_scan_f",
    )(rt, kt, bt, at, v, wc)


def _rwkv_readout_kernel(yf_ref, yb_ref, bonus_ref, g_ref, lnw_ref, lnb_ref, o_ref):
    y = yf_ref[...] + yb_ref[...]
    yc = y - _group_mean(y, RWKV_HD)
    yn = yc * lax.rsqrt(_group_mean(yc * yc, RWKV_HD) + RWKV_GN_EPS) * lnw_ref[...] + lnb_ref[...]
    o_ref[...] = ((yn + bonus_ref[...]) * g_ref[...].astype(F32)).astype(o_ref.dtype)


def _rwkv_readout(yf, yb, bonus, g, ln_w, ln_b, tm):
    m, d = yf.shape
    tspec = pl.BlockSpec((tm, d), lambda i: (i, 0))
    vspec = pl.BlockSpec((1, d), lambda i: (0, 0))
    return pl.pallas_call(
        _rwkv_readout_kernel,
        out_shape=jax.ShapeDtypeStruct((m, d), BF16),
        grid=(m // tm,),
        in_specs=[tspec] * 4 + [vspec] * 2,
        out_specs=tspec,
        compiler_params=_cparams(1),
        name="rwkv_readout",
    )(yf, yb, bonus, g, ln_w.reshape(1, d).astype(F32), ln_b.reshape(1, d).astype(F32))


def _router_kernel(x_ref, nw_ref, sh_ref, sc_ref, w_ref, o_ref, *, n_experts):
    h = _modnorm(x_ref[...], nw_ref[...], sh_ref[0], sc_ref[0])
    logits = _dot_hi(h, w_ref[...])
    lane = lax.broadcasted_iota(jnp.int32, logits.shape, 1).astype(F32)
    logits = jnp.where(lane < n_experts, logits, -jnp.inf)
    v1 = jnp.max(logits, axis=-1, keepdims=True)
    i1 = jnp.min(jnp.where(logits == v1, lane, float(LANES)), axis=-1, keepdims=True)
    rest = jnp.where(lane == i1, -jnp.inf, logits)
    v2 = jnp.max(rest, axis=-1, keepdims=True)
    i2 = jnp.min(jnp.where(rest == v2, lane, float(LANES)), axis=-1, keepdims=True)
    e2 = jnp.exp(v2 - v1)
    den = 1.0 + e2
    o_ref[...] = jnp.where(lane == i1, 1.0 / den, 0.0) + jnp.where(lane == i2, e2 / den, 0.0)


def _router(x, nw, shift, scale, router, *, tm, group_of_tile):
    m, d = x.shape
    n_experts = router.shape[1]
    w = jnp.zeros((d, LANES), F32).at[:, :n_experts].set(router.astype(F32))
    gmap = lambda i: (group_of_tile(i), 0, 0)
    return pl.pallas_call(
        functools.partial(_router_kernel, n_experts=n_experts),
        out_shape=jax.ShapeDtypeStruct((m, LANES), F32),
        grid=(m // tm,),
        in_specs=[pl.BlockSpec((tm, d), lambda i: (i, 0)), pl.BlockSpec((1, d), lambda i: (0, 0)),
                  pl.BlockSpec((1, 1, d), gmap), pl.BlockSpec((1, 1, d), gmap),
                  pl.BlockSpec((d, LANES), lambda i: (0, 0))],
        out_specs=pl.BlockSpec((tm, LANES), lambda i: (i, 0)),
        compiler_params=_cparams(1),
        name="router",
    )(x, nw.reshape(1, d), shift, scale, w)


def _to_heads(t, batch, n_heads):
    m, w = t.shape
    return t.reshape(batch, m // batch, n_heads, w // n_heads).transpose(0, 2, 1, 3)


def _from_heads(t):
    b, h, l, d = t.shape
    return t.transpose(0, 2, 1, 3).reshape(b * l, h * d)


def _rope_tables(seq, ctx_len):
    half = RET_HD // 4
    freqs = ROPE_BASE ** (-np.arange(half, dtype=np.float64) / half)
    t = np.arange(seq)
    ang_r = (t // GRID_W)[:, None] * freqs[None, :]
    ang_c = (t % GRID_W)[:, None] * freqs[None, :]
    cos = np.concatenate([np.cos(ang_r)] * 2 + [np.cos(ang_c)] * 2, axis=1)
    sin = np.concatenate([-np.sin(ang_r), np.sin(ang_r), -np.sin(ang_c), np.sin(ang_c)], axis=1)
    cos = np.concatenate([np.ones((ctx_len, RET_HD)), cos], axis=0)
    sin = np.concatenate([np.zeros((ctx_len, RET_HD)), sin], axis=0)
    return jnp.asarray(cos, F32), jnp.asarray(sin, F32)


def _tile_rows(m):
    return 512 if m % 512 == 0 else 256


def kernel(x, c, ctx, c_ctx, ada_w, ada_b, norm_mix_w, norm_ffn_w, ev_w_in, ev_ret_decay_f, ev_ret_decay_b,
           ev_ret_gn_w, ev_na_qn_w, ev_na_kn_w, ev_na_rpb, ev_w_out, ev_ffn_w13, ev_ffn_w2, od_mu, od_w_rkv,
           od_w0, od_w1, od_w2, od_a0, od_a1, od_a2, od_g1, od_g2, od_k_k, od_k_a, od_r_k, od_ln_w, od_ln_b,
           od_w_o, od_router, od_moe_w13, od_moe_w2):
    batch, seq, d = x.shape
    ctx_len = ctx.shape[1]
    m_l, m_c = batch * seq, batch * ctx_len
    x_l = x.reshape(m_l, d)
    x_c = ctx.reshape(m_c, d)
    tm_l, tm_c = _tile_rows(m_l), _tile_rows(m_c)
    grp_l = lambda tm: (lambda i: (i * tm) // seq)
    grp_c = lambda tm: (lambda i: batch)

    rows = ((batch + 1 + 7) // 8) * 8
    c_rows = jnp.zeros((rows, d), F32).at[:batch].set(c).at[batch].set(c_ctx)
    mods = _ada(c_rows, ada_w, ada_b)
    mods = mods.reshape(mods.shape[0], rows, 6, 1, d).transpose(0, 2, 1, 3, 4)

    mod = mods[0]
    w_in = ev_w_in[0].astype(BF16)
    proj_l = _modnorm_mm(x_l, norm_mix_w[0], mod[0], mod[1], w_in, group_of_tile=grp_l(tm_l), tm=tm_l, tn=512,
                         out_dtype=F32)
    proj_c = _modnorm_mm(x_c, norm_mix_w[0], mod[0], mod[1], w_in, group_of_tile=grp_c(tm_c), tm=tm_c, tn=512,
                         out_dtype=F32)

    def ret_heads(col):
        a = jnp.concatenate([_to_heads(proj_c[:, col:col + RET_W], batch, RET_HEADS),
                             _to_heads(proj_l[:, col:col + RET_W], batch, RET_HEADS)], axis=2)
        return a

    cos, sin = _rope_tables(seq, ctx_len)
    n_ctx_chunks = ctx_len // RET_CHUNK
    y_f, y_b = _retention(ret_heads(0), ret_heads(RET_W), ret_heads(2 * RET_W), cos, sin,
                          ev_ret_decay_f[0], ev_ret_decay_b[0], n_ctx_chunks)
    yf_c, yf_l = _from_heads(y_f[:, :, :ctx_len]), _from_heads(y_f[:, :, ctx_len:])
    yb_c, yb_l = _from_heads(y_b[:, :, :ctx_len]), _from_heads(y_b[:, :, ctx_len:])

    bias = _na_bias_table(ev_na_rpb[0].astype(F32))
    na_l = _na_latent(proj_l, proj_c, ev_na_qn_w[0], ev_na_kn_w[0], bias, batch, seq, ctx_len, 4 * RET_W)
    na_c = _ctx_attention(proj_c, ev_na_qn_w[0], ev_na_kn_w[0], batch, ctx_len, 4 * RET_W)

    mix_l = _ret_readout(yf_l, yb_l, proj_l, na_l, ev_ret_gn_w[0], tm_l)
    mix_c = _ret_readout(yf_c, yb_c, proj_c, na_c, ev_ret_gn_w[0], tm_c)
    w_out = ev_w_out[0].astype(BF16)
    x_l = _mm(mix_l, w_out, tm=tm_l, tn=512, tk=w_out.shape[0], res=x_l, gate=mod[2], group_of_tile=grp_l(tm_l))
    x_c = _mm(mix_c, w_out, tm=tm_c, tn=512, tk=w_out.shape[0], res=x_c, gate=mod[2], group_of_tile=grp_c(tm_c))

    w13 = ev_ffn_w13[0].astype(BF16)
    w2 = ev_ffn_w2[0].astype(BF16)
    hid = w2.shape[0]
    tn13 = 256 if hid % 256 == 0 else 128
    hid_l = _modnorm_mm(x_l, norm_ffn_w[0], mod[3], mod[4], w13, group_of_tile=grp_l(tm_l), tm=tm_l, tn=tn13,
                        out_dtype=BF16, swiglu=True)
    hid_c = _modnorm_mm(x_c, norm_ffn_w[0], mod[3], mod[4], w13, group_of_tile=grp_c(tm_c), tm=tm_c, tn=tn13,
                        out_dtype=BF16, swiglu=True)
    x_l = _mm(hid_l, w2, tm=tm_l, tn=512, tk=hid, res=x_l, gate=mod[5], group_of_tile=grp_l(tm_l))
    x_c = _mm(hid_c, w2, tm=tm_c, tn=512, tk=hid, res=x_c, gate=mod[5], group_of_tile=grp_c(tm_c))

    mod = mods[1]
    lora = od_w1.shape[-1]
    glora = od_g1.shape[-1]
    glora_p = ((glora + LANES - 1) // LANES) * LANES
    p = {
        "w1": jnp.concatenate([od_w1[0, 0], od_w1[0, 1]], axis=1).astype(BF16),
        "w2": od_w2[0].astype(BF16),
        "a1": jnp.concatenate([od_a1[0, 0], od_a1[0, 1]], axis=1).astype(BF16),
        "a2": od_a2[0].astype(BF16),
        "g1": jnp.zeros((d, glora_p), BF16).at[:, :glora].set(od_g1[0].astype(BF16)),
        "g2": jnp.zeros((glora_p, d), BF16).at[:glora].set(od_g2[0].astype(BF16)),
        "w0": od_w0[0].astype(F32), "a0": od_a0[0].astype(F32),
        "k_k": od_k_k[0].reshape(1, d).astype(F32), "k_a": od_k_a[0].reshape(1, d).astype(F32),
        "r_k": od_r_k[0].reshape(1, d).astype(F32),
    }
    w_rkv = od_w_rkv[0].astype(BF16)
    tm_p = 8 * RWKV_CHUNK
    n_heads = d // RWKV_HD

    def features(xs, m_rows, seq_len, tm, grp):
        xm = _rwkv_mix(xs, norm_mix_w[1], mod[0], mod[1], od_mu[0].astype(F32), seq=seq_len, tm=tm,
                       group_of_tile=grp(tm))
        r, k, v = [_mm(xm[s_i], w_rkv[s_i], tm=tm, tn=512, tk=d) for s_i in range(3)]
        outs = _rwkv_post(r, k, v, xm[3], xm[4], xm[5], p, tm=tm_p)
        return outs

    f_l = features(x_l, m_l, seq, tm_l, grp_l)
    f_c = features(x_c, m_c, ctx_len, tm_c, grp_c)
    n_ctx_chunks = ctx_len // RWKV_CHUNK

    def heads_cat(a_c, a_l):
        return jnp.concatenate([_to_heads(a_c, batch, n_heads), _to_heads(a_l, batch, n_heads)], axis=2)

    def wc_cat(a_c, a_l):
        a = jnp.concatenate([a_c.reshape(batch, ctx_len // RWKV_CHUNK, n_heads, RWKV_HD),
                             a_l.reshape(batch, seq // RWKV_CHUNK, n_heads, RWKV_HD)], axis=1)
        return a.reshape(-1, n_heads, 1, RWKV_HD)

    v_h = heads_cat(f_c[10], f_l[10])
    ys = []
    for z in range(2):
        o = 5 * z
        y = _rwkv_scan(heads_cat(f_c[o], f_l[o]), heads_cat(f_c[o + 1], f_l[o + 1]),
                       heads_cat(f_c[o + 2], f_l[o + 2]), heads_cat(f_c[o + 3], f_l[o + 3]), v_h,
                       wc_cat(f_c[o + 4], f_l[o + 4]), n_ctx_chunks=n_ctx_chunks, backward=(z == 1))
        ys.append(_from_heads(y[:, :, ctx_len:]))
    mixed = _rwkv_readout(ys[0], ys[1], f_l[12], f_l[11], od_ln_w[0], od_ln_b[0], tm_l)
    x_l = _mm(mixed, od_w_o[0].astype(BF16), tm=tm_l, tn=512, tk=d, res=x_l, gate=mod[2],
              group_of_tile=grp_l(tm_l))

    gates = _router(x_l, norm_ffn_w[1], mod[3], mod[4], od_router[0], tm=tm_l, group_of_tile=grp_l(tm_l))
    moe13 = od_moe_w13[0].astype(BF16)
    n_e, _, two_h = moe13.shape
    e_hid = two_h // 2
    tn_e = 512 if e_hid % 512 == 0 else 256
    hid_all = _modnorm_mm(x_l, norm_ffn_w[1], mod[3], mod[4], moe13, group_of_tile=grp_l(tm_l), tm=tm_l, tn=tn_e,
                          out_dtype=BF16, swiglu=True, gates=gates)
    moe2 = od_moe_w2[0].astype(BF16).reshape(n_e * e_hid, d)
    tk2 = e_hid // 2 if (e_hid // 2) % LANES == 0 else e_hid
    x_l = _mm(hid_all, moe2, tm=tm_l, tn=512, tk=tk2, res=x_l, gate=mod[5], group_of_tile=grp_l(tm_l))
    return x_l.reshape(batch, seq, d)
```

```python
import functools

import jax
import jax.numpy as jnp
import numpy as np
from jax import lax
from jax.experimental import pallas as pl
from jax.experimental.pallas import tpu as pltpu

F32 = jnp.float32
BF16 = jnp.bfloat16

LANES = 128
GRID_W = 64
RET_HEADS = 4
RET_HD = 128
RET_W = RET_HEADS * RET_HD
RET_CHUNK = 128
RET_GN_EPS = 1e-5
NA_HEADS = 8
NA_HD = 64
NA_W = NA_HEADS * NA_HD
NA_KR = 8
NA_KC = 16
RWKV_HD = 64
RWKV_GN_EPS = 64e-5
RWKV_CHUNK = 64
ROPE_BASE = 10000.0
NORM_EPS = 1e-6
NEG_BIG = -1e30
VMEM_LIMIT = 56 * 1024 * 1024


def _cparams(n_axes):
    return pltpu.CompilerParams(dimension_semantics=("arbitrary",) * n_axes, vmem_limit_bytes=VMEM_LIMIT)


def _dot(a, b):
    return jnp.dot(a.astype(BF16), b.astype(BF16), preferred_element_type=F32)


def _dot_nt(a, b):
    return lax.dot_general(a.astype(BF16), b.astype(BF16), (((1,), (1,)), ((), ())), preferred_element_type=F32)


def _dot_tn(a, b):
    return lax.dot_general(a.astype(BF16), b.astype(BF16), (((0,), (0,)), ((), ())), preferred_element_type=F32)


def _split2(x):
    hi = x.astype(BF16)
    return hi, (x - hi.astype(F32)).astype(BF16)


def _split3(x):
    hi = x.astype(BF16)
    r1 = x - hi.astype(F32)
    mid = r1.astype(BF16)
    lo = (r1 - mid.astype(F32)).astype(BF16)
    return hi, mid, lo


def _dot_hi(a, b):
    ah, am, al = _split3(a)
    bh, bm, bl = _split3(b)
    d = functools.partial(jnp.dot, preferred_element_type=F32)
    return (d(ah, bh) + (d(ah, bm) + d(am, bh)) + (d(am, bm) + d(ah, bl) + d(al, bh)))


def _dot_x3(a, b):
    ah, al = _split2(a)
    bh, bl = _split2(b)
    d = functools.partial(jnp.dot, preferred_element_type=F32)
    return d(ah, bh) + (d(ah, bl) + d(al, bh))


def _dot_exact_rhs(a, b_bf16):
    ah, am, al = _split3(a)
    d = functools.partial(jnp.dot, preferred_element_type=F32)
    return d(ah, b_bf16) + d(am, b_bf16) + d(al, b_bf16)


def _dot_exact_lhs(a_bf16, b):
    bh, bm, bl = _split3(b)
    d = functools.partial(jnp.dot, preferred_element_type=F32)
    return d(a_bf16, bh) + d(a_bf16, bm) + d(a_bf16, bl)


def _mod(v, n):
    return (v & (n - 1)) if n & (n - 1) == 0 else v % n


def _sigmoid(x):
    return 1.0 / (1.0 + jnp.exp(-x))


def _silu(x):
    return x * _sigmoid(x)


def _softplus(x):
    return jnp.maximum(x, 0.0) + jnp.log(1.0 + jnp.exp(-jnp.abs(x)))


def _group_mean_mat(width, group):
    r = lax.broadcasted_iota(jnp.int32, (width, width), 0) // group
    c = lax.broadcasted_iota(jnp.int32, (width, width), 1) // group
    return jnp.where(r == c, 1.0 / group, 0.0).astype(BF16)


def _group_mean(x, group):
    g = _group_mean_mat(LANES, group)
    cols = [_dot_exact_rhs(x[:, c:c + LANES], g) for c in range(0, x.shape[1], LANES)]
    return cols[0] if len(cols) == 1 else jnp.concatenate(cols, axis=1)


def _modnorm(x, nw, shift, scale):
    ms = jnp.mean(x * x, axis=-1, keepdims=True)
    y = x * lax.rsqrt(ms + NORM_EPS) * nw
    return y * (1.0 + scale) + shift


class _Rows:
    def __init__(self, batch, seq, ctx_len):
        self.batch, self.seq, self.ctx_len = batch, seq, ctx_len
        self.ml, self.mc = batch * seq, batch * ctx_len
        self.m = self.ml + self.mc

    def group_of_tile(self, tm):
        return lambda i: jnp.minimum((i * tm) // self.seq, self.batch)

    def chunk_block(self, chunk, backward):
        ncc, nlc = self.ctx_len // chunk, self.seq // chunk

        def block(b, t):
            if backward:
                t = jnp.where(t < ncc, ncc - 1 - t, nlc + 2 * ncc - 1 - t)
            return jnp.where(t < ncc, (self.ml + b * self.ctx_len) // chunk + t, (b * self.seq) // chunk + t - ncc)

        def position(t):
            if backward:
                t = jnp.where(t < ncc, ncc - 1 - t, nlc + 2 * ncc - 1 - t)
            return t

        return block, position


def _ada_kernel(c_ref, w_ref, b_ref, o_ref):
    s = _silu(c_ref[...])
    o_ref[0] = _dot_hi(s, w_ref[0]) + b_ref[0]


def _ada(c_rows, ada_w, ada_b):
    depth, d, n = ada_w.shape
    rows = c_rows.shape[0]
    tn = 1536
    return pl.pallas_call(
        _ada_kernel,
        out_shape=jax.ShapeDtypeStruct((depth, rows, n), F32),
        grid=(depth, n // tn),
        in_specs=[
            pl.BlockSpec((rows, d), lambda l, j: (0, 0)),
            pl.BlockSpec((1, d, tn), lambda l, j: (l, 0, j)),
            pl.BlockSpec((1, 1, tn), lambda l, j: (l, 0, j)),
        ],
        out_specs=pl.BlockSpec((1, rows, tn), lambda l, j: (l, 0, j)),
        compiler_params=_cparams(2),
        name="ada_mod",
    )(c_rows, ada_w, ada_b.reshape(depth, 1, n))


def _modnorm_mm_kernel(*refs, swiglu, gated):
    x_ref, nw_ref, sh_ref, sc_ref = refs[:4]
    pos = 4
    w_refs = refs[pos:pos + (2 if swiglu else 1)]
    pos += len(w_refs)
    gates_ref = refs[pos] if gated else None
    pos += 1 if gated else 0
    o_ref, h_ref = refs[pos], refs[pos + 1]

    first = pl.program_id(1) == 0
    if gated:
        first = jnp.logical_and(first, pl.program_id(2) == 0)

    @pl.when(first)
    def _():
        h_ref[...] = _modnorm(x_ref[...], nw_ref[...], sh_ref[0], sc_ref[0]).astype(BF16)

    h = h_ref[...]
    w0 = w_refs[0][0] if gated else w_refs[0][...]
    acc = jnp.dot(h, w0, preferred_element_type=F32)
    if swiglu:
        w1 = w_refs[1][0] if gated else w_refs[1][...]
        acc = _silu(acc) * jnp.dot(h, w1, preferred_element_type=F32)
    if gated:
        e = pl.program_id(1)
        lane = lax.broadcasted_iota(jnp.int32, gates_ref.shape, 1)
        g = jnp.sum(jnp.where(lane == e, gates_ref[...], 0.0), axis=-1, keepdims=True)
        acc = acc * g
    o_ref[...] = acc.astype(o_ref.dtype)


def _modnorm_mm(x, nw, shift, scale, w, *, group_of_tile, tm, tn, out_dtype, swiglu=False, gates=None, rows=None):
    m = x.shape[0] if rows is None else rows
    k = x.shape[1]
    gated = gates is not None
    n_total = w.shape[-1]
    n_out = n_total // 2 if swiglu else n_total
    nj = n_out // tn
    if gated:
        n_e = w.shape[0]
        grid = (m // tm, n_e, nj)
        xmap = lambda i, e, j: (i, 0)
        cmap = lambda i, e, j: (0, 0)
        gmap = lambda i, e, j: (group_of_tile(i), 0, 0)
        wspecs = [pl.BlockSpec((1, k, tn), lambda i, e, j: (e, 0, j))]
        if swiglu:
            wspecs.append(pl.BlockSpec((1, k, tn), lambda i, e, j: (e, 0, j + nj)))
        extra = [pl.BlockSpec((tm, LANES), lambda i, e, j: (i, 0))]
        omap = lambda i, e, j: (i, e * nj + j)
        out_cols = n_e * n_out
    else:
        grid = (m // tm, nj)
        xmap = lambda i, j: (i, 0)
        cmap = lambda i, j: (0, 0)
        gmap = lambda i, j: (group_of_tile(i), 0, 0)
        wspecs = [pl.BlockSpec((k, tn), lambda i, j: (0, j))]
        if swiglu:
            wspecs.append(pl.BlockSpec((k, tn), lambda i, j: (0, j + nj)))
        extra = []
        omap = lambda i, j: (i, j)
        out_cols = n_out
    args = [x, nw.reshape(1, k), shift, scale] + [w] * len(wspecs) + ([gates] if gated else [])
    return pl.pallas_call(
        functools.partial(_modnorm_mm_kernel, swiglu=swiglu, gated=gated),
        out_shape=jax.ShapeDtypeStruct((m, out_cols), out_dtype),
        grid=grid,
        in_specs=[pl.BlockSpec((tm, k), xmap), pl.BlockSpec((1, k), cmap),
                  pl.BlockSpec((1, 1, k), gmap), pl.BlockSpec((1, 1, k), gmap)] + wspecs + extra,
        out_specs=pl.BlockSpec((tm, tn), omap),
        scratch_shapes=[pltpu.VMEM((tm, k), BF16)],
        compiler_params=_cparams(len(grid)),
        name="modnorm_mm",
    )(*args)


def _mm_kernel(*refs, residual, nk):
    if residual:
        a_ref, w_ref, res_ref, gate_ref, o_ref, acc_ref = refs
    else:
        a_ref, w_ref, o_ref, acc_ref = refs
    kk = pl.program_id(2)
    part = jnp.dot(a_ref[...], w_ref[...], preferred_element_type=F32)

    def finish(acc):
        if residual:
            acc = res_ref[...] + gate_ref[0] * acc
        o_ref[...] = acc.astype(o_ref.dtype)

    if nk == 1:
        finish(part)
    else:
        @pl.when(kk == 0)
        def _():
            acc_ref[...] = part

        @pl.when(jnp.logical_and(kk > 0, kk < nk - 1))
        def _():
            acc_ref[...] += part

        @pl.when(kk == nk - 1)
        def _():
            finish(acc_ref[...] + part)


def _mm(a, w, *, tm, tn, tk, out_dtype=F32, res=None, gate=None, group_of_tile=None):
    m, k = a.shape
    n = w.shape[1]
    nk = k // tk
    residual = res is not None
    in_specs = [pl.BlockSpec((tm, tk), lambda i, j, q: (i, q)), pl.BlockSpec((tk, tn), lambda i, j, q: (q, j))]
    args = [a, w]
    if residual:
        in_specs += [pl.BlockSpec((tm, tn), lambda i, j, q: (i, j)),
                     pl.BlockSpec((1, 1, tn), lambda i, j, q: (group_of_tile(i), 0, j))]
        args += [res, gate]
    return pl.pallas_call(
        functools.partial(_mm_kernel, residual=residual, nk=nk),
        out_shape=jax.ShapeDtypeStruct((m, n), out_dtype),
        grid=(m // tm, n // tn, nk),
        in_specs=in_specs,
        out_specs=pl.BlockSpec((tm, tn), lambda i, j, q: (i, j)),
        scratch_shapes=[pltpu.VMEM((tm, tn), F32)],
        compiler_params=_cparams(3),
        name="mm_res" if residual else "mm",
    )(*args)


def _rope(x, cos, sin_signed):
    lane = lax.broadcasted_iota(jnp.int32, x.shape, 1)
    half = RET_HD // 4
    swapped = jnp.where(_mod(lane, 2 * half) < half, pltpu.roll(x, LANES - half, 1), pltpu.roll(x, half, 1))
    return x * cos + swapped * sin_signed


def _ret_direction(q, k, v, s_ref, log_g, backward):
    c = RET_CHUNK
    row = lax.broadcasted_iota(jnp.int32, (c, c), 0)
    col = lax.broadcasted_iota(jnp.int32, (c, c), 1)
    if backward:
        dist = (col - row).astype(F32)
        q_steps = (c - row).astype(F32)
        k_steps = row.astype(F32)
    else:
        dist = (row - col).astype(F32)
        q_steps = (row + 1).astype(F32)
        k_steps = (c - 1 - row).astype(F32)
    dmat = jnp.where(dist >= 0, jnp.exp(log_g * jnp.maximum(dist, 0.0)), 0.0)
    q_dec = jnp.exp(log_g * q_steps)
    k_dec = jnp.exp(log_g * k_steps)
    s_prev = s_ref[...]
    scores = _dot_nt(q, k) * dmat
    out = _dot(scores, v) + _dot(q * q_dec, s_prev)
    s_ref[...] = s_prev * jnp.exp(log_g * float(c)) + _dot_tn(k * k_dec, v)
    return out


def _retention_kernel(qf_ref, kf_ref, vf_ref, cf_ref, sf_ref, qb_ref, kb_ref, vb_ref, cb_ref, sb_ref,
                      decf_ref, decb_ref, of_ref, ob_ref, stf_ref, stb_ref):
    @pl.when(pl.program_id(2) == 0)
    def _():
        stf_ref[...] = jnp.zeros_like(stf_ref)
        stb_ref[...] = jnp.zeros_like(stb_ref)

    kscale = RET_HD ** -0.5
    lgf = -jnp.exp(decf_ref[0])
    lgb = -jnp.exp(decb_ref[0])
    qf = _rope(qf_ref[...], cf_ref[...], sf_ref[...])
    kf = _rope(kf_ref[...], cf_ref[...], sf_ref[...]) * kscale
    of_ref[...] = _ret_direction(qf, kf, vf_ref[...], stf_ref, lgf, False)
    qb = _rope(qb_ref[...], cb_ref[...], sb_ref[...])
    kb = _rope(kb_ref[...], cb_ref[...], sb_ref[...]) * kscale
    ob_ref[...] = _ret_direction(qb, kb, vb_ref[...], stb_ref, lgb, True)


def _retention(proj, cos, sin_signed, dec_f, dec_b, rows):
    c, h = RET_CHUNK, RET_HEADS
    n = (rows.seq + rows.ctx_len) // c
    blk_f, pos_f = rows.chunk_block(c, False)
    blk_b, pos_b = rows.chunk_block(c, True)

    def tok(blk, off):
        return pl.BlockSpec((c, RET_HD), lambda bi, hi, t: (blk(bi, t), off * h + hi))

    def tab(pos):
        return pl.BlockSpec((c, RET_HD), lambda bi, hi, t: (pos(t), 0))

    dec = lambda a: jnp.broadcast_to(a.astype(F32)[:, None, None], (h, 1, LANES))
    dspec = pl.BlockSpec((1, 1, LANES), lambda bi, hi, t: (hi, 0, 0))
    return pl.pallas_call(
        _retention_kernel,
        out_shape=[jax.ShapeDtypeStruct((rows.m, RET_W), F32)] * 2,
        grid=(rows.batch, h, n),
        in_specs=[tok(blk_f, 0), tok(blk_f, 1), tok(blk_f, 2), tab(pos_f), tab(pos_f),
                  tok(blk_b, 0), tok(blk_b, 1), tok(blk_b, 2), tab(pos_b), tab(pos_b), dspec, dspec],
        out_specs=[tok(blk_f, 0), tok(blk_b, 0)],
        scratch_shapes=[pltpu.VMEM((RET_HD, RET_HD), F32)] * 2,
        compiler_params=_cparams(3),
        name="retention",
    )(proj, proj, proj, cos, sin_signed, proj, proj, proj, cos, sin_signed, dec(dec_f), dec(dec_b))


def _na_qk_norm(x, w):
    ms = _group_mean(x * x, NA_HD)
    return x * lax.rsqrt(ms + NORM_EPS) * w


def _softmax_pv(parts):
    m = functools.reduce(jnp.maximum, [jnp.max(s, axis=-1, keepdims=True) for s, _ in parts])
    ps = [jnp.exp(s - m) for s, _ in parts]
    den = functools.reduce(jnp.add, [jnp.sum(p, axis=-1, keepdims=True) for p in ps])
    num = functools.reduce(jnp.add, [_dot(p, v) for p, (_, v) in zip(ps, parts)])
    return num / den


def _na_kernel(q_ref, k_ref, v_ref, kc_ref, vc_ref, qn_ref, kn_ref, bias_ref, o_ref, qs_ref, ks_ref, kcs_ref,
               *, rows):
    scale = NA_HD ** -0.5
    qs_ref[...] = (_na_qk_norm(q_ref[...], qn_ref[...]) * scale).astype(BF16)
    ks_ref[...] = _na_qk_norm(k_ref[...], kn_ref[...]).astype(BF16)
    kcs_ref[...] = _na_qk_norm(kc_ref[...], kn_ref[...]).astype(BF16)
    lane = lax.broadcasted_iota(jnp.int32, (GRID_W, LANES), 1)
    first = lane < NA_HD
    vc = vc_ref[...]
    kc = kcs_ref[...]

    def body(r, carry):
        r0 = jnp.clip(r - NA_KR // 2, 0, rows - NA_KR)
        cls = r0 - r + (NA_KR - 1)
        q_r = qs_ref[pl.ds(pl.multiple_of(r * GRID_W, GRID_W), GRID_W), :]
        win = pl.ds(pl.multiple_of(r0 * GRID_W, GRID_W), NA_KR * GRID_W)
        k_w = ks_ref[win, :]
        v_w = v_ref[win, :]
        outs = []
        for hh in range(2):
            qh = jnp.where(first if hh == 0 else jnp.logical_not(first), q_r, jnp.zeros_like(q_r))
            s_w = _dot_nt(qh, k_w) + bias_ref[hh, cls]
            s_c = _dot_nt(qh, kc)
            outs.append(_softmax_pv([(s_w, v_w), (s_c, vc)]))
        o_ref[pl.ds(pl.multiple_of(r * GRID_W, GRID_W), GRID_W), :] = jnp.where(first, outs[0], outs[1])
        return carry

    lax.fori_loop(0, rows, body, 0)


def _na_bias_table(rpb):
    cols = np.arange(GRID_W)
    start = np.clip(cols - NA_KC // 2, 0, GRID_W - NA_KC)
    kcol = np.arange(GRID_W)
    inside = (kcol[None, :] >= start[:, None]) & (kcol[None, :] < start[:, None] + NA_KC)
    col_off = np.clip(kcol[None, :] - cols[:, None] + (NA_KC - 1), 0, 2 * NA_KC - 2)
    row_off = np.arange(NA_KR)[:, None] + np.arange(NA_KR)[None, :]
    tab = rpb[:, row_off][:, :, :, col_off]
    tab = jnp.where(inside[None, None, None], tab, NEG_BIG)
    tab = jnp.transpose(tab, (0, 1, 3, 2, 4))
    return tab.reshape(rpb.shape[0], NA_KR, GRID_W, NA_KR * GRID_W).astype(F32)


def _na_latent(proj, qn_w, kn_w, bias, rows, col0):
    seq, ctx_len = rows.seq, rows.ctx_len
    grid_rows = seq // GRID_W
    pairs = NA_W // LANES
    cb = col0 // LANES
    cblk = rows.ml // ctx_len
    tile2 = lambda a: jnp.tile(a.astype(F32), 2).reshape(1, LANES)
    lat = lambda off: pl.BlockSpec((seq, LANES), lambda bi, p: (bi, cb + off * pairs + p))
    ctx = lambda off: pl.BlockSpec((ctx_len, LANES), lambda bi, p: (cblk + bi, cb + off * pairs + p))
    return pl.pallas_call(
        functools.partial(_na_kernel, rows=grid_rows),
        out_shape=jax.ShapeDtypeStruct((rows.ml, NA_W), F32),
        grid=(rows.batch, pairs),
        in_specs=[lat(0), lat(1), lat(2), ctx(1), ctx(2),
                  pl.BlockSpec((1, LANES), lambda bi, p: (0, 0)), pl.BlockSpec((1, LANES), lambda bi, p: (0, 0)),
                  pl.BlockSpec((2, NA_KR, GRID_W, NA_KR * GRID_W), lambda bi, p: (p, 0, 0, 0))],
        out_specs=pl.BlockSpec((seq, LANES), lambda bi, p: (bi, p)),
        scratch_shapes=[pltpu.VMEM((seq, LANES), BF16), pltpu.VMEM((seq, LANES), BF16),
                        pltpu.VMEM((ctx_len, LANES), BF16)],
        compiler_params=_cparams(2),
        name="na_latent",
    )(proj, proj, proj, proj, proj, tile2(qn_w), tile2(kn_w), bias)


def _ctx_attn_kernel(q_ref, k_ref, v_ref, qn_ref, kn_ref, o_ref):
    scale = NA_HD ** -0.5
    q = (_na_qk_norm(q_ref[...], qn_ref[...]) * scale).astype(BF16)
    k = _na_qk_norm(k_ref[...], kn_ref[...]).astype(BF16)
    v = v_ref[...]
    lane = lax.broadcasted_iota(jnp.int32, q.shape, 1)
    first = lane < NA_HD
    outs = []
    for hh in range(2):
        qh = jnp.where(first if hh == 0 else jnp.logical_not(first), q, jnp.zeros_like(q))
        outs.append(_softmax_pv([(_dot_nt(qh, k), v)]))
    o_ref[...] = jnp.where(first, outs[0], outs[1])


def _ctx_attention(proj, qn_w, kn_w, rows, col0):
    ctx_len = rows.ctx_len
    pairs = NA_W // LANES
    cb = col0 // LANES
    cblk = rows.ml // ctx_len
    tile2 = lambda a: jnp.tile(a.astype(F32), 2).reshape(1, LANES)
    blk = lambda off: pl.BlockSpec((ctx_len, LANES), lambda bi, p: (cblk + bi, cb + off * pairs + p))
    return pl.pallas_call(
        _ctx_attn_kernel,
        out_shape=jax.ShapeDtypeStruct((rows.mc, NA_W), F32),
        grid=(rows.batch, pairs),
        in_specs=[blk(0), blk(1), blk(2),
                  pl.BlockSpec((1, LANES), lambda bi, p: (0, 0)), pl.BlockSpec((1, LANES), lambda bi, p: (0, 0))],
        out_specs=pl.BlockSpec((ctx_len, LANES), lambda bi, p: (bi, p)),
        compiler_params=_cparams(2),
        name="ctx_attention",
    )(proj, proj, proj, tile2(qn_w), tile2(kn_w))


def _ret_readout_kernel(yf_ref, yb_ref, g_ref, nal_ref, nac_ref, gnw_ref, o_ref, *, n_latent_tiles):
    y = yf_ref[...] + yb_ref[...]
    for hh in range(RET_HEADS):
        sl = slice(hh * RET_HD, (hh + 1) * RET_HD)
        yh = y[:, sl]
        yc = yh - jnp.mean(yh, axis=-1, keepdims=True)
        yn = yc * lax.rsqrt(jnp.mean(yc * yc, axis=-1, keepdims=True) + RET_GN_EPS)
        o_ref[:, sl] = (yn * gnw_ref[:, sl] * _silu(g_ref[:, sl])).astype(o_ref.dtype)
    is_latent = pl.program_id(0) < n_latent_tiles
    o_ref[:, RET_W:] = jnp.where(is_latent, nal_ref[...], nac_ref[...]).astype(o_ref.dtype)


def _ret_readout(yf, yb, proj, na_l, na_c, gn_w, rows, tm):
    nl = rows.ml // tm
    row = lambda i: (i, 0)
    return pl.pallas_call(
        functools.partial(_ret_readout_kernel, n_latent_tiles=nl),
        out_shape=jax.ShapeDtypeStruct((rows.m, RET_W + NA_W), BF16),
        grid=(rows.m // tm,),
        in_specs=[pl.BlockSpec((tm, RET_W), row), pl.BlockSpec((tm, RET_W), row),
                  pl.BlockSpec((tm, RET_W), lambda i: (i, 3)),
                  pl.BlockSpec((tm, NA_W), lambda i: (jnp.minimum(i, nl - 1), 0)),
                  pl.BlockSpec((tm, NA_W), lambda i: (jnp.maximum(i - nl, 0), 0)),
                  pl.BlockSpec((1, RET_W), lambda i: (0, 0))],
        out_specs=pl.BlockSpec((tm, RET_W + NA_W), row),
        compiler_params=_cparams(1),
        name="ret_readout",
    )(yf, yb, proj, na_l, na_c, gn_w.reshape(1, RET_W).astype(F32))


def _rwkv_mix_kernel(x_ref, xp_ref, xn_ref, nw_ref, sh_ref, sc_ref, mu_ref, *o_refs, seq, ctx_len, ml, tm):
    nw, sh, sc = nw_ref[...], sh_ref[0], sc_ref[0]
    h = _modnorm(x_ref[...], nw, sh, sc)
    hp = _modnorm(xp_ref[...], nw, sh, sc)[7:8]
    hn = _modnorm(xn_ref[...], nw, sh, sc)[0:1]
    row = lax.broadcasted_iota(jnp.int32, h.shape, 0)
    g = pl.program_id(0) * tm + row
    latent = g < ml
    pos = jnp.where(latent, _mod(g, seq), _mod(g, ctx_len))
    first = pos == 0
    last = pos == jnp.where(latent, seq - 1, ctx_len - 1)
    up = jnp.where(row == 0, hp, pltpu.roll(h, 1, 0))
    up = jnp.where(first, 0.0, up)
    dn = jnp.where(row == tm - 1, hn, pltpu.roll(h, tm - 1, 0))
    dn = jnp.where(last, 0.0, dn)
    xx = 0.5 * (up + dn) - h
    for s_i, o_ref in enumerate(o_refs):
        o_ref[...] = (h + xx * mu_ref[s_i:s_i + 1, :]).astype(o_ref.dtype)


def _rwkv_mix(x, nw, shift, scale, mu, *, rows, tm):
    m, d = x.shape
    n_slab = m // 8
    per = tm // 8
    gmap = lambda i: (rows.group_of_tile(tm)(i), 0, 0)
    return pl.pallas_call(
        functools.partial(_rwkv_mix_kernel, seq=rows.seq, ctx_len=rows.ctx_len, ml=rows.ml, tm=tm),
        out_shape=[jax.ShapeDtypeStruct((m, d), BF16)] * 6,
        grid=(m // tm,),
        in_specs=[pl.BlockSpec((tm, d), lambda i: (i, 0)),
                  pl.BlockSpec((8, d), lambda i: (jnp.maximum(i * per - 1, 0), 0)),
                  pl.BlockSpec((8, d), lambda i: (jnp.minimum((i + 1) * per, n_slab - 1), 0)),
                  pl.BlockSpec((1, d), lambda i: (0, 0)),
                  pl.BlockSpec((1, 1, d), gmap), pl.BlockSpec((1, 1, d), gmap),
                  pl.BlockSpec((6, d), lambda i: (0, 0))],
        out_specs=[pl.BlockSpec((tm, d), lambda i: (i, 0))] * 6,
        compiler_params=_cparams(1),
        name="rwkv_mix",
    )(x, x, x, nw.reshape(1, d), shift, scale, mu)


def _rwkv_post_kernel(r_ref, k_ref, v_ref, xw_ref, xa_ref, xg_ref, w1_ref, w2_ref, a1_ref, a2_ref, g1_ref, g2_ref,
                      w0_ref, a0_ref, kk_ref, ka_ref, rk_ref,
                      rt0, kt0, bt0, at0, wc0, rt1, kt1, bt1, at1, wc1, vb_ref, g_ref, bonus_ref, *, tm):
    c = RWKV_CHUNK
    n_pairs = r_ref.shape[1] // LANES
    r, k, v = r_ref[...], k_ref[...], v_ref[...]
    lw_low = jnp.tanh(jnp.dot(xw_ref[...], w1_ref[...], preferred_element_type=F32))
    a_low = jnp.dot(xa_ref[...], a1_ref[...], preferred_element_type=F32)
    g_low = _sigmoid(jnp.dot(xg_ref[...], g1_ref[...], preferred_element_type=F32))
    g_ref[...] = _dot(g_low, g2_ref[...]).astype(g_ref.dtype)

    def put(o_ref, val):
        for p in range(n_pairs):
            o_ref[p] = val[:, p * LANES:(p + 1) * LANES].astype(o_ref.dtype)

    put(vb_ref, v)
    kk = k * kk_ref[...]
    nrm = jnp.sqrt(_group_mean(kk * kk, RWKV_HD) * float(RWKV_HD))
    kk = kk / jnp.maximum(nrm, 1e-12)

    tr = lax.broadcasted_iota(jnp.int32, (c, c), 0)
    tc = lax.broadcasted_iota(jnp.int32, (c, c), 1)
    tri = [(tr >= tc).astype(BF16), (tr <= tc).astype(BF16)]
    lora = w2_ref.shape[1]
    coeff_src = jnp.zeros_like(r)
    outs = [(rt0, kt0, bt0, at0, wc0), (rt1, kt1, bt1, at1, wc1)]
    for z in range(2):
        rt_ref, kt_ref, bt_ref, at_ref, wc_ref = outs[z]
        w_lora = _dot(lw_low[:, z * lora:(z + 1) * lora], w2_ref[z])
        log_w = -_softplus(-(w0_ref[z:z + 1, :] + w_lora)) - 0.5
        lw = -jnp.exp(log_w)
        a_z = _sigmoid(a0_ref[z:z + 1, :] + _dot(a_low[:, z * lora:(z + 1) * lora], a2_ref[z]))
        k_dir = k * (1.0 + (a_z - 1.0) * ka_ref[...])
        coeff_src = coeff_src + r * k_dir * rk_ref[...]
        cum = jnp.concatenate([_dot_exact_lhs(tri[z], lw[ci * c:(ci + 1) * c]) for ci in range(tm // c)], axis=0)
        e_pos = jnp.exp(cum)
        e_neg = jnp.exp(-cum)
        put(rt_ref, r * e_pos)
        put(kt_ref, k_dir * e_neg)
        put(bt_ref, kk * a_z * e_neg)
        put(at_ref, -kk * jnp.exp(cum - lw))
        for ci in range(tm // c):
            end = ci * c + (c - 1 if z == 0 else 0)
            for p in range(n_pairs):
                wc_ref[p, ci] = e_pos[end:end + 1, p * LANES:(p + 1) * LANES]
    coeff = _group_mean(coeff_src, RWKV_HD) * float(RWKV_HD)
    bonus_ref[...] = coeff * v


def _rwkv_post(r, k, v, xw, xa, xg, p, *, tm):
    m, d = r.shape
    c = RWKV_CHUNK
    lora = p["w2"].shape[1]
    glora = p["g1"].shape[1]
    n_pairs = d // LANES
    pair = jax.ShapeDtypeStruct((n_pairs, m, LANES), BF16)
    chunk = jax.ShapeDtypeStruct((n_pairs, m // c, 1, LANES), F32)
    row = lambda i: (i, 0)
    full2 = lambda i: (0, 0)
    full3 = lambda i: (0, 0, 0)
    tspec = pl.BlockSpec((tm, d), row)
    pspec = pl.BlockSpec((n_pairs, tm, LANES), lambda i: (0, i, 0))
    cspec = pl.BlockSpec((n_pairs, tm // c, 1, LANES), lambda i: (0, i, 0, 0))
    return pl.pallas_call(
        functools.partial(_rwkv_post_kernel, tm=tm),
        out_shape=[pair] * 4 + [chunk] + [pair] * 4 + [chunk]
        + [pair, jax.ShapeDtypeStruct((m, d), BF16), jax.ShapeDtypeStruct((m, d), F32)],
        grid=(m // tm,),
        in_specs=[tspec] * 6
        + [pl.BlockSpec((d, 2 * lora), full2), pl.BlockSpec((2, lora, d), full3),
           pl.BlockSpec((d, 2 * lora), full2), pl.BlockSpec((2, lora, d), full3),
           pl.BlockSpec((d, glora), full2), pl.BlockSpec((glora, d), full2),
           pl.BlockSpec((2, d), full2), pl.BlockSpec((2, d), full2)]
        + [pl.BlockSpec((1, d), full2)] * 3,
        out_specs=[pspec] * 4 + [cspec] + [pspec] * 4 + [cspec] + [pspec, tspec, tspec],
        compiler_params=_cparams(1),
        name="rwkv_post",
    )(r, k, v, xw, xa, xg, p["w1"], p["w2"], p["a1"], p["a2"], p["g1"], p["g2"], p["w0"], p["a0"],
      p["k_k"], p["k_a"], p["r_k"])


def _bdot(a, b):
    return jnp.einsum("ucd,ude->uce", a.astype(BF16), b.astype(BF16), preferred_element_type=F32)


def _bdot_nt(a, b):
    return jnp.einsum("ucd,usd->ucs", a.astype(BF16), b.astype(BF16), preferred_element_type=F32)


def _bdot_tn(a, b):
    return jnp.einsum("uce,ucd->ued", a.astype(BF16), b.astype(BF16), preferred_element_type=F32)


def _scan_chunks(rt, kt, bt, at, v, wc, s0, n_forward):
    c = RWKV_CHUNK
    n_units = rt.shape[0]
    lane = lax.broadcasted_iota(jnp.int32, (n_units, c, LANES), 2)
    row = lax.broadcasted_iota(jnp.int32, (n_units, c, LANES), 1)
    unit = lax.broadcasted_iota(jnp.int32, (n_units, c, LANES), 0)
    head0 = lane < RWKV_HD
    src = _mod(lane, RWKV_HD)
    ahead = jnp.where(unit < n_forward, row - src, src - row)
    strict = ahead > 0
    incl = ahead >= 0

    def dup(x):
        zero = jnp.zeros_like(x)
        h0 = head0[:, :x.shape[1]]
        return jnp.concatenate([jnp.where(h0, x, zero), jnp.where(h0, zero, x)], axis=1)

    ar = jnp.concatenate([at, rt], axis=1)
    mb = _bdot_nt(ar, dup(bt))
    mk = _bdot_nt(ar, dup(kt))
    p_ab = jnp.where(strict, mb[:, :c], 0.0)
    p_rb = jnp.where(incl, mb[:, c:], 0.0)
    p_ak = jnp.where(strict, mk[:, :c], 0.0)
    p_rk = jnp.where(incl, mk[:, c:], 0.0)
    vd = dup(v)
    rhs = _bdot_nt(at, s0) + _bdot(p_ak, vd)

    powers = [p_ab.astype(BF16)]
    for _ in range(int(np.log2(c)) - 1):
        powers.append(_bdot(powers[-1], dup(powers[-1])).astype(BF16))

    def apply_inverse(x):
        for pw in powers:
            x = x + _bdot(pw, dup(x))
        return x

    u = apply_inverse(rhs)
    ud = dup(u)
    ph, pl_ = _split2(p_ab)
    uh, ul = _split2(ud)
    resid = (rhs - u) + (_bdot(ph, uh) + (_bdot(ph, ul) + _bdot(pl_, uh)))
    u = u + apply_inverse(resid)

    y = _bdot_nt(rt, s0) + _bdot(jnp.concatenate([p_rb, p_rk], axis=2), jnp.concatenate([dup(u), vd], axis=1))
    upd = _bdot_tn(jnp.concatenate([u.astype(BF16), v], axis=1), jnp.concatenate([bt, kt], axis=1))
    er = lax.broadcasted_iota(jnp.int32, (1, LANES, LANES), 1) < RWKV_HD
    ec = lax.broadcasted_iota(jnp.int32, (1, LANES, LANES), 2) < RWKV_HD
    s1 = (s0 + jnp.where(er == ec, upd, 0.0)) * wc
    return y, s1


def _rwkv_scan_kernel(rtf, ktf, btf, atf, vf, wcf, rtb, ktb, btb, atb, vb, wcb, yf_ref, yb_ref, s_ref):
    @pl.when(pl.program_id(1) == 0)
    def _():
        s_ref[...] = jnp.zeros_like(s_ref)

    n_pairs = rtf.shape[0]
    both = lambda f, b: jnp.concatenate([f[...], b[...]], axis=0)
    wc = jnp.concatenate([wcf[:, 0], wcb[:, 0]], axis=0)
    y, s1 = _scan_chunks(both(rtf, rtb), both(ktf, ktb), both(btf, btb), both(atf, atb), both(vf, vb), wc,
                         s_ref[...], n_pairs)
    s_ref[...] = s1
    yf_ref[...] = y[:n_pairs]
    yb_ref[...] = y[n_pairs:]


def _rwkv_scan(feats_f, feats_b, v, rows):
    n_pairs, m, _ = v.shape
    c = RWKV_CHUNK
    d = n_pairs * LANES
    nchunks = (rows.seq + rows.ctx_len) // c
    blk_f, _ = rows.chunk_block(c, False)
    blk_b, _ = rows.chunk_block(c, True)

    def specs(blk):
        tok = pl.BlockSpec((n_pairs, c, LANES), lambda bi, t: (0, blk(bi, t), 0))
        return tok, pl.BlockSpec((n_pairs, 1, 1, LANES), lambda bi, t: (0, blk(bi, t), 0, 0))

    tok_f, wc_f = specs(blk_f)
    tok_b, wc_b = specs(blk_b)
    return pl.pallas_call(
        _rwkv_scan_kernel,
        out_shape=[jax.ShapeDtypeStruct((n_pairs, m, LANES), F32)] * 2,
        grid=(rows.batch, nchunks),
        in_specs=[tok_f] * 5 + [wc_f] + [tok_b] * 5 + [wc_b],
        out_specs=[tok_f, tok_b],
        scratch_shapes=[pltpu.VMEM((2 * n_pairs, LANES, LANES), F32)],
        compiler_params=_cparams(2),
        name="rwkv_scan",
    )(*feats_f[:4], v, feats_f[4], *feats_b[:4], v, feats_b[4])


def _rwkv_readout_kernel(yf_ref, yb_ref, bonus_ref, g_ref, lnw_ref, lnb_ref, o_ref):
    y = jnp.concatenate([yf_ref[p] + yb_ref[p] for p in range(yf_ref.shape[0])], axis=1)
    yc = y - _group_mean(y, RWKV_HD)
    yn = yc * lax.rsqrt(_group_mean(yc * yc, RWKV_HD) + RWKV_GN_EPS) * lnw_ref[...] + lnb_ref[...]
    o_ref[...] = ((yn + bonus_ref[...]) * g_ref[...].astype(F32)).astype(o_ref.dtype)


def _rwkv_readout(yf, yb, bonus, g, ln_w, ln_b, rows, tm):
    n_pairs = yf.shape[0]
    d = n_pairs * LANES
    pspec = pl.BlockSpec((n_pairs, tm, LANES), lambda i: (0, i, 0))
    tspec = pl.BlockSpec((tm, d), lambda i: (i, 0))
    vspec = pl.BlockSpec((1, d), lambda i: (0, 0))
    return pl.pallas_call(
        _rwkv_readout_kernel,
        out_shape=jax.ShapeDtypeStruct((rows.ml, d), BF16),
        grid=(rows.ml // tm,),
        in_specs=[pspec, pspec, tspec, tspec, vspec, vspec],
        out_specs=tspec,
        compiler_params=_cparams(1),
        name="rwkv_readout",
    )(yf, yb, bonus, g, ln_w.reshape(1, d).astype(F32), ln_b.reshape(1, d).astype(F32))


def _router_kernel(x_ref, nw_ref, sh_ref, sc_ref, w_ref, o_ref, *, n_experts):
    h = _modnorm(x_ref[...], nw_ref[...], sh_ref[0], sc_ref[0])
    logits = _dot_hi(h, w_ref[...])
    lane = lax.broadcasted_iota(jnp.int32, logits.shape, 1).astype(F32)
    logits = jnp.where(lane < n_experts, logits, -jnp.inf)
    v1 = jnp.max(logits, axis=-1, keepdims=True)
    i1 = jnp.min(jnp.where(logits == v1, lane, float(LANES)), axis=-1, keepdims=True)
    rest = jnp.where(lane == i1, -jnp.inf, logits)
    v2 = jnp.max(rest, axis=-1, keepdims=True)
    i2 = jnp.min(jnp.where(rest == v2, lane, float(LANES)), axis=-1, keepdims=True)
    e2 = jnp.exp(v2 - v1)
    den = 1.0 + e2
    o_ref[...] = jnp.where(lane == i1, 1.0 / den, 0.0) + jnp.where(lane == i2, e2 / den, 0.0)


def _router(x, nw, shift, scale, router, *, tm, group_of_tile):
    m, d = x.shape
    n_experts = router.shape[1]
    w = jnp.zeros((d, LANES), F32).at[:, :n_experts].set(router.astype(F32))
    gmap = lambda i: (group_of_tile(i), 0, 0)
    return pl.pallas_call(
        functools.partial(_router_kernel, n_experts=n_experts),
        out_shape=jax.ShapeDtypeStruct((m, LANES), F32),
        grid=(m // tm,),
        in_specs=[pl.BlockSpec((tm, d), lambda i: (i, 0)), pl.BlockSpec((1, d), lambda i: (0, 0)),
                  pl.BlockSpec((1, 1, d), gmap), pl.BlockSpec((1, 1, d), gmap),
                  pl.BlockSpec((d, LANES), lambda i: (0, 0))],
        out_specs=pl.BlockSpec((tm, LANES), lambda i: (i, 0)),
        compiler_params=_cparams(1),
        name="router",
    )(x, nw.reshape(1, d), shift, scale, w)


def _rope_tables(seq, ctx_len):
    half = RET_HD // 4
    freqs = ROPE_BASE ** (-np.arange(half, dtype=np.float64) / half)
    t = np.arange(seq)
    ang_r = (t // GRID_W)[:, None] * freqs[None, :]
    ang_c = (t % GRID_W)[:, None] * freqs[None, :]
    cos = np.concatenate([np.cos(ang_r)] * 2 + [np.cos(ang_c)] * 2, axis=1)
    sin = np.concatenate([-np.sin(ang_r), np.sin(ang_r), -np.sin(ang_c), np.sin(ang_c)], axis=1)
    cos = np.concatenate([np.ones((ctx_len, RET_HD)), cos], axis=0)
    sin = np.concatenate([np.zeros((ctx_len, RET_HD)), sin], axis=0)
    return jnp.asarray(cos, F32), jnp.asarray(sin, F32)


def kernel(x, c, ctx, c_ctx, ada_w, ada_b, norm_mix_w, norm_ffn_w, ev_w_in, ev_ret_decay_f, ev_ret_decay_b,
           ev_ret_gn_w, ev_na_qn_w, ev_na_kn_w, ev_na_rpb, ev_w_out, ev_ffn_w13, ev_ffn_w2, od_mu, od_w_rkv,
           od_w0, od_w1, od_w2, od_a0, od_a1, od_a2, od_g1, od_g2, od_k_k, od_k_a, od_r_k, od_ln_w, od_ln_b,
           od_w_o, od_router, od_moe_w13, od_moe_w2):
    batch, seq, d = x.shape
    ctx_len = ctx.shape[1]
    rows = _Rows(batch, seq, ctx_len)
    tm = 8 * RWKV_CHUNK
    grp = rows.group_of_tile(tm)
    xa = jnp.concatenate([x.reshape(rows.ml, d), ctx.reshape(rows.mc, d)], axis=0)

    n_mod = ((batch + 1 + 7) // 8) * 8
    c_rows = jnp.zeros((n_mod, d), F32).at[:batch].set(c).at[batch].set(c_ctx)
    mods = _ada(c_rows, ada_w, ada_b)
    mods = mods.reshape(mods.shape[0], n_mod, 6, 1, d).transpose(0, 2, 1, 3, 4)

    mod = mods[0]
    proj = _modnorm_mm(xa, norm_mix_w[0], mod[0], mod[1], ev_w_in[0].astype(BF16), group_of_tile=grp, tm=tm, tn=512,
                       out_dtype=F32)
    cos, sin = _rope_tables(seq, ctx_len)
    y_f, y_b = _retention(proj, cos, sin, ev_ret_decay_f[0], ev_ret_decay_b[0], rows)
    bias = _na_bias_table(ev_na_rpb[0].astype(F32))
    na_l = _na_latent(proj, ev_na_qn_w[0], ev_na_kn_w[0], bias, rows, 4 * RET_W)
    na_c = _ctx_attention(proj, ev_na_qn_w[0], ev_na_kn_w[0], rows, 4 * RET_W)
    mix = _ret_readout(y_f, y_b, proj, na_l, na_c, ev_ret_gn_w[0], rows, tm)
    w_out = ev_w_out[0].astype(BF16)
    xa = _mm(mix, w_out, tm=tm, tn=512, tk=w_out.shape[0], res=xa, gate=mod[2], group_of_tile=grp)

    w13 = ev_ffn_w13[0].astype(BF16)
    w2 = ev_ffn_w2[0].astype(BF16)
    hid = w2.shape[0]
    tn13 = 256 if hid % 256 == 0 else 128
    hidden = _modnorm_mm(xa, norm_ffn_w[0], mod[3], mod[4], w13, group_of_tile=grp, tm=tm, tn=tn13, out_dtype=BF16,
                         swiglu=True)
    xa = _mm(hidden, w2, tm=tm, tn=512, tk=hid, res=xa, gate=mod[5], group_of_tile=grp)

    mod = mods[1]
    glora = od_g1.shape[-1]
    glora_p = ((glora + LANES - 1) // LANES) * LANES
    p = {
        "w1": jnp.concatenate([od_w1[0, 0], od_w1[0, 1]], axis=1).astype(BF16),
        "w2": od_w2[0].astype(BF16),
        "a1": jnp.concatenate([od_a1[0, 0], od_a1[0, 1]], axis=1).astype(BF16),
        "a2": od_a2[0].astype(BF16),
        "g1": jnp.zeros((d, glora_p), BF16).at[:, :glora].set(od_g1[0].astype(BF16)),
        "g2": jnp.zeros((glora_p, d), BF16).at[:glora].set(od_g2[0].astype(BF16)),
        "w0": od_w0[0].astype(F32), "a0": od_a0[0].astype(F32),
        "k_k": od_k_k[0].reshape(1, d).astype(F32), "k_a": od_k_a[0].reshape(1, d).astype(F32),
        "r_k": od_r_k[0].reshape(1, d).astype(F32),
    }
    w_rkv = od_w_rkv[0].astype(BF16)
    xm = _rwkv_mix(xa, norm_mix_w[1], mod[0], mod[1], od_mu[0].astype(F32), rows=rows, tm=tm)
    r, k, v = [_mm(xm[s_i], w_rkv[s_i], tm=tm, tn=512, tk=d) for s_i in range(3)]
    f = _rwkv_post(r, k, v, xm[3], xm[4], xm[5], p, tm=tm)
    y_f, y_b = _rwkv_scan(f[0:5], f[5:10], f[10], rows)
    mixed = _rwkv_readout(y_f, y_b, f[12], f[11], od_ln_w[0], od_ln_b[0], rows, tm)
    x_l = _mm(mixed, od_w_o[0].astype(BF16), tm=tm, tn=512, tk=d, res=xa, gate=mod[2], group_of_tile=grp)

    gates = _router(x_l, norm_ffn_w[1], mod[3], mod[4], od_router[0], tm=tm, group_of_tile=grp)
    moe13 = od_moe_w13[0].astype(BF16)
    n_e, _, two_h = moe13.shape
    e_hid = two_h // 2
    tn_e = 512 if e_hid % 512 == 0 else 256
    hid_all = _modnorm_mm(x_l, norm_ffn_w[1], mod[3], mod[4], moe13, group_of_tile=grp, tm=tm, tn=tn_e,
                          out_dtype=BF16, swiglu=True, gates=gates)
    moe2 = od_moe_w2[0].astype(BF16).reshape(n_e * e_hid, d)
    tk2 = e_hid // 2 if (e_hid // 2) % LANES == 0 else e_hid
    x_l = _mm(hid_all, moe2, tm=tm, tn=512, tk=tk2, res=x_l, gate=mod[5], group_of_tile=grp)
    return x_l.reshape(batch, seq, d)
```

```python
import functools

import jax
import jax.numpy as jnp
import numpy as np
from jax import lax
from jax.experimental import pallas as pl
from jax.experimental.pallas import tpu as pltpu

F32 = jnp.float32
BF16 = jnp.bfloat16

LANES = 128
GRID_W = 64
RET_HEADS = 4
RET_HD = 128
RET_W = RET_HEADS * RET_HD
RET_CHUNK = 128
RET_GN_EPS = 1e-5
NA_HEADS = 8
NA_HD = 64
NA_W = NA_HEADS * NA_HD
NA_KR = 8
NA_KC = 16
RWKV_HD = 64
RWKV_GN_EPS = 64e-5
RWKV_CHUNK = 64
TOP_K = 2
ROPE_BASE = 10000.0
NORM_EPS = 1e-6
NEG_BIG = -1e30
VMEM_LIMIT = 56 * 1024 * 1024


def _cparams(n_axes):
    return pltpu.CompilerParams(dimension_semantics=("arbitrary",) * n_axes, vmem_limit_bytes=VMEM_LIMIT)


def _dot(a, b):
    return jnp.dot(a.astype(BF16), b.astype(BF16), preferred_element_type=F32)


def _dot_nt(a, b):
    return lax.dot_general(a.astype(BF16), b.astype(BF16), (((1,), (1,)), ((), ())), preferred_element_type=F32)


def _dot_tn(a, b):
    return lax.dot_general(a.astype(BF16), b.astype(BF16), (((0,), (0,)), ((), ())), preferred_element_type=F32)


def _split2(x):
    hi = x.astype(BF16)
    return hi, (x - hi.astype(F32)).astype(BF16)


def _split3(x):
    hi = x.astype(BF16)
    r1 = x - hi.astype(F32)
    mid = r1.astype(BF16)
    lo = (r1 - mid.astype(F32)).astype(BF16)
    return hi, mid, lo


def _dot_hi(a, b):
    ah, am, al = _split3(a)
    bh, bm, bl = _split3(b)
    d = functools.partial(jnp.dot, preferred_element_type=F32)
    return (d(ah, bh) + (d(ah, bm) + d(am, bh)) + (d(am, bm) + d(ah, bl) + d(al, bh)))


def _dot_x3(a, b):
    ah, al = _split2(a)
    bh, bl = _split2(b)
    d = functools.partial(jnp.dot, preferred_element_type=F32)
    return d(ah, bh) + (d(ah, bl) + d(al, bh))


def _dot_exact_rhs(a, b_bf16):
    ah, am, al = _split3(a)
    d = functools.partial(jnp.dot, preferred_element_type=F32)
    return d(ah, b_bf16) + d(am, b_bf16) + d(al, b_bf16)


def _dot_exact_lhs(a_bf16, b):
    bh, bm, bl = _split3(b)
    d = functools.partial(jnp.dot, preferred_element_type=F32)
    return d(a_bf16, bh) + d(a_bf16, bm) + d(a_bf16, bl)


def _mod(v, n):
    return (v & (n - 1)) if n & (n - 1) == 0 else v % n


def _sigmoid(x):
    return 1.0 / (1.0 + jnp.exp(-x))


def _silu(x):
    return x * _sigmoid(x)


def _softplus(x):
    return jnp.maximum(x, 0.0) + jnp.log(1.0 + jnp.exp(-jnp.abs(x)))


def _group_mean_mat(width, group):
    r = lax.broadcasted_iota(jnp.int32, (width, width), 0) // group
    c = lax.broadcasted_iota(jnp.int32, (width, width), 1) // group
    return jnp.where(r == c, 1.0 / group, 0.0).astype(BF16)


def _group_mean(x, group):
    g = _group_mean_mat(LANES, group)
    cols = [_dot_exact_rhs(x[:, c:c + LANES], g) for c in range(0, x.shape[1], LANES)]
    return cols[0] if len(cols) == 1 else jnp.concatenate(cols, axis=1)


def _modnorm(x, nw, shift, scale):
    ms = jnp.mean(x * x, axis=-1, keepdims=True)
    y = x * lax.rsqrt(ms + NORM_EPS) * nw
    return y * (1.0 + scale) + shift


class _Rows:
    def __init__(self, batch, seq, ctx_len):
        self.batch, self.seq, self.ctx_len = batch, seq, ctx_len
        self.ml, self.mc = batch * seq, batch * ctx_len
        self.m = self.ml + self.mc

    def group_of_tile(self, tm):
        return lambda i: jnp.minimum((i * tm) // self.seq, self.batch)

    def chunk_block(self, chunk, backward):
        ncc, nlc = self.ctx_len // chunk, self.seq // chunk

        def block(b, t):
            if backward:
                t = jnp.where(t < ncc, ncc - 1 - t, nlc + 2 * ncc - 1 - t)
            return jnp.where(t < ncc, (self.ml + b * self.ctx_len) // chunk + t, (b * self.seq) // chunk + t - ncc)

        def position(t):
            if backward:
                t = jnp.where(t < ncc, ncc - 1 - t, nlc + 2 * ncc - 1 - t)
            return t

        return block, position


def _ada_kernel(c_ref, w_ref, b_ref, o_ref):
    s = _silu(c_ref[...])
    o_ref[0] = _dot_hi(s, w_ref[0]) + b_ref[0]


def _ada(c_rows, ada_w, ada_b):
    depth, d, n = ada_w.shape
    rows = c_rows.shape[0]
    tn = 1536
    return pl.pallas_call(
        _ada_kernel,
        out_shape=jax.ShapeDtypeStruct((depth, rows, n), F32),
        grid=(depth, n // tn),
        in_specs=[
            pl.BlockSpec((rows, d), lambda l, j: (0, 0)),
            pl.BlockSpec((1, d, tn), lambda l, j: (l, 0, j)),
            pl.BlockSpec((1, 1, tn), lambda l, j: (l, 0, j)),
        ],
        out_specs=pl.BlockSpec((1, rows, tn), lambda l, j: (l, 0, j)),
        compiler_params=_cparams(2),
        name="ada_mod",
    )(c_rows, ada_w, ada_b.reshape(depth, 1, n))


def _modnorm_mm_kernel(*refs, swiglu, gated):
    x_ref, nw_ref, sh_ref, sc_ref = refs[:4]
    pos = 4
    w_refs = refs[pos:pos + (2 if swiglu else 1)]
    pos += len(w_refs)
    gates_ref = refs[pos] if gated else None
    pos += 1 if gated else 0
    o_ref, h_ref = refs[pos], refs[pos + 1]

    first = pl.program_id(1) == 0
    if gated:
        first = jnp.logical_and(first, pl.program_id(2) == 0)

    @pl.when(first)
    def _():
        h_ref[...] = _modnorm(x_ref[...], nw_ref[...], sh_ref[0], sc_ref[0]).astype(BF16)

    h = h_ref[...]
    w0 = w_refs[0][0] if gated else w_refs[0][...]
    acc = jnp.dot(h, w0, preferred_element_type=F32)
    if swiglu:
        w1 = w_refs[1][0] if gated else w_refs[1][...]
        acc = _silu(acc) * jnp.dot(h, w1, preferred_element_type=F32)
    if gated:
        e = pl.program_id(1)
        lane = lax.broadcasted_iota(jnp.int32, gates_ref.shape, 1)
        g = jnp.sum(jnp.where(lane == e, gates_ref[...], 0.0), axis=-1, keepdims=True)
        acc = acc * g
    o_ref[...] = acc.astype(o_ref.dtype)


def _modnorm_mm(x, nw, shift, scale, w, *, group_of_tile, tm, tn, out_dtype, swiglu=False, gates=None, rows=None):
    m = x.shape[0] if rows is None else rows
    k = x.shape[1]
    gated = gates is not None
    n_total = w.shape[-1]
    n_out = n_total // 2 if swiglu else n_total
    nj = n_out // tn
    if gated:
        n_e = w.shape[0]
        grid = (m // tm, n_e, nj)
        xmap = lambda i, e, j: (i, 0)
        cmap = lambda i, e, j: (0, 0)
        gmap = lambda i, e, j: (group_of_tile(i), 0, 0)
        wspecs = [pl.BlockSpec((1, k, tn), lambda i, e, j: (e, 0, j))]
        if swiglu:
            wspecs.append(pl.BlockSpec((1, k, tn), lambda i, e, j: (e, 0, j + nj)))
        extra = [pl.BlockSpec((tm, LANES), lambda i, e, j: (i, 0))]
        omap = lambda i, e, j: (i, e * nj + j)
        out_cols = n_e * n_out
    else:
        grid = (m // tm, nj)
        xmap = lambda i, j: (i, 0)
        cmap = lambda i, j: (0, 0)
        gmap = lambda i, j: (group_of_tile(i), 0, 0)
        wspecs = [pl.BlockSpec((k, tn), lambda i, j: (0, j))]
        if swiglu:
            wspecs.append(pl.BlockSpec((k, tn), lambda i, j: (0, j + nj)))
        extra = []
        omap = lambda i, j: (i, j)
        out_cols = n_out
    args = [x, nw.reshape(1, k), shift, scale] + [w] * len(wspecs) + ([gates] if gated else [])
    return pl.pallas_call(
        functools.partial(_modnorm_mm_kernel, swiglu=swiglu, gated=gated),
        out_shape=jax.ShapeDtypeStruct((m, out_cols), out_dtype),
        grid=grid,
        in_specs=[pl.BlockSpec((tm, k), xmap), pl.BlockSpec((1, k), cmap),
                  pl.BlockSpec((1, 1, k), gmap), pl.BlockSpec((1, 1, k), gmap)] + wspecs + extra,
        out_specs=pl.BlockSpec((tm, tn), omap),
        scratch_shapes=[pltpu.VMEM((tm, k), BF16)],
        compiler_params=_cparams(len(grid)),
        name="modnorm_mm",
    )(*args)


def _mm_kernel(*refs, residual, nk):
    if residual:
        a_ref, w_ref, res_ref, gate_ref, o_ref, acc_ref = refs
    else:
        a_ref, w_ref, o_ref, acc_ref = refs
    kk = pl.program_id(2)
    part = jnp.dot(a_ref[...], w_ref[...], preferred_element_type=F32)

    def finish(acc):
        if residual:
            acc = res_ref[...] + gate_ref[0] * acc
        o_ref[...] = acc.astype(o_ref.dtype)

    if nk == 1:
        finish(part)
    else:
        @pl.when(kk == 0)
        def _():
            acc_ref[...] = part

        @pl.when(jnp.logical_and(kk > 0, kk < nk - 1))
        def _():
            acc_ref[...] += part

        @pl.when(kk == nk - 1)
        def _():
            finish(acc_ref[...] + part)


def _mm(a, w, *, tm, tn, tk, out_dtype=F32, res=None, gate=None, group_of_tile=None):
    m, k = a.shape
    n = w.shape[1]
    nk = k // tk
    residual = res is not None
    in_specs = [pl.BlockSpec((tm, tk), lambda i, j, q: (i, q)), pl.BlockSpec((tk, tn), lambda i, j, q: (q, j))]
    args = [a, w]
    if residual:
        in_specs += [pl.BlockSpec((tm, tn), lambda i, j, q: (i, j)),
                     pl.BlockSpec((1, 1, tn), lambda i, j, q: (group_of_tile(i), 0, j))]
        args += [res, gate]
    return pl.pallas_call(
        functools.partial(_mm_kernel, residual=residual, nk=nk),
        out_shape=jax.ShapeDtypeStruct((m, n), out_dtype),
        grid=(m // tm, n // tn, nk),
        in_specs=in_specs,
        out_specs=pl.BlockSpec((tm, tn), lambda i, j, q: (i, j)),
        scratch_shapes=[pltpu.VMEM((tm, tn), F32)],
        compiler_params=_cparams(3),
        name="mm_res" if residual else "mm",
    )(*args)


def _rope(x, cos, sin_signed):
    lane = lax.broadcasted_iota(jnp.int32, x.shape, 1)
    half = RET_HD // 4
    swapped = jnp.where(_mod(lane, 2 * half) < half, pltpu.roll(x, LANES - half, 1), pltpu.roll(x, half, 1))
    return x * cos + swapped * sin_signed


def _ret_direction(q, k, v, s_ref, log_g, backward):
    c = RET_CHUNK
    row = lax.broadcasted_iota(jnp.int32, (c, c), 0)
    col = lax.broadcasted_iota(jnp.int32, (c, c), 1)
    if backward:
        dist = (col - row).astype(F32)
        q_steps = (c - row).astype(F32)
        k_steps = row.astype(F32)
    else:
        dist = (row - col).astype(F32)
        q_steps = (row + 1).astype(F32)
        k_steps = (c - 1 - row).astype(F32)
    dmat = jnp.where(dist >= 0, jnp.exp(log_g * jnp.maximum(dist, 0.0)), 0.0)
    q_dec = jnp.exp(log_g * q_steps)
    k_dec = jnp.exp(log_g * k_steps)
    s_prev = s_ref[...]
    scores = _dot_nt(q, k) * dmat
    out = _dot(scores, v) + _dot(q * q_dec, s_prev)
    s_ref[...] = s_prev * jnp.exp(log_g * float(c)) + _dot_tn(k * k_dec, v)
    return out


def _retention_kernel(qf_ref, kf_ref, vf_ref, cf_ref, sf_ref, qb_ref, kb_ref, vb_ref, cb_ref, sb_ref,
                      decf_ref, decb_ref, of_ref, ob_ref, stf_ref, stb_ref):
    @pl.when(pl.program_id(2) == 0)
    def _():
        stf_ref[...] = jnp.zeros_like(stf_ref)
        stb_ref[...] = jnp.zeros_like(stb_ref)

    kscale = RET_HD ** -0.5
    lgf = -jnp.exp(decf_ref[0])
    lgb = -jnp.exp(decb_ref[0])
    qf = _rope(qf_ref[...], cf_ref[...], sf_ref[...])
    kf = _rope(kf_ref[...], cf_ref[...], sf_ref[...]) * kscale
    of_ref[...] = _ret_direction(qf, kf, vf_ref[...], stf_ref, lgf, False)
    qb = _rope(qb_ref[...], cb_ref[...], sb_ref[...])
    kb = _rope(kb_ref[...], cb_ref[...], sb_ref[...]) * kscale
    ob_ref[...] = _ret_direction(qb, kb, vb_ref[...], stb_ref, lgb, True)


def _retention(proj, cos, sin_signed, dec_f, dec_b, rows):
    c, h = RET_CHUNK, RET_HEADS
    n = (rows.seq + rows.ctx_len) // c
    blk_f, pos_f = rows.chunk_block(c, False)
    blk_b, pos_b = rows.chunk_block(c, True)

    def tok(blk, off):
        return pl.BlockSpec((c, RET_HD), lambda bi, hi, t: (blk(bi, t), off * h + hi))

    def tab(pos):
        return pl.BlockSpec((c, RET_HD), lambda bi, hi, t: (pos(t), 0))

    dec = lambda a: jnp.broadcast_to(a.astype(F32)[:, None, None], (h, 1, LANES))
    dspec = pl.BlockSpec((1, 1, LANES), lambda bi, hi, t: (hi, 0, 0))
    return pl.pallas_call(
        _retention_kernel,
        out_shape=[jax.ShapeDtypeStruct((rows.m, RET_W), F32)] * 2,
        grid=(rows.batch, h, n),
        in_specs=[tok(blk_f, 0), tok(blk_f, 1), tok(blk_f, 2), tab(pos_f), tab(pos_f),
                  tok(blk_b, 0), tok(blk_b, 1), tok(blk_b, 2), tab(pos_b), tab(pos_b), dspec, dspec],
        out_specs=[tok(blk_f, 0), tok(blk_b, 0)],
        scratch_shapes=[pltpu.VMEM((RET_HD, RET_HD), F32)] * 2,
        compiler_params=_cparams(3),
        name="retention",
    )(proj, proj, proj, cos, sin_signed, proj, proj, proj, cos, sin_signed, dec(dec_f), dec(dec_b))


def _na_qk_norm(x, w):
    ms = _group_mean(x * x, NA_HD)
    return x * lax.rsqrt(ms + NORM_EPS) * w


def _softmax_pv(parts):
    m = functools.reduce(jnp.maximum, [jnp.max(s, axis=-1, keepdims=True) for s, _ in parts])
    ps = [jnp.exp(s - m) for s, _ in parts]
    den = functools.reduce(jnp.add, [jnp.sum(p, axis=-1, keepdims=True) for p in ps])
    num = functools.reduce(jnp.add, [_dot(p, v) for p, (_, v) in zip(ps, parts)])
    return num / den


def _na_kernel(q_ref, k_ref, v_ref, kc_ref, vc_ref, qn_ref, kn_ref, bias_ref, o_ref, qs_ref, ks_ref, kcs_ref,
               *, rows):
    scale = NA_HD ** -0.5
    qs_ref[...] = (_na_qk_norm(q_ref[...], qn_ref[...]) * scale).astype(BF16)
    ks_ref[...] = _na_qk_norm(k_ref[...], kn_ref[...]).astype(BF16)
    kcs_ref[...] = _na_qk_norm(kc_ref[...], kn_ref[...]).astype(BF16)
    lane = lax.broadcasted_iota(jnp.int32, (GRID_W, LANES), 1)
    first = lane < NA_HD
    vc = vc_ref[...]
    kc = kcs_ref[...]

    def body(r, carry):
        r0 = jnp.clip(r - NA_KR // 2, 0, rows - NA_KR)
        cls = r0 - r + (NA_KR - 1)
        q_r = qs_ref[pl.ds(pl.multiple_of(r * GRID_W, GRID_W), GRID_W), :]
        win = pl.ds(pl.multiple_of(r0 * GRID_W, GRID_W), NA_KR * GRID_W)
        k_w = ks_ref[win, :]
        v_w = v_ref[win, :]
        outs = []
        for hh in range(2):
            qh = jnp.where(first if hh == 0 else jnp.logical_not(first), q_r, jnp.zeros_like(q_r))
            s_w = _dot_nt(qh, k_w) + bias_ref[hh, cls]
            s_c = _dot_nt(qh, kc)
            outs.append(_softmax_pv([(s_w, v_w), (s_c, vc)]))
        o_ref[pl.ds(pl.multiple_of(r * GRID_W, GRID_W), GRID_W), :] = jnp.where(first, outs[0], outs[1])
        return carry

    lax.fori_loop(0, rows, body, 0)


def _na_bias_table(rpb):
    cols = np.arange(GRID_W)
    start = np.clip(cols - NA_KC // 2, 0, GRID_W - NA_KC)
    kcol = np.arange(GRID_W)
    inside = (kcol[None, :] >= start[:, None]) & (kcol[None, :] < start[:, None] + NA_KC)
    col_off = np.clip(kcol[None, :] - cols[:, None] + (NA_KC - 1), 0, 2 * NA_KC - 2)
    row_off = np.arange(NA_KR)[:, None] + np.arange(NA_KR)[None, :]
    tab = rpb[:, row_off][:, :, :, col_off]
    tab = jnp.where(inside[None, None, None], tab, NEG_BIG)
    tab = jnp.transpose(tab, (0, 1, 3, 2, 4))
    return tab.reshape(rpb.shape[0], NA_KR, GRID_W, NA_KR * GRID_W).astype(F32)


def _na_latent(proj, qn_w, kn_w, bias, rows, col0):
    seq, ctx_len = rows.seq, rows.ctx_len
    grid_rows = seq // GRID_W
    pairs = NA_W // LANES
    cb = col0 // LANES
    cblk = rows.ml // ctx_len
    tile2 = lambda a: jnp.tile(a.astype(F32), 2).reshape(1, LANES)
    lat = lambda off: pl.BlockSpec((seq, LANES), lambda bi, p: (bi, cb + off * pairs + p))
    ctx = lambda off: pl.BlockSpec((ctx_len, LANES), lambda bi, p: (cblk + bi, cb + off * pairs + p))
    return pl.pallas_call(
        functools.partial(_na_kernel, rows=grid_rows),
        out_shape=jax.ShapeDtypeStruct((rows.ml, NA_W), F32),
        grid=(rows.batch, pairs),
        in_specs=[lat(0), lat(1), lat(2), ctx(1), ctx(2),
                  pl.BlockSpec((1, LANES), lambda bi, p: (0, 0)), pl.BlockSpec((1, LANES), lambda bi, p: (0, 0)),
                  pl.BlockSpec((2, NA_KR, GRID_W, NA_KR * GRID_W), lambda bi, p: (p, 0, 0, 0))],
        out_specs=pl.BlockSpec((seq, LANES), lambda bi, p: (bi, p)),
        scratch_shapes=[pltpu.VMEM((seq, LANES), BF16), pltpu.VMEM((seq, LANES), BF16),
                        pltpu.VMEM((ctx_len, LANES), BF16)],
        compiler_params=_cparams(2),
        name="na_latent",
    )(proj, proj, proj, proj, proj, tile2(qn_w), tile2(kn_w), bias)


def _ctx_attn_kernel(q_ref, k_ref, v_ref, qn_ref, kn_ref, o_ref):
    scale = NA_HD ** -0.5
    q = (_na_qk_norm(q_ref[...], qn_ref[...]) * scale).astype(BF16)
    k = _na_qk_norm(k_ref[...], kn_ref[...]).astype(BF16)
    v = v_ref[...]
    lane = lax.broadcasted_iota(jnp.int32, q.shape, 1)
    first = lane < NA_HD
    outs = []
    for hh in range(2):
        qh = jnp.where(first if hh == 0 else jnp.logical_not(first), q, jnp.zeros_like(q))
        outs.append(_softmax_pv([(_dot_nt(qh, k), v)]))
    o_ref[...] = jnp.where(first, outs[0], outs[1])


def _ctx_attention(proj, qn_w, kn_w, rows, col0):
    ctx_len = rows.ctx_len
    pairs = NA_W // LANES
    cb = col0 // LANES
    cblk = rows.ml // ctx_len
    tile2 = lambda a: jnp.tile(a.astype(F32), 2).reshape(1, LANES)
    blk = lambda off: pl.BlockSpec((ctx_len, LANES), lambda bi, p: (cblk + bi, cb + off * pairs + p))
    return pl.pallas_call(
        _ctx_attn_kernel,
        out_shape=jax.ShapeDtypeStruct((rows.mc, NA_W), F32),
        grid=(rows.batch, pairs),
        in_specs=[blk(0), blk(1), blk(2),
                  pl.BlockSpec((1, LANES), lambda bi, p: (0, 0)), pl.BlockSpec((1, LANES), lambda bi, p: (0, 0))],
        out_specs=pl.BlockSpec((ctx_len, LANES), lambda bi, p: (bi, p)),
        compiler_params=_cparams(2),
        name="ctx_attention",
    )(proj, proj, proj, tile2(qn_w), tile2(kn_w))


def _ret_readout_kernel(yf_ref, yb_ref, g_ref, nal_ref, nac_ref, gnw_ref, o_ref, *, n_latent_tiles):
    y = yf_ref[...] + yb_ref[...]
    for hh in range(RET_HEADS):
        sl = slice(hh * RET_HD, (hh + 1) * RET_HD)
        yh = y[:, sl]
        yc = yh - jnp.mean(yh, axis=-1, keepdims=True)
        yn = yc * lax.rsqrt(jnp.mean(yc * yc, axis=-1, keepdims=True) + RET_GN_EPS)
        o_ref[:, sl] = (yn * gnw_ref[:, sl] * _silu(g_ref[:, sl])).astype(o_ref.dtype)
    is_latent = pl.program_id(0) < n_latent_tiles
    o_ref[:, RET_W:] = jnp.where(is_latent, nal_ref[...], nac_ref[...]).astype(o_ref.dtype)


def _ret_readout(yf, yb, proj, na_l, na_c, gn_w, rows, tm):
    nl = rows.ml // tm
    row = lambda i: (i, 0)
    return pl.pallas_call(
        functools.partial(_ret_readout_kernel, n_latent_tiles=nl),
        out_shape=jax.ShapeDtypeStruct((rows.m, RET_W + NA_W), BF16),
        grid=(rows.m // tm,),
        in_specs=[pl.BlockSpec((tm, RET_W), row), pl.BlockSpec((tm, RET_W), row),
                  pl.BlockSpec((tm, RET_W), lambda i: (i, 3)),
                  pl.BlockSpec((tm, NA_W), lambda i: (jnp.minimum(i, nl - 1), 0)),
                  pl.BlockSpec((tm, NA_W), lambda i: (jnp.maximum(i - nl, 0), 0)),
                  pl.BlockSpec((1, RET_W), lambda i: (0, 0))],
        out_specs=pl.BlockSpec((tm, RET_W + NA_W), row),
        compiler_params=_cparams(1),
        name="ret_readout",
    )(yf, yb, proj, na_l, na_c, gn_w.reshape(1, RET_W).astype(F32))


def _rwkv_mix_kernel(x_ref, xp_ref, xn_ref, nw_ref, sh_ref, sc_ref, mu_ref, *o_refs, seq, ctx_len, ml, tm):
    nw, sh, sc = nw_ref[...], sh_ref[0], sc_ref[0]
    h = _modnorm(x_ref[...], nw, sh, sc)
    hp = _modnorm(xp_ref[...], nw, sh, sc)[7:8]
    hn = _modnorm(xn_ref[...], nw, sh, sc)[0:1]
    row = lax.broadcasted_iota(jnp.int32, h.shape, 0)
    g = pl.program_id(0) * tm + row
    latent = g < ml
    pos = jnp.where(latent, _mod(g, seq), _mod(g, ctx_len))
    first = pos == 0
    last = pos == jnp.where(latent, seq - 1, ctx_len - 1)
    up = jnp.where(row == 0, hp, pltpu.roll(h, 1, 0))
    up = jnp.where(first, 0.0, up)
    dn = jnp.where(row == tm - 1, hn, pltpu.roll(h, tm - 1, 0))
    dn = jnp.where(last, 0.0, dn)
    xx = 0.5 * (up + dn) - h
    for s_i, o_ref in enumerate(o_refs):
        o_ref[...] = (h + xx * mu_ref[s_i:s_i + 1, :]).astype(o_ref.dtype)


def _rwkv_mix(x, nw, shift, scale, mu, *, rows, tm):
    m, d = x.shape
    n_slab = m // 8
    per = tm // 8
    gmap = lambda i: (rows.group_of_tile(tm)(i), 0, 0)
    return pl.pallas_call(
        functools.partial(_rwkv_mix_kernel, seq=rows.seq, ctx_len=rows.ctx_len, ml=rows.ml, tm=tm),
        out_shape=[jax.ShapeDtypeStruct((m, d), BF16)] * 6,
        grid=(m // tm,),
        in_specs=[pl.BlockSpec((tm, d), lambda i: (i, 0)),
                  pl.BlockSpec((8, d), lambda i: (jnp.maximum(i * per - 1, 0), 0)),
                  pl.BlockSpec((8, d), lambda i: (jnp.minimum((i + 1) * per, n_slab - 1), 0)),
                  pl.BlockSpec((1, d), lambda i: (0, 0)),
                  pl.BlockSpec((1, 1, d), gmap), pl.BlockSpec((1, 1, d), gmap),
                  pl.BlockSpec((6, d), lambda i: (0, 0))],
        out_specs=[pl.BlockSpec((tm, d), lambda i: (i, 0))] * 6,
        compiler_params=_cparams(1),
        name="rwkv_mix",
    )(x, x, x, nw.reshape(1, d), shift, scale, mu)


def _rwkv_post_kernel(r_ref, k_ref, v_ref, xw_ref, xa_ref, xg_ref, w1_ref, w2_ref, a1_ref, a2_ref, g1_ref, g2_ref,
                      w0_ref, a0_ref, kk_ref, ka_ref, rk_ref,
                      rt0, kt0, bt0, at0, wc0, rt1, kt1, bt1, at1, wc1, vb_ref, g_ref, bonus_ref, *, tm):
    c = RWKV_CHUNK
    n_pairs = r_ref.shape[1] // LANES
    r, k, v = r_ref[...], k_ref[...], v_ref[...]
    lw_low = jnp.tanh(jnp.dot(xw_ref[...], w1_ref[...], preferred_element_type=F32))
    a_low = jnp.dot(xa_ref[...], a1_ref[...], preferred_element_type=F32)
    g_low = _sigmoid(jnp.dot(xg_ref[...], g1_ref[...], preferred_element_type=F32))
    g_ref[...] = _dot(g_low, g2_ref[...]).astype(g_ref.dtype)

    def put(o_ref, val):
        for p in range(n_pairs):
            o_ref[p] = val[:, p * LANES:(p + 1) * LANES].astype(o_ref.dtype)

    put(vb_ref, v)
    kk = k * kk_ref[...]
    nrm = jnp.sqrt(_group_mean(kk * kk, RWKV_HD) * float(RWKV_HD))
    kk = kk / jnp.maximum(nrm, 1e-12)

    tr = lax.broadcasted_iota(jnp.int32, (c, c), 0)
    tc = lax.broadcasted_iota(jnp.int32, (c, c), 1)
    tri = [(tr >= tc).astype(BF16), (tr <= tc).astype(BF16)]
    lora = w2_ref.shape[1]
    coeff_src = jnp.zeros_like(r)
    outs = [(rt0, kt0, bt0, at0, wc0), (rt1, kt1, bt1, at1, wc1)]
    for z in range(2):
        rt_ref, kt_ref, bt_ref, at_ref, wc_ref = outs[z]
        w_lora = _dot(lw_low[:, z * lora:(z + 1) * lora], w2_ref[z])
        log_w = -_softplus(-(w0_ref[z:z + 1, :] + w_lora)) - 0.5
        lw = -jnp.exp(log_w)
        a_z = _sigmoid(a0_ref[z:z + 1, :] + _dot(a_low[:, z * lora:(z + 1) * lora], a2_ref[z]))
        k_dir = k * (1.0 + (a_z - 1.0) * ka_ref[...])
        coeff_src = coeff_src + r * k_dir * rk_ref[...]
        cum = jnp.concatenate([_dot_exact_lhs(tri[z], lw[ci * c:(ci + 1) * c]) for ci in range(tm // c)], axis=0)
        e_pos = jnp.exp(cum)
        e_neg = jnp.exp(-cum)
        put(rt_ref, r * e_pos)
        put(kt_ref, k_dir * e_neg)
        put(bt_ref, kk * a_z * e_neg)
        put(at_ref, -kk * jnp.exp(cum - lw))
        for ci in range(tm // c):
            end = ci * c + (c - 1 if z == 0 else 0)
            for p in range(n_pairs):
                wc_ref[p, ci] = e_pos[end:end + 1, p * LANES:(p + 1) * LANES]
    coeff = _group_mean(coeff_src, RWKV_HD) * float(RWKV_HD)
    bonus_ref[...] = coeff * v


def _rwkv_post(r, k, v, xw, xa, xg, p, *, tm):
    m, d = r.shape
    c = RWKV_CHUNK
    lora = p["w2"].shape[1]
    glora = p["g1"].shape[1]
    n_pairs = d // LANES
    pair = jax.ShapeDtypeStruct((n_pairs, m, LANES), BF16)
    chunk = jax.ShapeDtypeStruct((n_pairs, m // c, 1, LANES), F32)
    row = lambda i: (i, 0)
    full2 = lambda i: (0, 0)
    full3 = lambda i: (0, 0, 0)
    tspec = pl.BlockSpec((tm, d), row)
    pspec = pl.BlockSpec((n_pairs, tm, LANES), lambda i: (0, i, 0))
    cspec = pl.BlockSpec((n_pairs, tm // c, 1, LANES), lambda i: (0, i, 0, 0))
    return pl.pallas_call(
        functools.partial(_rwkv_post_kernel, tm=tm),
        out_shape=[pair] * 4 + [chunk] + [pair] * 4 + [chunk]
        + [pair, jax.ShapeDtypeStruct((m, d), BF16), jax.ShapeDtypeStruct((m, d), F32)],
        grid=(m // tm,),
        in_specs=[tspec] * 6
        + [pl.BlockSpec((d, 2 * lora), full2), pl.BlockSpec((2, lora, d), full3),
           pl.BlockSpec((d, 2 * lora), full2), pl.BlockSpec((2, lora, d), full3),
           pl.BlockSpec((d, glora), full2), pl.BlockSpec((glora, d), full2),
           pl.BlockSpec((2, d), full2), pl.BlockSpec((2, d), full2)]
        + [pl.BlockSpec((1, d), full2)] * 3,
        out_specs=[pspec] * 4 + [cspec] + [pspec] * 4 + [cspec] + [pspec, tspec, tspec],
        compiler_params=_cparams(1),
        name="rwkv_post",
    )(r, k, v, xw, xa, xg, p["w1"], p["w2"], p["a1"], p["a2"], p["g1"], p["g2"], p["w0"], p["a0"],
      p["k_k"], p["k_a"], p["r_k"])


def _bdot(a, b):
    return jnp.einsum("ucd,ude->uce", a.astype(BF16), b.astype(BF16), preferred_element_type=F32)


def _bdot_nt(a, b):
    return jnp.einsum("ucd,usd->ucs", a.astype(BF16), b.astype(BF16), preferred_element_type=F32)


def _bdot_tn(a, b):
    return jnp.einsum("uce,ucd->ued", a.astype(BF16), b.astype(BF16), preferred_element_type=F32)


def _scan_chunks(rt, kt, bt, at, v, wc, s0, n_forward):
    c = RWKV_CHUNK
    n_units = rt.shape[0]
    lane = lax.broadcasted_iota(jnp.int32, (n_units, c, LANES), 2)
    row = lax.broadcasted_iota(jnp.int32, (n_units, c, LANES), 1)
    unit = lax.broadcasted_iota(jnp.int32, (n_units, c, LANES), 0)
    head0 = lane < RWKV_HD
    src = _mod(lane, RWKV_HD)
    ahead = jnp.where(unit < n_forward, row - src, src - row)
    strict = ahead > 0
    incl = ahead >= 0

    def dup(x):
        zero = jnp.zeros_like(x)
        h0 = head0[:, :x.shape[1]]
        return jnp.concatenate([jnp.where(h0, x, zero), jnp.where(h0, zero, x)], axis=1)

    ar = jnp.concatenate([at, rt], axis=1)
    mb = _bdot_nt(ar, dup(bt))
    mk = _bdot_nt(ar, dup(kt))
    p_ab = jnp.where(strict, mb[:, :c], 0.0)
    p_rb = jnp.where(incl, mb[:, c:], 0.0)
    p_ak = jnp.where(strict, mk[:, :c], 0.0)
    p_rk = jnp.where(incl, mk[:, c:], 0.0)
    vd = dup(v)
    rhs = _bdot_nt(at, s0) + _bdot(p_ak, vd)

    powers = [p_ab.astype(BF16)]
    for _ in range(int(np.log2(c)) - 1):
        powers.append(_bdot(powers[-1], dup(powers[-1])).astype(BF16))

    def apply_inverse(x):
        for pw in powers:
            x = x + _bdot(pw, dup(x))
        return x

    u = apply_inverse(rhs)
    ud = dup(u)
    ph, pl_ = _split2(p_ab)
    uh, ul = _split2(ud)
    resid = (rhs - u) + (_bdot(ph, uh) + (_bdot(ph, ul) + _bdot(pl_, uh)))
    u = u + apply_inverse(resid)

    y = _bdot_nt(rt, s0) + _bdot(jnp.concatenate([p_rb, p_rk], axis=2), jnp.concatenate([dup(u), vd], axis=1))
    upd = _bdot_tn(jnp.concatenate([u.astype(BF16), v], axis=1), jnp.concatenate([bt, kt], axis=1))
    er = lax.broadcasted_iota(jnp.int32, (1, LANES, LANES), 1) < RWKV_HD
    ec = lax.broadcasted_iota(jnp.int32, (1, LANES, LANES), 2) < RWKV_HD
    s1 = (s0 + jnp.where(er == ec, upd, 0.0)) * wc
    return y, s1


def _rwkv_scan_kernel(rtf, ktf, btf, atf, vf, wcf, rtb, ktb, btb, atb, vb, wcb, yf_ref, yb_ref, s_ref):
    @pl.when(pl.program_id(1) == 0)
    def _():
        s_ref[...] = jnp.zeros_like(s_ref)

    n_pairs = rtf.shape[0]
    both = lambda f, b: jnp.concatenate([f[...], b[...]], axis=0)
    wc = jnp.concatenate([wcf[:, 0], wcb[:, 0]], axis=0)
    y, s1 = _scan_chunks(both(rtf, rtb), both(ktf, ktb), both(btf, btb), both(atf, atb), both(vf, vb), wc,
                         s_ref[...], n_pairs)
    s_ref[...] = s1
    yf_ref[...] = y[:n_pairs]
    yb_ref[...] = y[n_pairs:]


def _rwkv_scan(feats_f, feats_b, v, rows):
    n_pairs, m, _ = v.shape
    c = RWKV_CHUNK
    d = n_pairs * LANES
    nchunks = (rows.seq + rows.ctx_len) // c
    blk_f, _ = rows.chunk_block(c, False)
    blk_b, _ = rows.chunk_block(c, True)

    def specs(blk):
        tok = pl.BlockSpec((n_pairs, c, LANES), lambda bi, t: (0, blk(bi, t), 0))
        return tok, pl.BlockSpec((n_pairs, 1, 1, LANES), lambda bi, t: (0, blk(bi, t), 0, 0))

    tok_f, wc_f = specs(blk_f)
    tok_b, wc_b = specs(blk_b)
    return pl.pallas_call(
        _rwkv_scan_kernel,
        out_shape=[jax.ShapeDtypeStruct((n_pairs, m, LANES), F32)] * 2,
        grid=(rows.batch, nchunks),
        in_specs=[tok_f] * 5 + [wc_f] + [tok_b] * 5 + [wc_b],
        out_specs=[tok_f, tok_b],
        scratch_shapes=[pltpu.VMEM((2 * n_pairs, LANES, LANES), F32)],
        compiler_params=_cparams(2),
        name="rwkv_scan",
    )(*feats_f[:4], v, feats_f[4], *feats_b[:4], v, feats_b[4])


def _rwkv_readout_kernel(yf_ref, yb_ref, bonus_ref, g_ref, lnw_ref, lnb_ref, o_ref):
    y = jnp.concatenate([yf_ref[p] + yb_ref[p] for p in range(yf_ref.shape[0])], axis=1)
    yc = y - _group_mean(y, RWKV_HD)
    yn = yc * lax.rsqrt(_group_mean(yc * yc, RWKV_HD) + RWKV_GN_EPS) * lnw_ref[...] + lnb_ref[...]
    o_ref[...] = ((yn + bonus_ref[...]) * g_ref[...].astype(F32)).astype(o_ref.dtype)


def _rwkv_readout(yf, yb, bonus, g, ln_w, ln_b, rows, tm):
    n_pairs = yf.shape[0]
    d = n_pairs * LANES
    pspec = pl.BlockSpec((n_pairs, tm, LANES), lambda i: (0, i, 0))
    tspec = pl.BlockSpec((tm, d), lambda i: (i, 0))
    vspec = pl.BlockSpec((1, d), lambda i: (0, 0))
    return pl.pallas_call(
        _rwkv_readout_kernel,
        out_shape=jax.ShapeDtypeStruct((rows.ml, d), BF16),
        grid=(rows.ml // tm,),
        in_specs=[pspec, pspec, tspec, tspec, vspec, vspec],
        out_specs=tspec,
        compiler_params=_cparams(1),
        name="rwkv_readout",
    )(yf, yb, bonus, g, ln_w.reshape(1, d).astype(F32), ln_b.reshape(1, d).astype(F32))


def _router_kernel(x_ref, nw_ref, sh_ref, sc_ref, w_ref, o_ref, h_ref, *, n_experts):
    h = _modnorm(x_ref[...], nw_ref[...], sh_ref[0], sc_ref[0])
    h_ref[...] = h
    logits = _dot_hi(h, w_ref[...])
    lane = lax.broadcasted_iota(jnp.int32, logits.shape, 1).astype(F32)
    logits = jnp.where(lane < n_experts, logits, -jnp.inf)
    v1 = jnp.max(logits, axis=-1, keepdims=True)
    i1 = jnp.min(jnp.where(logits == v1, lane, float(LANES)), axis=-1, keepdims=True)
    rest = jnp.where(lane == i1, -jnp.inf, logits)
    v2 = jnp.max(rest, axis=-1, keepdims=True)
    i2 = jnp.min(jnp.where(rest == v2, lane, float(LANES)), axis=-1, keepdims=True)
    e2 = jnp.exp(v2 - v1)
    den = 1.0 + e2
    o_ref[...] = jnp.where(lane == i1, 1.0 / den, 0.0) + jnp.where(lane == i2, e2 / den, 0.0)


def _router(x, nw, shift, scale, router, *, tm, group_of_tile):
    m, d = x.shape
    n_experts = router.shape[1]
    w = jnp.zeros((d, LANES), F32).at[:, :n_experts].set(router.astype(F32))
    gmap = lambda i: (group_of_tile(i), 0, 0)
    return pl.pallas_call(
        functools.partial(_router_kernel, n_experts=n_experts),
        out_shape=[jax.ShapeDtypeStruct((m, LANES), F32), jax.ShapeDtypeStruct((m, d), F32)],
        grid=(m // tm,),
        in_specs=[pl.BlockSpec((tm, d), lambda i: (i, 0)), pl.BlockSpec((1, d), lambda i: (0, 0)),
                  pl.BlockSpec((1, 1, d), gmap), pl.BlockSpec((1, 1, d), gmap),
                  pl.BlockSpec((d, LANES), lambda i: (0, 0))],
        out_specs=[pl.BlockSpec((tm, LANES), lambda i: (i, 0)), pl.BlockSpec((tm, d), lambda i: (i, 0))],
        compiler_params=_cparams(1),
        name="router",
    )(x, nw.reshape(1, d), shift, scale, w)


MOE_TILE = 512


def _route_plan(gates, n_experts, tg):
    m = gates.shape[0]
    g = gates[:, :n_experts]
    sel = g > 0.0
    seli = sel.astype(jnp.int32)
    slot = jnp.cumsum(seli, axis=1) - 1
    rank = jnp.cumsum(seli, axis=0) - 1
    counts = jnp.sum(seli, axis=0)
    padded = ((counts + tg - 1) // tg) * tg
    ends = jnp.cumsum(padded)
    dest = (ends - padded)[None, :] + rank
    n_rows = TOP_K * m + n_experts * tg
    n_tiles = n_rows // tg
    first = sel & (slot == 0)
    second = sel & (slot == 1)
    pick = lambda msk, val: jnp.sum(jnp.where(msk, val, 0), axis=1)
    d1 = pick(first, dest)
    has2 = jnp.any(second, axis=1)
    d2 = jnp.where(has2, pick(second, dest), d1)
    w1 = pick(first, g)
    w2 = pick(second, g)
    tok = jnp.broadcast_to(jnp.arange(m, dtype=jnp.int32)[:, None], dest.shape)
    src = jnp.zeros((n_rows,), jnp.int32).at[jnp.where(sel, dest, n_rows).reshape(-1)].set(
        tok.reshape(-1), mode="drop")
    tile_expert = jnp.minimum(jnp.sum((jnp.arange(n_tiles, dtype=jnp.int32) * tg)[:, None] >= ends[None, :], axis=1),
                              n_experts - 1).astype(jnp.int32)
    n_valid = (ends[-1] // tg).astype(jnp.int32).reshape(1)
    wts = jnp.zeros((m, LANES), F32).at[:, 0].set(w1).at[:, 1].set(w2)
    return src.reshape(n_tiles, 1, tg), tile_expert, n_valid, d1.astype(jnp.int32), d2.astype(jnp.int32), wts


def _gather_rows_kernel(src_ref, h_hbm, o_ref, sem, *, tg):
    def row_copy(r, s):
        return pltpu.make_async_copy(h_hbm.at[pl.ds(s, 1), :], o_ref.at[pl.ds(r, 1), :], sem)

    def issue(r, carry):
        row_copy(r, src_ref[0, 0, r]).start()
        return carry

    def drain(r, carry):
        row_copy(r, 0).wait()
        return carry

    lax.fori_loop(0, tg, issue, 0)
    lax.fori_loop(0, tg, drain, 0)


def _gather_rows(h, src):
    n_tiles, _, tg = src.shape
    d = h.shape[1]
    return pl.pallas_call(
        functools.partial(_gather_rows_kernel, tg=tg),
        out_shape=jax.ShapeDtypeStruct((n_tiles * tg, d), F32),
        grid=(n_tiles,),
        in_specs=[pl.BlockSpec((1, 1, tg), lambda t: (t, 0, 0), memory_space=pltpu.SMEM),
                  pl.BlockSpec(memory_space=pl.ANY)],
        out_specs=pl.BlockSpec((tg, d), lambda t: (t, 0)),
        scratch_shapes=[pltpu.SemaphoreType.DMA],
        compiler_params=_cparams(1),
        name="moe_gather",
    )(src, h)


def _grouped_swiglu_kernel(te_ref, nv_ref, a_ref, wg_ref, wu_ref, o_ref, ab_ref):
    t = pl.program_id(0)

    @pl.when(jnp.logical_and(t < nv_ref[0], pl.program_id(1) == 0))
    def _():
        ab_ref[...] = a_ref[...].astype(BF16)

    @pl.when(t < nv_ref[0])
    def _():
        a = ab_ref[...]
        gate = jnp.dot(a, wg_ref[0], preferred_element_type=F32)
        up = jnp.dot(a, wu_ref[0], preferred_element_type=F32)
        o_ref[...] = (_silu(gate) * up).astype(o_ref.dtype)

    @pl.when(t >= nv_ref[0])
    def _():
        o_ref[...] = jnp.zeros_like(o_ref)


def _grouped_swiglu(a, w13, tile_expert, n_valid, *, tg, tn):
    p_rows, k = a.shape
    e_hid = w13.shape[2] // 2
    nj = e_hid // tn
    tile = lambda t, nv: jnp.minimum(t, nv[0] - 1)
    return pl.pallas_call(
        _grouped_swiglu_kernel,
        out_shape=jax.ShapeDtypeStruct((p_rows, e_hid), BF16),
        grid_spec=pltpu.PrefetchScalarGridSpec(
            num_scalar_prefetch=2,
            grid=(p_rows // tg, nj),
            in_specs=[pl.BlockSpec((tg, k), lambda t, j, te, nv: (tile(t, nv), 0)),
                      pl.BlockSpec((1, k, tn), lambda t, j, te, nv: (te[tile(t, nv)], 0, j)),
                      pl.BlockSpec((1, k, tn), lambda t, j, te, nv: (te[tile(t, nv)], 0, j + nj))],
            out_specs=pl.BlockSpec((tg, tn), lambda t, j, te, nv: (t, j)),
            scratch_shapes=[pltpu.VMEM((tg, k), BF16)],
        ),
        compiler_params=_cparams(2),
        name="moe_up",
    )(tile_expert, n_valid, a, w13, w13)


def _grouped_mm_kernel(te_ref, nv_ref, a_ref, w_ref, o_ref):
    @pl.when(pl.program_id(0) < nv_ref[0])
    def _():
        o_ref[...] = jnp.dot(a_ref[...], w_ref[0], preferred_element_type=F32)

    @pl.when(pl.program_id(0) >= nv_ref[0])
    def _():
        o_ref[...] = jnp.zeros_like(o_ref)


def _grouped_mm(a, w, tile_expert, n_valid, *, tg, tn):
    p_rows, k = a.shape
    n = w.shape[2]
    tile = lambda t, nv: jnp.minimum(t, nv[0] - 1)
    return pl.pallas_call(
        _grouped_mm_kernel,
        out_shape=jax.ShapeDtypeStruct((p_rows, n), F32),
        grid_spec=pltpu.PrefetchScalarGridSpec(
            num_scalar_prefetch=2,
            grid=(p_rows // tg, n // tn),
            in_specs=[pl.BlockSpec((tg, k), lambda t, j, te, nv: (tile(t, nv), 0)),
                      pl.BlockSpec((1, k, tn), lambda t, j, te, nv: (te[tile(t, nv)], 0, j))],
            out_specs=pl.BlockSpec((tg, tn), lambda t, j, te, nv: (t, j)),
        ),
        compiler_params=_cparams(2),
        name="moe_down",
    )(tile_expert, n_valid, a, w)


def _combine_kernel(d1_ref, d2_ref, y_hbm, wts_ref, x_ref, gate_ref, o_ref, buf_ref, sem, *, tm):
    def row_copy(which, r, s):
        return pltpu.make_async_copy(y_hbm.at[pl.ds(s, 1), :], buf_ref.at[which, pl.ds(r, 1), :], sem)

    def issue(r, carry):
        row_copy(0, r, d1_ref[0, 0, r]).start()
        row_copy(1, r, d2_ref[0, 0, r]).start()
        return carry

    def drain(r, carry):
        row_copy(0, r, 0).wait()
        row_copy(1, r, 0).wait()
        return carry

    lax.fori_loop(0, tm, issue, 0)
    lax.fori_loop(0, tm, drain, 0)
    lane = lax.broadcasted_iota(jnp.int32, wts_ref.shape, 1)
    w1 = jnp.sum(jnp.where(lane == 0, wts_ref[...], 0.0), axis=-1, keepdims=True)
    w2 = jnp.sum(jnp.where(lane == 1, wts_ref[...], 0.0), axis=-1, keepdims=True)
    o_ref[...] = x_ref[...] + gate_ref[0] * (w1 * buf_ref[0] + w2 * buf_ref[1])


def _combine(y, d1, d2, wts, x, gate, *, tm, group_of_tile):
    m, d = x.shape
    idx = lambda a: a.reshape(m // tm, 1, tm)
    ispec = pl.BlockSpec((1, 1, tm), lambda i: (i, 0, 0), memory_space=pltpu.SMEM)
    return pl.pallas_call(
        functools.partial(_combine_kernel, tm=tm),
        out_shape=jax.ShapeDtypeStruct((m, d), F32),
        grid=(m // tm,),
        in_specs=[ispec, ispec, pl.BlockSpec(memory_space=pl.ANY),
                  pl.BlockSpec((tm, LANES), lambda i: (i, 0)), pl.BlockSpec((tm, d), lambda i: (i, 0)),
                  pl.BlockSpec((1, 1, d), lambda i: (group_of_tile(i), 0, 0))],
        out_specs=pl.BlockSpec((tm, d), lambda i: (i, 0)),
        scratch_shapes=[pltpu.VMEM((2, tm, d), F32), pltpu.SemaphoreType.DMA],
        compiler_params=_cparams(1),
        name="moe_combine",
    )(idx(d1), idx(d2), y, wts, x, gate)


def _rope_tables(seq, ctx_len):
    half = RET_HD // 4
    freqs = ROPE_BASE ** (-np.arange(half, dtype=np.float64) / half)
    t = np.arange(seq)
    ang_r = (t // GRID_W)[:, None] * freqs[None, :]
    ang_c = (t % GRID_W)[:, None] * freqs[None, :]
    cos = np.concatenate([np.cos(ang_r)] * 2 + [np.cos(ang_c)] * 2, axis=1)
    sin = np.concatenate([-np.sin(ang_r), np.sin(ang_r), -np.sin(ang_c), np.sin(ang_c)], axis=1)
    cos = np.concatenate([np.ones((ctx_len, RET_HD)), cos], axis=0)
    sin = np.concatenate([np.zeros((ctx_len, RET_HD)), sin], axis=0)
    return jnp.asarray(cos, F32), jnp.asarray(sin, F32)


def kernel(x, c, ctx, c_ctx, ada_w, ada_b, norm_mix_w, norm_ffn_w, ev_w_in, ev_ret_decay_f, ev_ret_decay_b,
           ev_ret_gn_w, ev_na_qn_w, ev_na_kn_w, ev_na_rpb, ev_w_out, ev_ffn_w13, ev_ffn_w2, od_mu, od_w_rkv,
           od_w0, od_w1, od_w2, od_a0, od_a1, od_a2, od_g1, od_g2, od_k_k, od_k_a, od_r_k, od_ln_w, od_ln_b,
           od_w_o, od_router, od_moe_w13, od_moe_w2):
    batch, seq, d = x.shape
    ctx_len = ctx.shape[1]
    rows = _Rows(batch, seq, ctx_len)
    tm = 8 * RWKV_CHUNK
    grp = rows.group_of_tile(tm)
    xa = jnp.concatenate([x.reshape(rows.ml, d), ctx.reshape(rows.mc, d)], axis=0)

    n_mod = ((batch + 1 + 7) // 8) * 8
    c_rows = jnp.zeros((n_mod, d), F32).at[:batch].set(c).at[batch].set(c_ctx)
    mods = _ada(c_rows, ada_w, ada_b)
    mods = mods.reshape(mods.shape[0], n_mod, 6, 1, d).transpose(0, 2, 1, 3, 4)

    mod = mods[0]
    proj = _modnorm_mm(xa, norm_mix_w[0], mod[0], mod[1], ev_w_in[0].astype(BF16), group_of_tile=grp, tm=tm, tn=512,
                       out_dtype=F32)
    cos, sin = _rope_tables(seq, ctx_len)
    y_f, y_b = _retention(proj, cos, sin, ev_ret_decay_f[0], ev_ret_decay_b[0], rows)
    bias = _na_bias_table(ev_na_rpb[0].astype(F32))
    na_l = _na_latent(proj, ev_na_qn_w[0], ev_na_kn_w[0], bias, rows, 4 * RET_W)
    na_c = _ctx_attention(proj, ev_na_qn_w[0], ev_na_kn_w[0], rows, 4 * RET_W)
    mix = _ret_readout(y_f, y_b, proj, na_l, na_c, ev_ret_gn_w[0], rows, tm)
    w_out = ev_w_out[0].astype(BF16)
    xa = _mm(mix, w_out, tm=tm, tn=512, tk=w_out.shape[0], res=xa, gate=mod[2], group_of_tile=grp)

    w13 = ev_ffn_w13[0].astype(BF16)
    w2 = ev_ffn_w2[0].astype(BF16)
    hid = w2.shape[0]
    tn13 = 256 if hid % 256 == 0 else 128
    hidden = _modnorm_mm(xa, norm_ffn_w[0], mod[3], mod[4], w13, group_of_tile=grp, tm=tm, tn=tn13, out_dtype=BF16,
                         swiglu=True)
    xa = _mm(hidden, w2, tm=tm, tn=512, tk=hid, res=xa, gate=mod[5], group_of_tile=grp)

    mod = mods[1]
    glora = od_g1.shape[-1]
    glora_p = ((glora + LANES - 1) // LANES) * LANES
    p = {
        "w1": jnp.concatenate([od_w1[0, 0], od_w1[0, 1]], axis=1).astype(BF16),
        "w2": od_w2[0].astype(BF16),
        "a1": jnp.concatenate([od_a1[0, 0], od_a1[0, 1]], axis=1).astype(BF16),
        "a2": od_a2[0].astype(BF16),
        "g1": jnp.zeros((d, glora_p), BF16).at[:, :glora].set(od_g1[0].astype(BF16)),
        "g2": jnp.zeros((glora_p, d), BF16).at[:glora].set(od_g2[0].astype(BF16)),
        "w0": od_w0[0].astype(F32), "a0": od_a0[0].astype(F32),
        "k_k": od_k_k[0].reshape(1, d).astype(F32), "k_a": od_k_a[0].reshape(1, d).astype(F32),
        "r_k": od_r_k[0].reshape(1, d).astype(F32),
    }
    w_rkv = od_w_rkv[0].astype(BF16)
    xm = _rwkv_mix(xa, norm_mix_w[1], mod[0], mod[1], od_mu[0].astype(F32), rows=rows, tm=tm)
    r, k, v = [_mm(xm[s_i], w_rkv[s_i], tm=tm, tn=512, tk=d) for s_i in range(3)]
    f = _rwkv_post(r, k, v, xm[3], xm[4], xm[5], p, tm=tm)
    y_f, y_b = _rwkv_scan(f[0:5], f[5:10], f[10], rows)
    mixed = _rwkv_readout(y_f, y_b, f[12], f[11], od_ln_w[0], od_ln_b[0], rows, tm)
    x_l = _mm(mixed, od_w_o[0].astype(BF16), tm=tm, tn=512, tk=d, res=xa, gate=mod[2], group_of_tile=grp)

    gates, h2 = _router(x_l, norm_ffn_w[1], mod[3], mod[4], od_router[0], tm=tm, group_of_tile=grp)
    n_e = od_router.shape[-1]
    src, tile_expert, n_valid, d1, d2, wts = _route_plan(gates, n_e, MOE_TILE)
    a_sorted = _gather_rows(h2, src)
    hid_sorted = _grouped_swiglu(a_sorted, od_moe_w13[0].astype(BF16), tile_expert, n_valid, tg=MOE_TILE, tn=512)
    y_sorted = _grouped_mm(hid_sorted, od_moe_w2[0].astype(BF16), tile_expert, n_valid, tg=MOE_TILE, tn=512)
    tm_c = 256
    x_l = _combine(y_sorted, d1, d2, wts, x_l, mod[5], tm=tm_c, group_of_tile=rows.group_of_tile(tm_c))
    return x_l.reshape(batch, seq, d)
```

```python
import functools

import jax
import jax.numpy as jnp
import numpy as np
from jax import lax
from jax.experimental import pallas as pl
from jax.experimental.pallas import tpu as pltpu

F32 = jnp.float32
BF16 = jnp.bfloat16

LANES = 128
GRID_W = 64
RET_HEADS = 4
RET_HD = 128
RET_W = RET_HEADS * RET_HD
RET_CHUNK = 128
RET_GN_EPS = 1e-5
NA_HEADS = 8
NA_HD = 64
NA_W = NA_HEADS * NA_HD
NA_KR = 8
NA_KC = 16
RWKV_HD = 64
RWKV_GN_EPS = 64e-5
RWKV_CHUNK = 64
TOP_K = 2
ROPE_BASE = 10000.0
NORM_EPS = 1e-6
NEG_BIG = -1e30
VMEM_LIMIT = 56 * 1024 * 1024


def _cparams(n_axes):
    return pltpu.CompilerParams(dimension_semantics=("arbitrary",) * n_axes, vmem_limit_bytes=VMEM_LIMIT)


def _dot(a, b):
    return jnp.dot(a.astype(BF16), b.astype(BF16), preferred_element_type=F32)


def _dot_nt(a, b):
    return lax.dot_general(a.astype(BF16), b.astype(BF16), (((1,), (1,)), ((), ())), preferred_element_type=F32)


def _dot_tn(a, b):
    return lax.dot_general(a.astype(BF16), b.astype(BF16), (((0,), (0,)), ((), ())), preferred_element_type=F32)


def _split2(x):
    hi = x.astype(BF16)
    return hi, (x - hi.astype(F32)).astype(BF16)


def _split3(x):
    hi = x.astype(BF16)
    r1 = x - hi.astype(F32)
    mid = r1.astype(BF16)
    lo = (r1 - mid.astype(F32)).astype(BF16)
    return hi, mid, lo


def _dot_hi(a, b):
    ah, am, al = _split3(a)
    bh, bm, bl = _split3(b)
    d = functools.partial(jnp.dot, preferred_element_type=F32)
    return (d(ah, bh) + (d(ah, bm) + d(am, bh)) + (d(am, bm) + d(ah, bl) + d(al, bh)))


def _dot_x3(a, b):
    ah, al = _split2(a)
    bh, bl = _split2(b)
    d = functools.partial(jnp.dot, preferred_element_type=F32)
    return d(ah, bh) + (d(ah, bl) + d(al, bh))


def _dot_exact_rhs(a, b_bf16):
    ah, am, al = _split3(a)
    d = functools.partial(jnp.dot, preferred_element_type=F32)
    return d(ah, b_bf16) + d(am, b_bf16) + d(al, b_bf16)


def _dot_exact_lhs(a_bf16, b):
    bh, bm, bl = _split3(b)
    d = functools.partial(jnp.dot, preferred_element_type=F32)
    return d(a_bf16, bh) + d(a_bf16, bm) + d(a_bf16, bl)


def _mod(v, n):
    return (v & (n - 1)) if n & (n - 1) == 0 else v % n


def _sigmoid(x):
    return 1.0 / (1.0 + jnp.exp(-x))


def _silu(x):
    return x * _sigmoid(x)


def _softplus(x):
    return jnp.maximum(x, 0.0) + jnp.log(1.0 + jnp.exp(-jnp.abs(x)))


def _group_mean_mat(width, group):
    r = lax.broadcasted_iota(jnp.int32, (width, width), 0) // group
    c = lax.broadcasted_iota(jnp.int32, (width, width), 1) // group
    return jnp.where(r == c, 1.0 / group, 0.0).astype(BF16)


def _group_mean(x, group):
    g = _group_mean_mat(LANES, group)
    cols = [_dot_exact_rhs(x[:, c:c + LANES], g) for c in range(0, x.shape[1], LANES)]
    return cols[0] if len(cols) == 1 else jnp.concatenate(cols, axis=1)


def _modnorm(x, nw, shift, scale):
    ms = jnp.mean(x * x, axis=-1, keepdims=True)
    y = x * lax.rsqrt(ms + NORM_EPS) * nw
    return y * (1.0 + scale) + shift


class _Rows:
    def __init__(self, batch, seq, ctx_len):
        self.batch, self.seq, self.ctx_len = batch, seq, ctx_len
        self.ml, self.mc = batch * seq, batch * ctx_len
        self.m = self.ml + self.mc

    def group_of_tile(self, tm):
        return lambda i: jnp.minimum((i * tm) // self.seq, self.batch)

    def chunk_block(self, chunk, backward):
        ncc, nlc = self.ctx_len // chunk, self.seq // chunk

        def block(b, t):
            if backward:
                t = jnp.where(t < ncc, ncc - 1 - t, nlc + 2 * ncc - 1 - t)
            return jnp.where(t < ncc, (self.ml + b * self.ctx_len) // chunk + t, (b * self.seq) // chunk + t - ncc)

        def position(t):
            if backward:
                t = jnp.where(t < ncc, ncc - 1 - t, nlc + 2 * ncc - 1 - t)
            return t

        return block, position


def _ada_kernel(c_ref, w_ref, b_ref, o_ref):
    s = _silu(c_ref[...])
    o_ref[0] = _dot_hi(s, w_ref[0]) + b_ref[0]


def _ada(c_rows, ada_w, ada_b):
    depth, d, n = ada_w.shape
    rows = c_rows.shape[0]
    tn = 1536
    return pl.pallas_call(
        _ada_kernel,
        out_shape=jax.ShapeDtypeStruct((depth, rows, n), F32),
        grid=(depth, n // tn),
        in_specs=[
            pl.BlockSpec((rows, d), lambda l, j: (0, 0)),
            pl.BlockSpec((1, d, tn), lambda l, j: (l, 0, j)),
            pl.BlockSpec((1, 1, tn), lambda l, j: (l, 0, j)),
        ],
        out_specs=pl.BlockSpec((1, rows, tn), lambda l, j: (l, 0, j)),
        compiler_params=_cparams(2),
        name="ada_mod",
    )(c_rows, ada_w, ada_b.reshape(depth, 1, n))


def _modnorm_mm_kernel(*refs, swiglu, gated):
    x_ref, nw_ref, sh_ref, sc_ref = refs[:4]
    pos = 4
    w_refs = refs[pos:pos + (2 if swiglu else 1)]
    pos += len(w_refs)
    gates_ref = refs[pos] if gated else None
    pos += 1 if gated else 0
    o_ref, h_ref = refs[pos], refs[pos + 1]

    first = pl.program_id(1) == 0
    if gated:
        first = jnp.logical_and(first, pl.program_id(2) == 0)

    @pl.when(first)
    def _():
        h_ref[...] = _modnorm(x_ref[...], nw_ref[...], sh_ref[0], sc_ref[0]).astype(BF16)

    h = h_ref[...]
    w0 = w_refs[0][0] if gated else w_refs[0][...]
    acc = jnp.dot(h, w0, preferred_element_type=F32)
    if swiglu:
        w1 = w_refs[1][0] if gated else w_refs[1][...]
        acc = _silu(acc) * jnp.dot(h, w1, preferred_element_type=F32)
    if gated:
        e = pl.program_id(1)
        lane = lax.broadcasted_iota(jnp.int32, gates_ref.shape, 1)
        g = jnp.sum(jnp.where(lane == e, gates_ref[...], 0.0), axis=-1, keepdims=True)
        acc = acc * g
    o_ref[...] = acc.astype(o_ref.dtype)


def _modnorm_mm(x, nw, shift, scale, w, *, group_of_tile, tm, tn, out_dtype, swiglu=False, gates=None, rows=None):
    m = x.shape[0] if rows is None else rows
    k = x.shape[1]
    gated = gates is not None
    n_total = w.shape[-1]
    n_out = n_total // 2 if swiglu else n_total
    nj = n_out // tn
    if gated:
        n_e = w.shape[0]
        grid = (m // tm, n_e, nj)
        xmap = lambda i, e, j: (i, 0)
        cmap = lambda i, e, j: (0, 0)
        gmap = lambda i, e, j: (group_of_tile(i), 0, 0)
        wspecs = [pl.BlockSpec((1, k, tn), lambda i, e, j: (e, 0, j))]
        if swiglu:
            wspecs.append(pl.BlockSpec((1, k, tn), lambda i, e, j: (e, 0, j + nj)))
        extra = [pl.BlockSpec((tm, LANES), lambda i, e, j: (i, 0))]
        omap = lambda i, e, j: (i, e * nj + j)
        out_cols = n_e * n_out
    else:
        grid = (m // tm, nj)
        xmap = lambda i, j: (i, 0)
        cmap = lambda i, j: (0, 0)
        gmap = lambda i, j: (group_of_tile(i), 0, 0)
        wspecs = [pl.BlockSpec((k, tn), lambda i, j: (0, j))]
        if swiglu:
            wspecs.append(pl.BlockSpec((k, tn), lambda i, j: (0, j + nj)))
        extra = []
        omap = lambda i, j: (i, j)
        out_cols = n_out
    args = [x, nw.reshape(1, k), shift, scale] + [w] * len(wspecs) + ([gates] if gated else [])
    return pl.pallas_call(
        functools.partial(_modnorm_mm_kernel, swiglu=swiglu, gated=gated),
        out_shape=jax.ShapeDtypeStruct((m, out_cols), out_dtype),
        grid=grid,
        in_specs=[pl.BlockSpec((tm, k), xmap), pl.BlockSpec((1, k), cmap),
                  pl.BlockSpec((1, 1, k), gmap), pl.BlockSpec((1, 1, k), gmap)] + wspecs + extra,
        out_specs=pl.BlockSpec((tm, tn), omap),
        scratch_shapes=[pltpu.VMEM((tm, k), BF16)],
        compiler_params=_cparams(len(grid)),
        name="modnorm_mm",
    )(*args)


def _mm_kernel(*refs, residual, nk):
    acc_ref = refs[-1] if nk > 1 else None
    if residual:
        a_ref, w_ref, res_ref, gate_ref, o_ref = refs[:5]
    else:
        a_ref, w_ref, o_ref = refs[:3]
    kk = pl.program_id(2)
    part = jnp.dot(a_ref[...], w_ref[...], preferred_element_type=F32)

    def finish(acc):
        if residual:
            acc = res_ref[...] + gate_ref[0] * acc
        o_ref[...] = acc.astype(o_ref.dtype)

    if nk == 1:
        finish(part)
    else:
        @pl.when(kk == 0)
        def _():
            acc_ref[...] = part

        @pl.when(jnp.logical_and(kk > 0, kk < nk - 1))
        def _():
            acc_ref[...] += part

        @pl.when(kk == nk - 1)
        def _():
            finish(acc_ref[...] + part)


def _mm(a, w, *, tm, tn, tk, out_dtype=F32, res=None, gate=None, group_of_tile=None):
    m, k = a.shape
    n = w.shape[1]
    nk = k // tk
    residual = res is not None
    in_specs = [pl.BlockSpec((tm, tk), lambda i, j, q: (i, q)), pl.BlockSpec((tk, tn), lambda i, j, q: (q, j))]
    args = [a, w]
    if residual:
        in_specs += [pl.BlockSpec((tm, tn), lambda i, j, q: (i, j)),
                     pl.BlockSpec((1, 1, tn), lambda i, j, q: (group_of_tile(i), 0, j))]
        args += [res, gate]
    return pl.pallas_call(
        functools.partial(_mm_kernel, residual=residual, nk=nk),
        out_shape=jax.ShapeDtypeStruct((m, n), out_dtype),
        grid=(m // tm, n // tn, nk),
        in_specs=in_specs,
        out_specs=pl.BlockSpec((tm, tn), lambda i, j, q: (i, j)),
        scratch_shapes=[pltpu.VMEM((tm, tn), F32)] if nk > 1 else [],
        compiler_params=_cparams(3),
        name="mm_res" if residual else "mm",
    )(*args)


def _rope(x, cos, sin_signed):
    lane = lax.broadcasted_iota(jnp.int32, x.shape, 1)
    half = RET_HD // 4
    swapped = jnp.where(_mod(lane, 2 * half) < half, pltpu.roll(x, LANES - half, 1), pltpu.roll(x, half, 1))
    return x * cos + swapped * sin_signed


def _ret_chunks(q, k, v, s0, log_g, n_forward):
    c = RET_CHUNK
    shape = (q.shape[0], c, c)
    row = lax.broadcasted_iota(jnp.int32, shape, 1)
    col = lax.broadcasted_iota(jnp.int32, shape, 2)
    fwd = lax.broadcasted_iota(jnp.int32, shape, 0) < n_forward
    dist = jnp.where(fwd, row - col, col - row).astype(F32)
    q_steps = jnp.where(fwd, row + 1, c - row).astype(F32)
    k_steps = jnp.where(fwd, c - 1 - row, row).astype(F32)
    dmat = jnp.where(dist >= 0, jnp.exp(log_g * jnp.maximum(dist, 0.0)), 0.0)
    q_dec = jnp.exp(log_g * q_steps)
    k_dec = jnp.exp(log_g * k_steps)
    scores = _bdot_nt(q, k) * dmat
    out = _bdot(scores, v) + _bdot(q * q_dec, s0)
    s1 = s0 * jnp.exp(log_g * float(c)) + _bdot_tn(k * k_dec, v)
    return out, s1


def _retention_kernel(qf_ref, kf_ref, vf_ref, cf_ref, sf_ref, qb_ref, kb_ref, vb_ref, cb_ref, sb_ref,
                      decf_ref, decb_ref, of_ref, ob_ref, st_ref):
    @pl.when(pl.program_id(1) == 0)
    def _():
        st_ref[...] = jnp.zeros_like(st_ref)

    h = RET_HEADS
    kscale = RET_HD ** -0.5

    def heads(ref, cos, sin, scale=None):
        out = []
        for hh in range(h):
            xh = ref[:, hh * RET_HD:(hh + 1) * RET_HD]
            if cos is not None:
                xh = _rope(xh, cos, sin)
            out.append(xh if scale is None else xh * scale)
        return out

    cf, sf, cb, sb = cf_ref[...], sf_ref[...], cb_ref[...], sb_ref[...]
    q = jnp.stack(heads(qf_ref, cf, sf) + heads(qb_ref, cb, sb), axis=0)
    k = jnp.stack(heads(kf_ref, cf, sf, kscale) + heads(kb_ref, cb, sb, kscale), axis=0)
    v = jnp.stack(heads(vf_ref, None, None) + heads(vb_ref, None, None), axis=0)
    log_g = -jnp.exp(jnp.concatenate([decf_ref[...], decb_ref[...]], axis=0))
    out, s1 = _ret_chunks(q, k, v, st_ref[...], log_g, h)
    st_ref[...] = s1
    of_ref[...] = jnp.concatenate([out[hh] for hh in range(h)], axis=1)
    ob_ref[...] = jnp.concatenate([out[h + hh] for hh in range(h)], axis=1)


def _retention(proj, cos, sin_signed, dec_f, dec_b, rows):
    c, h = RET_CHUNK, RET_HEADS
    n = (rows.seq + rows.ctx_len) // c
    blk_f, pos_f = rows.chunk_block(c, False)
    blk_b, pos_b = rows.chunk_block(c, True)

    def tok(blk, off):
        return pl.BlockSpec((c, RET_W), lambda bi, t: (blk(bi, t), off))

    def tab(pos):
        return pl.BlockSpec((c, RET_HD), lambda bi, t: (pos(t), 0))

    dec = lambda a: jnp.broadcast_to(a.astype(F32)[:, None, None], (h, 1, LANES))
    dspec = pl.BlockSpec((h, 1, LANES), lambda bi, t: (0, 0, 0))
    return pl.pallas_call(
        _retention_kernel,
        out_shape=[jax.ShapeDtypeStruct((rows.m, RET_W), F32)] * 2,
        grid=(rows.batch, n),
        in_specs=[tok(blk_f, 0), tok(blk_f, 1), tok(blk_f, 2), tab(pos_f), tab(pos_f),
                  tok(blk_b, 0), tok(blk_b, 1), tok(blk_b, 2), tab(pos_b), tab(pos_b), dspec, dspec],
        out_specs=[tok(blk_f, 0), tok(blk_b, 0)],
        scratch_shapes=[pltpu.VMEM((2 * h, RET_HD, RET_HD), F32)],
        compiler_params=_cparams(2),
        name="retention",
    )(proj, proj, proj, cos, sin_signed, proj, proj, proj, cos, sin_signed, dec(dec_f), dec(dec_b))


def _na_qk_norm(x, w):
    ms = _group_mean(x * x, NA_HD)
    return x * lax.rsqrt(ms + NORM_EPS) * w


def _softmax_pv(parts):
    m = functools.reduce(jnp.maximum, [jnp.max(s, axis=-1, keepdims=True) for s, _ in parts])
    ps = [jnp.exp(s - m) for s, _ in parts]
    den = functools.reduce(jnp.add, [jnp.sum(p, axis=-1, keepdims=True) for p in ps])
    num = functools.reduce(jnp.add, [_dot(p, v) for p, (_, v) in zip(ps, parts)])
    return num / den


def _na_kernel(q_ref, k_ref, v_ref, kc_ref, vc_ref, qn_ref, kn_ref, bias_ref, o_ref, qs_ref, ks_ref, vs_ref,
               kcs_ref, *, rows):
    scale = NA_HD ** -0.5
    qs_ref[...] = (_na_qk_norm(q_ref[...], qn_ref[...]) * scale).astype(BF16)
    ks_ref[...] = _na_qk_norm(k_ref[...], kn_ref[...]).astype(BF16)
    vs_ref[...] = v_ref[...].astype(BF16)
    kcs_ref[...] = _na_qk_norm(kc_ref[...], kn_ref[...]).astype(BF16)
    lane = lax.broadcasted_iota(jnp.int32, (GRID_W, LANES), 1)
    first = lane < NA_HD
    vc = vc_ref[...]
    kc = kcs_ref[...]

    nq = 2 * GRID_W

    def body(rb, carry):
        qs, kw, vw, bw = [], [], [], []
        for i in range(NA_ROWS_PER_STEP):
            r = rb * NA_ROWS_PER_STEP + i
            r0 = jnp.clip(r - NA_KR // 2, 0, rows - NA_KR)
            q_r = qs_ref[pl.ds(pl.multiple_of(r * GRID_W, GRID_W), GRID_W), :]
            zero = jnp.zeros_like(q_r)
            qs.append(jnp.concatenate([jnp.where(first, q_r, zero), jnp.where(first, zero, q_r)], axis=0))
            win = pl.ds(pl.multiple_of(r0 * GRID_W, GRID_W), NA_KR * GRID_W)
            kw.append(ks_ref[win, :])
            vw.append(vs_ref[win, :])
            bw.append(bias_ref[0, r0 - r + (NA_KR - 1)])
        q = jnp.stack(qs, axis=0)
        s_w = _bdot_nt(q, jnp.stack(kw, axis=0)) + jnp.stack(bw, axis=0)
        q_flat = q.reshape(NA_ROWS_PER_STEP * nq, LANES)
        s_c = _dot_nt(q_flat, kc).reshape(NA_ROWS_PER_STEP, nq, kc.shape[0])
        m = jnp.maximum(jnp.max(s_w, axis=-1, keepdims=True), jnp.max(s_c, axis=-1, keepdims=True))
        p_w = jnp.exp(s_w - m)
        p_c = jnp.exp(s_c - m)
        den = jnp.sum(p_w, axis=-1, keepdims=True) + jnp.sum(p_c, axis=-1, keepdims=True)
        num = _bdot(p_w, jnp.stack(vw, axis=0)) + _dot(p_c.reshape(NA_ROWS_PER_STEP * nq, kc.shape[0]), vc).reshape(
            NA_ROWS_PER_STEP, nq, LANES)
        out = num / den
        for i in range(NA_ROWS_PER_STEP):
            r = rb * NA_ROWS_PER_STEP + i
            o_ref[pl.ds(pl.multiple_of(r * GRID_W, GRID_W), GRID_W), :] = jnp.where(
                first, out[i, :GRID_W], out[i, GRID_W:])
        return carry

    lax.fori_loop(0, rows // NA_ROWS_PER_STEP, body, 0)


NA_ROWS_PER_STEP = 4


def _na_bias_table(rpb):
    cols = np.arange(GRID_W)
    start = np.clip(cols - NA_KC // 2, 0, GRID_W - NA_KC)
    kcol = np.arange(GRID_W)
    inside = (kcol[None, :] >= start[:, None]) & (kcol[None, :] < start[:, None] + NA_KC)
    col_off = np.clip(kcol[None, :] - cols[:, None] + (NA_KC - 1), 0, 2 * NA_KC - 2)
    row_off = np.arange(NA_KR)[:, None] + np.arange(NA_KR)[None, :]
    tab = rpb[:, row_off][:, :, :, col_off]
    tab = jnp.where(inside[None, None, None], tab, NEG_BIG)
    tab = jnp.transpose(tab, (0, 1, 3, 2, 4))
    tab = tab.reshape(rpb.shape[0] // 2, 2, NA_KR, GRID_W, NA_KR * GRID_W).transpose(0, 2, 1, 3, 4)
    return tab.reshape(rpb.shape[0] // 2, NA_KR, 2 * GRID_W, NA_KR * GRID_W).astype(F32)


def _na_latent(proj, qn_w, kn_w, bias, rows, col0):
    seq, ctx_len = rows.seq, rows.ctx_len
    grid_rows = seq // GRID_W
    pairs = NA_W // LANES
    cb = col0 // LANES
    cblk = rows.ml // ctx_len
    tile2 = lambda a: jnp.tile(a.astype(F32), 2).reshape(1, LANES)
    lat = lambda off: pl.BlockSpec((seq, LANES), lambda bi, p: (bi, cb + off * pairs + p))
    ctx = lambda off: pl.BlockSpec((ctx_len, LANES), lambda bi, p: (cblk + bi, cb + off * pairs + p))
    return pl.pallas_call(
        functools.partial(_na_kernel, rows=grid_rows),
        out_shape=jax.ShapeDtypeStruct((rows.ml, NA_W), F32),
        grid=(rows.batch, pairs),
        in_specs=[lat(0), lat(1), lat(2), ctx(1), ctx(2),
                  pl.BlockSpec((1, LANES), lambda bi, p: (0, 0)), pl.BlockSpec((1, LANES), lambda bi, p: (0, 0)),
                  pl.BlockSpec((1, NA_KR, 2 * GRID_W, NA_KR * GRID_W), lambda bi, p: (p, 0, 0, 0))],
        out_specs=pl.BlockSpec((seq, LANES), lambda bi, p: (bi, p)),
        scratch_shapes=[pltpu.VMEM((seq, LANES), BF16)] * 3 + [pltpu.VMEM((ctx_len, LANES), BF16)],
        compiler_params=_cparams(2),
        name="na_latent",
    )(proj, proj, proj, proj, proj, tile2(qn_w), tile2(kn_w), bias)


def _ctx_attn_kernel(q_ref, k_ref, v_ref, qn_ref, kn_ref, o_ref):
    scale = NA_HD ** -0.5
    q = (_na_qk_norm(q_ref[...], qn_ref[...]) * scale).astype(BF16)
    k = _na_qk_norm(k_ref[...], kn_ref[...]).astype(BF16)
    v = v_ref[...]
    lane = lax.broadcasted_iota(jnp.int32, q.shape, 1)
    first = lane < NA_HD
    outs = []
    for hh in range(2):
        qh = jnp.where(first if hh == 0 else jnp.logical_not(first), q, jnp.zeros_like(q))
        outs.append(_softmax_pv([(_dot_nt(qh, k), v)]))
    o_ref[...] = jnp.where(first, outs[0], outs[1])


def _ctx_attention(proj, qn_w, kn_w, rows, col0):
    ctx_len = rows.ctx_len
    pairs = NA_W // LANES
    cb = col0 // LANES
    cblk = rows.ml // ctx_len
    tile2 = lambda a: jnp.tile(a.astype(F32), 2).reshape(1, LANES)
    blk = lambda off: pl.BlockSpec((ctx_len, LANES), lambda bi, p: (cblk + bi, cb + off * pairs + p))
    return pl.pallas_call(
        _ctx_attn_kernel,
        out_shape=jax.ShapeDtypeStruct((rows.mc, NA_W), F32),
        grid=(rows.batch, pairs),
        in_specs=[blk(0), blk(1), blk(2),
                  pl.BlockSpec((1, LANES), lambda bi, p: (0, 0)), pl.BlockSpec((1, LANES), lambda bi, p: (0, 0))],
        out_specs=pl.BlockSpec((ctx_len, LANES), lambda bi, p: (bi, p)),
        compiler_params=_cparams(2),
        name="ctx_attention",
    )(proj, proj, proj, tile2(qn_w), tile2(kn_w))


def _ret_readout_kernel(yf_ref, yb_ref, g_ref, nal_ref, nac_ref, gnw_ref, o_ref, *, n_latent_tiles):
    y = yf_ref[...] + yb_ref[...]
    for hh in range(RET_HEADS):
        sl = slice(hh * RET_HD, (hh + 1) * RET_HD)
        yh = y[:, sl]
        yc = yh - jnp.mean(yh, axis=-1, keepdims=True)
        yn = yc * lax.rsqrt(jnp.mean(yc * yc, axis=-1, keepdims=True) + RET_GN_EPS)
        o_ref[:, sl] = (yn * gnw_ref[:, sl] * _silu(g_ref[:, sl])).astype(o_ref.dtype)
    is_latent = pl.program_id(0) < n_latent_tiles
    o_ref[:, RET_W:] = jnp.where(is_latent, nal_ref[...], nac_ref[...]).astype(o_ref.dtype)


def _ret_readout(yf, yb, proj, na_l, na_c, gn_w, rows, tm):
    nl = rows.ml // tm
    row = lambda i: (i, 0)
    return pl.pallas_call(
        functools.partial(_ret_readout_kernel, n_latent_tiles=nl),
        out_shape=jax.ShapeDtypeStruct((rows.m, RET_W + NA_W), BF16),
        grid=(rows.m // tm,),
        in_specs=[pl.BlockSpec((tm, RET_W), row), pl.BlockSpec((tm, RET_W), row),
                  pl.BlockSpec((tm, RET_W), lambda i: (i, 3)),
                  pl.BlockSpec((tm, NA_W), lambda i: (jnp.minimum(i, nl - 1), 0)),
                  pl.BlockSpec((tm, NA_W), lambda i: (jnp.maximum(i - nl, 0), 0)),
                  pl.BlockSpec((1, RET_W), lambda i: (0, 0))],
        out_specs=pl.BlockSpec((tm, RET_W + NA_W), row),
        compiler_params=_cparams(1),
        name="ret_readout",
    )(yf, yb, proj, na_l, na_c, gn_w.reshape(1, RET_W).astype(F32))


def _rwkv_mix_kernel(x_ref, xp_ref, xn_ref, nw_ref, sh_ref, sc_ref, mu_ref, *o_refs, seq, ctx_len, ml, tm):
    nw, sh, sc = nw_ref[...], sh_ref[0], sc_ref[0]
    h = _modnorm(x_ref[...], nw, sh, sc)
    hp = _modnorm(xp_ref[...], nw, sh, sc)[7:8]
    hn = _modnorm(xn_ref[...], nw, sh, sc)[0:1]
    row = lax.broadcasted_iota(jnp.int32, h.shape, 0)
    g = pl.program_id(0) * tm + row
    latent = g < ml
    pos = jnp.where(latent, _mod(g, seq), _mod(g, ctx_len))
    first = pos == 0
    last = pos == jnp.where(latent, seq - 1, ctx_len - 1)
    up = jnp.where(row == 0, hp, pltpu.roll(h, 1, 0))
    up = jnp.where(first, 0.0, up)
    dn = jnp.where(row == tm - 1, hn, pltpu.roll(h, tm - 1, 0))
    dn = jnp.where(last, 0.0, dn)
    xx = 0.5 * (up + dn) - h
    for s_i, o_ref in enumerate(o_refs):
        o_ref[...] = (h + xx * mu_ref[s_i:s_i + 1, :]).astype(o_ref.dtype)


def _rwkv_mix(x, nw, shift, scale, mu, *, rows, tm):
    m, d = x.shape
    n_slab = m // 8
    per = tm // 8
    gmap = lambda i: (rows.group_of_tile(tm)(i), 0, 0)
    return pl.pallas_call(
        functools.partial(_rwkv_mix_kernel, seq=rows.seq, ctx_len=rows.ctx_len, ml=rows.ml, tm=tm),
        out_shape=[jax.ShapeDtypeStruct((m, d), BF16)] * 6,
        grid=(m // tm,),
        in_specs=[pl.BlockSpec((tm, d), lambda i: (i, 0)),
                  pl.BlockSpec((8, d), lambda i: (jnp.maximum(i * per - 1, 0), 0)),
                  pl.BlockSpec((8, d), lambda i: (jnp.minimum((i + 1) * per, n_slab - 1), 0)),
                  pl.BlockSpec((1, d), lambda i: (0, 0)),
                  pl.BlockSpec((1, 1, d), gmap), pl.BlockSpec((1, 1, d), gmap),
                  pl.BlockSpec((6, d), lambda i: (0, 0))],
        out_specs=[pl.BlockSpec((tm, d), lambda i: (i, 0))] * 6,
        compiler_params=_cparams(1),
        name="rwkv_mix",
    )(x, x, x, nw.reshape(1, d), shift, scale, mu)


def _rwkv_post_kernel(r_ref, k_ref, v_ref, xw_ref, xa_ref, xg_ref, w1_ref, w2_ref, a1_ref, a2_ref, g1_ref, g2_ref,
                      w0_ref, a0_ref, kk_ref, ka_ref, rk_ref,
                      rt0, kt0, bt0, at0, wc0, rt1, kt1, bt1, at1, wc1, vb_ref, g_ref, bonus_ref, *, tm):
    c = RWKV_CHUNK
    n_pairs = r_ref.shape[1] // LANES
    r, k, v = r_ref[...], k_ref[...], v_ref[...]
    lw_low = jnp.tanh(jnp.dot(xw_ref[...], w1_ref[...], preferred_element_type=F32))
    a_low = jnp.dot(xa_ref[...], a1_ref[...], preferred_element_type=F32)
    g_low = _sigmoid(jnp.dot(xg_ref[...], g1_ref[...], preferred_element_type=F32))
    g_ref[...] = _dot(g_low, g2_ref[...]).astype(g_ref.dtype)

    def put(o_ref, val):
        for p in range(n_pairs):
            o_ref[p] = val[:, p * LANES:(p + 1) * LANES].astype(o_ref.dtype)

    put(vb_ref, v)
    kk = k * kk_ref[...]
    nrm = jnp.sqrt(_group_mean(kk * kk, RWKV_HD) * float(RWKV_HD))
    kk = kk / jnp.maximum(nrm, 1e-12)

    tr = lax.broadcasted_iota(jnp.int32, (c, c), 0)
    tc = lax.broadcasted_iota(jnp.int32, (c, c), 1)
    tri = [(tr >= tc).astype(BF16), (tr <= tc).astype(BF16)]
    lora = w2_ref.shape[1]
    coeff_src = jnp.zeros_like(r)
    outs = [(rt0, kt0, bt0, at0, wc0), (rt1, kt1, bt1, at1, wc1)]
    for z in range(2):
        rt_ref, kt_ref, bt_ref, at_ref, wc_ref = outs[z]
        w_lora = _dot(lw_low[:, z * lora:(z + 1) * lora], w2_ref[z])
        log_w = -_softplus(-(w0_ref[z:z + 1, :] + w_lora)) - 0.5
        lw = -jnp.exp(log_w)
        a_z = _sigmoid(a0_ref[z:z + 1, :] + _dot(a_low[:, z * lora:(z + 1) * lora], a2_ref[z]))
        k_dir = k * (1.0 + (a_z - 1.0) * ka_ref[...])
        coeff_src = coeff_src + r * k_dir * rk_ref[...]
        cum = jnp.concatenate([_dot_exact_lhs(tri[z], lw[ci * c:(ci + 1) * c]) for ci in range(tm // c)], axis=0)
        e_pos = jnp.exp(cum)
        e_neg = jnp.exp(-cum)
        put(rt_ref, r * e_pos)
        put(kt_ref, k_dir * e_neg)
        put(bt_ref, kk * a_z * e_neg)
        put(at_ref, -kk * jnp.exp(cum - lw))
        for ci in range(tm // c):
            end = ci * c + (c - 1 if z == 0 else 0)
            for p in range(n_pairs):
                wc_ref[p, ci] = e_pos[end:end + 1, p * LANES:(p + 1) * LANES]
    coeff = _group_mean(coeff_src, RWKV_HD) * float(RWKV_HD)
    bonus_ref[...] = coeff * v


def _rwkv_post(r, k, v, xw, xa, xg, p, *, tm):
    m, d = r.shape
    c = RWKV_CHUNK
    lora = p["w2"].shape[1]
    glora = p["g1"].shape[1]
    n_pairs = d // LANES
    pair = jax.ShapeDtypeStruct((n_pairs, m, LANES), BF16)
    chunk = jax.ShapeDtypeStruct((n_pairs, m // c, 1, LANES), F32)
    row = lambda i: (i, 0)
    full2 = lambda i: (0, 0)
    full3 = lambda i: (0, 0, 0)
    tspec = pl.BlockSpec((tm, d), row)
    pspec = pl.BlockSpec((n_pairs, tm, LANES), lambda i: (0, i, 0))
    cspec = pl.BlockSpec((n_pairs, tm // c, 1, LANES), lambda i: (0, i, 0, 0))
    return pl.pallas_call(
        functools.partial(_rwkv_post_kernel, tm=tm),
        out_shape=[pair] * 4 + [chunk] + [pair] * 4 + [chunk]
        + [pair, jax.ShapeDtypeStruct((m, d), BF16), jax.ShapeDtypeStruct((m, d), F32)],
        grid=(m // tm,),
        in_specs=[tspec] * 6
        + [pl.BlockSpec((d, 2 * lora), full2), pl.BlockSpec((2, lora, d), full3),
           pl.BlockSpec((d, 2 * lora), full2), pl.BlockSpec((2, lora, d), full3),
           pl.BlockSpec((d, glora), full2), pl.BlockSpec((glora, d), full2),
           pl.BlockSpec((2, d), full2), pl.BlockSpec((2, d), full2)]
        + [pl.BlockSpec((1, d), full2)] * 3,
        out_specs=[pspec] * 4 + [cspec] + [pspec] * 4 + [cspec] + [pspec, tspec, tspec],
        compiler_params=_cparams(1),
        name="rwkv_post",
    )(r, k, v, xw, xa, xg, p["w1"], p["w2"], p["a1"], p["a2"], p["g1"], p["g2"], p["w0"], p["a0"],
      p["k_k"], p["k_a"], p["r_k"])


def _bdot(a, b):
    return jnp.einsum("ucd,ude->uce", a.astype(BF16), b.astype(BF16), preferred_element_type=F32)


def _bdot_nt(a, b):
    return jnp.einsum("ucd,usd->ucs", a.astype(BF16), b.astype(BF16), preferred_element_type=F32)


def _bdot_tn(a, b):
    return jnp.einsum("uce,ucd->ued", a.astype(BF16), b.astype(BF16), preferred_element_type=F32)


def _scan_chunks(rt, kt, bt, at, v, wc, s0, n_forward):
    c = RWKV_CHUNK
    n_units = rt.shape[0]
    lane = lax.broadcasted_iota(jnp.int32, (n_units, c, LANES), 2)
    row = lax.broadcasted_iota(jnp.int32, (n_units, c, LANES), 1)
    unit = lax.broadcasted_iota(jnp.int32, (n_units, c, LANES), 0)
    head0 = lane < RWKV_HD
    src = _mod(lane, RWKV_HD)
    ahead = jnp.where(unit < n_forward, row - src, src - row)
    strict = ahead > 0
    incl = ahead >= 0

    def dup(x):
        zero = jnp.zeros_like(x)
        h0 = head0[:, :x.shape[1]]
        return jnp.concatenate([jnp.where(h0, x, zero), jnp.where(h0, zero, x)], axis=1)

    ar = jnp.concatenate([at, rt], axis=1)
    mb = _bdot_nt(ar, dup(bt))
    mk = _bdot_nt(ar, dup(kt))
    p_ab = jnp.where(strict, mb[:, :c], 0.0)
    p_rb = jnp.where(incl, mb[:, c:], 0.0)
    p_ak = jnp.where(strict, mk[:, :c], 0.0)
    p_rk = jnp.where(incl, mk[:, c:], 0.0)
    vd = dup(v)
    rhs = _bdot_nt(at, s0) + _bdot(p_ak, vd)

    powers = [p_ab.astype(BF16)]
    for _ in range(int(np.log2(c)) - 1):
        powers.append(_bdot(powers[-1], dup(powers[-1])).astype(BF16))

    def apply_inverse(x):
        for pw in powers:
            x = x + _bdot(pw, dup(x))
        return x

    u = apply_inverse(rhs)
    ud = dup(u)
    ph, pl_ = _split2(p_ab)
    uh, ul = _split2(ud)
    resid = (rhs - u) + (_bdot(ph, uh) + (_bdot(ph, ul) + _bdot(pl_, uh)))
    u = u + apply_inverse(resid)

    y = _bdot_nt(rt, s0) + _bdot(jnp.concatenate([p_rb, p_rk], axis=2), jnp.concatenate([dup(u), vd], axis=1))
    upd = _bdot_tn(jnp.concatenate([u.astype(BF16), v], axis=1), jnp.concatenate([bt, kt], axis=1))
    er = lax.broadcasted_iota(jnp.int32, (1, LANES, LANES), 1) < RWKV_HD
    ec = lax.broadcasted_iota(jnp.int32, (1, LANES, LANES), 2) < RWKV_HD
    s1 = (s0 + jnp.where(er == ec, upd, 0.0)) * wc
    return y, s1


def _rwkv_scan_kernel(rtf, ktf, btf, atf, vf, wcf, rtb, ktb, btb, atb, vb, wcb, yf_ref, yb_ref, s_ref):
    @pl.when(pl.program_id(1) == 0)
    def _():
        s_ref[...] = jnp.zeros_like(s_ref)

    n_pairs = rtf.shape[0]
    both = lambda f, b: jnp.concatenate([f[...], b[...]], axis=0)
    wc = jnp.concatenate([wcf[:, 0], wcb[:, 0]], axis=0)
    y, s1 = _scan_chunks(both(rtf, rtb), both(ktf, ktb), both(btf, btb), both(atf, atb), both(vf, vb), wc,
                         s_ref[...], n_pairs)
    s_ref[...] = s1
    yf_ref[...] = y[:n_pairs]
    yb_ref[...] = y[n_pairs:]


def _rwkv_scan(feats_f, feats_b, v, rows):
    n_pairs, m, _ = v.shape
    c = RWKV_CHUNK
    d = n_pairs * LANES
    nchunks = (rows.seq + rows.ctx_len) // c
    blk_f, _ = rows.chunk_block(c, False)
    blk_b, _ = rows.chunk_block(c, True)

    def specs(blk):
        tok = pl.BlockSpec((n_pairs, c, LANES), lambda bi, t: (0, blk(bi, t), 0))
        return tok, pl.BlockSpec((n_pairs, 1, 1, LANES), lambda bi, t: (0, blk(bi, t), 0, 0))

    tok_f, wc_f = specs(blk_f)
    tok_b, wc_b = specs(blk_b)
    return pl.pallas_call(
        _rwkv_scan_kernel,
        out_shape=[jax.ShapeDtypeStruct((n_pairs, m, LANES), F32)] * 2,
        grid=(rows.batch, nchunks),
        in_specs=[tok_f] * 5 + [wc_f] + [tok_b] * 5 + [wc_b],
        out_specs=[tok_f, tok_b],
        scratch_shapes=[pltpu.VMEM((2 * n_pairs, LANES, LANES), F32)],
        compiler_params=_cparams(2),
        name="rwkv_scan",
    )(*feats_f[:4], v, feats_f[4], *feats_b[:4], v, feats_b[4])


def _rwkv_readout_kernel(yf_ref, yb_ref, bonus_ref, g_ref, lnw_ref, lnb_ref, o_ref):
    y = jnp.concatenate([yf_ref[p] + yb_ref[p] for p in range(yf_ref.shape[0])], axis=1)
    yc = y - _group_mean(y, RWKV_HD)
    yn = yc * lax.rsqrt(_group_mean(yc * yc, RWKV_HD) + RWKV_GN_EPS) * lnw_ref[...] + lnb_ref[...]
    o_ref[...] = ((yn + bonus_ref[...]) * g_ref[...].astype(F32)).astype(o_ref.dtype)


def _rwkv_readout(yf, yb, bonus, g, ln_w, ln_b, rows, tm):
    n_pairs = yf.shape[0]
    d = n_pairs * LANES
    pspec = pl.BlockSpec((n_pairs, tm, LANES), lambda i: (0, i, 0))
    tspec = pl.BlockSpec((tm, d), lambda i: (i, 0))
    vspec = pl.BlockSpec((1, d), lambda i: (0, 0))
    return pl.pallas_call(
        _rwkv_readout_kernel,
        out_shape=jax.ShapeDtypeStruct((rows.ml, d), BF16),
        grid=(rows.ml // tm,),
        in_specs=[pspec, pspec, tspec, tspec, vspec, vspec],
        out_specs=tspec,
        compiler_params=_cparams(1),
        name="rwkv_readout",
    )(yf, yb, bonus, g, ln_w.reshape(1, d).astype(F32), ln_b.reshape(1, d).astype(F32))


def _router_kernel(x_ref, nw_ref, sh_ref, sc_ref, w_ref, o_ref, *, n_experts):
    h = _modnorm(x_ref[...], nw_ref[...], sh_ref[0], sc_ref[0])
    logits = _dot_hi(h, w_ref[...])
    lane = lax.broadcasted_iota(jnp.int32, logits.shape, 1).astype(F32)
    logits = jnp.where(lane < n_experts, logits, -jnp.inf)
    v1 = jnp.max(logits, axis=-1, keepdims=True)
    i1 = jnp.min(jnp.where(logits == v1, lane, float(LANES)), axis=-1, keepdims=True)
    rest = jnp.where(lane == i1, -jnp.inf, logits)
    v2 = jnp.max(rest, axis=-1, keepdims=True)
    i2 = jnp.min(jnp.where(rest == v2, lane, float(LANES)), axis=-1, keepdims=True)
    e2 = jnp.exp(v2 - v1)
    den = 1.0 + e2
    o_ref[...] = jnp.where(lane == i1, 1.0 / den, 0.0) + jnp.where(lane == i2, e2 / den, 0.0)


def _router(x, nw, shift, scale, router, *, tm, group_of_tile):
    m, d = x.shape
    n_experts = router.shape[1]
    w = jnp.zeros((d, LANES), F32).at[:, :n_experts].set(router.astype(F32))
    gmap = lambda i: (group_of_tile(i), 0, 0)
    return pl.pallas_call(
        functools.partial(_router_kernel, n_experts=n_experts),
        out_shape=jax.ShapeDtypeStruct((m, LANES), F32),
        grid=(m // tm,),
        in_specs=[pl.BlockSpec((tm, d), lambda i: (i, 0)), pl.BlockSpec((1, d), lambda i: (0, 0)),
                  pl.BlockSpec((1, 1, d), gmap), pl.BlockSpec((1, 1, d), gmap),
                  pl.BlockSpec((d, LANES), lambda i: (0, 0))],
        out_specs=pl.BlockSpec((tm, LANES), lambda i: (i, 0)),
        compiler_params=_cparams(1),
        name="router",
    )(x, nw.reshape(1, d), shift, scale, w)


MOE_TILE = 512


def _route_plan(gates, n_experts, tg):
    m = gates.shape[0]
    g = gates[:, :n_experts]
    sel = g > 0.0
    seli = sel.astype(jnp.int32)
    slot = jnp.cumsum(seli, axis=1) - 1
    rank = jnp.cumsum(seli, axis=0) - 1
    counts = jnp.sum(seli, axis=0)
    padded = ((counts + tg - 1) // tg) * tg
    ends = jnp.cumsum(padded)
    dest = (ends - padded)[None, :] + rank
    n_rows = TOP_K * m + n_experts * tg
    n_tiles = n_rows // tg
    first = sel & (slot == 0)
    second = sel & (slot == 1)
    pick = lambda msk, val: jnp.sum(jnp.where(msk, val, 0), axis=1)
    d1 = pick(first, dest)
    has2 = jnp.any(second, axis=1)
    d2 = pick(second, dest)
    w1 = pick(first, g)
    w2 = pick(second, g)
    tile_expert = jnp.minimum(jnp.sum((jnp.arange(n_tiles, dtype=jnp.int32) * tg)[:, None] >= ends[None, :], axis=1),
                              n_experts - 1).astype(jnp.int32)
    n_valid = (ends[-1] // tg).astype(jnp.int32).reshape(1)
    wts = jnp.zeros((m, LANES), F32).at[:, 0].set(w1).at[:, 1].set(w2)
    i32 = lambda a: a.astype(jnp.int32)
    return (n_rows, tile_expert, n_valid, i32(d1), i32(jnp.where(has2, d2, n_rows)), i32(jnp.where(has2, d2, d1)),
            wts)


def _slab_rows(ref3, val):
    for s in range(ref3.shape[1]):
        ref3[:, s, :] = val[:, s * LANES:(s + 1) * LANES].astype(ref3.dtype)


def _unslab_rows(ref3):
    return jnp.concatenate([ref3[:, s, :] for s in range(ref3.shape[1])], axis=1)


def _moe_scatter_kernel(d1_ref, d2_ref, x_ref, nw_ref, sh_ref, sc_ref, a0_hbm, a_hbm, h_ref, sem, *, tm):
    del a0_hbm
    _slab_rows(h_ref, _modnorm(x_ref[...], nw_ref[...], sh_ref[0], sc_ref[0]))

    def row_copy(r, dst):
        return pltpu.make_async_copy(h_ref.at[pl.ds(r, 1)], a_hbm.at[pl.ds(dst, 1)], sem)

    def issue(r, carry):
        row_copy(r, d1_ref[0, 0, r]).start()
        row_copy(r, d2_ref[0, 0, r]).start()
        return carry

    def drain(r, carry):
        row_copy(r, 0).wait()
        row_copy(r, 0).wait()
        return carry

    lax.fori_loop(0, tm, issue, 0)
    lax.fori_loop(0, tm, drain, 0)


def _moe_scatter(x, nw, shift, scale, d1, d2, n_rows, *, tm, tg, group_of_tile):
    m, d = x.shape
    slabs = d // LANES
    idx = lambda a: a.reshape(m // tm, 1, tm)
    ispec = pl.BlockSpec((1, 1, tm), lambda i: (i, 0, 0), memory_space=pltpu.SMEM)
    gmap = lambda i: (group_of_tile(i), 0, 0)
    zeros = jnp.zeros((n_rows + tg, slabs, LANES), F32)
    return pl.pallas_call(
        functools.partial(_moe_scatter_kernel, tm=tm),
        out_shape=jax.ShapeDtypeStruct(zeros.shape, F32),
        grid=(m // tm,),
        in_specs=[ispec, ispec, pl.BlockSpec((tm, d), lambda i: (i, 0)), pl.BlockSpec((1, d), lambda i: (0, 0)),
                  pl.BlockSpec((1, 1, d), gmap), pl.BlockSpec((1, 1, d), gmap), pl.BlockSpec(memory_space=pl.ANY)],
        out_specs=pl.BlockSpec(memory_space=pl.ANY),
        scratch_shapes=[pltpu.VMEM((tm, slabs, LANES), F32), pltpu.SemaphoreType.DMA],
        input_output_aliases={6: 0},
        compiler_params=_cparams(1),
        name="moe_scatter",
    )(idx(d1), idx(d2), x, nw.reshape(1, d), shift, scale, zeros)


def _grouped_swiglu_kernel(te_ref, nv_ref, a_ref, wg_ref, wu_ref, o_ref, ab_ref):
    t = pl.program_id(0)

    @pl.when(jnp.logical_and(t < nv_ref[0], pl.program_id(1) == 0))
    def _():
        ab_ref[...] = _unslab_rows(a_ref).astype(BF16)

    @pl.when(t < nv_ref[0])
    def _():
        a = ab_ref[...]
        gate = jnp.dot(a, wg_ref[0], preferred_element_type=F32)
        up = jnp.dot(a, wu_ref[0], preferred_element_type=F32)
        o_ref[...] = (_silu(gate) * up).astype(o_ref.dtype)

    @pl.when(t >= nv_ref[0])
    def _():
        o_ref[...] = jnp.zeros_like(o_ref)


def _grouped_swiglu(a, w13, tile_expert, n_valid, p_rows, *, tg, tn):
    slabs = a.shape[1]
    k = slabs * LANES
    e_hid = w13.shape[2] // 2
    nj = e_hid // tn
    tile = lambda t, nv: jnp.minimum(t, nv[0] - 1)
    return pl.pallas_call(
        _grouped_swiglu_kernel,
        out_shape=jax.ShapeDtypeStruct((p_rows, e_hid), BF16),
        grid_spec=pltpu.PrefetchScalarGridSpec(
            num_scalar_prefetch=2,
            grid=(p_rows // tg, nj),
            in_specs=[pl.BlockSpec((tg, slabs, LANES), lambda t, j, te, nv: (tile(t, nv), 0, 0)),
                      pl.BlockSpec((1, k, tn), lambda t, j, te, nv: (te[tile(t, nv)], 0, j)),
                      pl.BlockSpec((1, k, tn), lambda t, j, te, nv: (te[tile(t, nv)], 0, j + nj))],
            out_specs=pl.BlockSpec((tg, tn), lambda t, j, te, nv: (t, j)),
            scratch_shapes=[pltpu.VMEM((tg, k), BF16)],
        ),
        compiler_params=_cparams(2),
        name="moe_up",
    )(tile_expert, n_valid, a, w13, w13)


def _grouped_mm_kernel(te_ref, nv_ref, a_ref, w_ref, o_ref):
    @pl.when(pl.program_id(0) < nv_ref[0])
    def _():
        _slab_rows(o_ref, jnp.dot(a_ref[...], w_ref[0], preferred_element_type=F32))

    @pl.when(pl.program_id(0) >= nv_ref[0])
    def _():
        o_ref[...] = jnp.zeros_like(o_ref)


def _grouped_mm(a, w, tile_expert, n_valid, *, tg):
    p_rows, k = a.shape
    n = w.shape[2]
    slabs = n // LANES
    tile = lambda t, nv: jnp.minimum(t, nv[0] - 1)
    return pl.pallas_call(
        _grouped_mm_kernel,
        out_shape=jax.ShapeDtypeStruct((p_rows, slabs, LANES), F32),
        grid_spec=pltpu.PrefetchScalarGridSpec(
            num_scalar_prefetch=2,
            grid=(p_rows // tg,),
            in_specs=[pl.BlockSpec((tg, k), lambda t, te, nv: (tile(t, nv), 0)),
                      pl.BlockSpec((1, k, n), lambda t, te, nv: (te[tile(t, nv)], 0, 0))],
            out_specs=pl.BlockSpec((tg, slabs, LANES), lambda t, te, nv: (t, 0, 0)),
        ),
        compiler_params=_cparams(1),
        name="moe_down",
    )(tile_expert, n_valid, a, w)


def _combine_kernel(d1_ref, d2_ref, y_hbm, wts_ref, x_ref, gate_ref, o_ref, buf_ref, sem, *, tm):
    def row_copy(which, r, s):
        return pltpu.make_async_copy(y_hbm.at[pl.ds(s, 1)], buf_ref.at[which, pl.ds(r, 1)], sem)

    def issue(r, carry):
        row_copy(0, r, d1_ref[0, 0, r]).start()
        row_copy(1, r, d2_ref[0, 0, r]).start()
        return carry

    def drain(r, carry):
        row_copy(0, r, 0).wait()
        row_copy(1, r, 0).wait()
        return carry

    lax.fori_loop(0, tm, issue, 0)
    lax.fori_loop(0, tm, drain, 0)
    lane = lax.broadcasted_iota(jnp.int32, wts_ref.shape, 1)
    w1 = jnp.sum(jnp.where(lane == 0, wts_ref[...], 0.0), axis=-1, keepdims=True)
    w2 = jnp.sum(jnp.where(lane == 1, wts_ref[...], 0.0), axis=-1, keepdims=True)
    y = w1 * _unslab_rows(buf_ref.at[0]) + w2 * _unslab_rows(buf_ref.at[1])
    o_ref[...] = x_ref[...] + gate_ref[0] * y


def _combine(y, d1, d2, wts, x, gate, *, tm, group_of_tile):
    m, d = x.shape
    idx = lambda a: a.reshape(m // tm, 1, tm)
    ispec = pl.BlockSpec((1, 1, tm), lambda i: (i, 0, 0), memory_space=pltpu.SMEM)
    return pl.pallas_call(
        functools.partial(_combine_kernel, tm=tm),
        out_shape=jax.ShapeDtypeStruct((m, d), F32),
        grid=(m // tm,),
        in_specs=[ispec, ispec, pl.BlockSpec(memory_space=pl.ANY),
                  pl.BlockSpec((tm, LANES), lambda i: (i, 0)), pl.BlockSpec((tm, d), lambda i: (i, 0)),
                  pl.BlockSpec((1, 1, d), lambda i: (group_of_tile(i), 0, 0))],
        out_specs=pl.BlockSpec((tm, d), lambda i: (i, 0)),
        scratch_shapes=[pltpu.VMEM((2, tm, d // LANES, LANES), F32), pltpu.SemaphoreType.DMA],
        compiler_params=_cparams(1),
        name="moe_combine",
    )(idx(d1), idx(d2), y, wts, x, gate)


def _rope_tables(seq, ctx_len):
    half = RET_HD // 4
    freqs = ROPE_BASE ** (-np.arange(half, dtype=np.float64) / half)
    t = np.arange(seq)
    ang_r = (t // GRID_W)[:, None] * freqs[None, :]
    ang_c = (t % GRID_W)[:, None] * freqs[None, :]
    cos = np.concatenate([np.cos(ang_r)] * 2 + [np.cos(ang_c)] * 2, axis=1)
    sin = np.concatenate([-np.sin(ang_r), np.sin(ang_r), -np.sin(ang_c), np.sin(ang_c)], axis=1)
    cos = np.concatenate([np.ones((ctx_len, RET_HD)), cos], axis=0)
    sin = np.concatenate([np.zeros((ctx_len, RET_HD)), sin], axis=0)
    return jnp.asarray(cos, F32), jnp.asarray(sin, F32)


def _half_or_full(n):
    return n // 2 if (n // 2) % LANES == 0 else n


def kernel(x, c, ctx, c_ctx, ada_w, ada_b, norm_mix_w, norm_ffn_w, ev_w_in, ev_ret_decay_f, ev_ret_decay_b,
           ev_ret_gn_w, ev_na_qn_w, ev_na_kn_w, ev_na_rpb, ev_w_out, ev_ffn_w13, ev_ffn_w2, od_mu, od_w_rkv,
           od_w0, od_w1, od_w2, od_a0, od_a1, od_a2, od_g1, od_g2, od_k_k, od_k_a, od_r_k, od_ln_w, od_ln_b,
           od_w_o, od_router, od_moe_w13, od_moe_w2):
    batch, seq, d = x.shape
    ctx_len = ctx.shape[1]
    rows = _Rows(batch, seq, ctx_len)
    tm = 8 * RWKV_CHUNK
    grp = rows.group_of_tile(tm)
    tm_big = 2 * tm if (rows.mc % (2 * tm) == 0 and seq % (2 * tm) == 0) else tm
    grp_big = rows.group_of_tile(tm_big)
    xa = jnp.concatenate([x.reshape(rows.ml, d), ctx.reshape(rows.mc, d)], axis=0)

    n_mod = ((batch + 1 + 7) // 8) * 8
    c_rows = jnp.zeros((n_mod, d), F32).at[:batch].set(c).at[batch].set(c_ctx)
    mods = _ada(c_rows, ada_w, ada_b)
    mods = mods.reshape(mods.shape[0], n_mod, 6, 1, d).transpose(0, 2, 1, 3, 4)

    mod = mods[0]
    w_in = ev_w_in[0].astype(BF16)
    proj = _modnorm_mm(xa, norm_mix_w[0], mod[0], mod[1], w_in, group_of_tile=grp_big, tm=tm_big,
                       tn=_half_or_full(w_in.shape[1]), out_dtype=F32)
    cos, sin = _rope_tables(seq, ctx_len)
    y_f, y_b = _retention(proj, cos, sin, ev_ret_decay_f[0], ev_ret_decay_b[0], rows)
    bias = _na_bias_table(ev_na_rpb[0].astype(F32))
    na_l = _na_latent(proj, ev_na_qn_w[0], ev_na_kn_w[0], bias, rows, 4 * RET_W)
    na_c = _ctx_attention(proj, ev_na_qn_w[0], ev_na_kn_w[0], rows, 4 * RET_W)
    mix = _ret_readout(y_f, y_b, proj, na_l, na_c, ev_ret_gn_w[0], rows, tm)
    w_out = ev_w_out[0].astype(BF16)
    xa = _mm(mix, w_out, tm=tm, tn=d, tk=w_out.shape[0], res=xa, gate=mod[2], group_of_tile=grp)

    w13 = ev_ffn_w13[0].astype(BF16)
    w2 = ev_ffn_w2[0].astype(BF16)
    hid = w2.shape[0]
    hidden = _modnorm_mm(xa, norm_ffn_w[0], mod[3], mod[4], w13, group_of_tile=grp_big, tm=tm_big,
                         tn=_half_or_full(hid), out_dtype=BF16, swiglu=True)
    xa = _mm(hidden, w2, tm=tm, tn=d, tk=hid, res=xa, gate=mod[5], group_of_tile=grp)

    mod = mods[1]
    glora = od_g1.shape[-1]
    glora_p = ((glora + LANES - 1) // LANES) * LANES
    p = {
        "w1": jnp.concatenate([od_w1[0, 0], od_w1[0, 1]], axis=1).astype(BF16),
        "w2": od_w2[0].astype(BF16),
        "a1": jnp.concatenate([od_a1[0, 0], od_a1[0, 1]], axis=1).astype(BF16),
        "a2": od_a2[0].astype(BF16),
        "g1": jnp.zeros((d, glora_p), BF16).at[:, :glora].set(od_g1[0].astype(BF16)),
        "g2": jnp.zeros((glora_p, d), BF16).at[:glora].set(od_g2[0].astype(BF16)),
        "w0": od_w0[0].astype(F32), "a0": od_a0[0].astype(F32),
        "k_k": od_k_k[0].reshape(1, d).astype(F32), "k_a": od_k_a[0].reshape(1, d).astype(F32),
        "r_k": od_r_k[0].reshape(1, d).astype(F32),
    }
    w_rkv = od_w_rkv[0].astype(BF16)
    xm = _rwkv_mix(xa, norm_mix_w[1], mod[0], mod[1], od_mu[0].astype(F32), rows=rows, tm=tm)
    r, k, v = [_mm(xm[s_i], w_rkv[s_i], tm=tm, tn=d, tk=d) for s_i in range(3)]
    f = _rwkv_post(r, k, v, xm[3], xm[4], xm[5], p, tm=tm)
    y_f, y_b = _rwkv_scan(f[0:5], f[5:10], f[10], rows)
    mixed = _rwkv_readout(y_f, y_b, f[12], f[11], od_ln_w[0], od_ln_b[0], rows, tm)
    x_l = _mm(mixed, od_w_o[0].astype(BF16), tm=tm, tn=d, tk=d, res=xa, gate=mod[2], group_of_tile=grp)

    gates = _router(x_l, norm_ffn_w[1], mod[3], mod[4], od_router[0], tm=tm, group_of_tile=grp)
    n_e = od_router.shape[-1]
    n_rows, tile_expert, n_valid, d1, d2_scatter, d2_combine, wts = _route_plan(gates, n_e, MOE_TILE)
    tm_r = 256
    grp_r = rows.group_of_tile(tm_r)
    a_sorted = _moe_scatter(x_l, norm_ffn_w[1], mod[3], mod[4], d1, d2_scatter, n_rows, tm=tm_r, tg=MOE_TILE,
                            group_of_tile=grp_r)
    moe13 = od_moe_w13[0].astype(BF16)
    hid_sorted = _grouped_swiglu(a_sorted, moe13, tile_expert, n_valid, n_rows, tg=MOE_TILE,
                                 tn=_half_or_full(moe13.shape[2] // 2))
    y_sorted = _grouped_mm(hid_sorted, od_moe_w2[0].astype(BF16), tile_expert, n_valid, tg=MOE_TILE)
    x_l = _combine(y_sorted, d1, d2_combine, wts, x_l, mod[5], tm=tm_r, group_of_tile=grp_r)
    return x_l.reshape(batch, seq, d)
```

```python
import functools

import jax
import jax.numpy as jnp
import numpy as np
from jax import lax
from jax.experimental import pallas as pl
from jax.experimental.pallas import tpu as pltpu

F32 = jnp.float32
BF16 = jnp.bfloat16

LANES = 128
GRID_W = 64
RET_HEADS = 4
RET_HD = 128
RET_W = RET_HEADS * RET_HD
RET_CHUNK = 128
RET_GN_EPS = 1e-5
NA_HEADS = 8
NA_HD = 64
NA_W = NA_HEADS * NA_HD
NA_KR = 8
NA_KC = 16
RWKV_HD = 64
RWKV_GN_EPS = 64e-5
RWKV_CHUNK = 64
TOP_K = 2
ROPE_BASE = 10000.0
NORM_EPS = 1e-6
NEG_BIG = -1e30
VMEM_LIMIT = 56 * 1024 * 1024


def _cparams(n_axes):
    return pltpu.CompilerParams(dimension_semantics=("arbitrary",) * n_axes, vmem_limit_bytes=VMEM_LIMIT)


def _dot(a, b):
    return jnp.dot(a.astype(BF16), b.astype(BF16), preferred_element_type=F32)


def _dot_nt(a, b):
    return lax.dot_general(a.astype(BF16), b.astype(BF16), (((1,), (1,)), ((), ())), preferred_element_type=F32)


def _dot_tn(a, b):
    return lax.dot_general(a.astype(BF16), b.astype(BF16), (((0,), (0,)), ((), ())), preferred_element_type=F32)


def _split2(x):
    hi = x.astype(BF16)
    return hi, (x - hi.astype(F32)).astype(BF16)


def _split3(x):
    hi = x.astype(BF16)
    r1 = x - hi.astype(F32)
    mid = r1.astype(BF16)
    lo = (r1 - mid.astype(F32)).astype(BF16)
    return hi, mid, lo


def _dot_hi(a, b):
    ah, am, al = _split3(a)
    bh, bm, bl = _split3(b)
    d = functools.partial(jnp.dot, preferred_element_type=F32)
    return (d(ah, bh) + (d(ah, bm) + d(am, bh)) + (d(am, bm) + d(ah, bl) + d(al, bh)))


def _dot_x3(a, b):
    ah, al = _split2(a)
    bh, bl = _split2(b)
    d = functools.partial(jnp.dot, preferred_element_type=F32)
    return d(ah, bh) + (d(ah, bl) + d(al, bh))


def _dot_exact_rhs(a, b_bf16):
    ah, am, al = _split3(a)
    d = functools.partial(jnp.dot, preferred_element_type=F32)
    return d(ah, b_bf16) + d(am, b_bf16) + d(al, b_bf16)


def _dot_exact_lhs(a_bf16, b):
    bh, bm, bl = _split3(b)
    d = functools.partial(jnp.dot, preferred_element_type=F32)
    return d(a_bf16, bh) + d(a_bf16, bm) + d(a_bf16, bl)


def _mod(v, n):
    return (v & (n - 1)) if n & (n - 1) == 0 else v % n


def _sigmoid(x):
    return 1.0 / (1.0 + jnp.exp(-x))


def _silu(x):
    return x * _sigmoid(x)


def _softplus(x):
    return jnp.maximum(x, 0.0) + jnp.log(1.0 + jnp.exp(-jnp.abs(x)))


def _group_mean_mat(width, group):
    r = lax.broadcasted_iota(jnp.int32, (width, width), 0) // group
    c = lax.broadcasted_iota(jnp.int32, (width, width), 1) // group
    return jnp.where(r == c, 1.0 / group, 0.0).astype(BF16)


def _group_mean(x, group):
    g = _group_mean_mat(LANES, group)
    cols = [_dot_exact_rhs(x[:, c:c + LANES], g) for c in range(0, x.shape[1], LANES)]
    return cols[0] if len(cols) == 1 else jnp.concatenate(cols, axis=1)


def _modnorm(x, nw, shift, scale):
    ms = jnp.mean(x * x, axis=-1, keepdims=True)
    y = x * lax.rsqrt(ms + NORM_EPS) * nw
    return y * (1.0 + scale) + shift


class _Rows:
    def __init__(self, batch, seq, ctx_len):
        self.batch, self.seq, self.ctx_len = batch, seq, ctx_len
        self.ml, self.mc = batch * seq, batch * ctx_len
        self.m = self.ml + self.mc

    def group_of_tile(self, tm):
        return lambda i: jnp.minimum((i * tm) // self.seq, self.batch)

    def chunk_block(self, chunk, backward):
        ncc, nlc = self.ctx_len // chunk, self.seq // chunk

        def block(b, t):
            if backward:
                t = jnp.where(t < ncc, ncc - 1 - t, nlc + 2 * ncc - 1 - t)
            return jnp.where(t < ncc, (self.ml + b * self.ctx_len) // chunk + t, (b * self.seq) // chunk + t - ncc)

        def position(t):
            if backward:
                t = jnp.where(t < ncc, ncc - 1 - t, nlc + 2 * ncc - 1 - t)
            return t

        return block, position


def _ada_kernel(c_ref, w_ref, b_ref, o_ref):
    s = _silu(c_ref[...])
    o_ref[0] = _dot_hi(s, w_ref[0]) + b_ref[0]


def _ada(c_rows, ada_w, ada_b):
    depth, d, n = ada_w.shape
    rows = c_rows.shape[0]
    tn = 1536
    return pl.pallas_call(
        _ada_kernel,
        out_shape=jax.ShapeDtypeStruct((depth, rows, n), F32),
        grid=(depth, n // tn),
        in_specs=[
            pl.BlockSpec((rows, d), lambda l, j: (0, 0)),
            pl.BlockSpec((1, d, tn), lambda l, j: (l, 0, j)),
            pl.BlockSpec((1, 1, tn), lambda l, j: (l, 0, j)),
        ],
        out_specs=pl.BlockSpec((1, rows, tn), lambda l, j: (l, 0, j)),
        compiler_params=_cparams(2),
        name="ada_mod",
    )(c_rows, ada_w, ada_b.reshape(depth, 1, n))


def _modnorm_mm_kernel(*refs, swiglu, gated):
    x_ref, nw_ref, sh_ref, sc_ref = refs[:4]
    pos = 4
    w_refs = refs[pos:pos + (2 if swiglu else 1)]
    pos += len(w_refs)
    gates_ref = refs[pos] if gated else None
    pos += 1 if gated else 0
    o_ref, h_ref = refs[pos], refs[pos + 1]

    first = pl.program_id(1) == 0
    if gated:
        first = jnp.logical_and(first, pl.program_id(2) == 0)

    @pl.when(first)
    def _():
        h_ref[...] = _modnorm(x_ref[...], nw_ref[...], sh_ref[0], sc_ref[0]).astype(BF16)

    h = h_ref[...]
    w0 = w_refs[0][0] if gated else w_refs[0][...]
    acc = jnp.dot(h, w0, preferred_element_type=F32)
    if swiglu:
        w1 = w_refs[1][0] if gated else w_refs[1][...]
        acc = _silu(acc) * jnp.dot(h, w1, preferred_element_type=F32)
    if gated:
        e = pl.program_id(1)
        lane = lax.broadcasted_iota(jnp.int32, gates_ref.shape, 1)
        g = jnp.sum(jnp.where(lane == e, gates_ref[...], 0.0), axis=-1, keepdims=True)
        acc = acc * g
    o_ref[...] = acc.astype(o_ref.dtype)


def _modnorm_mm(x, nw, shift, scale, w, *, group_of_tile, tm, tn, out_dtype, swiglu=False, gates=None, rows=None):
    m = x.shape[0] if rows is None else rows
    k = x.shape[1]
    gated = gates is not None
    n_total = w.shape[-1]
    n_out = n_total // 2 if swiglu else n_total
    nj = n_out // tn
    if gated:
        n_e = w.shape[0]
        grid = (m // tm, n_e, nj)
        xmap = lambda i, e, j: (i, 0)
        cmap = lambda i, e, j: (0, 0)
        gmap = lambda i, e, j: (group_of_tile(i), 0, 0)
        wspecs = [pl.BlockSpec((1, k, tn), lambda i, e, j: (e, 0, j))]
        if swiglu:
            wspecs.append(pl.BlockSpec((1, k, tn), lambda i, e, j: (e, 0, j + nj)))
        extra = [pl.BlockSpec((tm, LANES), lambda i, e, j: (i, 0))]
        omap = lambda i, e, j: (i, e * nj + j)
        out_cols = n_e * n_out
    else:
        grid = (m // tm, nj)
        xmap = lambda i, j: (i, 0)
        cmap = lambda i, j: (0, 0)
        gmap = lambda i, j: (group_of_tile(i), 0, 0)
        wspecs = [pl.BlockSpec((k, tn), lambda i, j: (0, j))]
        if swiglu:
            wspecs.append(pl.BlockSpec((k, tn), lambda i, j: (0, j + nj)))
        extra = []
        omap = lambda i, j: (i, j)
        out_cols = n_out
    args = [x, nw.reshape(1, k), shift, scale] + [w] * len(wspecs) + ([gates] if gated else [])
    return pl.pallas_call(
        functools.partial(_modnorm_mm_kernel, swiglu=swiglu, gated=gated),
        out_shape=jax.ShapeDtypeStruct((m, out_cols), out_dtype),
        grid=grid,
        in_specs=[pl.BlockSpec((tm, k), xmap), pl.BlockSpec((1, k), cmap),
                  pl.BlockSpec((1, 1, k), gmap), pl.BlockSpec((1, 1, k), gmap)] + wspecs + extra,
        out_specs=pl.BlockSpec((tm, tn), omap),
        scratch_shapes=[pltpu.VMEM((tm, k), BF16)],
        compiler_params=_cparams(len(grid)),
        name="modnorm_mm",
    )(*args)


def _mm_kernel(*refs, residual, nk):
    acc_ref = refs[-1] if nk > 1 else None
    if residual:
        a_ref, w_ref, res_ref, gate_ref, o_ref = refs[:5]
    else:
        a_ref, w_ref, o_ref = refs[:3]
    kk = pl.program_id(2)
    part = jnp.dot(a_ref[...], w_ref[...], preferred_element_type=F32)

    def finish(acc):
        if residual:
            acc = res_ref[...] + gate_ref[0] * acc
        o_ref[...] = acc.astype(o_ref.dtype)

    if nk == 1:
        finish(part)
    else:
        @pl.when(kk == 0)
        def _():
            acc_ref[...] = part

        @pl.when(jnp.logical_and(kk > 0, kk < nk - 1))
        def _():
            acc_ref[...] += part

        @pl.when(kk == nk - 1)
        def _():
            finish(acc_ref[...] + part)


def _mm(a, w, *, tm, tn, tk, out_dtype=F32, res=None, gate=None, group_of_tile=None):
    m, k = a.shape
    n = w.shape[1]
    nk = k // tk
    residual = res is not None
    in_specs = [pl.BlockSpec((tm, tk), lambda i, j, q: (i, q)), pl.BlockSpec((tk, tn), lambda i, j, q: (q, j))]
    args = [a, w]
    if residual:
        in_specs += [pl.BlockSpec((tm, tn), lambda i, j, q: (i, j)),
                     pl.BlockSpec((1, 1, tn), lambda i, j, q: (group_of_tile(i), 0, j))]
        args += [res, gate]
    return pl.pallas_call(
        functools.partial(_mm_kernel, residual=residual, nk=nk),
        out_shape=jax.ShapeDtypeStruct((m, n), out_dtype),
        grid=(m // tm, n // tn, nk),
        in_specs=in_specs,
        out_specs=pl.BlockSpec((tm, tn), lambda i, j, q: (i, j)),
        scratch_shapes=[pltpu.VMEM((tm, tn), F32)] if nk > 1 else [],
        compiler_params=_cparams(3),
        name="mm_res" if residual else "mm",
    )(*args)


def _rope(x, cos, sin_signed):
    lane = lax.broadcasted_iota(jnp.int32, x.shape, 1)
    half = RET_HD // 4
    swapped = jnp.where(_mod(lane, 2 * half) < half, pltpu.roll(x, LANES - half, 1), pltpu.roll(x, half, 1))
    return x * cos + swapped * sin_signed


def _ret_chunks(q, k, v, s0, log_g, n_forward):
    c = RET_CHUNK
    shape = (q.shape[0], c, c)
    row = lax.broadcasted_iota(jnp.int32, shape, 1)
    col = lax.broadcasted_iota(jnp.int32, shape, 2)
    fwd = lax.broadcasted_iota(jnp.int32, shape, 0) < n_forward
    dist = jnp.where(fwd, row - col, col - row).astype(F32)
    q_steps = jnp.where(fwd, row + 1, c - row).astype(F32)
    k_steps = jnp.where(fwd, c - 1 - row, row).astype(F32)
    dmat = jnp.where(dist >= 0, jnp.exp(log_g * jnp.maximum(dist, 0.0)), 0.0)
    q_dec = jnp.exp(log_g * q_steps)
    k_dec = jnp.exp(log_g * k_steps)
    scores = _bdot_nt(q, k) * dmat
    out = _bdot(scores, v) + _bdot(q * q_dec, s0)
    s1 = s0 * jnp.exp(log_g * float(c)) + _bdot_tn(k * k_dec, v)
    return out, s1


def _retention_kernel(qf_ref, kf_ref, vf_ref, cf_ref, sf_ref, qb_ref, kb_ref, vb_ref, cb_ref, sb_ref,
                      decf_ref, decb_ref, of_ref, ob_ref, st_ref):
    @pl.when(pl.program_id(1) == 0)
    def _():
        st_ref[...] = jnp.zeros_like(st_ref)

    h = RET_HEADS
    kscale = RET_HD ** -0.5

    def heads(ref, cos, sin, scale=None):
        out = []
        for hh in range(h):
            xh = ref[:, hh * RET_HD:(hh + 1) * RET_HD]
            if cos is not None:
                xh = _rope(xh, cos, sin)
            out.append(xh if scale is None else xh * scale)
        return out

    cf, sf, cb, sb = cf_ref[...], sf_ref[...], cb_ref[...], sb_ref[...]
    q = jnp.stack(heads(qf_ref, cf, sf) + heads(qb_ref, cb, sb), axis=0)
    k = jnp.stack(heads(kf_ref, cf, sf, kscale) + heads(kb_ref, cb, sb, kscale), axis=0)
    v = jnp.stack(heads(vf_ref, None, None) + heads(vb_ref, None, None), axis=0)
    log_g = -jnp.exp(jnp.concatenate([decf_ref[...], decb_ref[...]], axis=0))
    out, s1 = _ret_chunks(q, k, v, st_ref[...], log_g, h)
    st_ref[...] = s1
    of_ref[...] = jnp.concatenate([out[hh] for hh in range(h)], axis=1)
    ob_ref[...] = jnp.concatenate([out[h + hh] for hh in range(h)], axis=1)


def _retention(proj, cos, sin_signed, dec_f, dec_b, rows):
    c, h = RET_CHUNK, RET_HEADS
    n = (rows.seq + rows.ctx_len) // c
    blk_f, pos_f = rows.chunk_block(c, False)
    blk_b, pos_b = rows.chunk_block(c, True)

    def tok(blk, off):
        return pl.BlockSpec((c, RET_W), lambda bi, t: (blk(bi, t), off))

    def tab(pos):
        return pl.BlockSpec((c, RET_HD), lambda bi, t: (pos(t), 0))

    dec = lambda a: jnp.broadcast_to(a.astype(F32)[:, None, None], (h, 1, LANES))
    dspec = pl.BlockSpec((h, 1, LANES), lambda bi, t: (0, 0, 0))
    return pl.pallas_call(
        _retention_kernel,
        out_shape=[jax.ShapeDtypeStruct((rows.m, RET_W), F32)] * 2,
        grid=(rows.batch, n),
        in_specs=[tok(blk_f, 0), tok(blk_f, 1), tok(blk_f, 2), tab(pos_f), tab(pos_f),
                  tok(blk_b, 0), tok(blk_b, 1), tok(blk_b, 2), tab(pos_b), tab(pos_b), dspec, dspec],
        out_specs=[tok(blk_f, 0), tok(blk_b, 0)],
        scratch_shapes=[pltpu.VMEM((2 * h, RET_HD, RET_HD), F32)],
        compiler_params=_cparams(2),
        name="retention",
    )(proj, proj, proj, cos, sin_signed, proj, proj, proj, cos, sin_signed, dec(dec_f), dec(dec_b))


def _na_qk_norm(x, w):
    ms = _group_mean(x * x, NA_HD)
    return x * lax.rsqrt(ms + NORM_EPS) * w


def _softmax_pv(parts):
    m = functools.reduce(jnp.maximum, [jnp.max(s, axis=-1, keepdims=True) for s, _ in parts])
    ps = [jnp.exp(s - m) for s, _ in parts]
    den = functools.reduce(jnp.add, [jnp.sum(p, axis=-1, keepdims=True) for p in ps])
    num = functools.reduce(jnp.add, [_dot(p, v) for p, (_, v) in zip(ps, parts)])
    return num / den


def _na_kernel(q_ref, k_ref, v_ref, kc_ref, vc_ref, qn_ref, kn_ref, bias_ref, o_ref, qs_ref, ks_ref, vs_ref,
               kcs_ref, *, rows):
    scale = NA_HD ** -0.5
    qs_ref[...] = (_na_qk_norm(q_ref[...], qn_ref[...]) * scale).astype(BF16)
    ks_ref[...] = _na_qk_norm(k_ref[...], kn_ref[...]).astype(BF16)
    vs_ref[...] = v_ref[...].astype(BF16)
    kcs_ref[...] = _na_qk_norm(kc_ref[...], kn_ref[...]).astype(BF16)
    lane = lax.broadcasted_iota(jnp.int32, (GRID_W, LANES), 1)
    first = lane < NA_HD
    vc = vc_ref[...]
    kc = kcs_ref[...]

    nq = 2 * GRID_W

    def body(rb, carry):
        qs, kw, vw, bw = [], [], [], []
        for i in range(NA_ROWS_PER_STEP):
            r = rb * NA_ROWS_PER_STEP + i
            r0 = jnp.clip(r - NA_KR // 2, 0, rows - NA_KR)
            q_r = qs_ref[pl.ds(pl.multiple_of(r * GRID_W, GRID_W), GRID_W), :]
            zero = jnp.zeros_like(q_r)
            qs.append(jnp.concatenate([jnp.where(first, q_r, zero), jnp.where(first, zero, q_r)], axis=0))
            win = pl.ds(pl.multiple_of(r0 * GRID_W, GRID_W), NA_KR * GRID_W)
            kw.append(ks_ref[win, :])
            vw.append(vs_ref[win, :])
            bw.append(bias_ref[0, r0 - r + (NA_KR - 1)])
        q = jnp.stack(qs, axis=0)
        s_w = _bdot_nt(q, jnp.stack(kw, axis=0)) + jnp.stack(bw, axis=0)
        q_flat = q.reshape(NA_ROWS_PER_STEP * nq, LANES)
        s_c = _dot_nt(q_flat, kc).reshape(NA_ROWS_PER_STEP, nq, kc.shape[0])
        m = jnp.maximum(jnp.max(s_w, axis=-1, keepdims=True), jnp.max(s_c, axis=-1, keepdims=True))
        p_w = jnp.exp(s_w - m)
        p_c = jnp.exp(s_c - m)
        den = jnp.sum(p_w, axis=-1, keepdims=True) + jnp.sum(p_c, axis=-1, keepdims=True)
        num = _bdot(p_w, jnp.stack(vw, axis=0)) + _dot(p_c.reshape(NA_ROWS_PER_STEP * nq, kc.shape[0]), vc).reshape(
            NA_ROWS_PER_STEP, nq, LANES)
        out = num / den
        for i in range(NA_ROWS_PER_STEP):
            r = rb * NA_ROWS_PER_STEP + i
            o_ref[pl.ds(pl.multiple_of(r * GRID_W, GRID_W), GRID_W), :] = jnp.where(
                first, out[i, :GRID_W], out[i, GRID_W:])
        return carry

    lax.fori_loop(0, rows // NA_ROWS_PER_STEP, body, 0)


NA_ROWS_PER_STEP = 4


def _na_bias_table(rpb):
    cols = np.arange(GRID_W)
    start = np.clip(cols - NA_KC // 2, 0, GRID_W - NA_KC)
    kcol = np.arange(GRID_W)
    inside = (kcol[None, :] >= start[:, None]) & (kcol[None, :] < start[:, None] + NA_KC)
    col_off = kcol[None, :] - cols[:, None] + (NA_KC - 1)
    by_row = jnp.stack([rpb[:, cls:cls + NA_KR] for cls in range(NA_KR)], axis=1)
    tab = jnp.full(by_row.shape[:3] + col_off.shape, NEG_BIG, F32)
    for off in range(2 * NA_KC - 1):
        hit = jnp.asarray(inside & (col_off == off))
        tab = jnp.where(hit[None, None, None], by_row[:, :, :, off][..., None, None], tab)
    tab = jnp.transpose(tab, (0, 1, 3, 2, 4))
    tab = tab.reshape(rpb.shape[0] // 2, 2, NA_KR, GRID_W, NA_KR * GRID_W).transpose(0, 2, 1, 3, 4)
    return tab.reshape(rpb.shape[0] // 2, NA_KR, 2 * GRID_W, NA_KR * GRID_W).astype(F32)


def _na_latent(proj, qn_w, kn_w, bias, rows, col0):
    seq, ctx_len = rows.seq, rows.ctx_len
    grid_rows = seq // GRID_W
    pairs = NA_W // LANES
    cb = col0 // LANES
    cblk = rows.ml // ctx_len
    tile2 = lambda a: jnp.tile(a.astype(F32), 2).reshape(1, LANES)
    lat = lambda off: pl.BlockSpec((seq, LANES), lambda bi, p: (bi, cb + off * pairs + p))
    ctx = lambda off: pl.BlockSpec((ctx_len, LANES), lambda bi, p: (cblk + bi, cb + off * pairs + p))
    return pl.pallas_call(
        functools.partial(_na_kernel, rows=grid_rows),
        out_shape=jax.ShapeDtypeStruct((rows.ml, NA_W), F32),
        grid=(rows.batch, pairs),
        in_specs=[lat(0), lat(1), lat(2), ctx(1), ctx(2),
                  pl.BlockSpec((1, LANES), lambda bi, p: (0, 0)), pl.BlockSpec((1, LANES), lambda bi, p: (0, 0)),
                  pl.BlockSpec((1, NA_KR, 2 * GRID_W, NA_KR * GRID_W), lambda bi, p: (p, 0, 0, 0))],
        out_specs=pl.BlockSpec((seq, LANES), lambda bi, p: (bi, p)),
        scratch_shapes=[pltpu.VMEM((seq, LANES), BF16)] * 3 + [pltpu.VMEM((ctx_len, LANES), BF16)],
        compiler_params=_cparams(2),
        name="na_latent",
    )(proj, proj, proj, proj, proj, tile2(qn_w), tile2(kn_w), bias)


def _ctx_attn_kernel(q_ref, k_ref, v_ref, qn_ref, kn_ref, o_ref):
    scale = NA_HD ** -0.5
    q = (_na_qk_norm(q_ref[...], qn_ref[...]) * scale).astype(BF16)
    k = _na_qk_norm(k_ref[...], kn_ref[...]).astype(BF16)
    v = v_ref[...]
    lane = lax.broadcasted_iota(jnp.int32, q.shape, 1)
    first = lane < NA_HD
    outs = []
    for hh in range(2):
        qh = jnp.where(first if hh == 0 else jnp.logical_not(first), q, jnp.zeros_like(q))
        outs.append(_softmax_pv([(_dot_nt(qh, k), v)]))
    o_ref[...] = jnp.where(first, outs[0], outs[1])


def _ctx_attention(proj, qn_w, kn_w, rows, col0):
    ctx_len = rows.ctx_len
    pairs = NA_W // LANES
    cb = col0 // LANES
    cblk = rows.ml // ctx_len
    tile2 = lambda a: jnp.tile(a.astype(F32), 2).reshape(1, LANES)
    blk = lambda off: pl.BlockSpec((ctx_len, LANES), lambda bi, p: (cblk + bi, cb + off * pairs + p))
    return pl.pallas_call(
        _ctx_attn_kernel,
        out_shape=jax.ShapeDtypeStruct((rows.mc, NA_W), F32),
        grid=(rows.batch, pairs),
        in_specs=[blk(0), blk(1), blk(2),
                  pl.BlockSpec((1, LANES), lambda bi, p: (0, 0)), pl.BlockSpec((1, LANES), lambda bi, p: (0, 0))],
        out_specs=pl.BlockSpec((ctx_len, LANES), lambda bi, p: (bi, p)),
        compiler_params=_cparams(2),
        name="ctx_attention",
    )(proj, proj, proj, tile2(qn_w), tile2(kn_w))


def _ret_readout_kernel(yf_ref, yb_ref, g_ref, nal_ref, nac_ref, gnw_ref, o_ref, *, n_latent_tiles):
    y = yf_ref[...] + yb_ref[...]
    for hh in range(RET_HEADS):
        sl = slice(hh * RET_HD, (hh + 1) * RET_HD)
        yh = y[:, sl]
        yc = yh - jnp.mean(yh, axis=-1, keepdims=True)
        yn = yc * lax.rsqrt(jnp.mean(yc * yc, axis=-1, keepdims=True) + RET_GN_EPS)
        o_ref[:, sl] = (yn * gnw_ref[:, sl] * _silu(g_ref[:, sl])).astype(o_ref.dtype)
    is_latent = pl.program_id(0) < n_latent_tiles
    o_ref[:, RET_W:] = jnp.where(is_latent, nal_ref[...], nac_ref[...]).astype(o_ref.dtype)


def _ret_readout(yf, yb, proj, na_l, na_c, gn_w, rows, tm):
    nl = rows.ml // tm
    row = lambda i: (i, 0)
    return pl.pallas_call(
        functools.partial(_ret_readout_kernel, n_latent_tiles=nl),
        out_shape=jax.ShapeDtypeStruct((rows.m, RET_W + NA_W), BF16),
        grid=(rows.m // tm,),
        in_specs=[pl.BlockSpec((tm, RET_W), row), pl.BlockSpec((tm, RET_W), row),
                  pl.BlockSpec((tm, RET_W), lambda i: (i, 3)),
                  pl.BlockSpec((tm, NA_W), lambda i: (jnp.minimum(i, nl - 1), 0)),
                  pl.BlockSpec((tm, NA_W), lambda i: (jnp.maximum(i - nl, 0), 0)),
                  pl.BlockSpec((1, RET_W), lambda i: (0, 0))],
        out_specs=pl.BlockSpec((tm, RET_W + NA_W), row),
        compiler_params=_cparams(1),
        name="ret_readout",
    )(yf, yb, proj, na_l, na_c, gn_w.reshape(1, RET_W).astype(F32))


def _rwkv_mix_kernel(x_ref, xp_ref, xn_ref, nw_ref, sh_ref, sc_ref, mu_ref, *o_refs, seq, ctx_len, ml, tm):
    nw, sh, sc = nw_ref[...], sh_ref[0], sc_ref[0]
    h = _modnorm(x_ref[...], nw, sh, sc)
    hp = _modnorm(xp_ref[...], nw, sh, sc)[7:8]
    hn = _modnorm(xn_ref[...], nw, sh, sc)[0:1]
    row = lax.broadcasted_iota(jnp.int32, h.shape, 0)
    g = pl.program_id(0) * tm + row
    latent = g < ml
    pos = jnp.where(latent, _mod(g, seq), _mod(g, ctx_len))
    first = pos == 0
    last = pos == jnp.where(latent, seq - 1, ctx_len - 1)
    up = jnp.where(row == 0, hp, pltpu.roll(h, 1, 0))
    up = jnp.where(first, 0.0, up)
    dn = jnp.where(row == tm - 1, hn, pltpu.roll(h, tm - 1, 0))
    dn = jnp.where(last, 0.0, dn)
    xx = 0.5 * (up + dn) - h
    for s_i, o_ref in enumerate(o_refs):
        o_ref[...] = (h + xx * mu_ref[s_i:s_i + 1, :]).astype(o_ref.dtype)


def _rwkv_mix(x, nw, shift, scale, mu, *, rows, tm):
    m, d = x.shape
    n_slab = m // 8
    per = tm // 8
    gmap = lambda i: (rows.group_of_tile(tm)(i), 0, 0)
    return pl.pallas_call(
        functools.partial(_rwkv_mix_kernel, seq=rows.seq, ctx_len=rows.ctx_len, ml=rows.ml, tm=tm),
        out_shape=[jax.ShapeDtypeStruct((m, d), BF16)] * 6,
        grid=(m // tm,),
        in_specs=[pl.BlockSpec((tm, d), lambda i: (i, 0)),
                  pl.BlockSpec((8, d), lambda i: (jnp.maximum(i * per - 1, 0), 0)),
                  pl.BlockSpec((8, d), lambda i: (jnp.minimum((i + 1) * per, n_slab - 1), 0)),
                  pl.BlockSpec((1, d), lambda i: (0, 0)),
                  pl.BlockSpec((1, 1, d), gmap), pl.BlockSpec((1, 1, d), gmap),
                  pl.BlockSpec((6, d), lambda i: (0, 0))],
        out_specs=[pl.BlockSpec((tm, d), lambda i: (i, 0))] * 6,
        compiler_params=_cparams(1),
        name="rwkv_mix",
    )(x, x, x, nw.reshape(1, d), shift, scale, mu)


def _rwkv_post_kernel(r_ref, k_ref, v_ref, xw_ref, xa_ref, xg_ref, w1_ref, w2_ref, a1_ref, a2_ref, g1_ref, g2_ref,
                      w0_ref, a0_ref, kk_ref, ka_ref, rk_ref,
                      rt0, kt0, bt0, at0, wc0, rt1, kt1, bt1, at1, wc1, vb_ref, g_ref, bonus_ref, *, tm):
    c = RWKV_CHUNK
    n_pairs = r_ref.shape[1] // LANES
    r, k, v = r_ref[...], k_ref[...], v_ref[...]
    lw_low = jnp.tanh(jnp.dot(xw_ref[...], w1_ref[...], preferred_element_type=F32))
    a_low = jnp.dot(xa_ref[...], a1_ref[...], preferred_element_type=F32)
    g_low = _sigmoid(jnp.dot(xg_ref[...], g1_ref[...], preferred_element_type=F32))
    g_ref[...] = _dot(g_low, g2_ref[...]).astype(g_ref.dtype)

    def put(o_ref, val):
        for p in range(n_pairs):
            o_ref[p] = val[:, p * LANES:(p + 1) * LANES].astype(o_ref.dtype)

    put(vb_ref, v)
    kk = k * kk_ref[...]
    nrm = jnp.sqrt(_group_mean(kk * kk, RWKV_HD) * float(RWKV_HD))
    kk = kk / jnp.maximum(nrm, 1e-12)

    tr = lax.broadcasted_iota(jnp.int32, (c, c), 0)
    tc = lax.broadcasted_iota(jnp.int32, (c, c), 1)
    tri = [(tr >= tc).astype(BF16), (tr <= tc).astype(BF16)]
    lora = w2_ref.shape[1]
    coeff_src = jnp.zeros_like(r)
    outs = [(rt0, kt0, bt0, at0, wc0), (rt1, kt1, bt1, at1, wc1)]
    for z in range(2):
        rt_ref, kt_ref, bt_ref, at_ref, wc_ref = outs[z]
        w_lora = _dot(lw_low[:, z * lora:(z + 1) * lora], w2_ref[z])
        lw = -float(np.exp(-0.5)) * _sigmoid(w0_ref[z:z + 1, :] + w_lora)
        a_z = _sigmoid(a0_ref[z:z + 1, :] + _dot(a_low[:, z * lora:(z + 1) * lora], a2_ref[z]))
        k_dir = k * (1.0 + (a_z - 1.0) * ka_ref[...])
        coeff_src = coeff_src + r * k_dir * rk_ref[...]
        cum = jnp.concatenate([_dot_exact_lhs(tri[z], lw[ci * c:(ci + 1) * c]) for ci in range(tm // c)], axis=0)
        e_pos = jnp.exp(cum)
        e_neg = jnp.exp(-cum)
        put(rt_ref, r * e_pos)
        put(kt_ref, k_dir * e_neg)
        put(bt_ref, kk * a_z * e_neg)
        put(at_ref, -kk * jnp.exp(cum - lw))
        for ci in range(tm // c):
            end = ci * c + (c - 1 if z == 0 else 0)
            for p in range(n_pairs):
                wc_ref[p, ci] = e_pos[end:end + 1, p * LANES:(p + 1) * LANES]
    coeff = _group_mean(coeff_src, RWKV_HD) * float(RWKV_HD)
    bonus_ref[...] = coeff * v


def _rwkv_post(r, k, v, xw, xa, xg, p, *, tm):
    m, d = r.shape
    c = RWKV_CHUNK
    lora = p["w2"].shape[1]
    glora = p["g1"].shape[1]
    n_pairs = d // LANES
    pair = jax.ShapeDtypeStruct((n_pairs, m, LANES), BF16)
    chunk = jax.ShapeDtypeStruct((n_pairs, m // c, 1, LANES), F32)
    row = lambda i: (i, 0)
    full2 = lambda i: (0, 0)
    full3 = lambda i: (0, 0, 0)
    tspec = pl.BlockSpec((tm, d), row)
    pspec = pl.BlockSpec((n_pairs, tm, LANES), lambda i: (0, i, 0))
    cspec = pl.BlockSpec((n_pairs, tm // c, 1, LANES), lambda i: (0, i, 0, 0))
    return pl.pallas_call(
        functools.partial(_rwkv_post_kernel, tm=tm),
        out_shape=[pair] * 4 + [chunk] + [pair] * 4 + [chunk]
        + [pair, jax.ShapeDtypeStruct((m, d), BF16), jax.ShapeDtypeStruct((m, d), F32)],
        grid=(m // tm,),
        in_specs=[tspec] * 6
        + [pl.BlockSpec((d, 2 * lora), full2), pl.BlockSpec((2, lora, d), full3),
           pl.BlockSpec((d, 2 * lora), full2), pl.BlockSpec((2, lora, d), full3),
           pl.BlockSpec((d, glora), full2), pl.BlockSpec((glora, d), full2),
           pl.BlockSpec((2, d), full2), pl.BlockSpec((2, d), full2)]
        + [pl.BlockSpec((1, d), full2)] * 3,
        out_specs=[pspec] * 4 + [cspec] + [pspec] * 4 + [cspec] + [pspec, tspec, tspec],
        compiler_params=_cparams(1),
        name="rwkv_post",
    )(r, k, v, xw, xa, xg, p["w1"], p["w2"], p["a1"], p["a2"], p["g1"], p["g2"], p["w0"], p["a0"],
      p["k_k"], p["k_a"], p["r_k"])


def _bdot(a, b):
    return jnp.einsum("ucd,ude->uce", a.astype(BF16), b.astype(BF16), preferred_element_type=F32)


def _bdot_nt(a, b):
    return jnp.einsum("ucd,usd->ucs", a.astype(BF16), b.astype(BF16), preferred_element_type=F32)


def _bdot_tn(a, b):
    return jnp.einsum("uce,ucd->ued", a.astype(BF16), b.astype(BF16), preferred_element_type=F32)


def _scan_chunks(rt, kt, bt, at, v, wc, s0, n_forward):
    c = RWKV_CHUNK
    n_units = rt.shape[0]
    lane = lax.broadcasted_iota(jnp.int32, (n_units, c, LANES), 2)
    row = lax.broadcasted_iota(jnp.int32, (n_units, c, LANES), 1)
    unit = lax.broadcasted_iota(jnp.int32, (n_units, c, LANES), 0)
    head0 = lane < RWKV_HD
    src = _mod(lane, RWKV_HD)
    ahead = jnp.where(unit < n_forward, row - src, src - row)
    strict = ahead > 0
    incl = ahead >= 0

    def dup(x):
        zero = jnp.zeros_like(x)
        h0 = head0[:, :x.shape[1]]
        return jnp.concatenate([jnp.where(h0, x, zero), jnp.where(h0, zero, x)], axis=1)

    ar = jnp.concatenate([at, rt], axis=1)
    mb = _bdot_nt(ar, dup(bt))
    mk = _bdot_nt(ar, dup(kt))
    p_ab = jnp.where(strict, mb[:, :c], 0.0)
    p_rb = jnp.where(incl, mb[:, c:], 0.0)
    p_ak = jnp.where(strict, mk[:, :c], 0.0)
    p_rk = jnp.where(incl, mk[:, c:], 0.0)
    vd = dup(v)
    rhs = _bdot_nt(at, s0) + _bdot(p_ak, vd)

    power = p_ab.astype(BF16)
    inv = jnp.where(row == src, 1.0, 0.0) + p_ab
    for _ in range(int(np.log2(c)) - 1):
        power = _bdot(power, dup(power)).astype(BF16)
        inv = inv + _bdot(inv, dup(power))
    inv = inv.astype(BF16)
    u = _bdot(inv, dup(rhs))
    ph, pl_ = _split2(p_ab)
    uh, ul = _split2(dup(u))
    resid = (rhs - u) + (_bdot(ph, uh) + (_bdot(ph, ul) + _bdot(pl_, uh)))
    u = u + _bdot(inv, dup(resid))

    y = _bdot_nt(rt, s0) + _bdot(jnp.concatenate([p_rb, p_rk], axis=2), jnp.concatenate([dup(u), vd], axis=1))
    upd = _bdot_tn(jnp.concatenate([u.astype(BF16), v], axis=1), jnp.concatenate([bt, kt], axis=1))
    er = lax.broadcasted_iota(jnp.int32, (1, LANES, LANES), 1) < RWKV_HD
    ec = lax.broadcasted_iota(jnp.int32, (1, LANES, LANES), 2) < RWKV_HD
    s1 = (s0 + jnp.where(er == ec, upd, 0.0)) * wc
    return y, s1


def _rwkv_scan_kernel(rtf, ktf, btf, atf, vf, wcf, rtb, ktb, btb, atb, vb, wcb, yf_ref, yb_ref, s_ref):
    @pl.when(pl.program_id(1) == 0)
    def _():
        s_ref[...] = jnp.zeros_like(s_ref)

    n_pairs = rtf.shape[0]
    both = lambda f, b: jnp.concatenate([f[...], b[...]], axis=0)
    wc = jnp.concatenate([wcf[:, 0], wcb[:, 0]], axis=0)
    y, s1 = _scan_chunks(both(rtf, rtb), both(ktf, ktb), both(btf, btb), both(atf, atb), both(vf, vb), wc,
                         s_ref[...], n_pairs)
    s_ref[...] = s1
    yf_ref[...] = y[:n_pairs]
    yb_ref[...] = y[n_pairs:]


def _rwkv_scan(feats_f, feats_b, v, rows):
    n_pairs, m, _ = v.shape
    c = RWKV_CHUNK
    d = n_pairs * LANES
    nchunks = (rows.seq + rows.ctx_len) // c
    blk_f, _ = rows.chunk_block(c, False)
    blk_b, _ = rows.chunk_block(c, True)

    def specs(blk):
        tok = pl.BlockSpec((n_pairs, c, LANES), lambda bi, t: (0, blk(bi, t), 0))
        return tok, pl.BlockSpec((n_pairs, 1, 1, LANES), lambda bi, t: (0, blk(bi, t), 0, 0))

    tok_f, wc_f = specs(blk_f)
    tok_b, wc_b = specs(blk_b)
    return pl.pallas_call(
        _rwkv_scan_kernel,
        out_shape=[jax.ShapeDtypeStruct((n_pairs, m, LANES), F32)] * 2,
        grid=(rows.batch, nchunks),
        in_specs=[tok_f] * 5 + [wc_f] + [tok_b] * 5 + [wc_b],
        out_specs=[tok_f, tok_b],
        scratch_shapes=[pltpu.VMEM((2 * n_pairs, LANES, LANES), F32)],
        compiler_params=_cparams(2),
        name="rwkv_scan",
    )(*feats_f[:4], v, feats_f[4], *feats_b[:4], v, feats_b[4])


def _rwkv_readout_kernel(yf_ref, yb_ref, bonus_ref, g_ref, lnw_ref, lnb_ref, o_ref):
    y = jnp.concatenate([yf_ref[p] + yb_ref[p] for p in range(yf_ref.shape[0])], axis=1)
    yc = y - _group_mean(y, RWKV_HD)
    yn = yc * lax.rsqrt(_group_mean(yc * yc, RWKV_HD) + RWKV_GN_EPS) * lnw_ref[...] + lnb_ref[...]
    o_ref[...] = ((yn + bonus_ref[...]) * g_ref[...].astype(F32)).astype(o_ref.dtype)


def _rwkv_readout(yf, yb, bonus, g, ln_w, ln_b, rows, tm):
    n_pairs = yf.shape[0]
    d = n_pairs * LANES
    pspec = pl.BlockSpec((n_pairs, tm, LANES), lambda i: (0, i, 0))
    tspec = pl.BlockSpec((tm, d), lambda i: (i, 0))
    vspec = pl.BlockSpec((1, d), lambda i: (0, 0))
    return pl.pallas_call(
        _rwkv_readout_kernel,
        out_shape=jax.ShapeDtypeStruct((rows.ml, d), BF16),
        grid=(rows.ml // tm,),
        in_specs=[pspec, pspec, tspec, tspec, vspec, vspec],
        out_specs=tspec,
        compiler_params=_cparams(1),
        name="rwkv_readout",
    )(yf, yb, bonus, g, ln_w.reshape(1, d).astype(F32), ln_b.reshape(1, d).astype(F32))


def _router_kernel(x_ref, nw_ref, sh_ref, sc_ref, w_ref, o_ref, *, n_experts):
    h = _modnorm(x_ref[...], nw_ref[...], sh_ref[0], sc_ref[0])
    logits = _dot_hi(h, w_ref[...])
    lane = lax.broadcasted_iota(jnp.int32, logits.shape, 1).astype(F32)
    logits = jnp.where(lane < n_experts, logits, -jnp.inf)
    v1 = jnp.max(logits, axis=-1, keepdims=True)
    i1 = jnp.min(jnp.where(logits == v1, lane, float(LANES)), axis=-1, keepdims=True)
    rest = jnp.where(lane == i1, -jnp.inf, logits)
    v2 = jnp.max(rest, axis=-1, keepdims=True)
    i2 = jnp.min(jnp.where(rest == v2, lane, float(LANES)), axis=-1, keepdims=True)
    e2 = jnp.exp(v2 - v1)
    den = 1.0 + e2
    o_ref[...] = jnp.where(lane == i1, 1.0 / den, 0.0) + jnp.where(lane == i2, e2 / den, 0.0)


def _router(x, nw, shift, scale, router, *, tm, group_of_tile):
    m, d = x.shape
    n_experts = router.shape[1]
    w = jnp.zeros((d, LANES), F32).at[:, :n_experts].set(router.astype(F32))
    gmap = lambda i: (group_of_tile(i), 0, 0)
    return pl.pallas_call(
        functools.partial(_router_kernel, n_experts=n_experts),
        out_shape=jax.ShapeDtypeStruct((m, LANES), F32),
        grid=(m // tm,),
        in_specs=[pl.BlockSpec((tm, d), lambda i: (i, 0)), pl.BlockSpec((1, d), lambda i: (0, 0)),
                  pl.BlockSpec((1, 1, d), gmap), pl.BlockSpec((1, 1, d), gmap),
                  pl.BlockSpec((d, LANES), lambda i: (0, 0))],
        out_specs=pl.BlockSpec((tm, LANES), lambda i: (i, 0)),
        compiler_params=_cparams(1),
        name="router",
    )(x, nw.reshape(1, d), shift, scale, w)


MOE_TILE = 512
DMA_UNROLL = 8


def _route_plan(gates, n_experts, tg):
    m = gates.shape[0]
    g = gates[:, :n_experts]
    sel = g > 0.0
    seli = sel.astype(jnp.int32)
    slot = jnp.cumsum(seli, axis=1) - 1
    rank = jnp.cumsum(seli, axis=0) - 1
    counts = jnp.sum(seli, axis=0)
    padded = ((counts + tg - 1) // tg) * tg
    ends = jnp.cumsum(padded)
    dest = (ends - padded)[None, :] + rank
    n_rows = TOP_K * m + n_experts * tg
    n_tiles = n_rows // tg
    first = sel & (slot == 0)
    second = sel & (slot == 1)
    pick = lambda msk, val: jnp.sum(jnp.where(msk, val, 0), axis=1)
    d1 = pick(first, dest)
    has2 = jnp.any(second, axis=1)
    d2 = pick(second, dest)
    w1 = pick(first, g)
    w2 = pick(second, g)
    tile_expert = jnp.minimum(jnp.sum((jnp.arange(n_tiles, dtype=jnp.int32) * tg)[:, None] >= ends[None, :], axis=1),
                              n_experts - 1).astype(jnp.int32)
    n_valid = (ends[-1] // tg).astype(jnp.int32).reshape(1)
    wts = jnp.zeros((m, LANES), F32).at[:, 0].set(w1).at[:, 1].set(w2)
    i32 = lambda a: a.astype(jnp.int32)
    return (n_rows, tile_expert, n_valid, i32(d1), i32(jnp.where(has2, d2, n_rows)), i32(jnp.where(has2, d2, d1)),
            wts)


def _slab_rows(ref3, val):
    for s in range(ref3.shape[1]):
        ref3[:, s, :] = val[:, s * LANES:(s + 1) * LANES].astype(ref3.dtype)


def _unslab_rows(ref3):
    return jnp.concatenate([ref3[:, s, :] for s in range(ref3.shape[1])], axis=1)


def _moe_scatter_kernel(d1_ref, d2_ref, x_ref, nw_ref, sh_ref, sc_ref, a0_hbm, a_hbm, h_ref, sem, *, tm):
    del a0_hbm
    _slab_rows(h_ref, _modnorm(x_ref[...], nw_ref[...], sh_ref[0], sc_ref[0]))

    def row_copy(r, dst):
        return pltpu.make_async_copy(h_ref.at[pl.ds(r, 1)], a_hbm.at[pl.ds(dst, 1)], sem)

    def issue(r, carry):
        row_copy(r, d1_ref[0, 0, r]).start()
        row_copy(r, d2_ref[0, 0, r]).start()
        return carry

    def drain(r, carry):
        row_copy(r, 0).wait()
        row_copy(r, 0).wait()
        return carry

    lax.fori_loop(0, tm, issue, 0, unroll=DMA_UNROLL)
    lax.fori_loop(0, tm, drain, 0, unroll=DMA_UNROLL)


def _moe_scatter(x, nw, shift, scale, d1, d2, n_rows, *, tm, tg, group_of_tile):
    m, d = x.shape
    slabs = d // LANES
    idx = lambda a: a.reshape(m // tm, 1, tm)
    ispec = pl.BlockSpec((1, 1, tm), lambda i: (i, 0, 0), memory_space=pltpu.SMEM)
    gmap = lambda i: (group_of_tile(i), 0, 0)
    zeros = jnp.zeros((n_rows + tg, slabs, LANES), F32)
    return pl.pallas_call(
        functools.partial(_moe_scatter_kernel, tm=tm),
        out_shape=jax.ShapeDtypeStruct(zeros.shape, F32),
        grid=(m // tm,),
        in_specs=[ispec, ispec, pl.BlockSpec((tm, d), lambda i: (i, 0)), pl.BlockSpec((1, d), lambda i: (0, 0)),
                  pl.BlockSpec((1, 1, d), gmap), pl.BlockSpec((1, 1, d), gmap), pl.BlockSpec(memory_space=pl.ANY)],
        out_specs=pl.BlockSpec(memory_space=pl.ANY),
        scratch_shapes=[pltpu.VMEM((tm, slabs, LANES), F32), pltpu.SemaphoreType.DMA],
        input_output_aliases={6: 0},
        compiler_params=_cparams(1),
        name="moe_scatter",
    )(idx(d1), idx(d2), x, nw.reshape(1, d), shift, scale, zeros)


def _grouped_swiglu_kernel(te_ref, nv_ref, a_ref, wg_ref, wu_ref, o_ref, wgb_ref, wub_ref):
    t = pl.program_id(1)
    valid = t < nv_ref[0]
    new_weights = jnp.logical_or(t == 0, te_ref[t] != te_ref[jnp.maximum(t - 1, 0)])

    @pl.when(jnp.logical_and(valid, new_weights))
    def _():
        wgb_ref[...] = wg_ref[0].astype(BF16)
        wub_ref[...] = wu_ref[0].astype(BF16)

    @pl.when(valid)
    def _():
        a = _unslab_rows(a_ref).astype(BF16)
        gate = jnp.dot(a, wgb_ref[...], preferred_element_type=F32)
        up = jnp.dot(a, wub_ref[...], preferred_element_type=F32)
        o_ref[...] = (_silu(gate) * up).astype(o_ref.dtype)

    @pl.when(jnp.logical_not(valid))
    def _():
        o_ref[...] = jnp.zeros_like(o_ref)


def _grouped_swiglu(a, w13, tile_expert, n_valid, p_rows, *, tg, tn):
    slabs = a.shape[1]
    k = slabs * LANES
    e_hid = w13.shape[2] // 2
    nj = e_hid // tn
    tile = lambda t, nv: jnp.minimum(t, nv[0] - 1)
    return pl.pallas_call(
        _grouped_swiglu_kernel,
        out_shape=jax.ShapeDtypeStruct((p_rows, e_hid), BF16),
        grid_spec=pltpu.PrefetchScalarGridSpec(
            num_scalar_prefetch=2,
            grid=(nj, p_rows // tg),
            in_specs=[pl.BlockSpec((tg, slabs, LANES), lambda j, t, te, nv: (tile(t, nv), 0, 0)),
                      pl.BlockSpec((1, k, tn), lambda j, t, te, nv: (te[tile(t, nv)], 0, j)),
                      pl.BlockSpec((1, k, tn), lambda j, t, te, nv: (te[tile(t, nv)], 0, j + nj))],
            out_specs=pl.BlockSpec((tg, tn), lambda j, t, te, nv: (t, j)),
            scratch_shapes=[pltpu.VMEM((k, tn), BF16)] * 2,
        ),
        compiler_params=_cparams(2),
        name="moe_up",
    )(tile_expert, n_valid, a, w13, w13)


def _grouped_mm_kernel(te_ref, nv_ref, a_ref, w_ref, o_ref):
    @pl.when(pl.program_id(0) < nv_ref[0])
    def _():
        _slab_rows(o_ref, jnp.dot(a_ref[...], w_ref[0], preferred_element_type=F32))

    @pl.when(pl.program_id(0) >= nv_ref[0])
    def _():
        o_ref[...] = jnp.zeros_like(o_ref)


def _grouped_mm(a, w, tile_expert, n_valid, *, tg):
    p_rows, k = a.shape
    n = w.shape[2]
    slabs = n // LANES
    tile = lambda t, nv: jnp.minimum(t, nv[0] - 1)
    return pl.pallas_call(
        _grouped_mm_kernel,
        out_shape=jax.ShapeDtypeStruct((p_rows, slabs, LANES), F32),
        grid_spec=pltpu.PrefetchScalarGridSpec(
            num_scalar_prefetch=2,
            grid=(p_rows // tg,),
            in_specs=[pl.BlockSpec((tg, k), lambda t, te, nv: (tile(t, nv), 0)),
                      pl.BlockSpec((1, k, n), lambda t, te, nv: (te[tile(t, nv)], 0, 0))],
            out_specs=pl.BlockSpec((tg, slabs, LANES), lambda t, te, nv: (t, 0, 0)),
        ),
        compiler_params=_cparams(1),
        name="moe_down",
    )(tile_expert, n_valid, a, w)


def _combine_kernel(d1_ref, d2_ref, y_hbm, wts_ref, x_ref, gate_ref, o_ref, buf_ref, sem, *, tm):
    def row_copy(which, r, s):
        return pltpu.make_async_copy(y_hbm.at[pl.ds(s, 1)], buf_ref.at[which, pl.ds(r, 1)], sem)

    def issue(r, carry):
        row_copy(0, r, d1_ref[0, 0, r]).start()
        row_copy(1, r, d2_ref[0, 0, r]).start()
        return carry

    def drain(r, carry):
        row_copy(0, r, 0).wait()
        row_copy(1, r, 0).wait()
        return carry

    lax.fori_loop(0, tm, issue, 0, unroll=DMA_UNROLL)
    lax.fori_loop(0, tm, drain, 0, unroll=DMA_UNROLL)
    lane = lax.broadcasted_iota(jnp.int32, wts_ref.shape, 1)
    w1 = jnp.sum(jnp.where(lane == 0, wts_ref[...], 0.0), axis=-1, keepdims=True)
    w2 = jnp.sum(jnp.where(lane == 1, wts_ref[...], 0.0), axis=-1, keepdims=True)
    y = w1 * _unslab_rows(buf_ref.at[0]) + w2 * _unslab_rows(buf_ref.at[1])
    o_ref[...] = x_ref[...] + gate_ref[0] * y


def _combine(y, d1, d2, wts, x, gate, *, tm, group_of_tile):
    m, d = x.shape
    idx = lambda a: a.reshape(m // tm, 1, tm)
    ispec = pl.BlockSpec((1, 1, tm), lambda i: (i, 0, 0), memory_space=pltpu.SMEM)
    return pl.pallas_call(
        functools.partial(_combine_kernel, tm=tm),
        out_shape=jax.ShapeDtypeStruct((m, d), F32),
        grid=(m // tm,),
        in_specs=[ispec, ispec, pl.BlockSpec(memory_space=pl.ANY),
                  pl.BlockSpec((tm, LANES), lambda i: (i, 0)), pl.BlockSpec((tm, d), lambda i: (i, 0)),
                  pl.BlockSpec((1, 1, d), lambda i: (group_of_tile(i), 0, 0))],
        out_specs=pl.BlockSpec((tm, d), lambda i: (i, 0)),
        scratch_shapes=[pltpu.VMEM((2, tm, d // LANES, LANES), F32), pltpu.SemaphoreType.DMA],
        compiler_params=_cparams(1),
        name="moe_combine",
    )(idx(d1), idx(d2), y, wts, x, gate)


def _rope_tables(seq, ctx_len):
    half = RET_HD // 4
    freqs = ROPE_BASE ** (-np.arange(half, dtype=np.float64) / half)
    t = np.arange(seq)
    ang_r = (t // GRID_W)[:, None] * freqs[None, :]
    ang_c = (t % GRID_W)[:, None] * freqs[None, :]
    cos = np.concatenate([np.cos(ang_r)] * 2 + [np.cos(ang_c)] * 2, axis=1)
    sin = np.concatenate([-np.sin(ang_r), np.sin(ang_r), -np.sin(ang_c), np.sin(ang_c)], axis=1)
    cos = np.concatenate([np.ones((ctx_len, RET_HD)), cos], axis=0)
    sin = np.concatenate([np.zeros((ctx_len, RET_HD)), sin], axis=0)
    return jnp.asarray(cos, F32), jnp.asarray(sin, F32)


def _half_or_full(n):
    return n // 2 if (n // 2) % LANES == 0 else n


def kernel(x, c, ctx, c_ctx, ada_w, ada_b, norm_mix_w, norm_ffn_w, ev_w_in, ev_ret_decay_f, ev_ret_decay_b,
           ev_ret_gn_w, ev_na_qn_w, ev_na_kn_w, ev_na_rpb, ev_w_out, ev_ffn_w13, ev_ffn_w2, od_mu, od_w_rkv,
           od_w0, od_w1, od_w2, od_a0, od_a1, od_a2, od_g1, od_g2, od_k_k, od_k_a, od_r_k, od_ln_w, od_ln_b,
           od_w_o, od_router, od_moe_w13, od_moe_w2):
    batch, seq, d = x.shape
    ctx_len = ctx.shape[1]
    rows = _Rows(batch, seq, ctx_len)
    tm = 8 * RWKV_CHUNK
    grp = rows.group_of_tile(tm)
    tm_big = 2 * tm if (rows.mc % (2 * tm) == 0 and seq % (2 * tm) == 0) else tm
    grp_big = rows.group_of_tile(tm_big)
    xa = jnp.concatenate([x.reshape(rows.ml, d), ctx.reshape(rows.mc, d)], axis=0)

    n_mod = ((batch + 1 + 7) // 8) * 8
    c_rows = jnp.zeros((n_mod, d), F32).at[:batch].set(c).at[batch].set(c_ctx)
    mods = _ada(c_rows, ada_w, ada_b)
    mods = mods.reshape(mods.shape[0], n_mod, 6, 1, d).transpose(0, 2, 1, 3, 4)

    mod = mods[0]
    w_in = ev_w_in[0].astype(BF16)
    proj = _modnorm_mm(xa, norm_mix_w[0], mod[0], mod[1], w_in, group_of_tile=grp_big, tm=tm_big,
                       tn=_half_or_full(w_in.shape[1]), out_dtype=F32)
    cos, sin = _rope_tables(seq, ctx_len)
    y_f, y_b = _retention(proj, cos, sin, ev_ret_decay_f[0], ev_ret_decay_b[0], rows)
    bias = _na_bias_table(ev_na_rpb[0].astype(F32))
    na_l = _na_latent(proj, ev_na_qn_w[0], ev_na_kn_w[0], bias, rows, 4 * RET_W)
    na_c = _ctx_attention(proj, ev_na_qn_w[0], ev_na_kn_w[0], rows, 4 * RET_W)
    mix = _ret_readout(y_f, y_b, proj, na_l, na_c, ev_ret_gn_w[0], rows, tm)
    w_out = ev_w_out[0].astype(BF16)
    xa = _mm(mix, w_out, tm=tm, tn=d, tk=w_out.shape[0], res=xa, gate=mod[2], group_of_tile=grp)

    w13 = ev_ffn_w13[0].astype(BF16)
    w2 = ev_ffn_w2[0].astype(BF16)
    hid = w2.shape[0]
    hidden = _modnorm_mm(xa, norm_ffn_w[0], mod[3], mod[4], w13, group_of_tile=grp_big, tm=tm_big,
                         tn=_half_or_full(hid), out_dtype=BF16, swiglu=True)
    xa = _mm(hidden, w2, tm=tm, tn=d, tk=hid, res=xa, gate=mod[5], group_of_tile=grp)

    mod = mods[1]
    glora = od_g1.shape[-1]
    glora_p = ((glora + LANES - 1) // LANES) * LANES
    p = {
        "w1": jnp.concatenate([od_w1[0, 0], od_w1[0, 1]], axis=1).astype(BF16),
        "w2": od_w2[0].astype(BF16),
        "a1": jnp.concatenate([od_a1[0, 0], od_a1[0, 1]], axis=1).astype(BF16),
        "a2": od_a2[0].astype(BF16),
        "g1": jnp.zeros((d, glora_p), BF16).at[:, :glora].set(od_g1[0].astype(BF16)),
        "g2": jnp.zeros((glora_p, d), BF16).at[:glora].set(od_g2[0].astype(BF16)),
        "w0": od_w0[0].astype(F32), "a0": od_a0[0].astype(F32),
        "k_k": od_k_k[0].reshape(1, d).astype(F32), "k_a": od_k_a[0].reshape(1, d).astype(F32),
        "r_k": od_r_k[0].reshape(1, d).astype(F32),
    }
    w_rkv = od_w_rkv[0].astype(BF16)
    xm = _rwkv_mix(xa, norm_mix_w[1], mod[0], mod[1], od_mu[0].astype(F32), rows=rows, tm=tm)
    r, k, v = [_mm(xm[s_i], w_rkv[s_i], tm=tm, tn=d, tk=d) for s_i in range(3)]
    f = _rwkv_post(r, k, v, xm[3], xm[4], xm[5], p, tm=tm // 2)
    y_f, y_b = _rwkv_scan(f[0:5], f[5:10], f[10], rows)
    mixed = _rwkv_readout(y_f, y_b, f[12], f[11], od_ln_w[0], od_ln_b[0], rows, tm)
    x_l = _mm(mixed, od_w_o[0].astype(BF16), tm=tm, tn=d, tk=d, res=xa, gate=mod[2], group_of_tile=grp)

    gates = _router(x_l, norm_ffn_w[1], mod[3], mod[4], od_router[0], tm=tm, group_of_tile=grp)
    n_e = od_router.shape[-1]
    n_rows, tile_expert, n_valid, d1, d2_scatter, d2_combine, wts = _route_plan(gates, n_e, MOE_TILE)
    tm_r = 256
    grp_r = rows.group_of_tile(tm_r)
    a_sorted = _moe_scatter(x_l, norm_ffn_w[1], mod[3], mod[4], d1, d2_scatter, n_rows, tm=tm_r, tg=MOE_TILE,
                            group_of_tile=grp_r)
    e_hid = od_moe_w13.shape[-1] // 2
    tn_e = e_hid // 4 if (e_hid // 4) % LANES == 0 else _half_or_full(e_hid)
    hid_sorted = _grouped_swiglu(a_sorted, od_moe_w13[0].astype(F32), tile_expert, n_valid, n_rows, tg=MOE_TILE,
                                 tn=tn_e)
    y_sorted = _grouped_mm(hid_sorted, od_moe_w2[0].astype(BF16), tile_expert, n_valid, tg=MOE_TILE)
    x_l = _combine(y_sorted, d1, d2_combine, wts, x_l, mod[5], tm=tm_r, group_of_tile=grp_r)
    return x_l.reshape(batch, seq, d)
```

```python
import functools

import jax
import jax.numpy as jnp
import numpy as np
from jax import lax
from jax.experimental import pallas as pl
from jax.experimental.pallas import tpu as pltpu

F32 = jnp.float32
BF16 = jnp.bfloat16

LANES = 128
GRID_W = 64
RET_HEADS = 4
RET_HD = 128
RET_W = RET_HEADS * RET_HD
RET_CHUNK = 128
RET_GN_EPS = 1e-5
NA_HEADS = 8
NA_HD = 64
NA_W = NA_HEADS * NA_HD
NA_KR = 8
NA_KC = 16
RWKV_HD = 64
RWKV_GN_EPS = 64e-5
RWKV_CHUNK = 64
TOP_K = 2
ROPE_BASE = 10000.0
NORM_EPS = 1e-6
NEG_BIG = -1e30
VMEM_LIMIT = 56 * 1024 * 1024


def _cparams(n_axes):
    return pltpu.CompilerParams(dimension_semantics=("arbitrary",) * n_axes, vmem_limit_bytes=VMEM_LIMIT)


def _dot(a, b):
    return jnp.dot(a.astype(BF16), b.astype(BF16), preferred_element_type=F32)


def _dot_nt(a, b):
    return lax.dot_general(a.astype(BF16), b.astype(BF16), (((1,), (1,)), ((), ())), preferred_element_type=F32)


def _dot_tn(a, b):
    return lax.dot_general(a.astype(BF16), b.astype(BF16), (((0,), (0,)), ((), ())), preferred_element_type=F32)


def _split2(x):
    hi = x.astype(BF16)
    return hi, (x - hi.astype(F32)).astype(BF16)


def _split3(x):
    hi = x.astype(BF16)
    r1 = x - hi.astype(F32)
    mid = r1.astype(BF16)
    lo = (r1 - mid.astype(F32)).astype(BF16)
    return hi, mid, lo


def _dot_hi(a, b):
    ah, am, al = _split3(a)
    bh, bm, bl = _split3(b)
    d = functools.partial(jnp.dot, preferred_element_type=F32)
    return (d(ah, bh) + (d(ah, bm) + d(am, bh)) + (d(am, bm) + d(ah, bl) + d(al, bh)))


def _dot_x3(a, b):
    ah, al = _split2(a)
    bh, bl = _split2(b)
    d = functools.partial(jnp.dot, preferred_element_type=F32)
    return d(ah, bh) + (d(ah, bl) + d(al, bh))


def _dot_exact_rhs(a, b_bf16):
    ah, am, al = _split3(a)
    d = functools.partial(jnp.dot, preferred_element_type=F32)
    return d(ah, b_bf16) + d(am, b_bf16) + d(al, b_bf16)


def _dot_exact_lhs(a_bf16, b):
    bh, bm, bl = _split3(b)
    d = functools.partial(jnp.dot, preferred_element_type=F32)
    return d(a_bf16, bh) + d(a_bf16, bm) + d(a_bf16, bl)


def _mod(v, n):
    return (v & (n - 1)) if n & (n - 1) == 0 else v % n


def _sigmoid(x):
    return 1.0 / (1.0 + jnp.exp(-x))


def _silu(x):
    return x * _sigmoid(x)


def _softplus(x):
    return jnp.maximum(x, 0.0) + jnp.log(1.0 + jnp.exp(-jnp.abs(x)))


def _group_mean_mat(width, group):
    r = lax.broadcasted_iota(jnp.int32, (width, width), 0) // group
    c = lax.broadcasted_iota(jnp.int32, (width, width), 1) // group
    return jnp.where(r == c, 1.0 / group, 0.0).astype(BF16)


def _group_mean(x, group):
    g = _group_mean_mat(LANES, group)
    cols = [_dot_exact_rhs(x[:, c:c + LANES], g) for c in range(0, x.shape[1], LANES)]
    return cols[0] if len(cols) == 1 else jnp.concatenate(cols, axis=1)


def _modnorm(x, nw, shift, scale):
    ms = jnp.mean(x * x, axis=-1, keepdims=True)
    y = x * lax.rsqrt(ms + NORM_EPS) * nw
    return y * (1.0 + scale) + shift


class _Rows:
    def __init__(self, batch, seq, ctx_len):
        self.batch, self.seq, self.ctx_len = batch, seq, ctx_len
        self.ml, self.mc = batch * seq, batch * ctx_len
        self.m = self.ml + self.mc

    def group_of_tile(self, tm):
        return lambda i: jnp.minimum((i * tm) // self.seq, self.batch)

    def chunk_block(self, chunk, backward):
        ncc, nlc = self.ctx_len // chunk, self.seq // chunk

        def block(b, t):
            if backward:
                t = jnp.where(t < ncc, ncc - 1 - t, nlc + 2 * ncc - 1 - t)
            return jnp.where(t < ncc, (self.ml + b * self.ctx_len) // chunk + t, (b * self.seq) // chunk + t - ncc)

        def position(t):
            if backward:
                t = jnp.where(t < ncc, ncc - 1 - t, nlc + 2 * ncc - 1 - t)
            return t

        return block, position


def _ada_kernel(c_ref, w_ref, b_ref, o_ref):
    s = _silu(c_ref[...])
    o_ref[0] = _dot_hi(s, w_ref[0]) + b_ref[0]


def _ada(c_rows, ada_w, ada_b):
    depth, d, n = ada_w.shape
    rows = c_rows.shape[0]
    tn = 1536
    return pl.pallas_call(
        _ada_kernel,
        out_shape=jax.ShapeDtypeStruct((depth, rows, n), F32),
        grid=(depth, n // tn),
        in_specs=[
            pl.BlockSpec((rows, d), lambda l, j: (0, 0)),
            pl.BlockSpec((1, d, tn), lambda l, j: (l, 0, j)),
            pl.BlockSpec((1, 1, tn), lambda l, j: (l, 0, j)),
        ],
        out_specs=pl.BlockSpec((1, rows, tn), lambda l, j: (l, 0, j)),
        compiler_params=_cparams(2),
        name="ada_mod",
    )(c_rows, ada_w, ada_b.reshape(depth, 1, n))


def _modnorm_mm_kernel(*refs, swiglu, gated):
    x_ref, nw_ref, sh_ref, sc_ref = refs[:4]
    pos = 4
    w_refs = refs[pos:pos + (2 if swiglu else 1)]
    pos += len(w_refs)
    gates_ref = refs[pos] if gated else None
    pos += 1 if gated else 0
    o_ref, h_ref = refs[pos], refs[pos + 1]

    first = pl.program_id(1) == 0
    if gated:
        first = jnp.logical_and(first, pl.program_id(2) == 0)

    @pl.when(first)
    def _():
        h_ref[...] = _modnorm(x_ref[...], nw_ref[...], sh_ref[0], sc_ref[0]).astype(BF16)

    h = h_ref[...]
    w0 = w_refs[0][0] if gated else w_refs[0][...]
    acc = jnp.dot(h, w0, preferred_element_type=F32)
    if swiglu:
        w1 = w_refs[1][0] if gated else w_refs[1][...]
        acc = _silu(acc) * jnp.dot(h, w1, preferred_element_type=F32)
    if gated:
        e = pl.program_id(1)
        lane = lax.broadcasted_iota(jnp.int32, gates_ref.shape, 1)
        g = jnp.sum(jnp.where(lane == e, gates_ref[...], 0.0), axis=-1, keepdims=True)
        acc = acc * g
    o_ref[...] = acc.astype(o_ref.dtype)


def _modnorm_mm(x, nw, shift, scale, w, *, group_of_tile, tm, tn, out_dtype, swiglu=False, gates=None, rows=None):
    m = x.shape[0] if rows is None else rows
    k = x.shape[1]
    gated = gates is not None
    n_total = w.shape[-1]
    n_out = n_total // 2 if swiglu else n_total
    nj = n_out // tn
    if gated:
        n_e = w.shape[0]
        grid = (m // tm, n_e, nj)
        xmap = lambda i, e, j: (i, 0)
        cmap = lambda i, e, j: (0, 0)
        gmap = lambda i, e, j: (group_of_tile(i), 0, 0)
        wspecs = [pl.BlockSpec((1, k, tn), lambda i, e, j: (e, 0, j))]
        if swiglu:
            wspecs.append(pl.BlockSpec((1, k, tn), lambda i, e, j: (e, 0, j + nj)))
        extra = [pl.BlockSpec((tm, LANES), lambda i, e, j: (i, 0))]
        omap = lambda i, e, j: (i, e * nj + j)
        out_cols = n_e * n_out
    else:
        grid = (m // tm, nj)
        xmap = lambda i, j: (i, 0)
        cmap = lambda i, j: (0, 0)
        gmap = lambda i, j: (group_of_tile(i), 0, 0)
        wspecs = [pl.BlockSpec((k, tn), lambda i, j: (0, j))]
        if swiglu:
            wspecs.append(pl.BlockSpec((k, tn), lambda i, j: (0, j + nj)))
        extra = []
        omap = lambda i, j: (i, j)
        out_cols = n_out
    args = [x, nw.reshape(1, k), shift, scale] + [w] * len(wspecs) + ([gates] if gated else [])
    return pl.pallas_call(
        functools.partial(_modnorm_mm_kernel, swiglu=swiglu, gated=gated),
        out_shape=jax.ShapeDtypeStruct((m, out_cols), out_dtype),
        grid=grid,
        in_specs=[pl.BlockSpec((tm, k), xmap), pl.BlockSpec((1, k), cmap),
                  pl.BlockSpec((1, 1, k), gmap), pl.BlockSpec((1, 1, k), gmap)] + wspecs + extra,
        out_specs=pl.BlockSpec((tm, tn), omap),
        scratch_shapes=[pltpu.VMEM((tm, k), BF16)],
        compiler_params=_cparams(len(grid)),
        name="modnorm_mm",
    )(*args)


def _mm_kernel(*refs, residual, nk):
    acc_ref = refs[-1] if nk > 1 else None
    if residual:
        a_ref, w_ref, res_ref, gate_ref, o_ref = refs[:5]
    else:
        a_ref, w_ref, o_ref = refs[:3]
    kk = pl.program_id(2)
    part = jnp.dot(a_ref[...], w_ref[...], preferred_element_type=F32)

    def finish(acc):
        if residual:
            acc = res_ref[...] + gate_ref[0] * acc
        o_ref[...] = acc.astype(o_ref.dtype)

    if nk == 1:
        finish(part)
    else:
        @pl.when(kk == 0)
        def _():
            acc_ref[...] = part

        @pl.when(jnp.logical_and(kk > 0, kk < nk - 1))
        def _():
            acc_ref[...] += part

        @pl.when(kk == nk - 1)
        def _():
            finish(acc_ref[...] + part)


def _mm(a, w, *, tm, tn, tk, out_dtype=F32, res=None, gate=None, group_of_tile=None):
    m, k = a.shape
    n = w.shape[1]
    nk = k // tk
    residual = res is not None
    in_specs = [pl.BlockSpec((tm, tk), lambda i, j, q: (i, q)), pl.BlockSpec((tk, tn), lambda i, j, q: (q, j))]
    args = [a, w]
    if residual:
        in_specs += [pl.BlockSpec((tm, tn), lambda i, j, q: (i, j)),
                     pl.BlockSpec((1, 1, tn), lambda i, j, q: (group_of_tile(i), 0, j))]
        args += [res, gate]
    return pl.pallas_call(
        functools.partial(_mm_kernel, residual=residual, nk=nk),
        out_shape=jax.ShapeDtypeStruct((m, n), out_dtype),
        grid=(m // tm, n // tn, nk),
        in_specs=in_specs,
        out_specs=pl.BlockSpec((tm, tn), lambda i, j, q: (i, j)),
        scratch_shapes=[pltpu.VMEM((tm, tn), F32)] if nk > 1 else [],
        compiler_params=_cparams(3),
        name="mm_res" if residual else "mm",
    )(*args)


def _rope(x, cos, sin_signed):
    lane = lax.broadcasted_iota(jnp.int32, x.shape, 1)
    half = RET_HD // 4
    swapped = jnp.where(_mod(lane, 2 * half) < half, pltpu.roll(x, LANES - half, 1), pltpu.roll(x, half, 1))
    return x * cos + swapped * sin_signed


def _ret_chunks(q, k, v, s0, log_g, n_forward):
    c = RET_CHUNK
    shape = (q.shape[0], c, c)
    row = lax.broadcasted_iota(jnp.int32, shape, 1)
    col = lax.broadcasted_iota(jnp.int32, shape, 2)
    fwd = lax.broadcasted_iota(jnp.int32, shape, 0) < n_forward
    dist = jnp.where(fwd, row - col, col - row).astype(F32)
    q_steps = jnp.where(fwd, row + 1, c - row).astype(F32)
    k_steps = jnp.where(fwd, c - 1 - row, row).astype(F32)
    dmat = jnp.where(dist >= 0, jnp.exp(log_g * jnp.maximum(dist, 0.0)), 0.0)
    q_dec = jnp.exp(log_g * q_steps)
    k_dec = jnp.exp(log_g * k_steps)
    scores = _bdot_nt(q, k) * dmat
    out = _bdot(scores, v) + _bdot(q * q_dec, s0)
    s1 = s0 * jnp.exp(log_g * float(c)) + _bdot_tn(k * k_dec, v)
    return out, s1


def _retention_kernel(qf_ref, kf_ref, vf_ref, cf_ref, sf_ref, qb_ref, kb_ref, vb_ref, cb_ref, sb_ref,
                      decf_ref, decb_ref, of_ref, ob_ref, st_ref):
    @pl.when(pl.program_id(1) == 0)
    def _():
        st_ref[...] = jnp.zeros_like(st_ref)

    h = RET_HEADS
    kscale = RET_HD ** -0.5

    def heads(ref, cos, sin, scale=None):
        out = []
        for hh in range(h):
            xh = ref[:, hh * RET_HD:(hh + 1) * RET_HD]
            if cos is not None:
                xh = _rope(xh, cos, sin)
            out.append(xh if scale is None else xh * scale)
        return out

    cf, sf, cb, sb = cf_ref[...], sf_ref[...], cb_ref[...], sb_ref[...]
    q = jnp.stack(heads(qf_ref, cf, sf) + heads(qb_ref, cb, sb), axis=0)
    k = jnp.stack(heads(kf_ref, cf, sf, kscale) + heads(kb_ref, cb, sb, kscale), axis=0)
    v = jnp.stack(heads(vf_ref, None, None) + heads(vb_ref, None, None), axis=0)
    log_g = -jnp.exp(jnp.concatenate([decf_ref[...], decb_ref[...]], axis=0))
    out, s1 = _ret_chunks(q, k, v, st_ref[...], log_g, h)
    st_ref[...] = s1
    of_ref[...] = jnp.concatenate([out[hh] for hh in range(h)], axis=1)
    ob_ref[...] = jnp.concatenate([out[h + hh] for hh in range(h)], axis=1)


def _retention(proj, cos, sin_signed, dec_f, dec_b, rows):
    c, h = RET_CHUNK, RET_HEADS
    n = (rows.seq + rows.ctx_len) // c
    blk_f, pos_f = rows.chunk_block(c, False)
    blk_b, pos_b = rows.chunk_block(c, True)

    def tok(blk, off):
        return pl.BlockSpec((c, RET_W), lambda bi, t: (blk(bi, t), off))

    def tab(pos):
        return pl.BlockSpec((c, RET_HD), lambda bi, t: (pos(t), 0))

    dec = lambda a: jnp.broadcast_to(a.astype(F32)[:, None, None], (h, 1, LANES))
    dspec = pl.BlockSpec((h, 1, LANES), lambda bi, t: (0, 0, 0))
    return pl.pallas_call(
        _retention_kernel,
        out_shape=[jax.ShapeDtypeStruct((rows.m, RET_W), F32)] * 2,
        grid=(rows.batch, n),
        in_specs=[tok(blk_f, 0), tok(blk_f, 1), tok(blk_f, 2), tab(pos_f), tab(pos_f),
                  tok(blk_b, 0), tok(blk_b, 1), tok(blk_b, 2), tab(pos_b), tab(pos_b), dspec, dspec],
        out_specs=[tok(blk_f, 0), tok(blk_b, 0)],
        scratch_shapes=[pltpu.VMEM((2 * h, RET_HD, RET_HD), F32)],
        compiler_params=_cparams(2),
        name="retention",
    )(proj, proj, proj, cos, sin_signed, proj, proj, proj, cos, sin_signed, dec(dec_f), dec(dec_b))


def _na_qk_norm(x, w):
    ms = _group_mean(x * x, NA_HD)
    return x * lax.rsqrt(ms + NORM_EPS) * w


def _softmax_pv(parts):
    m = functools.reduce(jnp.maximum, [jnp.max(s, axis=-1, keepdims=True) for s, _ in parts])
    ps = [jnp.exp(s - m) for s, _ in parts]
    den = functools.reduce(jnp.add, [jnp.sum(p, axis=-1, keepdims=True) for p in ps])
    num = functools.reduce(jnp.add, [_dot(p, v) for p, (_, v) in zip(ps, parts)])
    return num / den


def _na_kernel(q_ref, k_ref, v_ref, kc_ref, vc_ref, qn_ref, kn_ref, bias_ref, o_ref, qs_ref, ks_ref, vs_ref,
               kcs_ref, *, rows):
    scale = NA_HD ** -0.5
    qs_ref[...] = (_na_qk_norm(q_ref[...], qn_ref[...]) * scale).astype(BF16)
    ks_ref[...] = _na_qk_norm(k_ref[...], kn_ref[...]).astype(BF16)
    vs_ref[...] = v_ref[...].astype(BF16)
    kcs_ref[...] = _na_qk_norm(kc_ref[...], kn_ref[...]).astype(BF16)
    lane = lax.broadcasted_iota(jnp.int32, (GRID_W, LANES), 1)
    first = lane < NA_HD
    vc = vc_ref[...]
    kc = kcs_ref[...]

    nq = 2 * GRID_W

    def body(rb, carry):
        qs, kw, vw, bw = [], [], [], []
        for i in range(NA_ROWS_PER_STEP):
            r = rb * NA_ROWS_PER_STEP + i
            r0 = jnp.clip(r - NA_KR // 2, 0, rows - NA_KR)
            q_r = qs_ref[pl.ds(pl.multiple_of(r * GRID_W, GRID_W), GRID_W), :]
            zero = jnp.zeros_like(q_r)
            qs.append(jnp.concatenate([jnp.where(first, q_r, zero), jnp.where(first, zero, q_r)], axis=0))
            win = pl.ds(pl.multiple_of(r0 * GRID_W, GRID_W), NA_KR * GRID_W)
            kw.append(ks_ref[win, :])
            vw.append(vs_ref[win, :])
            bw.append(bias_ref[0, r0 - r + (NA_KR - 1)])
        q = jnp.stack(qs, axis=0)
        s_w = _bdot_nt(q, jnp.stack(kw, axis=0)) + jnp.stack(bw, axis=0)
        q_flat = q.reshape(NA_ROWS_PER_STEP * nq, LANES)
        s_c = _dot_nt(q_flat, kc).reshape(NA_ROWS_PER_STEP, nq, kc.shape[0])
        m = jnp.maximum(jnp.max(s_w, axis=-1, keepdims=True), jnp.max(s_c, axis=-1, keepdims=True))
        p_w = jnp.exp(s_w - m)
        p_c = jnp.exp(s_c - m)
        den = jnp.sum(p_w, axis=-1, keepdims=True) + jnp.sum(p_c, axis=-1, keepdims=True)
        num = _bdot(p_w, jnp.stack(vw, axis=0)) + _dot(p_c.reshape(NA_ROWS_PER_STEP * nq, kc.shape[0]), vc).reshape(
            NA_ROWS_PER_STEP, nq, LANES)
        out = num / den
        for i in range(NA_ROWS_PER_STEP):
            r = rb * NA_ROWS_PER_STEP + i
            o_ref[pl.ds(pl.multiple_of(r * GRID_W, GRID_W), GRID_W), :] = jnp.where(
                first, out[i, :GRID_W], out[i, GRID_W:])
        return carry

    lax.fori_loop(0, rows // NA_ROWS_PER_STEP, body, 0)


NA_ROWS_PER_STEP = 4


def _na_bias_table(rpb):
    cols = np.arange(GRID_W)
    start = np.clip(cols - NA_KC // 2, 0, GRID_W - NA_KC)
    kcol = np.arange(GRID_W)
    inside = (kcol[None, :] >= start[:, None]) & (kcol[None, :] < start[:, None] + NA_KC)
    col_off = np.clip(kcol[None, :] - cols[:, None] + (NA_KC - 1), 0, 2 * NA_KC - 2)
    row_off = np.arange(NA_KR)[:, None] + np.arange(NA_KR)[None, :]
    tab = rpb[:, row_off][:, :, :, col_off]
    tab = jnp.where(inside[None, None, None], tab, NEG_BIG)
    tab = jnp.transpose(tab, (0, 1, 3, 2, 4))
    tab = tab.reshape(rpb.shape[0] // 2, 2, NA_KR, GRID_W, NA_KR * GRID_W).transpose(0, 2, 1, 3, 4)
    return tab.reshape(rpb.shape[0] // 2, NA_KR, 2 * GRID_W, NA_KR * GRID_W).astype(F32)


def _na_latent(proj, qn_w, kn_w, bias, rows, col0):
    seq, ctx_len = rows.seq, rows.ctx_len
    grid_rows = seq // GRID_W
    pairs = NA_W // LANES
    cb = col0 // LANES
    cblk = rows.ml // ctx_len
    tile2 = lambda a: jnp.tile(a.astype(F32), 2).reshape(1, LANES)
    lat = lambda off: pl.BlockSpec((seq, LANES), lambda bi, p: (bi, cb + off * pairs + p))
    ctx = lambda off: pl.BlockSpec((ctx_len, LANES), lambda bi, p: (cblk + bi, cb + off * pairs + p))
    return pl.pallas_call(
        functools.partial(_na_kernel, rows=grid_rows),
        out_shape=jax.ShapeDtypeStruct((rows.ml, NA_W), F32),
        grid=(rows.batch, pairs),
        in_specs=[lat(0), lat(1), lat(2), ctx(1), ctx(2),
                  pl.BlockSpec((1, LANES), lambda bi, p: (0, 0)), pl.BlockSpec((1, LANES), lambda bi, p: (0, 0)),
                  pl.BlockSpec((1, NA_KR, 2 * GRID_W, NA_KR * GRID_W), lambda bi, p: (p, 0, 0, 0))],
        out_specs=pl.BlockSpec((seq, LANES), lambda bi, p: (bi, p)),
        scratch_shapes=[pltpu.VMEM((seq, LANES), BF16)] * 3 + [pltpu.VMEM((ctx_len, LANES), BF16)],
        compiler_params=_cparams(2),
        name="na_latent",
    )(proj, proj, proj, proj, proj, tile2(qn_w), tile2(kn_w), bias)


def _ctx_attn_kernel(q_ref, k_ref, v_ref, qn_ref, kn_ref, o_ref):
    scale = NA_HD ** -0.5
    q = (_na_qk_norm(q_ref[...], qn_ref[...]) * scale).astype(BF16)
    k = _na_qk_norm(k_ref[...], kn_ref[...]).astype(BF16)
    v = v_ref[...]
    lane = lax.broadcasted_iota(jnp.int32, q.shape, 1)
    first = lane < NA_HD
    outs = []
    for hh in range(2):
        qh = jnp.where(first if hh == 0 else jnp.logical_not(first), q, jnp.zeros_like(q))
        outs.append(_softmax_pv([(_dot_nt(qh, k), v)]))
    o_ref[...] = jnp.where(first, outs[0], outs[1])


def _ctx_attention(proj, qn_w, kn_w, rows, col0):
    ctx_len = rows.ctx_len
    pairs = NA_W // LANES
    cb = col0 // LANES
    cblk = rows.ml // ctx_len
    tile2 = lambda a: jnp.tile(a.astype(F32), 2).reshape(1, LANES)
    blk = lambda off: pl.BlockSpec((ctx_len, LANES), lambda bi, p: (cblk + bi, cb + off * pairs + p))
    return pl.pallas_call(
        _ctx_attn_kernel,
        out_shape=jax.ShapeDtypeStruct((rows.mc, NA_W), F32),
        grid=(rows.batch, pairs),
        in_specs=[blk(0), blk(1), blk(2),
                  pl.BlockSpec((1, LANES), lambda bi, p: (0, 0)), pl.BlockSpec((1, LANES), lambda bi, p: (0, 0))],
        out_specs=pl.BlockSpec((ctx_len, LANES), lambda bi, p: (bi, p)),
        compiler_params=_cparams(2),
        name="ctx_attention",
    )(proj, proj, proj, tile2(qn_w), tile2(kn_w))


def _ret_readout_kernel(yf_ref, yb_ref, g_ref, nal_ref, nac_ref, gnw_ref, o_ref, *, n_latent_tiles):
    y = yf_ref[...] + yb_ref[...]
    for hh in range(RET_HEADS):
        sl = slice(hh * RET_HD, (hh + 1) * RET_HD)
        yh = y[:, sl]
        yc = yh - jnp.mean(yh, axis=-1, keepdims=True)
        yn = yc * lax.rsqrt(jnp.mean(yc * yc, axis=-1, keepdims=True) + RET_GN_EPS)
        o_ref[:, sl] = (yn * gnw_ref[:, sl] * _silu(g_ref[:, sl])).astype(o_ref.dtype)
    is_latent = pl.program_id(0) < n_latent_tiles
    o_ref[:, RET_W:] = jnp.where(is_latent, nal_ref[...], nac_ref[...]).astype(o_ref.dtype)


def _ret_readout(yf, yb, proj, na_l, na_c, gn_w, rows, tm):
    nl = rows.ml // tm
    row = lambda i: (i, 0)
    return pl.pallas_call(
        functools.partial(_ret_readout_kernel, n_latent_tiles=nl),
        out_shape=jax.ShapeDtypeStruct((rows.m, RET_W + NA_W), BF16),
        grid=(rows.m // tm,),
        in_specs=[pl.BlockSpec((tm, RET_W), row), pl.BlockSpec((tm, RET_W), row),
                  pl.BlockSpec((tm, RET_W), lambda i: (i, 3)),
                  pl.BlockSpec((tm, NA_W), lambda i: (jnp.minimum(i, nl - 1), 0)),
                  pl.BlockSpec((tm, NA_W), lambda i: (jnp.maximum(i - nl, 0), 0)),
                  pl.BlockSpec((1, RET_W), lambda i: (0, 0))],
        out_specs=pl.BlockSpec((tm, RET_W + NA_W), row),
        compiler_params=_cparams(1),
        name="ret_readout",
    )(yf, yb, proj, na_l, na_c, gn_w.reshape(1, RET_W).astype(F32))


def _rwkv_mix_kernel(x_ref, xp_ref, xn_ref, nw_ref, sh_ref, sc_ref, mu_ref, wrkv_ref, w1_ref, a1_ref, g1_ref,
                     r_ref, k_ref, v_ref, lwl_ref, al_ref, gl_ref, *, seq, ctx_len, ml, tm):
    nw, sh, sc = nw_ref[...], sh_ref[0], sc_ref[0]
    h = _modnorm(x_ref[...], nw, sh, sc)
    hp = _modnorm(xp_ref[...], nw, sh, sc)[7:8]
    hn = _modnorm(xn_ref[...], nw, sh, sc)[0:1]
    row = lax.broadcasted_iota(jnp.int32, h.shape, 0)
    g = pl.program_id(0) * tm + row
    latent = g < ml
    pos = jnp.where(latent, _mod(g, seq), _mod(g, ctx_len))
    first = pos == 0
    last = pos == jnp.where(latent, seq - 1, ctx_len - 1)
    up = jnp.where(row == 0, hp, pltpu.roll(h, 1, 0))
    up = jnp.where(first, 0.0, up)
    dn = jnp.where(row == tm - 1, hn, pltpu.roll(h, tm - 1, 0))
    dn = jnp.where(last, 0.0, dn)
    xx = 0.5 * (up + dn) - h
    mix = lambda s_i: (h + xx * mu_ref[s_i:s_i + 1, :]).astype(BF16)
    for s_i, o_ref in enumerate((r_ref, k_ref, v_ref)):
        o_ref[...] = jnp.dot(mix(s_i), wrkv_ref[s_i], preferred_element_type=F32)
    lwl_ref[...] = jnp.tanh(jnp.dot(mix(3), w1_ref[...], preferred_element_type=F32))
    al_ref[...] = jnp.dot(mix(4), a1_ref[...], preferred_element_type=F32)
    gl_ref[...] = _sigmoid(jnp.dot(mix(5), g1_ref[...], preferred_element_type=F32))


def _rwkv_mix(x, nw, shift, scale, mu, w_rkv, p, *, rows, tm):
    m, d = x.shape
    n_slab = m // 8
    per = tm // 8
    gmap = lambda i: (rows.group_of_tile(tm)(i), 0, 0)
    row = lambda i: (i, 0)
    full2 = lambda i: (0, 0)
    smalls = [p["w1"], p["a1"], p["g1"]]
    return pl.pallas_call(
        functools.partial(_rwkv_mix_kernel, seq=rows.seq, ctx_len=rows.ctx_len, ml=rows.ml, tm=tm),
        out_shape=[jax.ShapeDtypeStruct((m, d), F32)] * 3
        + [jax.ShapeDtypeStruct((m, w.shape[1]), F32) for w in smalls],
        grid=(m // tm,),
        in_specs=[pl.BlockSpec((tm, d), row),
                  pl.BlockSpec((8, d), lambda i: (jnp.maximum(i * per - 1, 0), 0)),
                  pl.BlockSpec((8, d), lambda i: (jnp.minimum((i + 1) * per, n_slab - 1), 0)),
                  pl.BlockSpec((1, d), full2),
                  pl.BlockSpec((1, 1, d), gmap), pl.BlockSpec((1, 1, d), gmap),
                  pl.BlockSpec((6, d), full2),
                  pl.BlockSpec((3, d, d), lambda i: (0, 0, 0))]
        + [pl.BlockSpec(w.shape, full2) for w in smalls],
        out_specs=[pl.BlockSpec((tm, d), row)] * 3 + [pl.BlockSpec((tm, w.shape[1]), row) for w in smalls],
        compiler_params=_cparams(1),
        name="rwkv_mix",
    )(x, x, x, nw.reshape(1, d), shift, scale, mu, w_rkv, *smalls)


def _rwkv_post_kernel(r_ref, k_ref, v_ref, lwl_ref, al_ref, gl_ref, w2_ref, a2_ref, g2_ref,
                      w0_ref, a0_ref, kk_ref, ka_ref, rk_ref,
                      rt0, kt0, bt0, at0, wc0, rt1, kt1, bt1, at1, wc1, vb_ref, g_ref, bonus_ref, *, tm):
    c = RWKV_CHUNK
    n_pairs = r_ref.shape[1] // LANES
    r, k, v = r_ref[...], k_ref[...], v_ref[...]
    lw_low, a_low = lwl_ref[...], al_ref[...]
    g_ref[...] = _dot(gl_ref[...], g2_ref[...]).astype(g_ref.dtype)

    def put(o_ref, val):
        for p in range(n_pairs):
            o_ref[p] = val[:, p * LANES:(p + 1) * LANES].astype(o_ref.dtype)

    put(vb_ref, v)
    kk = k * kk_ref[...]
    nrm = jnp.sqrt(_group_mean(kk * kk, RWKV_HD) * float(RWKV_HD))
    kk = kk / jnp.maximum(nrm, 1e-12)

    tr = lax.broadcasted_iota(jnp.int32, (c, c), 0)
    tc = lax.broadcasted_iota(jnp.int32, (c, c), 1)
    tri = [(tr >= tc).astype(BF16), (tr <= tc).astype(BF16)]
    lora = w2_ref.shape[1]
    coeff_src = jnp.zeros_like(r)
    outs = [(rt0, kt0, bt0, at0, wc0), (rt1, kt1, bt1, at1, wc1)]
    for z in range(2):
        rt_ref, kt_ref, bt_ref, at_ref, wc_ref = outs[z]
        w_lora = _dot(lw_low[:, z * lora:(z + 1) * lora], w2_ref[z])
        lw = -float(np.exp(-0.5)) * _sigmoid(w0_ref[z:z + 1, :] + w_lora)
        a_z = _sigmoid(a0_ref[z:z + 1, :] + _dot(a_low[:, z * lora:(z + 1) * lora], a2_ref[z]))
        k_dir = k * (1.0 + (a_z - 1.0) * ka_ref[...])
        coeff_src = coeff_src + r * k_dir * rk_ref[...]
        cum = jnp.concatenate([_dot_exact_lhs(tri[z], lw[ci * c:(ci + 1) * c]) for ci in range(tm // c)], axis=0)
        e_pos = jnp.exp(cum)
        e_neg = jnp.exp(-cum)
        put(rt_ref, r * e_pos)
        put(kt_ref, k_dir * e_neg)
        put(bt_ref, kk * a_z * e_neg)
        put(at_ref, -kk * jnp.exp(cum - lw))
        for ci in range(tm // c):
            end = ci * c + (c - 1 if z == 0 else 0)
            for p in range(n_pairs):
                wc_ref[p, ci] = e_pos[end:end + 1, p * LANES:(p + 1) * LANES]
    coeff = _group_mean(coeff_src, RWKV_HD) * float(RWKV_HD)
    bonus_ref[...] = coeff * v


def _rwkv_post(r, k, v, lw_low, a_low, g_low, p, *, tm):
    m, d = r.shape
    c = RWKV_CHUNK
    lora = p["w2"].shape[1]
    glora = p["g1"].shape[1]
    small = lambda a: pl.BlockSpec((tm, a.shape[1]), lambda i: (i, 0))
    n_pairs = d // LANES
    pair = jax.ShapeDtypeStruct((n_pairs, m, LANES), BF16)
    chunk = jax.ShapeDtypeStruct((n_pairs, m // c, 1, LANES), F32)
    row = lambda i: (i, 0)
    full2 = lambda i: (0, 0)
    full3 = lambda i: (0, 0, 0)
    tspec = pl.BlockSpec((tm, d), row)
    pspec = pl.BlockSpec((n_pairs, tm, LANES), lambda i: (0, i, 0))
    cspec = pl.BlockSpec((n_pairs, tm // c, 1, LANES), lambda i: (0, i, 0, 0))
    return pl.pallas_call(
        functools.partial(_rwkv_post_kernel, tm=tm),
        out_shape=[pair] * 4 + [chunk] + [pair] * 4 + [chunk]
        + [pair, jax.ShapeDtypeStruct((m, d), BF16), jax.ShapeDtypeStruct((m, d), F32)],
        grid=(m // tm,),
        in_specs=[tspec] * 3 + [small(lw_low), small(a_low), small(g_low)]
        + [pl.BlockSpec((2, lora, d), full3), pl.BlockSpec((2, lora, d), full3), pl.BlockSpec((glora, d), full2),
           pl.BlockSpec((2, d), full2), pl.BlockSpec((2, d), full2)]
        + [pl.BlockSpec((1, d), full2)] * 3,
        out_specs=[pspec] * 4 + [cspec] + [pspec] * 4 + [cspec] + [pspec, tspec, tspec],
        compiler_params=_cparams(1),
        name="rwkv_post",
    )(r, k, v, lw_low, a_low, g_low, p["w2"], p["a2"], p["g2"], p["w0"], p["a0"], p["k_k"], p["k_a"], p["r_k"])


def _bdot(a, b):
    return jnp.einsum("ucd,ude->uce", a.astype(BF16), b.astype(BF16), preferred_element_type=F32)


def _bdot_nt(a, b):
    return jnp.einsum("ucd,usd->ucs", a.astype(BF16), b.astype(BF16), preferred_element_type=F32)


def _bdot_tn(a, b):
    return jnp.einsum("uce,ucd->ued", a.astype(BF16), b.astype(BF16), preferred_element_type=F32)


def _scan_chunks(rt, kt, bt, at, v, wc, s0, n_forward):
    c = RWKV_CHUNK
    n_units = rt.shape[0]
    lane = lax.broadcasted_iota(jnp.int32, (n_units, c, LANES), 2)
    row = lax.broadcasted_iota(jnp.int32, (n_units, c, LANES), 1)
    unit = lax.broadcasted_iota(jnp.int32, (n_units, c, LANES), 0)
    head0 = lane < RWKV_HD
    src = _mod(lane, RWKV_HD)
    ahead = jnp.where(unit < n_forward, row - src, src - row)
    strict = ahead > 0
    incl = ahead >= 0

    def dup(x):
        zero = jnp.zeros_like(x)
        h0 = head0[:, :x.shape[1]]
        return jnp.concatenate([jnp.where(h0, x, zero), jnp.where(h0, zero, x)], axis=1)

    ar = jnp.concatenate([at, rt], axis=1)
    mb = _bdot_nt(ar, dup(bt))
    mk = _bdot_nt(ar, dup(kt))
    p_ab = jnp.where(strict, mb[:, :c], 0.0)
    p_rb = jnp.where(incl, mb[:, c:], 0.0)
    p_ak = jnp.where(strict, mk[:, :c], 0.0)
    p_rk = jnp.where(incl, mk[:, c:], 0.0)
    vd = dup(v)
    rhs = _bdot_nt(at, s0) + _bdot(p_ak, vd)

    power = p_ab.astype(BF16)
    inv = jnp.where(row == src, 1.0, 0.0) + p_ab
    for _ in range(int(np.log2(c)) - 1):
        power = _bdot(power, dup(power)).astype(BF16)
        inv = inv + _bdot(inv, dup(power))
    inv = inv.astype(BF16)
    u = _bdot(inv, dup(rhs))
    resid = (rhs - u) + _bdot(p_ab, dup(u))
    u = u + _bdot(inv, dup(resid))

    y = _bdot_nt(rt, s0) + _bdot(jnp.concatenate([p_rb, p_rk], axis=2), jnp.concatenate([dup(u), vd], axis=1))
    upd = _bdot_tn(jnp.concatenate([u.astype(BF16), v], axis=1), jnp.concatenate([bt, kt], axis=1))
    er = lax.broadcasted_iota(jnp.int32, (1, LANES, LANES), 1) < RWKV_HD
    ec = lax.broadcasted_iota(jnp.int32, (1, LANES, LANES), 2) < RWKV_HD
    s1 = (s0 + jnp.where(er == ec, upd, 0.0)) * wc
    return y, s1


def _rwkv_scan_kernel(rtf, ktf, btf, atf, vf, wcf, rtb, ktb, btb, atb, vb, wcb, yf_ref, yb_ref, s_ref):
    @pl.when(pl.program_id(1) == 0)
    def _():
        s_ref[...] = jnp.zeros_like(s_ref)

    n_pairs = rtf.shape[0]
    both = lambda f, b: jnp.concatenate([f[...], b[...]], axis=0)
    wc = jnp.concatenate([wcf[:, 0], wcb[:, 0]], axis=0)
    y, s1 = _scan_chunks(both(rtf, rtb), both(ktf, ktb), both(btf, btb), both(atf, atb), both(vf, vb), wc,
                         s_ref[...], n_pairs)
    s_ref[...] = s1
    yf_ref[...] = y[:n_pairs]
    yb_ref[...] = y[n_pairs:]


def _rwkv_scan(feats_f, feats_b, v, rows):
    n_pairs, m, _ = v.shape
    c = RWKV_CHUNK
    d = n_pairs * LANES
    nchunks = (rows.seq + rows.ctx_len) // c
    blk_f, _ = rows.chunk_block(c, False)
    blk_b, _ = rows.chunk_block(c, True)

    def specs(blk):
        tok = pl.BlockSpec((n_pairs, c, LANES), lambda bi, t: (0, blk(bi, t), 0))
        return tok, pl.BlockSpec((n_pairs, 1, 1, LANES), lambda bi, t: (0, blk(bi, t), 0, 0))

    tok_f, wc_f = specs(blk_f)
    tok_b, wc_b = specs(blk_b)
    return pl.pallas_call(
        _rwkv_scan_kernel,
        out_shape=[jax.ShapeDtypeStruct((n_pairs, m, LANES), F32)] * 2,
        grid=(rows.batch, nchunks),
        in_specs=[tok_f] * 5 + [wc_f] + [tok_b] * 5 + [wc_b],
        out_specs=[tok_f, tok_b],
        scratch_shapes=[pltpu.VMEM((2 * n_pairs, LANES, LANES), F32)],
        compiler_params=_cparams(2),
        name="rwkv_scan",
    )(*feats_f[:4], v, feats_f[4], *feats_b[:4], v, feats_b[4])


def _rwkv_readout_kernel(yf_ref, yb_ref, bonus_ref, g_ref, lnw_ref, lnb_ref, o_ref):
    y = jnp.concatenate([yf_ref[p] + yb_ref[p] for p in range(yf_ref.shape[0])], axis=1)
    yc = y - _group_mean(y, RWKV_HD)
    yn = yc * lax.rsqrt(_group_mean(yc * yc, RWKV_HD) + RWKV_GN_EPS) * lnw_ref[...] + lnb_ref[...]
    o_ref[...] = ((yn + bonus_ref[...]) * g_ref[...].astype(F32)).astype(o_ref.dtype)


def _rwkv_readout(yf, yb, bonus, g, ln_w, ln_b, rows, tm):
    n_pairs = yf.shape[0]
    d = n_pairs * LANES
    pspec = pl.BlockSpec((n_pairs, tm, LANES), lambda i: (0, i, 0))
    tspec = pl.BlockSpec((tm, d), lambda i: (i, 0))
    vspec = pl.BlockSpec((1, d), lambda i: (0, 0))
    return pl.pallas_call(
        _rwkv_readout_kernel,
        out_shape=jax.ShapeDtypeStruct((rows.ml, d), BF16),
        grid=(rows.ml // tm,),
        in_specs=[pspec, pspec, tspec, tspec, vspec, vspec],
        out_specs=tspec,
        compiler_params=_cparams(1),
        name="rwkv_readout",
    )(yf, yb, bonus, g, ln_w.reshape(1, d).astype(F32), ln_b.reshape(1, d).astype(F32))


def _router_kernel(x_ref, nw_ref, sh_ref, sc_ref, w_ref, o_ref, *, n_experts):
    h = _modnorm(x_ref[...], nw_ref[...], sh_ref[0], sc_ref[0])
    logits = _dot_hi(h, w_ref[...])
    lane = lax.broadcasted_iota(jnp.int32, logits.shape, 1).astype(F32)
    logits = jnp.where(lane < n_experts, logits, -jnp.inf)
    v1 = jnp.max(logits, axis=-1, keepdims=True)
    i1 = jnp.min(jnp.where(logits == v1, lane, float(LANES)), axis=-1, keepdims=True)
    rest = jnp.where(lane == i1, -jnp.inf, logits)
    v2 = jnp.max(rest, axis=-1, keepdims=True)
    i2 = jnp.min(jnp.where(rest == v2, lane, float(LANES)), axis=-1, keepdims=True)
    e2 = jnp.exp(v2 - v1)
    den = 1.0 + e2
    o_ref[...] = jnp.where(lane == i1, 1.0 / den, 0.0) + jnp.where(lane == i2, e2 / den, 0.0)


def _router(x, nw, shift, scale, router, *, tm, group_of_tile):
    m, d = x.shape
    n_experts = router.shape[1]
    w = jnp.zeros((d, LANES), F32).at[:, :n_experts].set(router.astype(F32))
    gmap = lambda i: (group_of_tile(i), 0, 0)
    return pl.pallas_call(
        functools.partial(_router_kernel, n_experts=n_experts),
        out_shape=jax.ShapeDtypeStruct((m, LANES), F32),
        grid=(m // tm,),
        in_specs=[pl.BlockSpec((tm, d), lambda i: (i, 0)), pl.BlockSpec((1, d), lambda i: (0, 0)),
                  pl.BlockSpec((1, 1, d), gmap), pl.BlockSpec((1, 1, d), gmap),
                  pl.BlockSpec((d, LANES), lambda i: (0, 0))],
        out_specs=pl.BlockSpec((tm, LANES), lambda i: (i, 0)),
        compiler_params=_cparams(1),
        name="router",
    )(x, nw.reshape(1, d), shift, scale, w)


MOE_TILE = 512
DMA_UNROLL = 8


def _route_plan(gates, n_experts, tg):
    m = gates.shape[0]
    g = gates[:, :n_experts]
    sel = g > 0.0
    seli = sel.astype(jnp.int32)
    slot = jnp.cumsum(seli, axis=1) - 1
    rank = jnp.cumsum(seli, axis=0) - 1
    counts = jnp.sum(seli, axis=0)
    padded = ((counts + tg - 1) // tg) * tg
    ends = jnp.cumsum(padded)
    dest = (ends - padded)[None, :] + rank
    n_rows = TOP_K * m + n_experts * tg
    n_tiles = n_rows // tg
    first = sel & (slot == 0)
    second = sel & (slot == 1)
    pick = lambda msk, val: jnp.sum(jnp.where(msk, val, 0), axis=1)
    d1 = pick(first, dest)
    has2 = jnp.any(second, axis=1)
    d2 = pick(second, dest)
    w1 = pick(first, g)
    w2 = pick(second, g)
    tile_expert = jnp.minimum(jnp.sum((jnp.arange(n_tiles, dtype=jnp.int32) * tg)[:, None] >= ends[None, :], axis=1),
                              n_experts - 1).astype(jnp.int32)
    n_valid = (ends[-1] // tg).astype(jnp.int32).reshape(1)
    wts = jnp.zeros((m, LANES), F32).at[:, 0].set(w1).at[:, 1].set(w2)
    i32 = lambda a: a.astype(jnp.int32)
    return (n_rows, tile_expert, n_valid, i32(d1), i32(jnp.where(has2, d2, n_rows)), i32(jnp.where(has2, d2, d1)),
            wts)


def _slab_rows(ref3, val):
    for s in range(ref3.shape[1]):
        ref3[:, s, :] = val[:, s * LANES:(s + 1) * LANES].astype(ref3.dtype)


def _unslab_rows(ref3):
    return jnp.concatenate([ref3[:, s, :] for s in range(ref3.shape[1])], axis=1)


def _moe_scatter_kernel(d1_ref, d2_ref, x_ref, nw_ref, sh_ref, sc_ref, a0_hbm, a_hbm, h_ref, sem, *, tm):
    del a0_hbm
    _slab_rows(h_ref, _modnorm(x_ref[...], nw_ref[...], sh_ref[0], sc_ref[0]))

    def row_copy(r, dst):
        return pltpu.make_async_copy(h_ref.at[pl.ds(r, 1)], a_hbm.at[pl.ds(dst, 1)], sem)

    def issue(r, carry):
        row_copy(r, d1_ref[0, 0, r]).start(priority=0)
        row_copy(r, d2_ref[0, 0, r]).start(priority=1)
        return carry

    def drain(r, carry):
        row_copy(r, 0).wait()
        row_copy(r, 0).wait()
        return carry

    lax.fori_loop(0, tm, issue, 0, unroll=DMA_UNROLL)
    lax.fori_loop(0, tm, drain, 0, unroll=DMA_UNROLL)


def _moe_scatter(x, nw, shift, scale, d1, d2, n_rows, *, tm, tg, group_of_tile):
    m, d = x.shape
    slabs = d // LANES
    idx = lambda a: a.reshape(m // tm, 1, tm)
    ispec = pl.BlockSpec((1, 1, tm), lambda i: (i, 0, 0), memory_space=pltpu.SMEM)
    gmap = lambda i: (group_of_tile(i), 0, 0)
    zeros = jnp.zeros((n_rows + tg, slabs, LANES), F32)
    return pl.pallas_call(
        functools.partial(_moe_scatter_kernel, tm=tm),
        out_shape=jax.ShapeDtypeStruct(zeros.shape, F32),
        grid=(m // tm,),
        in_specs=[ispec, ispec, pl.BlockSpec((tm, d), lambda i: (i, 0)), pl.BlockSpec((1, d), lambda i: (0, 0)),
                  pl.BlockSpec((1, 1, d), gmap), pl.BlockSpec((1, 1, d), gmap), pl.BlockSpec(memory_space=pl.ANY)],
        out_specs=pl.BlockSpec(memory_space=pl.ANY),
        scratch_shapes=[pltpu.VMEM((tm, slabs, LANES), F32), pltpu.SemaphoreType.DMA],
        input_output_aliases={6: 0},
        compiler_params=_cparams(1),
        name="moe_scatter",
    )(idx(d1), idx(d2), x, nw.reshape(1, d), shift, scale, zeros)


def _grouped_swiglu_kernel(te_ref, nv_ref, a_ref, wg_ref, wu_ref, o_ref, ab_ref):
    t = pl.program_id(0)

    @pl.when(jnp.logical_and(t < nv_ref[0], pl.program_id(1) == 0))
    def _():
        ab_ref[...] = _unslab_rows(a_ref).astype(BF16)

    @pl.when(t < nv_ref[0])
    def _():
        a = ab_ref[...]
        gate = jnp.dot(a, wg_ref[0], preferred_element_type=F32)
        up = jnp.dot(a, wu_ref[0], preferred_element_type=F32)
        o_ref[...] = (_silu(gate) * up).astype(o_ref.dtype)

    @pl.when(t >= nv_ref[0])
    def _():
        o_ref[...] = jnp.zeros_like(o_ref)


def _grouped_swiglu(a, w13, tile_expert, n_valid, p_rows, *, tg, tn):
    slabs = a.shape[1]
    k = slabs * LANES
    e_hid = w13.shape[2] // 2
    nj = e_hid // tn
    tile = lambda t, nv: jnp.minimum(t, nv[0] - 1)
    return pl.pallas_call(
        _grouped_swiglu_kernel,
        out_shape=jax.ShapeDtypeStruct((p_rows, e_hid), BF16),
        grid_spec=pltpu.PrefetchScalarGridSpec(
            num_scalar_prefetch=2,
            grid=(p_rows // tg, nj),
            in_specs=[pl.BlockSpec((tg, slabs, LANES), lambda t, j, te, nv: (tile(t, nv), 0, 0)),
                      pl.BlockSpec((1, k, tn), lambda t, j, te, nv: (te[tile(t, nv)], 0, j)),
                      pl.BlockSpec((1, k, tn), lambda t, j, te, nv: (te[tile(t, nv)], 0, j + nj))],
            out_specs=pl.BlockSpec((tg, tn), lambda t, j, te, nv: (t, j)),
            scratch_shapes=[pltpu.VMEM((tg, k), BF16)],
        ),
        compiler_params=_cparams(2),
        name="moe_up",
    )(tile_expert, n_valid, a, w13, w13)


def _grouped_mm_kernel(te_ref, nv_ref, a_ref, w_ref, o_ref):
    @pl.when(pl.program_id(0) < nv_ref[0])
    def _():
        _slab_rows(o_ref, jnp.dot(a_ref[...], w_ref[0], preferred_element_type=F32))

    @pl.when(pl.program_id(0) >= nv_ref[0])
    def _():
        o_ref[...] = jnp.zeros_like(o_ref)


def _grouped_mm(a, w, tile_expert, n_valid, *, tg):
    p_rows, k = a.shape
    n = w.shape[2]
    slabs = n // LANES
    tile = lambda t, nv: jnp.minimum(t, nv[0] - 1)
    return pl.pallas_call(
        _grouped_mm_kernel,
        out_shape=jax.ShapeDtypeStruct((p_rows, slabs, LANES), F32),
        grid_spec=pltpu.PrefetchScalarGridSpec(
            num_scalar_prefetch=2,
            grid=(p_rows // tg,),
            in_specs=[pl.BlockSpec((tg, k), lambda t, te, nv: (tile(t, nv), 0)),
                      pl.BlockSpec((1, k, n), lambda t, te, nv: (te[tile(t, nv)], 0, 0))],
            out_specs=pl.BlockSpec((tg, slabs, LANES), lambda t, te, nv: (t, 0, 0)),
        ),
        compiler_params=_cparams(1),
        name="moe_down",
    )(tile_expert, n_valid, a, w)


def _combine_kernel(d1_ref, d2_ref, y_hbm, wts_ref, x_ref, gate_ref, o_ref, buf_ref, sem, *, tm):
    def row_copy(which, r, s):
        return pltpu.make_async_copy(y_hbm.at[pl.ds(s, 1)], buf_ref.at[which, pl.ds(r, 1)], sem)

    def issue(r, carry):
        row_copy(0, r, d1_ref[0, 0, r]).start(priority=0)
        row_copy(1, r, d2_ref[0, 0, r]).start(priority=1)
        return carry

    def drain(r, carry):
        row_copy(0, r, 0).wait()
        row_copy(1, r, 0).wait()
        return carry

    lax.fori_loop(0, tm, issue, 0, unroll=DMA_UNROLL)
    lax.fori_loop(0, tm, drain, 0, unroll=DMA_UNROLL)
    lane = lax.broadcasted_iota(jnp.int32, wts_ref.shape, 1)
    w1 = jnp.sum(jnp.where(lane == 0, wts_ref[...], 0.0), axis=-1, keepdims=True)
    w2 = jnp.sum(jnp.where(lane == 1, wts_ref[...], 0.0), axis=-1, keepdims=True)
    y = w1 * _unslab_rows(buf_ref.at[0]) + w2 * _unslab_rows(buf_ref.at[1])
    o_ref[...] = x_ref[...] + gate_ref[0] * y


def _combine(y, d1, d2, wts, x, gate, *, tm, group_of_tile):
    m, d = x.shape
    idx = lambda a: a.reshape(m // tm, 1, tm)
    ispec = pl.BlockSpec((1, 1, tm), lambda i: (i, 0, 0), memory_space=pltpu.SMEM)
    return pl.pallas_call(
        functools.partial(_combine_kernel, tm=tm),
        out_shape=jax.ShapeDtypeStruct((m, d), F32),
        grid=(m // tm,),
        in_specs=[ispec, ispec, pl.BlockSpec(memory_space=pl.ANY),
                  pl.BlockSpec((tm, LANES), lambda i: (i, 0)), pl.BlockSpec((tm, d), lambda i: (i, 0)),
                  pl.BlockSpec((1, 1, d), lambda i: (group_of_tile(i), 0, 0))],
        out_specs=pl.BlockSpec((tm, d), lambda i: (i, 0)),
        scratch_shapes=[pltpu.VMEM((2, tm, d // LANES, LANES), F32), pltpu.SemaphoreType.DMA],
        compiler_params=_cparams(1),
        name="moe_combine",
    )(idx(d1), idx(d2), y, wts, x, gate)


def _rope_tables(seq, ctx_len):
    half = RET_HD // 4
    freqs = ROPE_BASE ** (-np.arange(half, dtype=np.float64) / half)
    t = np.arange(seq)
    ang_r = (t // GRID_W)[:, None] * freqs[None, :]
    ang_c = (t % GRID_W)[:, None] * freqs[None, :]
    cos = np.concatenate([np.cos(ang_r)] * 2 + [np.cos(ang_c)] * 2, axis=1)
    sin = np.concatenate([-np.sin(ang_r), np.sin(ang_r), -np.sin(ang_c), np.sin(ang_c)], axis=1)
    cos = np.concatenate([np.ones((ctx_len, RET_HD)), cos], axis=0)
    sin = np.concatenate([np.zeros((ctx_len, RET_HD)), sin], axis=0)
    return jnp.asarray(cos, F32), jnp.asarray(sin, F32)


def _half_or_full(n):
    return n // 2 if (n // 2) % LANES == 0 else n


def kernel(x, c, ctx, c_ctx, ada_w, ada_b, norm_mix_w, norm_ffn_w, ev_w_in, ev_ret_decay_f, ev_ret_decay_b,
           ev_ret_gn_w, ev_na_qn_w, ev_na_kn_w, ev_na_rpb, ev_w_out, ev_ffn_w13, ev_ffn_w2, od_mu, od_w_rkv,
           od_w0, od_w1, od_w2, od_a0, od_a1, od_a2, od_g1, od_g2, od_k_k, od_k_a, od_r_k, od_ln_w, od_ln_b,
           od_w_o, od_router, od_moe_w13, od_moe_w2):
    batch, seq, d = x.shape
    ctx_len = ctx.shape[1]
    rows = _Rows(batch, seq, ctx_len)
    tm = 8 * RWKV_CHUNK
    grp = rows.group_of_tile(tm)
    tm_big = 2 * tm if (rows.mc % (2 * tm) == 0 and seq % (2 * tm) == 0) else tm
    grp_big = rows.group_of_tile(tm_big)
    xa = jnp.concatenate([x.reshape(rows.ml, d), ctx.reshape(rows.mc, d)], axis=0)

    n_mod = ((batch + 1 + 7) // 8) * 8
    c_rows = jnp.zeros((n_mod, d), F32).at[:batch].set(c).at[batch].set(c_ctx)
    mods = _ada(c_rows, ada_w, ada_b)
    mods = mods.reshape(mods.shape[0], n_mod, 6, 1, d).transpose(0, 2, 1, 3, 4)

    mod = mods[0]
    w_in = ev_w_in[0].astype(BF16)
    proj = _modnorm_mm(xa, norm_mix_w[0], mod[0], mod[1], w_in, group_of_tile=grp_big, tm=tm_big,
                       tn=_half_or_full(w_in.shape[1]), out_dtype=F32)
    cos, sin = _rope_tables(seq, ctx_len)
    y_f, y_b = _retention(proj, cos, sin, ev_ret_decay_f[0], ev_ret_decay_b[0], rows)
    bias = _na_bias_table(ev_na_rpb[0].astype(F32))
    na_l = _na_latent(proj, ev_na_qn_w[0], ev_na_kn_w[0], bias, rows, 4 * RET_W)
    na_c = _ctx_attention(proj, ev_na_qn_w[0], ev_na_kn_w[0], rows, 4 * RET_W)
    mix = _ret_readout(y_f, y_b, proj, na_l, na_c, ev_ret_gn_w[0], rows, tm)
    w_out = ev_w_out[0].astype(BF16)
    xa = _mm(mix, w_out, tm=tm, tn=d, tk=w_out.shape[0], res=xa, gate=mod[2], group_of_tile=grp)

    w13 = ev_ffn_w13[0].astype(BF16)
    w2 = ev_ffn_w2[0].astype(BF16)
    hid = w2.shape[0]
    hidden = _modnorm_mm(xa, norm_ffn_w[0], mod[3], mod[4], w13, group_of_tile=grp_big, tm=tm_big,
                         tn=_half_or_full(hid), out_dtype=BF16, swiglu=True)
    xa = _mm(hidden, w2, tm=tm, tn=d, tk=hid, res=xa, gate=mod[5], group_of_tile=grp)

    mod = mods[1]
    glora = od_g1.shape[-1]
    glora_p = ((glora + LANES - 1) // LANES) * LANES
    p = {
        "w1": jnp.concatenate([od_w1[0, 0], od_w1[0, 1]], axis=1).astype(BF16),
        "w2": od_w2[0].astype(BF16),
        "a1": jnp.concatenate([od_a1[0, 0], od_a1[0, 1]], axis=1).astype(BF16),
        "a2": od_a2[0].astype(BF16),
        "g1": jnp.zeros((d, glora_p), BF16).at[:, :glora].set(od_g1[0].astype(BF16)),
        "g2": jnp.zeros((glora_p, d), BF16).at[:glora].set(od_g2[0].astype(BF16)),
        "w0": od_w0[0].astype(F32), "a0": od_a0[0].astype(F32),
        "k_k": od_k_k[0].reshape(1, d).astype(F32), "k_a": od_k_a[0].reshape(1, d).astype(F32),
        "r_k": od_r_k[0].reshape(1, d).astype(F32),
    }
    w_rkv = od_w_rkv[0].astype(BF16)
    r, k, v, lw_low, a_low, g_low = _rwkv_mix(xa, norm_mix_w[1], mod[0], mod[1], od_mu[0].astype(F32), w_rkv, p,
                                              rows=rows, tm=tm)
    f = _rwkv_post(r, k, v, lw_low, a_low, g_low, p, tm=tm // 2)
    y_f, y_b = _rwkv_scan(f[0:5], f[5:10], f[10], rows)
    mixed = _rwkv_readout(y_f, y_b, f[12], f[11], od_ln_w[0], od_ln_b[0], rows, tm)
    x_l = _mm(mixed, od_w_o[0].astype(BF16), tm=tm, tn=d, tk=d, res=xa, gate=mod[2], group_of_tile=grp)

    gates = _router(x_l, norm_ffn_w[1], mod[3], mod[4], od_router[0], tm=tm, group_of_tile=grp)
    n_e = od_router.shape[-1]
    n_rows, tile_expert, n_valid, d1, d2_scatter, d2_combine, wts = _route_plan(gates, n_e, MOE_TILE)
    tm_r = 256
    grp_r = rows.group_of_tile(tm_r)
    a_sorted = _moe_scatter(x_l, norm_ffn_w[1], mod[3], mod[4], d1, d2_scatter, n_rows, tm=tm_r, tg=MOE_TILE,
                            group_of_tile=grp_r)
    moe13 = od_moe_w13[0].astype(BF16)
    hid_sorted = _grouped_swiglu(a_sorted, moe13, tile_expert, n_valid, n_rows, tg=MOE_TILE,
                                 tn=_half_or_full(moe13.shape[2] // 2))
    y_sorted = _grouped_mm(hid_sorted, od_moe_w2[0].astype(BF16), tile_expert, n_valid, tg=MOE_TILE)
    x_l = _combine(y_sorted, d1, d2_combine, wts, x_l, mod[5], tm=tm_r, group_of_tile=grp_r)
    return x_l.reshape(batch, seq, d)
```

```python
import functools

import jax
import jax.numpy as jnp
import numpy as np
from jax import lax
from jax.experimental import pallas as pl
from jax.experimental.pallas import tpu as pltpu

F32 = jnp.float32
BF16 = jnp.bfloat16

LANES = 128
GRID_W = 64
RET_HEADS = 4
RET_HD = 128
RET_W = RET_HEADS * RET_HD
RET_CHUNK = 128
RET_GN_EPS = 1e-5
NA_HEADS = 8
NA_HD = 64
NA_W = NA_HEADS * NA_HD
NA_KR = 8
NA_KC = 16
NA_ROWS_PER_STEP = 4
RWKV_HD = 64
RWKV_GN_EPS = 64e-5
RWKV_CHUNK = 64
TOP_K = 2
ROPE_BASE = 10000.0
NORM_EPS = 1e-6
NEG_BIG = -1e30
VMEM_LIMIT = 56 * 1024 * 1024


def _cparams(n_axes):
    return pltpu.CompilerParams(dimension_semantics=("arbitrary",) * n_axes, vmem_limit_bytes=VMEM_LIMIT)


def _dot(a, b):
    return jnp.dot(a.astype(BF16), b.astype(BF16), preferred_element_type=F32)


def _dot_nt(a, b):
    return lax.dot_general(a.astype(BF16), b.astype(BF16), (((1,), (1,)), ((), ())), preferred_element_type=F32)


def _split2(x):
    hi = x.astype(BF16)
    return hi, (x - hi.astype(F32)).astype(BF16)


def _split3(x):
    hi = x.astype(BF16)
    r1 = x - hi.astype(F32)
    mid = r1.astype(BF16)
    lo = (r1 - mid.astype(F32)).astype(BF16)
    return hi, mid, lo


def _dot_hi(a, b):
    ah, am, al = _split3(a)
    bh, bm, bl = _split3(b)
    d = functools.partial(jnp.dot, preferred_element_type=F32)
    return (d(ah, bh) + (d(ah, bm) + d(am, bh)) + (d(am, bm) + d(ah, bl) + d(al, bh)))


def _dot_exact_rhs(a, b_bf16):
    ah, al = _split2(a)
    d = functools.partial(jnp.dot, preferred_element_type=F32)
    return d(ah, b_bf16) + d(al, b_bf16)


def _dot_exact_lhs(a_bf16, b):
    bh, bl = _split2(b)
    d = functools.partial(jnp.dot, preferred_element_type=F32)
    return d(a_bf16, bh) + d(a_bf16, bl)


def _mod(v, n):
    return (v & (n - 1)) if n & (n - 1) == 0 else v % n


def _sigmoid(x):
    return 1.0 / (1.0 + jnp.exp(-x))


def _silu(x):
    return x * _sigmoid(x)


def _group_mean_mat(width, group):
    r = lax.broadcasted_iota(jnp.int32, (width, width), 0) // group
    c = lax.broadcasted_iota(jnp.int32, (width, width), 1) // group
    return jnp.where(r == c, 1.0 / group, 0.0).astype(BF16)


def _group_mean(x, group):
    g = _group_mean_mat(LANES, group)
    cols = [_dot_exact_rhs(x[:, c:c + LANES], g) for c in range(0, x.shape[1], LANES)]
    return cols[0] if len(cols) == 1 else jnp.concatenate(cols, axis=1)


def _modnorm(x, nw, shift, scale):
    ms = jnp.mean(x * x, axis=-1, keepdims=True)
    y = x * lax.rsqrt(ms + NORM_EPS) * nw
    return y * (1.0 + scale) + shift


class _Rows:
    def __init__(self, batch, seq, ctx_len):
        self.batch, self.seq, self.ctx_len = batch, seq, ctx_len
        self.ml, self.mc = batch * seq, batch * ctx_len
        self.m = self.ml + self.mc

    def group_of_tile(self, tm):
        return lambda i: jnp.minimum((i * tm) // self.seq, self.batch)

    def chunk_block(self, chunk, backward):
        ncc, nlc = self.ctx_len // chunk, self.seq // chunk

        def block(b, t):
            if backward:
                t = jnp.where(t < ncc, ncc - 1 - t, nlc + 2 * ncc - 1 - t)
            return jnp.where(t < ncc, (self.ml + b * self.ctx_len) // chunk + t, (b * self.seq) // chunk + t - ncc)

        def position(t):
            if backward:
                t = jnp.where(t < ncc, ncc - 1 - t, nlc + 2 * ncc - 1 - t)
            return t

        return block, position


def _ada_kernel(c_ref, w_ref, b_ref, o_ref):
    s = _silu(c_ref[...])
    o_ref[0] = _dot_hi(s, w_ref[0]) + b_ref[0]


def _ada(c_rows, ada_w, ada_b):
    depth, d, n = ada_w.shape
    rows = c_rows.shape[0]
    tn = 1536
    return pl.pallas_call(
        _ada_kernel,
        out_shape=jax.ShapeDtypeStruct((depth, rows, n), F32),
        grid=(depth, n // tn),
        in_specs=[
            pl.BlockSpec((rows, d), lambda l, j: (0, 0)),
            pl.BlockSpec((1, d, tn), lambda l, j: (l, 0, j)),
            pl.BlockSpec((1, 1, tn), lambda l, j: (l, 0, j)),
        ],
        out_specs=pl.BlockSpec((1, rows, tn), lambda l, j: (l, 0, j)),
        compiler_params=_cparams(2),
        name="ada_mod",
    )(c_rows, ada_w, ada_b.reshape(depth, 1, n))


def _modnorm_mm_kernel(*refs, swiglu):
    x_ref, nw_ref, sh_ref, sc_ref = refs[:4]
    w_refs = refs[4:-2]
    o_ref, h_ref = refs[-2:]

    @pl.when(pl.program_id(1) == 0)
    def _():
        h_ref[...] = _modnorm(x_ref[...], nw_ref[...], sh_ref[0], sc_ref[0]).astype(BF16)

    h = h_ref[...]
    acc = jnp.dot(h, w_refs[0][...], preferred_element_type=F32)
    if swiglu:
        acc = _silu(acc) * jnp.dot(h, w_refs[1][...], preferred_element_type=F32)
    o_ref[...] = acc.astype(o_ref.dtype)


def _modnorm_mm(x, nw, shift, scale, w, *, group_of_tile, tm, tn, out_dtype, swiglu=False):
    m, k = x.shape
    n_out = w.shape[1] // 2 if swiglu else w.shape[1]
    nj = n_out // tn
    gmap = lambda i, j: (group_of_tile(i), 0, 0)
    wspecs = [pl.BlockSpec((k, tn), lambda i, j: (0, j))]
    if swiglu:
        wspecs.append(pl.BlockSpec((k, tn), lambda i, j: (0, j + nj)))
    return pl.pallas_call(
        functools.partial(_modnorm_mm_kernel, swiglu=swiglu),
        out_shape=jax.ShapeDtypeStruct((m, n_out), out_dtype),
        grid=(m // tm, nj),
        in_specs=[pl.BlockSpec((tm, k), lambda i, j: (i, 0)), pl.BlockSpec((1, k), lambda i, j: (0, 0)),
                  pl.BlockSpec((1, 1, k), gmap), pl.BlockSpec((1, 1, k), gmap)] + wspecs,
        out_specs=pl.BlockSpec((tm, tn), lambda i, j: (i, j)),
        scratch_shapes=[pltpu.VMEM((tm, k), BF16)],
        compiler_params=_cparams(2),
        name="modnorm_mm",
    )(x, nw.reshape(1, k), shift, scale, *([w] * len(wspecs)))


def _mm_res_kernel(a_ref, w_ref, res_ref, gate_ref, o_ref):
    o_ref[...] = res_ref[...] + gate_ref[0] * jnp.dot(a_ref[...], w_ref[...], preferred_element_type=F32)


def _mm_res(a, w, res, gate, *, tm, group_of_tile):
    m, k = a.shape
    n = w.shape[1]
    return pl.pallas_call(
        _mm_res_kernel,
        out_shape=jax.ShapeDtypeStruct((m, n), F32),
        grid=(m // tm,),
        in_specs=[pl.BlockSpec((tm, k), lambda i: (i, 0)), pl.BlockSpec((k, n), lambda i: (0, 0)),
                  pl.BlockSpec((tm, n), lambda i: (i, 0)),
                  pl.BlockSpec((1, 1, n), lambda i: (group_of_tile(i), 0, 0))],
        out_specs=pl.BlockSpec((tm, n), lambda i: (i, 0)),
        compiler_params=_cparams(1),
        name="mm_res",
    )(a, w, res, gate)


def _rope(x, cos, sin_signed):
    lane = lax.broadcasted_iota(jnp.int32, x.shape, 1)
    half = RET_HD // 4
    swapped = jnp.where(_mod(lane, 2 * half) < half, pltpu.roll(x, LANES - half, 1), pltpu.roll(x, half, 1))
    return x * cos + swapped * sin_signed


def _ret_chunks(q, k, v, s0, log_g, n_forward):
    c = RET_CHUNK
    shape = (q.shape[0], c, c)
    row = lax.broadcasted_iota(jnp.int32, shape, 1)
    col = lax.broadcasted_iota(jnp.int32, shape, 2)
    fwd = lax.broadcasted_iota(jnp.int32, shape, 0) < n_forward
    dist = jnp.where(fwd, row - col, col - row).astype(F32)
    q_steps = jnp.where(fwd, row + 1, c - row).astype(F32)
    k_steps = jnp.where(fwd, c - 1 - row, row).astype(F32)
    dmat = jnp.where(dist >= 0, jnp.exp(log_g * jnp.maximum(dist, 0.0)), 0.0)
    q_dec = jnp.exp(log_g * q_steps)
    k_dec = jnp.exp(log_g * k_steps)
    scores = _bdot_nt(q, k) * dmat
    out = _bdot(scores, v) + _bdot(q * q_dec, s0)
    s1 = s0 * jnp.exp(log_g * float(c)) + _bdot_tn(k * k_dec, v)
    return out, s1


def _retention_kernel(qf_ref, kf_ref, vf_ref, cf_ref, sf_ref, qb_ref, kb_ref, vb_ref, cb_ref, sb_ref,
                      decf_ref, decb_ref, of_ref, ob_ref, st_ref):
    @pl.when(pl.program_id(1) == 0)
    def _():
        st_ref[...] = jnp.zeros_like(st_ref)

    h = RET_HEADS
    kscale = RET_HD ** -0.5

    def heads(ref, cos, sin, scale=None):
        out = []
        for hh in range(h):
            xh = ref[:, hh * RET_HD:(hh + 1) * RET_HD]
            if cos is not None:
                xh = _rope(xh, cos, sin)
            out.append(xh if scale is None else xh * scale)
        return out

    cf, sf, cb, sb = cf_ref[...], sf_ref[...], cb_ref[...], sb_ref[...]
    q = jnp.stack(heads(qf_ref, cf, sf) + heads(qb_ref, cb, sb), axis=0)
    k = jnp.stack(heads(kf_ref, cf, sf, kscale) + heads(kb_ref, cb, sb, kscale), axis=0)
    v = jnp.stack(heads(vf_ref, None, None) + heads(vb_ref, None, None), axis=0)
    log_g = -jnp.exp(jnp.concatenate([decf_ref[...], decb_ref[...]], axis=0))
    out, s1 = _ret_chunks(q, k, v, st_ref[...], log_g, h)
    st_ref[...] = s1
    of_ref[...] = jnp.concatenate([out[hh] for hh in range(h)], axis=1)
    ob_ref[...] = jnp.concatenate([out[h + hh] for hh in range(h)], axis=1)


def _retention(proj, cos, sin_signed, dec_f, dec_b, rows):
    c, h = RET_CHUNK, RET_HEADS
    n = (rows.seq + rows.ctx_len) // c
    blk_f, pos_f = rows.chunk_block(c, False)
    blk_b, pos_b = rows.chunk_block(c, True)

    def tok(blk, off):
        return pl.BlockSpec((c, RET_W), lambda bi, t: (blk(bi, t), off))

    def tab(pos):
        return pl.BlockSpec((c, RET_HD), lambda bi, t: (pos(t), 0))

    dec = lambda a: jnp.broadcast_to(a.astype(F32)[:, None, None], (h, 1, LANES))
    dspec = pl.BlockSpec((h, 1, LANES), lambda bi, t: (0, 0, 0))
    return pl.pallas_call(
        _retention_kernel,
        out_shape=[jax.ShapeDtypeStruct((rows.m, RET_W), F32)] * 2,
        grid=(rows.batch, n),
        in_specs=[tok(blk_f, 0), tok(blk_f, 1), tok(blk_f, 2), tab(pos_f), tab(pos_f),
                  tok(blk_b, 0), tok(blk_b, 1), tok(blk_b, 2), tab(pos_b), tab(pos_b), dspec, dspec],
        out_specs=[tok(blk_f, 0), tok(blk_b, 0)],
        scratch_shapes=[pltpu.VMEM((2 * h, RET_HD, RET_HD), F32)],
        compiler_params=_cparams(2),
        name="retention",
    )(proj, proj, proj, cos, sin_signed, proj, proj, proj, cos, sin_signed, dec(dec_f), dec(dec_b))


def _na_qk_norm(x, w):
    ms = _group_mean(x * x, NA_HD)
    return x * lax.rsqrt(ms + NORM_EPS) * w


def _softmax_pv(parts):
    m = functools.reduce(jnp.maximum, [jnp.max(s, axis=-1, keepdims=True) for s, _ in parts])
    ps = [jnp.exp(s - m) for s, _ in parts]
    den = functools.reduce(jnp.add, [jnp.sum(p, axis=-1, keepdims=True) for p in ps])
    num = functools.reduce(jnp.add, [_dot(p, v) for p, (_, v) in zip(ps, parts)])
    return num / den


def _na_kernel(q_ref, k_ref, v_ref, kc_ref, vc_ref, qn_ref, kn_ref, bias_ref, o_ref, qs_ref, ks_ref, vs_ref,
               kcs_ref, *, rows):
    scale = NA_HD ** -0.5
    qs_ref[...] = (_na_qk_norm(q_ref[...], qn_ref[...]) * scale).astype(BF16)
    ks_ref[...] = _na_qk_norm(k_ref[...], kn_ref[...]).astype(BF16)
    vs_ref[...] = v_ref[...].astype(BF16)
    kcs_ref[...] = _na_qk_norm(kc_ref[...], kn_ref[...]).astype(BF16)
    lane = lax.broadcasted_iota(jnp.int32, (GRID_W, LANES), 1)
    first = lane < NA_HD
    vc = vc_ref[...]
    kc = kcs_ref[...]

    nq = 2 * GRID_W

    def body(rb, carry):
        qs, kw, vw, bw = [], [], [], []
        for i in range(NA_ROWS_PER_STEP):
            r = rb * NA_ROWS_PER_STEP + i
            r0 = jnp.clip(r - NA_KR // 2, 0, rows - NA_KR)
            q_r = qs_ref[pl.ds(pl.multiple_of(r * GRID_W, GRID_W), GRID_W), :]
            zero = jnp.zeros_like(q_r)
            qs.append(jnp.concatenate([jnp.where(first, q_r, zero), jnp.where(first, zero, q_r)], axis=0))
            win = pl.ds(pl.multiple_of(r0 * GRID_W, GRID_W), NA_KR * GRID_W)
            kw.append(ks_ref[win, :])
            vw.append(vs_ref[win, :])
            bw.append(bias_ref[0, r0 - r + (NA_KR - 1)])
        q = jnp.stack(qs, axis=0)
        s_w = _bdot_nt(q, jnp.stack(kw, axis=0)) + jnp.stack(bw, axis=0)
        q_flat = q.reshape(NA_ROWS_PER_STEP * nq, LANES)
        s_c = _dot_nt(q_flat, kc).reshape(NA_ROWS_PER_STEP, nq, kc.shape[0])
        m = jnp.maximum(jnp.max(s_w, axis=-1, keepdims=True), jnp.max(s_c, axis=-1, keepdims=True))
        p_w = jnp.exp(s_w - m)
        p_c = jnp.exp(s_c - m)
        den = jnp.sum(p_w, axis=-1, keepdims=True) + jnp.sum(p_c, axis=-1, keepdims=True)
        num = _bdot(p_w, jnp.stack(vw, axis=0)) + _dot(p_c.reshape(NA_ROWS_PER_STEP * nq, kc.shape[0]), vc).reshape(
            NA_ROWS_PER_STEP, nq, LANES)
        out = num / den
        for i in range(NA_ROWS_PER_STEP):
            r = rb * NA_ROWS_PER_STEP + i
            o_ref[pl.ds(pl.multiple_of(r * GRID_W, GRID_W), GRID_W), :] = jnp.where(
                first, out[i, :GRID_W], out[i, GRID_W:])
        return carry

    lax.fori_loop(0, rows // NA_ROWS_PER_STEP, body, 0)


def _na_bias_table(rpb):
    cols = np.arange(GRID_W)
    start = np.clip(cols - NA_KC // 2, 0, GRID_W - NA_KC)
    kcol = np.arange(GRID_W)
    inside = (kcol[None, :] >= start[:, None]) & (kcol[None, :] < start[:, None] + NA_KC)
    col_off = np.clip(kcol[None, :] - cols[:, None] + (NA_KC - 1), 0, 2 * NA_KC - 2)
    row_off = np.arange(NA_KR)[:, None] + np.arange(NA_KR)[None, :]
    tab = rpb[:, row_off][:, :, :, col_off]
    tab = jnp.where(inside[None, None, None], tab, NEG_BIG)
    tab = jnp.transpose(tab, (0, 1, 3, 2, 4))
    tab = tab.reshape(rpb.shape[0] // 2, 2, NA_KR, GRID_W, NA_KR * GRID_W).transpose(0, 2, 1, 3, 4)
    return tab.reshape(rpb.shape[0] // 2, NA_KR, 2 * GRID_W, NA_KR * GRID_W).astype(F32)


def _na_latent(proj, qn_w, kn_w, bias, rows, col0):
    seq, ctx_len = rows.seq, rows.ctx_len
    grid_rows = seq // GRID_W
    pairs = NA_W // LANES
    cb = col0 // LANES
    cblk = rows.ml // ctx_len
    tile2 = lambda a: jnp.tile(a.astype(F32), 2).reshape(1, LANES)
    lat = lambda off: pl.BlockSpec((seq, LANES), lambda bi, p: (bi, cb + off * pairs + p))
    ctx = lambda off: pl.BlockSpec((ctx_len, LANES), lambda bi, p: (cblk + bi, cb + off * pairs + p))
    return pl.pallas_call(
        functools.partial(_na_kernel, rows=grid_rows),
        out_shape=jax.ShapeDtypeStruct((rows.ml, NA_W), F32),
        grid=(rows.batch, pairs),
        in_specs=[lat(0), lat(1), lat(2), ctx(1), ctx(2),
                  pl.BlockSpec((1, LANES), lambda bi, p: (0, 0)), pl.BlockSpec((1, LANES), lambda bi, p: (0, 0)),
                  pl.BlockSpec((1, NA_KR, 2 * GRID_W, NA_KR * GRID_W), lambda bi, p: (p, 0, 0, 0))],
        out_specs=pl.BlockSpec((seq, LANES), lambda bi, p: (bi, p)),
        scratch_shapes=[pltpu.VMEM((seq, LANES), BF16)] * 3 + [pltpu.VMEM((ctx_len, LANES), BF16)],
        compiler_params=_cparams(2),
        name="na_latent",
    )(proj, proj, proj, proj, proj, tile2(qn_w), tile2(kn_w), bias)


def _ctx_attn_kernel(q_ref, k_ref, v_ref, qn_ref, kn_ref, o_ref):
    scale = NA_HD ** -0.5
    q = (_na_qk_norm(q_ref[...], qn_ref[...]) * scale).astype(BF16)
    k = _na_qk_norm(k_ref[...], kn_ref[...]).astype(BF16)
    v = v_ref[...]
    lane = lax.broadcasted_iota(jnp.int32, q.shape, 1)
    first = lane < NA_HD
    outs = []
    for hh in range(2):
        qh = jnp.where(first if hh == 0 else jnp.logical_not(first), q, jnp.zeros_like(q))
        outs.append(_softmax_pv([(_dot_nt(qh, k), v)]))
    o_ref[...] = jnp.where(first, outs[0], outs[1])


def _ctx_attention(proj, qn_w, kn_w, rows, col0):
    ctx_len = rows.ctx_len
    pairs = NA_W // LANES
    cb = col0 // LANES
    cblk = rows.ml // ctx_len
    tile2 = lambda a: jnp.tile(a.astype(F32), 2).reshape(1, LANES)
    blk = lambda off: pl.BlockSpec((ctx_len, LANES), lambda bi, p: (cblk + bi, cb + off * pairs + p))
    return pl.pallas_call(
        _ctx_attn_kernel,
        out_shape=jax.ShapeDtypeStruct((rows.mc, NA_W), F32),
        grid=(rows.batch, pairs),
        in_specs=[blk(0), blk(1), blk(2),
                  pl.BlockSpec((1, LANES), lambda bi, p: (0, 0)), pl.BlockSpec((1, LANES), lambda bi, p: (0, 0))],
        out_specs=pl.BlockSpec((ctx_len, LANES), lambda bi, p: (bi, p)),
        compiler_params=_cparams(2),
        name="ctx_attention",
    )(proj, proj, proj, tile2(qn_w), tile2(kn_w))


def _ret_readout_kernel(yf_ref, yb_ref, g_ref, nal_ref, nac_ref, gnw_ref, o_ref, *, n_latent_tiles):
    y = yf_ref[...] + yb_ref[...]
    for hh in range(RET_HEADS):
        sl = slice(hh * RET_HD, (hh + 1) * RET_HD)
        yh = y[:, sl]
        yc = yh - jnp.mean(yh, axis=-1, keepdims=True)
        yn = yc * lax.rsqrt(jnp.mean(yc * yc, axis=-1, keepdims=True) + RET_GN_EPS)
        o_ref[:, sl] = (yn * gnw_ref[:, sl] * _silu(g_ref[:, sl])).astype(o_ref.dtype)
    is_latent = pl.program_id(0) < n_latent_tiles
    o_ref[:, RET_W:] = jnp.where(is_latent, nal_ref[...], nac_ref[...]).astype(o_ref.dtype)


def _ret_readout(yf, yb, proj, na_l, na_c, gn_w, rows, tm):
    nl = rows.ml // tm
    row = lambda i: (i, 0)
    return pl.pallas_call(
        functools.partial(_ret_readout_kernel, n_latent_tiles=nl),
        out_shape=jax.ShapeDtypeStruct((rows.m, RET_W + NA_W), BF16),
        grid=(rows.m // tm,),
        in_specs=[pl.BlockSpec((tm, RET_W), row), pl.BlockSpec((tm, RET_W), row),
                  pl.BlockSpec((tm, RET_W), lambda i: (i, 3)),
                  pl.BlockSpec((tm, NA_W), lambda i: (jnp.minimum(i, nl - 1), 0)),
                  pl.BlockSpec((tm, NA_W), lambda i: (jnp.maximum(i - nl, 0), 0)),
                  pl.BlockSpec((1, RET_W), lambda i: (0, 0))],
        out_specs=pl.BlockSpec((tm, RET_W + NA_W), row),
        compiler_params=_cparams(1),
        name="ret_readout",
    )(yf, yb, proj, na_l, na_c, gn_w.reshape(1, RET_W).astype(F32))


def _rwkv_mix_kernel(x_ref, xp_ref, xn_ref, nw_ref, sh_ref, sc_ref, mu_ref, wrkv_ref, w1_ref, a1_ref, g1_ref,
                     r_ref, k_ref, v_ref, lwl_ref, al_ref, gl_ref, *, seq, ctx_len, ml, tm):
    nw, sh, sc = nw_ref[...], sh_ref[0], sc_ref[0]
    h = _modnorm(x_ref[...], nw, sh, sc)
    hp = _modnorm(xp_ref[...], nw, sh, sc)[7:8]
    hn = _modnorm(xn_ref[...], nw, sh, sc)[0:1]
    row = lax.broadcasted_iota(jnp.int32, h.shape, 0)
    g = pl.program_id(0) * tm + row
    latent = g < ml
    pos = jnp.where(latent, _mod(g, seq), _mod(g, ctx_len))
    first = pos == 0
    last = pos == jnp.where(latent, seq - 1, ctx_len - 1)
    up = jnp.where(row == 0, hp, pltpu.roll(h, 1, 0))
    up = jnp.where(first, 0.0, up)
    dn = jnp.where(row == tm - 1, hn, pltpu.roll(h, tm - 1, 0))
    dn = jnp.where(last, 0.0, dn)
    xx = 0.5 * (up + dn) - h
    mix = lambda s_i: (h + xx * mu_ref[s_i:s_i + 1, :]).astype(BF16)
    for s_i, o_ref in enumerate((r_ref, k_ref, v_ref)):
        o_ref[...] = jnp.dot(mix(s_i), wrkv_ref[s_i], preferred_element_type=F32)
    lwl_ref[...] = jnp.tanh(jnp.dot(mix(3), w1_ref[...], preferred_element_type=F32))
    al_ref[...] = jnp.dot(mix(4), a1_ref[...], preferred_element_type=F32)
    gl_ref[...] = _sigmoid(jnp.dot(mix(5), g1_ref[...], preferred_element_type=F32))


def _rwkv_mix(x, nw, shift, scale, mu, w_rkv, p, *, rows, tm):
    m, d = x.shape
    n_slab = m // 8
    per = tm // 8
    gmap = lambda i: (rows.group_of_tile(tm)(i), 0, 0)
    row = lambda i: (i, 0)
    full2 = lambda i: (0, 0)
    smalls = [p["w1"], p["a1"], p["g1"]]
    return pl.pallas_call(
        functools.partial(_rwkv_mix_kernel, seq=rows.seq, ctx_len=rows.ctx_len, ml=rows.ml, tm=tm),
        out_shape=[jax.ShapeDtypeStruct((m, d), F32)] * 3
        + [jax.ShapeDtypeStruct((m, w.shape[1]), F32) for w in smalls],
        grid=(m // tm,),
        in_specs=[pl.BlockSpec((tm, d), row),
                  pl.BlockSpec((8, d), lambda i: (jnp.maximum(i * per - 1, 0), 0)),
                  pl.BlockSpec((8, d), lambda i: (jnp.minimum((i + 1) * per, n_slab - 1), 0)),
                  pl.BlockSpec((1, d), full2),
                  pl.BlockSpec((1, 1, d), gmap), pl.BlockSpec((1, 1, d), gmap),
                  pl.BlockSpec((6, d), full2),
                  pl.BlockSpec((3, d, d), lambda i: (0, 0, 0))]
        + [pl.BlockSpec(w.shape, full2) for w in smalls],
        out_specs=[pl.BlockSpec((tm, d), row)] * 3 + [pl.BlockSpec((tm, w.shape[1]), row) for w in smalls],
        compiler_params=_cparams(1),
        name="rwkv_mix",
    )(x, x, x, nw.reshape(1, d), shift, scale, mu, w_rkv, *smalls)


def _rwkv_post_kernel(r_ref, k_ref, v_ref, lwl_ref, al_ref, gl_ref, w2_ref, a2_ref, g2_ref,
                      w0_ref, a0_ref, kk_ref, ka_ref, rk_ref,
                      rt0, kt0, bt0, at0, wc0, rt1, kt1, bt1, at1, wc1, vb_ref, g_ref, bonus_ref, *, tm):
    c = RWKV_CHUNK
    n_pairs = r_ref.shape[1] // LANES
    r, k, v = r_ref[...], k_ref[...], v_ref[...]
    lw_low, a_low = lwl_ref[...], al_ref[...]
    g_ref[...] = _dot(gl_ref[...], g2_ref[...]).astype(g_ref.dtype)

    def put(o_ref, val):
        for p in range(n_pairs):
            o_ref[p] = val[:, p * LANES:(p + 1) * LANES].astype(o_ref.dtype)

    put(vb_ref, v)
    kk = k * kk_ref[...]
    nrm = jnp.sqrt(_group_mean(kk * kk, RWKV_HD) * float(RWKV_HD))
    kk = kk / jnp.maximum(nrm, 1e-12)

    tr = lax.broadcasted_iota(jnp.int32, (c, c), 0)
    tc = lax.broadcasted_iota(jnp.int32, (c, c), 1)
    tri = [(tr >= tc).astype(BF16), (tr <= tc).astype(BF16)]
    lora = w2_ref.shape[1]
    coeff_src = jnp.zeros_like(r)
    outs = [(rt0, kt0, bt0, at0, wc0), (rt1, kt1, bt1, at1, wc1)]
    for z in range(2):
        rt_ref, kt_ref, bt_ref, at_ref, wc_ref = outs[z]
        w_lora = _dot(lw_low[:, z * lora:(z + 1) * lora], w2_ref[z])
        lw = -float(np.exp(-0.5)) * _sigmoid(w0_ref[z:z + 1, :] + w_lora)
        a_z = _sigmoid(a0_ref[z:z + 1, :] + _dot(a_low[:, z * lora:(z + 1) * lora], a2_ref[z]))
        k_dir = k * (1.0 + (a_z - 1.0) * ka_ref[...])
        coeff_src = coeff_src + r * k_dir * rk_ref[...]
        cum = jnp.concatenate([_dot_exact_lhs(tri[z], lw[ci * c:(ci + 1) * c]) for ci in range(tm // c)], axis=0)
        e_pos = jnp.exp(cum)
        e_neg = jnp.exp(-cum)
        put(rt_ref, r * e_pos)
        put(kt_ref, k_dir * e_neg)
        put(bt_ref, kk * a_z * e_neg)
        put(at_ref, -kk * jnp.exp(cum - lw))
        for ci in range(tm // c):
            end = ci * c + (c - 1 if z == 0 else 0)
            for p in range(n_pairs):
                wc_ref[p, ci] = e_pos[end:end + 1, p * LANES:(p + 1) * LANES]
    coeff = _group_mean(coeff_src, RWKV_HD) * float(RWKV_HD)
    bonus_ref[...] = coeff * v


def _rwkv_post(r, k, v, lw_low, a_low, g_low, p, *, tm):
    m, d = r.shape
    c = RWKV_CHUNK
    lora = p["w2"].shape[1]
    glora = p["g1"].shape[1]
    small = lambda a: pl.BlockSpec((tm, a.shape[1]), lambda i: (i, 0))
    n_pairs = d // LANES
    pair = jax.ShapeDtypeStruct((n_pairs, m, LANES), BF16)
    chunk = jax.ShapeDtypeStruct((n_pairs, m // c, 1, LANES), F32)
    row = lambda i: (i, 0)
    full2 = lambda i: (0, 0)
    full3 = lambda i: (0, 0, 0)
    tspec = pl.BlockSpec((tm, d), row)
    pspec = pl.BlockSpec((n_pairs, tm, LANES), lambda i: (0, i, 0))
    cspec = pl.BlockSpec((n_pairs, tm // c, 1, LANES), lambda i: (0, i, 0, 0))
    return pl.pallas_call(
        functools.partial(_rwkv_post_kernel, tm=tm),
        out_shape=[pair] * 4 + [chunk] + [pair] * 4 + [chunk]
        + [pair, jax.ShapeDtypeStruct((m, d), BF16), jax.ShapeDtypeStruct((m, d), F32)],
        grid=(m // tm,),
        in_specs=[tspec] * 3 + [small(lw_low), small(a_low), small(g_low)]
        + [pl.BlockSpec((2, lora, d), full3), pl.BlockSpec((2, lora, d), full3), pl.BlockSpec((glora, d), full2),
           pl.BlockSpec((2, d), full2), pl.BlockSpec((2, d), full2)]
        + [pl.BlockSpec((1, d), full2)] * 3,
        out_specs=[pspec] * 4 + [cspec] + [pspec] * 4 + [cspec] + [pspec, tspec, tspec],
        compiler_params=_cparams(1),
        name="rwkv_post",
    )(r, k, v, lw_low, a_low, g_low, p["w2"], p["a2"], p["g2"], p["w0"], p["a0"], p["k_k"], p["k_a"], p["r_k"])


def _bdot(a, b):
    return jnp.einsum("ucd,ude->uce", a.astype(BF16), b.astype(BF16), preferred_element_type=F32)


def _bdot_nt(a, b):
    return jnp.einsum("ucd,usd->ucs", a.astype(BF16), b.astype(BF16), preferred_element_type=F32)


def _bdot_tn(a, b):
    return jnp.einsum("uce,ucd->ued", a.astype(BF16), b.astype(BF16), preferred_element_type=F32)


def _scan_chunks(rt, kt, bt, at, v, wc, s0, n_forward):
    c = RWKV_CHUNK
    n_units = rt.shape[0]
    lane = lax.broadcasted_iota(jnp.int32, (n_units, c, LANES), 2)
    row = lax.broadcasted_iota(jnp.int32, (n_units, c, LANES), 1)
    unit = lax.broadcasted_iota(jnp.int32, (n_units, c, LANES), 0)
    head0 = lane < RWKV_HD
    src = _mod(lane, RWKV_HD)
    ahead = jnp.where(unit < n_forward, row - src, src - row)
    strict = ahead > 0
    incl = ahead >= 0

    def dup(x):
        zero = jnp.zeros_like(x)
        h0 = head0[:, :x.shape[1]]
        return jnp.concatenate([jnp.where(h0, x, zero), jnp.where(h0, zero, x)], axis=1)

    ar = jnp.concatenate([at, rt], axis=1)
    mb = _bdot_nt(ar, dup(bt))
    mk = _bdot_nt(ar, dup(kt))
    p_ab = jnp.where(strict, mb[:, :c], 0.0)
    p_rb = jnp.where(incl, mb[:, c:], 0.0)
    p_ak = jnp.where(strict, mk[:, :c], 0.0)
    p_rk = jnp.where(incl, mk[:, c:], 0.0)
    vd = dup(v)
    rhs = _bdot_nt(at, s0) + _bdot(p_ak, vd)

    power = p_ab.astype(BF16)
    inv = jnp.where(row == src, 1.0, 0.0) + p_ab
    for _ in range(int(np.log2(c)) - 1):
        power = _bdot(power, dup(power)).astype(BF16)
        inv = inv + _bdot(inv, dup(power))
    inv = inv.astype(BF16)
    u = _bdot(inv, dup(rhs))
    resid = (rhs - u) + _bdot(p_ab, dup(u))
    u = u + _bdot(inv, dup(resid))

    y = _bdot_nt(rt, s0) + _bdot(jnp.concatenate([p_rb, p_rk], axis=2), jnp.concatenate([dup(u), vd], axis=1))
    upd = _bdot_tn(jnp.concatenate([u.astype(BF16), v], axis=1), jnp.concatenate([bt, kt], axis=1))
    er = lax.broadcasted_iota(jnp.int32, (1, LANES, LANES), 1) < RWKV_HD
    ec = lax.broadcasted_iota(jnp.int32, (1, LANES, LANES), 2) < RWKV_HD
    s1 = (s0 + jnp.where(er == ec, upd, 0.0)) * wc
    return y, s1


def _rwkv_scan_kernel(rtf, ktf, btf, atf, vf, wcf, rtb, ktb, btb, atb, vb, wcb, yf_ref, yb_ref, s_ref):
    @pl.when(pl.program_id(1) == 0)
    def _():
        s_ref[...] = jnp.zeros_like(s_ref)

    n_pairs = rtf.shape[0]
    both = lambda f, b: jnp.concatenate([f[...], b[...]], axis=0)
    wc = jnp.concatenate([wcf[:, 0], wcb[:, 0]], axis=0)
    y, s1 = _scan_chunks(both(rtf, rtb), both(ktf, ktb), both(btf, btb), both(atf, atb), both(vf, vb), wc,
                         s_ref[...], n_pairs)
    s_ref[...] = s1
    yf_ref[...] = y[:n_pairs]
    yb_ref[...] = y[n_pairs:]


def _rwkv_scan(feats_f, feats_b, v, rows):
    n_pairs, m, _ = v.shape
    c = RWKV_CHUNK
    d = n_pairs * LANES
    nchunks = (rows.seq + rows.ctx_len) // c
    blk_f, _ = rows.chunk_block(c, False)
    blk_b, _ = rows.chunk_block(c, True)

    def specs(blk):
        tok = pl.BlockSpec((n_pairs, c, LANES), lambda bi, t: (0, blk(bi, t), 0))
        return tok, pl.BlockSpec((n_pairs, 1, 1, LANES), lambda bi, t: (0, blk(bi, t), 0, 0))

    tok_f, wc_f = specs(blk_f)
    tok_b, wc_b = specs(blk_b)
    return pl.pallas_call(
        _rwkv_scan_kernel,
        out_shape=[jax.ShapeDtypeStruct((n_pairs, m, LANES), F32)] * 2,
        grid=(rows.batch, nchunks),
        in_specs=[tok_f] * 5 + [wc_f] + [tok_b] * 5 + [wc_b],
        out_specs=[tok_f, tok_b],
        scratch_shapes=[pltpu.VMEM((2 * n_pairs, LANES, LANES), F32)],
        compiler_params=_cparams(2),
        name="rwkv_scan",
    )(*feats_f[:4], v, feats_f[4], *feats_b[:4], v, feats_b[4])


def _rwkv_readout_kernel(yf_ref, yb_ref, bonus_ref, g_ref, lnw_ref, lnb_ref, o_ref):
    y = jnp.concatenate([yf_ref[p] + yb_ref[p] for p in range(yf_ref.shape[0])], axis=1)
    yc = y - _group_mean(y, RWKV_HD)
    yn = yc * lax.rsqrt(_group_mean(yc * yc, RWKV_HD) + RWKV_GN_EPS) * lnw_ref[...] + lnb_ref[...]
    o_ref[...] = ((yn + bonus_ref[...]) * g_ref[...].astype(F32)).astype(o_ref.dtype)


def _rwkv_readout(yf, yb, bonus, g, ln_w, ln_b, rows, tm):
    n_pairs = yf.shape[0]
    d = n_pairs * LANES
    pspec = pl.BlockSpec((n_pairs, tm, LANES), lambda i: (0, i, 0))
    tspec = pl.BlockSpec((tm, d), lambda i: (i, 0))
    vspec = pl.BlockSpec((1, d), lambda i: (0, 0))
    return pl.pallas_call(
        _rwkv_readout_kernel,
        out_shape=jax.ShapeDtypeStruct((rows.ml, d), BF16),
        grid=(rows.ml // tm,),
        in_specs=[pspec, pspec, tspec, tspec, vspec, vspec],
        out_specs=tspec,
        compiler_params=_cparams(1),
        name="rwkv_readout",
    )(yf, yb, bonus, g, ln_w.reshape(1, d).astype(F32), ln_b.reshape(1, d).astype(F32))


def _router_kernel(x_ref, nw_ref, sh_ref, sc_ref, w_ref, o_ref, *, n_experts):
    h = _modnorm(x_ref[...], nw_ref[...], sh_ref[0], sc_ref[0])
    logits = _dot_hi(h, w_ref[...])
    lane = lax.broadcasted_iota(jnp.int32, logits.shape, 1).astype(F32)
    logits = jnp.where(lane < n_experts, logits, -jnp.inf)
    v1 = jnp.max(logits, axis=-1, keepdims=True)
    i1 = jnp.min(jnp.where(logits == v1, lane, float(LANES)), axis=-1, keepdims=True)
    rest = jnp.where(lane == i1, -jnp.inf, logits)
    v2 = jnp.max(rest, axis=-1, keepdims=True)
    i2 = jnp.min(jnp.where(rest == v2, lane, float(LANES)), axis=-1, keepdims=True)
    e2 = jnp.exp(v2 - v1)
    den = 1.0 + e2
    o_ref[...] = jnp.where(lane == i1, 1.0 / den, 0.0) + jnp.where(lane == i2, e2 / den, 0.0)


def _router(x, nw, shift, scale, router, *, tm, group_of_tile):
    m, d = x.shape
    n_experts = router.shape[1]
    w = jnp.zeros((d, LANES), F32).at[:, :n_experts].set(router.astype(F32))
    gmap = lambda i: (group_of_tile(i), 0, 0)
    return pl.pallas_call(
        functools.partial(_router_kernel, n_experts=n_experts),
        out_shape=jax.ShapeDtypeStruct((m, LANES), F32),
        grid=(m // tm,),
        in_specs=[pl.BlockSpec((tm, d), lambda i: (i, 0)), pl.BlockSpec((1, d), lambda i: (0, 0)),
                  pl.BlockSpec((1, 1, d), gmap), pl.BlockSpec((1, 1, d), gmap),
                  pl.BlockSpec((d, LANES), lambda i: (0, 0))],
        out_specs=pl.BlockSpec((tm, LANES), lambda i: (i, 0)),
        compiler_params=_cparams(1),
        name="router",
    )(x, nw.reshape(1, d), shift, scale, w)


MOE_TILE = 512
DMA_UNROLL = 8


def _route_plan(gates, n_experts, tg):
    m = gates.shape[0]
    g = gates[:, :n_experts]
    sel = g > 0.0
    seli = sel.astype(jnp.int32)
    slot = jnp.cumsum(seli, axis=1) - 1
    rank = jnp.cumsum(seli, axis=0) - 1
    counts = jnp.sum(seli, axis=0)
    padded = ((counts + tg - 1) // tg) * tg
    ends = jnp.cumsum(padded)
    dest = (ends - padded)[None, :] + rank
    n_rows = TOP_K * m + n_experts * tg
    n_tiles = n_rows // tg
    first = sel & (slot == 0)
    second = sel & (slot == 1)
    pick = lambda msk, val: jnp.sum(jnp.where(msk, val, 0), axis=1)
    d1 = pick(first, dest)
    has2 = jnp.any(second, axis=1)
    d2 = pick(second, dest)
    w1 = pick(first, g)
    w2 = pick(second, g)
    tile_expert = jnp.minimum(jnp.sum((jnp.arange(n_tiles, dtype=jnp.int32) * tg)[:, None] >= ends[None, :], axis=1),
                              n_experts - 1).astype(jnp.int32)
    n_valid = (ends[-1] // tg).astype(jnp.int32).reshape(1)
    wts = jnp.zeros((m, LANES), F32).at[:, 0].set(w1).at[:, 1].set(w2)
    i32 = lambda a: a.astype(jnp.int32)
    return (n_rows, tile_expert, n_valid, i32(d1), i32(jnp.where(has2, d2, n_rows)), i32(jnp.where(has2, d2, d1)),
            wts)


def _slab_rows(ref3, val):
    for s in range(ref3.shape[1]):
        ref3[:, s, :] = val[:, s * LANES:(s + 1) * LANES].astype(ref3.dtype)


def _unslab_rows(ref3):
    return jnp.concatenate([ref3[:, s, :] for s in range(ref3.shape[1])], axis=1)


def _moe_scatter_kernel(d1_ref, d2_ref, x_ref, nw_ref, sh_ref, sc_ref, a0_hbm, a_hbm, h_ref, sem, *, tm):
    del a0_hbm
    i = pl.program_id(0)
    slot = lax.rem(i, 2)
    _slab_rows(h_ref.at[slot], _modnorm(x_ref[...], nw_ref[...], sh_ref[0], sc_ref[0]))

    def row_copy(s_i, r, dst):
        return pltpu.make_async_copy(h_ref.at[s_i, pl.ds(r, 1)], a_hbm.at[pl.ds(dst, 1)], sem.at[s_i])

    def issue(r, carry):
        row_copy(slot, r, d1_ref[0, 0, r]).start(priority=0)
        row_copy(slot, r, d2_ref[0, 0, r]).start(priority=1)
        return carry

    def drain_slot(s_i):
        def drain(r, carry):
            row_copy(s_i, r, 0).wait()
            row_copy(s_i, r, 0).wait()
            return carry

        lax.fori_loop(0, tm, drain, 0, unroll=DMA_UNROLL)

    lax.fori_loop(0, tm, issue, 0, unroll=DMA_UNROLL)

    @pl.when(i > 0)
    def _():
        drain_slot(1 - slot)

    @pl.when(i == pl.num_programs(0) - 1)
    def _():
        drain_slot(slot)


def _moe_scatter(x, nw, shift, scale, d1, d2, n_rows, *, tm, tg, group_of_tile):
    m, d = x.shape
    slabs = d // LANES
    idx = lambda a: a.reshape(m // tm, 1, tm)
    ispec = pl.BlockSpec((1, 1, tm), lambda i: (i, 0, 0), memory_space=pltpu.SMEM)
    gmap = lambda i: (group_of_tile(i), 0, 0)
    zeros = jnp.zeros((n_rows + tg, slabs, LANES), F32)
    return pl.pallas_call(
        functools.partial(_moe_scatter_kernel, tm=tm),
        out_shape=jax.ShapeDtypeStruct(zeros.shape, F32),
        grid=(m // tm,),
        in_specs=[ispec, ispec, pl.BlockSpec((tm, d), lambda i: (i, 0)), pl.BlockSpec((1, d), lambda i: (0, 0)),
                  pl.BlockSpec((1, 1, d), gmap), pl.BlockSpec((1, 1, d), gmap), pl.BlockSpec(memory_space=pl.ANY)],
        out_specs=pl.BlockSpec(memory_space=pl.ANY),
        scratch_shapes=[pltpu.VMEM((2, tm, slabs, LANES), F32), pltpu.SemaphoreType.DMA((2,))],
        input_output_aliases={6: 0},
        compiler_params=_cparams(1),
        name="moe_scatter",
    )(idx(d1), idx(d2), x, nw.reshape(1, d), shift, scale, zeros)


def _grouped_swiglu_kernel(te_ref, nv_ref, a_ref, wg_ref, wu_ref, o_ref, ab_ref):
    t = pl.program_id(0)

    @pl.when(jnp.logical_and(t < nv_ref[0], pl.program_id(1) == 0))
    def _():
        ab_ref[...] = _unslab_rows(a_ref).astype(BF16)

    @pl.when(t < nv_ref[0])
    def _():
        a = ab_ref[...]
        gate = jnp.dot(a, wg_ref[0], preferred_element_type=F32)
        up = jnp.dot(a, wu_ref[0], preferred_element_type=F32)
        o_ref[...] = (_silu(gate) * up).astype(o_ref.dtype)

    @pl.when(t >= nv_ref[0])
    def _():
        o_ref[...] = jnp.zeros_like(o_ref)


def _grouped_swiglu(a, w13, tile_expert, n_valid, p_rows, *, tg, tn):
    slabs = a.shape[1]
    k = slabs * LANES
    e_hid = w13.shape[2] // 2
    nj = e_hid // tn
    tile = lambda t, nv: jnp.minimum(t, nv[0] - 1)
    return pl.pallas_call(
        _grouped_swiglu_kernel,
        out_shape=jax.ShapeDtypeStruct((p_rows, e_hid), BF16),
        grid_spec=pltpu.PrefetchScalarGridSpec(
            num_scalar_prefetch=2,
            grid=(p_rows // tg, nj),
            in_specs=[pl.BlockSpec((tg, slabs, LANES), lambda t, j, te, nv: (tile(t, nv), 0, 0)),
                      pl.BlockSpec((1, k, tn), lambda t, j, te, nv: (te[tile(t, nv)], 0, j)),
                      pl.BlockSpec((1, k, tn), lambda t, j, te, nv: (te[tile(t, nv)], 0, j + nj))],
            out_specs=pl.BlockSpec((tg, tn), lambda t, j, te, nv: (t, j)),
            scratch_shapes=[pltpu.VMEM((tg, k), BF16)],
        ),
        compiler_params=_cparams(2),
        name="moe_up",
    )(tile_expert, n_valid, a, w13, w13)


def _grouped_mm_kernel(te_ref, nv_ref, a_ref, w_ref, o_ref):
    @pl.when(pl.program_id(0) < nv_ref[0])
    def _():
        _slab_rows(o_ref, jnp.dot(a_ref[...], w_ref[0], preferred_element_type=F32))

    @pl.when(pl.program_id(0) >= nv_ref[0])
    def _():
        o_ref[...] = jnp.zeros_like(o_ref)


def _grouped_mm(a, w, tile_expert, n_valid, *, tg):
    p_rows, k = a.shape
    n = w.shape[2]
    slabs = n // LANES
    tile = lambda t, nv: jnp.minimum(t, nv[0] - 1)
    return pl.pallas_call(
        _grouped_mm_kernel,
        out_shape=jax.ShapeDtypeStruct((p_rows, slabs, LANES), F32),
        grid_spec=pltpu.PrefetchScalarGridSpec(
            num_scalar_prefetch=2,
            grid=(p_rows // tg,),
            in_specs=[pl.BlockSpec((tg, k), lambda t, te, nv: (tile(t, nv), 0)),
                      pl.BlockSpec((1, k, n), lambda t, te, nv: (te[tile(t, nv)], 0, 0))],
            out_specs=pl.BlockSpec((tg, slabs, LANES), lambda t, te, nv: (t, 0, 0)),
        ),
        compiler_params=_cparams(1),
        name="moe_down",
    )(tile_expert, n_valid, a, w)


def _combine_kernel(d1_ref, d2_ref, n1_ref, n2_ref, y_hbm, wts_ref, x_ref, gate_ref, o_ref, buf_ref, sem, *, tm):
    i = pl.program_id(0)
    n_steps = pl.num_programs(0)
    slot = lax.rem(i, 2)

    def row_copy(s_i, which, r, src):
        return pltpu.make_async_copy(y_hbm.at[pl.ds(src, 1)], buf_ref.at[s_i, which, pl.ds(r, 1)], sem.at[s_i])

    def fetch(s_i, first_ref, second_ref):
        def issue(r, carry):
            row_copy(s_i, 0, r, first_ref[0, 0, r]).start(priority=0)
            row_copy(s_i, 1, r, second_ref[0, 0, r]).start(priority=1)
            return carry

        lax.fori_loop(0, tm, issue, 0, unroll=DMA_UNROLL)

    @pl.when(i == 0)
    def _():
        fetch(0, d1_ref, d2_ref)

    @pl.when(i + 1 < n_steps)
    def _():
        fetch(1 - slot, n1_ref, n2_ref)

    def drain(r, carry):
        row_copy(slot, 0, r, 0).wait()
        row_copy(slot, 1, r, 0).wait()
        return carry

    lax.fori_loop(0, tm, drain, 0, unroll=DMA_UNROLL)
    lane = lax.broadcasted_iota(jnp.int32, wts_ref.shape, 1)
    w1 = jnp.sum(jnp.where(lane == 0, wts_ref[...], 0.0), axis=-1, keepdims=True)
    w2 = jnp.sum(jnp.where(lane == 1, wts_ref[...], 0.0), axis=-1, keepdims=True)
    y = w1 * _unslab_rows(buf_ref.at[slot, 0]) + w2 * _unslab_rows(buf_ref.at[slot, 1])
    o_ref[...] = x_ref[...] + gate_ref[0] * y


def _combine(y, d1, d2, wts, x, gate, *, tm, group_of_tile):
    m, d = x.shape
    n_steps = m // tm
    idx = lambda a: a.reshape(n_steps, 1, tm)
    ispec = pl.BlockSpec((1, 1, tm), lambda i: (i, 0, 0), memory_space=pltpu.SMEM)
    nspec = pl.BlockSpec((1, 1, tm), lambda i: (jnp.minimum(i + 1, n_steps - 1), 0, 0), memory_space=pltpu.SMEM)
    return pl.pallas_call(
        functools.partial(_combine_kernel, tm=tm),
        out_shape=jax.ShapeDtypeStruct((m, d), F32),
        grid=(n_steps,),
        in_specs=[ispec, ispec, nspec, nspec, pl.BlockSpec(memory_space=pl.ANY),
                  pl.BlockSpec((tm, LANES), lambda i: (i, 0)), pl.BlockSpec((tm, d), lambda i: (i, 0)),
                  pl.BlockSpec((1, 1, d), lambda i: (group_of_tile(i), 0, 0))],
        out_specs=pl.BlockSpec((tm, d), lambda i: (i, 0)),
        scratch_shapes=[pltpu.VMEM((2, 2, tm, d // LANES, LANES), F32), pltpu.SemaphoreType.DMA((2,))],
        compiler_params=_cparams(1),
        name="moe_combine",
    )(idx(d1), idx(d2), idx(d1), idx(d2), y, wts, x, gate)


def _rope_tables(seq, ctx_len):
    half = RET_HD // 4
    freqs = ROPE_BASE ** (-np.arange(half, dtype=np.float64) / half)
    t = np.arange(seq)
    ang_r = (t // GRID_W)[:, None] * freqs[None, :]
    ang_c = (t % GRID_W)[:, None] * freqs[None, :]
    cos = np.concatenate([np.cos(ang_r)] * 2 + [np.cos(ang_c)] * 2, axis=1)
    sin = np.concatenate([-np.sin(ang_r), np.sin(ang_r), -np.sin(ang_c), np.sin(ang_c)], axis=1)
    cos = np.concatenate([np.ones((ctx_len, RET_HD)), cos], axis=0)
    sin = np.concatenate([np.zeros((ctx_len, RET_HD)), sin], axis=0)
    return jnp.asarray(cos, F32), jnp.asarray(sin, F32)


def _half_or_full(n):
    return n // 2 if (n // 2) % LANES == 0 else n


def kernel(x, c, ctx, c_ctx, ada_w, ada_b, norm_mix_w, norm_ffn_w, ev_w_in, ev_ret_decay_f, ev_ret_decay_b,
           ev_ret_gn_w, ev_na_qn_w, ev_na_kn_w, ev_na_rpb, ev_w_out, ev_ffn_w13, ev_ffn_w2, od_mu, od_w_rkv,
           od_w0, od_w1, od_w2, od_a0, od_a1, od_a2, od_g1, od_g2, od_k_k, od_k_a, od_r_k, od_ln_w, od_ln_b,
           od_w_o, od_router, od_moe_w13, od_moe_w2):
    batch, seq, d = x.shape
    ctx_len = ctx.shape[1]
    rows = _Rows(batch, seq, ctx_len)
    tm = 8 * RWKV_CHUNK
    grp = rows.group_of_tile(tm)
    tm_big = 2 * tm if (rows.mc % (2 * tm) == 0 and seq % (2 * tm) == 0) else tm
    grp_big = rows.group_of_tile(tm_big)
    xa = jnp.concatenate([x.reshape(rows.ml, d), ctx.reshape(rows.mc, d)], axis=0)

    n_mod = ((batch + 1 + 7) // 8) * 8
    c_rows = jnp.zeros((n_mod, d), F32).at[:batch].set(c).at[batch].set(c_ctx)
    mods = _ada(c_rows, ada_w, ada_b)
    mods = mods.reshape(mods.shape[0], n_mod, 6, 1, d).transpose(0, 2, 1, 3, 4)

    mod = mods[0]
    w_in = ev_w_in[0].astype(BF16)
    proj = _modnorm_mm(xa, norm_mix_w[0], mod[0], mod[1], w_in, group_of_tile=grp_big, tm=tm_big,
                       tn=_half_or_full(w_in.shape[1]), out_dtype=F32)
    cos, sin = _rope_tables(seq, ctx_len)
    y_f, y_b = _retention(proj, cos, sin, ev_ret_decay_f[0], ev_ret_decay_b[0], rows)
    bias = _na_bias_table(ev_na_rpb[0].astype(F32))
    na_l = _na_latent(proj, ev_na_qn_w[0], ev_na_kn_w[0], bias, rows, 4 * RET_W)
    na_c = _ctx_attention(proj, ev_na_qn_w[0], ev_na_kn_w[0], rows, 4 * RET_W)
    mix = _ret_readout(y_f, y_b, proj, na_l, na_c, ev_ret_gn_w[0], rows, tm)
    w_out = ev_w_out[0].astype(BF16)
    xa = _mm_res(mix, w_out, xa, mod[2], tm=tm, group_of_tile=grp)

    w13 = ev_ffn_w13[0].astype(BF16)
    w2 = ev_ffn_w2[0].astype(BF16)
    hid = w2.shape[0]
    hidden = _modnorm_mm(xa, norm_ffn_w[0], mod[3], mod[4], w13, group_of_tile=grp_big, tm=tm_big,
                         tn=_half_or_full(hid), out_dtype=BF16, swiglu=True)
    xa = _mm_res(hidden, w2, xa, mod[5], tm=tm, group_of_tile=grp)

    mod = mods[1]
    glora = od_g1.shape[-1]
    glora_p = ((glora + LANES - 1) // LANES) * LANES
    p = {
        "w1": jnp.concatenate([od_w1[0, 0], od_w1[0, 1]], axis=1).astype(BF16),
        "w2": od_w2[0].astype(BF16),
        "a1": jnp.concatenate([od_a1[0, 0], od_a1[0, 1]], axis=1).astype(BF16),
        "a2": od_a2[0].astype(BF16),
        "g1": jnp.zeros((d, glora_p), BF16).at[:, :glora].set(od_g1[0].astype(BF16)),
        "g2": jnp.zeros((glora_p, d), BF16).at[:glora].set(od_g2[0].astype(BF16)),
        "w0": od_w0[0].astype(F32), "a0": od_a0[0].astype(F32),
        "k_k": od_k_k[0].reshape(1, d).astype(F32), "k_a": od_k_a[0].reshape(1, d).astype(F32),
        "r_k": od_r_k[0].reshape(1, d).astype(F32),
    }
    w_rkv = od_w_rkv[0].astype(BF16)
    r, k, v, lw_low, a_low, g_low = _rwkv_mix(xa, norm_mix_w[1], mod[0], mod[1], od_mu[0].astype(F32), w_rkv, p,
                                              rows=rows, tm=tm)
    f = _rwkv_post(r, k, v, lw_low, a_low, g_low, p, tm=tm // 2)
    y_f, y_b = _rwkv_scan(f[0:5], f[5:10], f[10], rows)
    mixed = _rwkv_readout(y_f, y_b, f[12], f[11], od_ln_w[0], od_ln_b[0], rows, tm)
    x_l = _mm_res(mixed, od_w_o[0].astype(BF16), xa, mod[2], tm=tm, group_of_tile=grp)

    gates = _router(x_l, norm_ffn_w[1], mod[3], mod[4], od_router[0], tm=tm, group_of_tile=grp)
    n_e = od_router.shape[-1]
    n_rows, tile_expert, n_valid, d1, d2_scatter, d2_combine, wts = _route_plan(gates, n_e, MOE_TILE)
    tm_r = 256
    grp_r = rows.group_of_tile(tm_r)
    a_sorted = _moe_scatter(x_l, norm_ffn_w[1], mod[3], mod[4], d1, d2_scatter, n_rows, tm=tm_r, tg=MOE_TILE,
                            group_of_tile=grp_r)
    moe13 = od_moe_w13[0].astype(BF16)
    hid_sorted = _grouped_swiglu(a_sorted, moe13, tile_expert, n_valid, n_rows, tg=MOE_TILE,
                                 tn=_half_or_full(moe13.shape[2] // 2))
    y_sorted = _grouped_mm(hid_sorted, od_moe_w2[0].astype(BF16), tile_expert, n_valid, tg=MOE_TILE)
    x_l = _combine(y_sorted, d1, d2_combine, wts, x_l, mod[5], tm=tm_r, group_of_tile=grp_r)
    return x_l.reshape(batch, seq, d)
```

```python
import functools

import jax
import jax.numpy as jnp
import numpy as np
from jax import lax
from jax.experimental import pallas as pl
from jax.experimental.pallas import tpu as pltpu

F32 = jnp.float32
BF16 = jnp.bfloat16

LANES = 128
GRID_W = 64
RET_HEADS = 4
RET_HD = 128
RET_W = RET_HEADS * RET_HD
RET_CHUNK = 128
RET_GN_EPS = 1e-5
NA_HEADS = 8
NA_HD = 64
NA_W = NA_HEADS * NA_HD
NA_KR = 8
NA_KC = 16
NA_ROWS_PER_STEP = 8
RWKV_HD = 64
RWKV_GN_EPS = 64e-5
RWKV_CHUNK = 64
TOP_K = 2
ROPE_BASE = 10000.0
NORM_EPS = 1e-6
NEG_BIG = -1e30
VMEM_LIMIT = 56 * 1024 * 1024


def _cparams(n_axes):
    return pltpu.CompilerParams(dimension_semantics=("arbitrary",) * n_axes, vmem_limit_bytes=VMEM_LIMIT)


def _dot(a, b):
    return jnp.dot(a.astype(BF16), b.astype(BF16), preferred_element_type=F32)


def _dot_nt(a, b):
    return lax.dot_general(a.astype(BF16), b.astype(BF16), (((1,), (1,)), ((), ())), preferred_element_type=F32)


def _split2(x):
    hi = x.astype(BF16)
    return hi, (x - hi.astype(F32)).astype(BF16)


def _split3(x):
    hi = x.astype(BF16)
    r1 = x - hi.astype(F32)
    mid = r1.astype(BF16)
    lo = (r1 - mid.astype(F32)).astype(BF16)
    return hi, mid, lo


def _dot_hi(a, b):
    ah, am, al = _split3(a)
    bh, bm, bl = _split3(b)
    d = functools.partial(jnp.dot, preferred_element_type=F32)
    return (d(ah, bh) + (d(ah, bm) + d(am, bh)) + (d(am, bm) + d(ah, bl) + d(al, bh)))


def _dot_exact_rhs(a, b_bf16):
    ah, al = _split2(a)
    d = functools.partial(jnp.dot, preferred_element_type=F32)
    return d(ah, b_bf16) + d(al, b_bf16)


def _dot_exact_lhs(a_bf16, b):
    bh, bl = _split2(b)
    d = functools.partial(jnp.dot, preferred_element_type=F32)
    return d(a_bf16, bh) + d(a_bf16, bl)


def _mod(v, n):
    return (v & (n - 1)) if n & (n - 1) == 0 else v % n


def _sigmoid(x):
    return 1.0 / (1.0 + jnp.exp(-x))


def _silu(x):
    return x * _sigmoid(x)


def _group_mean_mat(width, group):
    r = lax.broadcasted_iota(jnp.int32, (width, width), 0) // group
    c = lax.broadcasted_iota(jnp.int32, (width, width), 1) // group
    return jnp.where(r == c, 1.0 / group, 0.0).astype(BF16)


def _group_mean(x, group):
    g = _group_mean_mat(LANES, group)
    cols = [_dot_exact_rhs(x[:, c:c + LANES], g) for c in range(0, x.shape[1], LANES)]
    return cols[0] if len(cols) == 1 else jnp.concatenate(cols, axis=1)


def _modnorm(x, nw, shift, scale):
    ms = jnp.mean(x * x, axis=-1, keepdims=True)
    y = x * lax.rsqrt(ms + NORM_EPS) * nw
    return y * (1.0 + scale) + shift


class _Rows:
    def __init__(self, batch, seq, ctx_len):
        self.batch, self.seq, self.ctx_len = batch, seq, ctx_len
        self.ml, self.mc = batch * seq, batch * ctx_len
        self.m = self.ml + self.mc

    def group_of_tile(self, tm):
        return lambda i: jnp.minimum((i * tm) // self.seq, self.batch)

    def chunk_block(self, chunk, backward):
        ncc, nlc = self.ctx_len // chunk, self.seq // chunk

        def block(b, t):
            if backward:
                t = jnp.where(t < ncc, ncc - 1 - t, nlc + 2 * ncc - 1 - t)
            return jnp.where(t < ncc, (self.ml + b * self.ctx_len) // chunk + t, (b * self.seq) // chunk + t - ncc)

        def position(t):
            if backward:
                t = jnp.where(t < ncc, ncc - 1 - t, nlc + 2 * ncc - 1 - t)
            return t

        return block, position


def _ada_kernel(c_ref, w_ref, b_ref, o_ref):
    s = _silu(c_ref[...])
    o_ref[0] = _dot_hi(s, w_ref[0]) + b_ref[0]


def _ada(c_rows, ada_w, ada_b):
    depth, d, n = ada_w.shape
    rows = c_rows.shape[0]
    tn = 1536
    return pl.pallas_call(
        _ada_kernel,
        out_shape=jax.ShapeDtypeStruct((depth, rows, n), F32),
        grid=(depth, n // tn),
        in_specs=[
            pl.BlockSpec((rows, d), lambda l, j: (0, 0)),
            pl.BlockSpec((1, d, tn), lambda l, j: (l, 0, j)),
            pl.BlockSpec((1, 1, tn), lambda l, j: (l, 0, j)),
        ],
        out_specs=pl.BlockSpec((1, rows, tn), lambda l, j: (l, 0, j)),
        compiler_params=_cparams(2),
        name="ada_mod",
    )(c_rows, ada_w, ada_b.reshape(depth, 1, n))


def _modnorm_mm_kernel(*refs, swiglu, head_tiles):
    x_ref, xt_ref, nw_ref, sh_ref, sc_ref = refs[:5]
    w_refs = refs[5:-2]
    o_ref, h_ref = refs[-2:]

    @pl.when(pl.program_id(1) == 0)
    def _():
        x = jnp.where(pl.program_id(0) < head_tiles, x_ref[...], xt_ref[...])
        h_ref[...] = _modnorm(x, nw_ref[...], sh_ref[0], sc_ref[0]).astype(BF16)

    h = h_ref[...]
    acc = jnp.dot(h, w_refs[0][...], preferred_element_type=F32)
    if swiglu:
        acc = _silu(acc) * jnp.dot(h, w_refs[1][...], preferred_element_type=F32)
    o_ref[...] = acc.astype(o_ref.dtype)


def _modnorm_mm(x, nw, shift, scale, w, *, group_of_tile, tm, tn, out_dtype, swiglu=False, x_tail=None):
    k = x.shape[1]
    head_tiles = x.shape[0] // tm
    if x_tail is None:
        x_tail = x
        m = x.shape[0]
    else:
        m = x.shape[0] + x_tail.shape[0]
    n_out = w.shape[1] // 2 if swiglu else w.shape[1]
    nj = n_out // tn
    gmap = lambda i, j: (group_of_tile(i), 0, 0)
    wspecs = [pl.BlockSpec((k, tn), lambda i, j: (0, j))]
    if swiglu:
        wspecs.append(pl.BlockSpec((k, tn), lambda i, j: (0, j + nj)))
    return pl.pallas_call(
        functools.partial(_modnorm_mm_kernel, swiglu=swiglu, head_tiles=head_tiles),
        out_shape=jax.ShapeDtypeStruct((m, n_out), out_dtype),
        grid=(m // tm, nj),
        in_specs=[pl.BlockSpec((tm, k), lambda i, j: (jnp.minimum(i, head_tiles - 1), 0)),
                  pl.BlockSpec((tm, k), lambda i, j: (jnp.maximum(i - head_tiles, 0), 0)),
                  pl.BlockSpec((1, k), lambda i, j: (0, 0)),
                  pl.BlockSpec((1, 1, k), gmap), pl.BlockSpec((1, 1, k), gmap)] + wspecs,
        out_specs=pl.BlockSpec((tm, tn), lambda i, j: (i, j)),
        scratch_shapes=[pltpu.VMEM((tm, k), BF16)],
        compiler_params=_cparams(2),
        name="modnorm_mm",
    )(x, x_tail, nw.reshape(1, k), shift, scale, *([w] * len(wspecs)))


def _mm_res_kernel(a_ref, w_ref, res_ref, rest_ref, gate_ref, o_ref, *, head_tiles):
    res = jnp.where(pl.program_id(0) < head_tiles, res_ref[...], rest_ref[...])
    o_ref[...] = res + gate_ref[0] * jnp.dot(a_ref[...], w_ref[...], preferred_element_type=F32)


def _mm_res(a, w, res, gate, *, tm, group_of_tile, res_tail=None):
    m, k = a.shape
    n = w.shape[1]
    head_tiles = m // tm if res_tail is None else res.shape[0] // tm
    tail = res if res_tail is None else res_tail
    return pl.pallas_call(
        functools.partial(_mm_res_kernel, head_tiles=head_tiles),
        out_shape=jax.ShapeDtypeStruct((m, n), F32),
        grid=(m // tm,),
        in_specs=[pl.BlockSpec((tm, k), lambda i: (i, 0)), pl.BlockSpec((k, n), lambda i: (0, 0)),
                  pl.BlockSpec((tm, n), lambda i: (jnp.minimum(i, head_tiles - 1), 0)),
                  pl.BlockSpec((tm, n), lambda i: (jnp.maximum(i - head_tiles, 0), 0)),
                  pl.BlockSpec((1, 1, n), lambda i: (group_of_tile(i), 0, 0))],
        out_specs=pl.BlockSpec((tm, n), lambda i: (i, 0)),
        compiler_params=_cparams(1),
        name="mm_res",
    )(a, w, res, tail, gate)


def _rope(x, cos, sin_signed):
    lane = lax.broadcasted_iota(jnp.int32, x.shape, 1)
    half = RET_HD // 4
    swapped = jnp.where(_mod(lane, 2 * half) < half, pltpu.roll(x, LANES - half, 1), pltpu.roll(x, half, 1))
    return x * cos + swapped * sin_signed


def _ret_chunks(q, k, v, s0, log_g, n_forward):
    c = RET_CHUNK
    shape = (q.shape[0], c, c)
    row = lax.broadcasted_iota(jnp.int32, shape, 1)
    col = lax.broadcasted_iota(jnp.int32, shape, 2)
    fwd = lax.broadcasted_iota(jnp.int32, shape, 0) < n_forward
    dist = jnp.where(fwd, row - col, col - row).astype(F32)
    q_steps = jnp.where(fwd, row + 1, c - row).astype(F32)
    k_steps = jnp.where(fwd, c - 1 - row, row).astype(F32)
    dmat = jnp.where(dist >= 0, jnp.exp(log_g * jnp.maximum(dist, 0.0)), 0.0)
    q_dec = jnp.exp(log_g * q_steps)
    k_dec = jnp.exp(log_g * k_steps)
    scores = _bdot_nt(q, k) * dmat
    out = _bdot(scores, v) + _bdot(q * q_dec, s0)
    s1 = s0 * jnp.exp(log_g * float(c)) + _bdot_tn(k * k_dec, v)
    return out, s1


def _retention_kernel(qf_ref, kf_ref, vf_ref, cf_ref, sf_ref, qb_ref, kb_ref, vb_ref, cb_ref, sb_ref,
                      decf_ref, decb_ref, of_ref, ob_ref, st_ref):
    @pl.when(pl.program_id(1) == 0)
    def _():
        st_ref[...] = jnp.zeros_like(st_ref)

    h = RET_HEADS
    kscale = RET_HD ** -0.5

    def heads(ref, cos, sin, scale=None):
        out = []
        for hh in range(h):
            xh = ref[:, hh * RET_HD:(hh + 1) * RET_HD]
            if cos is not None:
                xh = _rope(xh, cos, sin)
            out.append(xh if scale is None else xh * scale)
        return out

    cf, sf, cb, sb = cf_ref[...], sf_ref[...], cb_ref[...], sb_ref[...]
    q = jnp.stack(heads(qf_ref, cf, sf) + heads(qb_ref, cb, sb), axis=0)
    k = jnp.stack(heads(kf_ref, cf, sf, kscale) + heads(kb_ref, cb, sb, kscale), axis=0)
    v = jnp.stack(heads(vf_ref, None, None) + heads(vb_ref, None, None), axis=0)
    log_g = -jnp.exp(jnp.concatenate([decf_ref[...], decb_ref[...]], axis=0))
    out, s1 = _ret_chunks(q, k, v, st_ref[...], log_g, h)
    st_ref[...] = s1
    of_ref[...] = jnp.concatenate([out[hh] for hh in range(h)], axis=1)
    ob_ref[...] = jnp.concatenate([out[h + hh] for hh in range(h)], axis=1)


def _retention(proj, cos, sin_signed, dec_f, dec_b, rows):
    c, h = RET_CHUNK, RET_HEADS
    n = (rows.seq + rows.ctx_len) // c
    blk_f, pos_f = rows.chunk_block(c, False)
    blk_b, pos_b = rows.chunk_block(c, True)

    def tok(blk, off):
        return pl.BlockSpec((c, RET_W), lambda bi, t: (blk(bi, t), off))

    def tab(pos):
        return pl.BlockSpec((c, RET_HD), lambda bi, t: (pos(t), 0))

    dec = lambda a: jnp.broadcast_to(a.astype(F32)[:, None, None], (h, 1, LANES))
    dspec = pl.BlockSpec((h, 1, LANES), lambda bi, t: (0, 0, 0))
    return pl.pallas_call(
        _retention_kernel,
        out_shape=[jax.ShapeDtypeStruct((rows.m, RET_W), F32)] * 2,
        grid=(rows.batch, n),
        in_specs=[tok(blk_f, 0), tok(blk_f, 1), tok(blk_f, 2), tab(pos_f), tab(pos_f),
                  tok(blk_b, 0), tok(blk_b, 1), tok(blk_b, 2), tab(pos_b), tab(pos_b), dspec, dspec],
        out_specs=[tok(blk_f, 0), tok(blk_b, 0)],
        scratch_shapes=[pltpu.VMEM((2 * h, RET_HD, RET_HD), F32)],
        compiler_params=_cparams(2),
        name="retention",
    )(proj, proj, proj, cos, sin_signed, proj, proj, proj, cos, sin_signed, dec(dec_f), dec(dec_b))


def _na_qk_norm(x, w):
    ms = _group_mean(x * x, NA_HD)
    return x * lax.rsqrt(ms + NORM_EPS) * w


def _softmax_pv(parts):
    m = functools.reduce(jnp.maximum, [jnp.max(s, axis=-1, keepdims=True) for s, _ in parts])
    ps = [jnp.exp(s - m) for s, _ in parts]
    den = functools.reduce(jnp.add, [jnp.sum(p, axis=-1, keepdims=True) for p in ps])
    num = functools.reduce(jnp.add, [_dot(p, v) for p, (_, v) in zip(ps, parts)])
    return num / den


def _na_kernel(q_ref, k_ref, v_ref, kc_ref, vc_ref, qn_ref, kn_ref, bias_ref, o_ref, qs_ref, ks_ref, vs_ref,
               kcs_ref, *, rows):
    scale = NA_HD ** -0.5
    qs_ref[...] = (_na_qk_norm(q_ref[...], qn_ref[...]) * scale).astype(BF16)
    ks_ref[...] = _na_qk_norm(k_ref[...], kn_ref[...]).astype(BF16)
    vs_ref[...] = v_ref[...].astype(BF16)
    kcs_ref[...] = _na_qk_norm(kc_ref[...], kn_ref[...]).astype(BF16)
    lane = lax.broadcasted_iota(jnp.int32, (GRID_W, LANES), 1)
    first = lane < NA_HD
    vc = vc_ref[...]
    kc = kcs_ref[...]

    nq = 2 * GRID_W

    def body(rb, carry):
        qs, kw, vw, bw = [], [], [], []
        for i in range(NA_ROWS_PER_STEP):
            r = rb * NA_ROWS_PER_STEP + i
            r0 = jnp.clip(r - NA_KR // 2, 0, rows - NA_KR)
            q_r = qs_ref[pl.ds(pl.multiple_of(r * GRID_W, GRID_W), GRID_W), :]
            zero = jnp.zeros_like(q_r)
            qs.append(jnp.concatenate([jnp.where(first, q_r, zero), jnp.where(first, zero, q_r)], axis=0))
            win = pl.ds(pl.multiple_of(r0 * GRID_W, GRID_W), NA_KR * GRID_W)
            kw.append(ks_ref[win, :])
            vw.append(vs_ref[win, :])
            bw.append(bias_ref[0, r0 - r + (NA_KR - 1)])
        q = jnp.stack(qs, axis=0)
        s_w = _bdot_nt(q, jnp.stack(kw, axis=0)) + jnp.stack(bw, axis=0)
        q_flat = q.reshape(NA_ROWS_PER_STEP * nq, LANES)
        s_c = _dot_nt(q_flat, kc).reshape(NA_ROWS_PER_STEP, nq, kc.shape[0])
        m = jnp.maximum(jnp.max(s_w, axis=-1, keepdims=True), jnp.max(s_c, axis=-1, keepdims=True))
        p_w = jnp.exp(s_w - m)
        p_c = jnp.exp(s_c - m)
        den = jnp.sum(p_w, axis=-1, keepdims=True) + jnp.sum(p_c, axis=-1, keepdims=True)
        num = _bdot(p_w, jnp.stack(vw, axis=0)) + _dot(p_c.reshape(NA_ROWS_PER_STEP * nq, kc.shape[0]), vc).reshape(
            NA_ROWS_PER_STEP, nq, LANES)
        out = num / den
        for i in range(NA_ROWS_PER_STEP):
            r = rb * NA_ROWS_PER_STEP + i
            o_ref[pl.ds(pl.multiple_of(r * GRID_W, GRID_W), GRID_W), :] = jnp.where(
                first, out[i, :GRID_W], out[i, GRID_W:])
        return carry

    lax.fori_loop(0, rows // NA_ROWS_PER_STEP, body, 0)


def _na_bias_table(rpb):
    cols = np.arange(GRID_W)
    start = np.clip(cols - NA_KC // 2, 0, GRID_W - NA_KC)
    kcol = np.arange(GRID_W)
    inside = (kcol[None, :] >= start[:, None]) & (kcol[None, :] < start[:, None] + NA_KC)
    col_off = np.clip(kcol[None, :] - cols[:, None] + (NA_KC - 1), 0, 2 * NA_KC - 2)
    row_off = np.arange(NA_KR)[:, None] + np.arange(NA_KR)[None, :]
    tab = rpb[:, row_off][:, :, :, col_off]
    tab = jnp.where(inside[None, None, None], tab, NEG_BIG)
    tab = jnp.transpose(tab, (0, 1, 3, 2, 4))
    tab = tab.reshape(rpb.shape[0] // 2, 2, NA_KR, GRID_W, NA_KR * GRID_W).transpose(0, 2, 1, 3, 4)
    return tab.reshape(rpb.shape[0] // 2, NA_KR, 2 * GRID_W, NA_KR * GRID_W).astype(F32)


def _na_latent(proj, qn_w, kn_w, bias, rows, col0):
    seq, ctx_len = rows.seq, rows.ctx_len
    grid_rows = seq // GRID_W
    pairs = NA_W // LANES
    cb = col0 // LANES
    cblk = rows.ml // ctx_len
    tile2 = lambda a: jnp.tile(a.astype(F32), 2).reshape(1, LANES)
    lat = lambda off: pl.BlockSpec((seq, LANES), lambda bi, p: (bi, cb + off * pairs + p))
    ctx = lambda off: pl.BlockSpec((ctx_len, LANES), lambda bi, p: (cblk + bi, cb + off * pairs + p))
    return pl.pallas_call(
        functools.partial(_na_kernel, rows=grid_rows),
        out_shape=jax.ShapeDtypeStruct((rows.ml, NA_W), F32),
        grid=(rows.batch, pairs),
        in_specs=[lat(0), lat(1), lat(2), ctx(1), ctx(2),
                  pl.BlockSpec((1, LANES), lambda bi, p: (0, 0)), pl.BlockSpec((1, LANES), lambda bi, p: (0, 0)),
                  pl.BlockSpec((1, NA_KR, 2 * GRID_W, NA_KR * GRID_W), lambda bi, p: (p, 0, 0, 0))],
        out_specs=pl.BlockSpec((seq, LANES), lambda bi, p: (bi, p)),
        scratch_shapes=[pltpu.VMEM((seq, LANES), BF16)] * 3 + [pltpu.VMEM((ctx_len, LANES), BF16)],
        compiler_params=_cparams(2),
        name="na_latent",
    )(proj, proj, proj, proj, proj, tile2(qn_w), tile2(kn_w), bias)


def _ctx_attn_kernel(q_ref, k_ref, v_ref, qn_ref, kn_ref, o_ref):
    scale = NA_HD ** -0.5
    q = (_na_qk_norm(q_ref[...], qn_ref[...]) * scale).astype(BF16)
    k = _na_qk_norm(k_ref[...], kn_ref[...]).astype(BF16)
    v = v_ref[...]
    lane = lax.broadcasted_iota(jnp.int32, q.shape, 1)
    first = lane < NA_HD
    outs = []
    for hh in range(2):
        qh = jnp.where(first if hh == 0 else jnp.logical_not(first), q, jnp.zeros_like(q))
        outs.append(_softmax_pv([(_dot_nt(qh, k), v)]))
    o_ref[...] = jnp.where(first, outs[0], outs[1])


def _ctx_attention(proj, qn_w, kn_w, rows, col0):
    ctx_len = rows.ctx_len
    pairs = NA_W // LANES
    cb = col0 // LANES
    cblk = rows.ml // ctx_len
    tile2 = lambda a: jnp.tile(a.astype(F32), 2).reshape(1, LANES)
    blk = lambda off: pl.BlockSpec((ctx_len, LANES), lambda bi, p: (cblk + bi, cb + off * pairs + p))
    return pl.pallas_call(
        _ctx_attn_kernel,
        out_shape=jax.ShapeDtypeStruct((rows.mc, NA_W), F32),
        grid=(rows.batch, pairs),
        in_specs=[blk(0), blk(1), blk(2),
                  pl.BlockSpec((1, LANES), lambda bi, p: (0, 0)), pl.BlockSpec((1, LANES), lambda bi, p: (0, 0))],
        out_specs=pl.BlockSpec((ctx_len, LANES), lambda bi, p: (bi, p)),
        compiler_params=_cparams(2),
        name="ctx_attention",
    )(proj, proj, proj, tile2(qn_w), tile2(kn_w))


def _ret_readout_kernel(yf_ref, yb_ref, g_ref, nal_ref, nac_ref, gnw_ref, o_ref, *, n_latent_tiles):
    y = yf_ref[...] + yb_ref[...]
    for hh in range(RET_HEADS):
        sl = slice(hh * RET_HD, (hh + 1) * RET_HD)
        yh = y[:, sl]
        yc = yh - jnp.mean(yh, axis=-1, keepdims=True)
        yn = yc * lax.rsqrt(jnp.mean(yc * yc, axis=-1, keepdims=True) + RET_GN_EPS)
        o_ref[:, sl] = (yn * gnw_ref[:, sl] * _silu(g_ref[:, sl])).astype(o_ref.dtype)
    is_latent = pl.program_id(0) < n_latent_tiles
    o_ref[:, RET_W:] = jnp.where(is_latent, nal_ref[...], nac_ref[...]).astype(o_ref.dtype)


def _ret_readout(yf, yb, proj, na_l, na_c, gn_w, rows, tm):
    nl = rows.ml // tm
    row = lambda i: (i, 0)
    return pl.pallas_call(
        functools.partial(_ret_readout_kernel, n_latent_tiles=nl),
        out_shape=jax.ShapeDtypeStruct((rows.m, RET_W + NA_W), BF16),
        grid=(rows.m // tm,),
        in_specs=[pl.BlockSpec((tm, RET_W), row), pl.BlockSpec((tm, RET_W), row),
                  pl.BlockSpec((tm, RET_W), lambda i: (i, 3)),
                  pl.BlockSpec((tm, NA_W), lambda i: (jnp.minimum(i, nl - 1), 0)),
                  pl.BlockSpec((tm, NA_W), lambda i: (jnp.maximum(i - nl, 0), 0)),
                  pl.BlockSpec((1, RET_W), lambda i: (0, 0))],
        out_specs=pl.BlockSpec((tm, RET_W + NA_W), row),
        compiler_params=_cparams(1),
        name="ret_readout",
    )(yf, yb, proj, na_l, na_c, gn_w.reshape(1, RET_W).astype(F32))


def _rwkv_mix_kernel(x_ref, xp_ref, xn_ref, nw_ref, sh_ref, sc_ref, mu_ref, wrkv_ref, w1_ref, a1_ref, g1_ref,
                     r_ref, k_ref, v_ref, lwl_ref, al_ref, gl_ref, *, seq, ctx_len, ml, tm):
    nw, sh, sc = nw_ref[...], sh_ref[0], sc_ref[0]
    h = _modnorm(x_ref[...], nw, sh, sc)
    hp = _modnorm(xp_ref[...], nw, sh, sc)[7:8]
    hn = _modnorm(xn_ref[...], nw, sh, sc)[0:1]
    row = lax.broadcasted_iota(jnp.int32, h.shape, 0)
    g = pl.program_id(0) * tm + row
    latent = g < ml
    pos = jnp.where(latent, _mod(g, seq), _mod(g, ctx_len))
    first = pos == 0
    last = pos == jnp.where(latent, seq - 1, ctx_len - 1)
    up = jnp.where(row == 0, hp, pltpu.roll(h, 1, 0))
    up = jnp.where(first, 0.0, up)
    dn = jnp.where(row == tm - 1, hn, pltpu.roll(h, tm - 1, 0))
    dn = jnp.where(last, 0.0, dn)
    xx = 0.5 * (up + dn) - h
    mix = lambda s_i: (h + xx * mu_ref[s_i:s_i + 1, :]).astype(BF16)
    for s_i, o_ref in enumerate((r_ref, k_ref, v_ref)):
        o_ref[...] = jnp.dot(mix(s_i), wrkv_ref[s_i], preferred_element_type=F32)
    lwl_ref[...] = jnp.tanh(jnp.dot(mix(3), w1_ref[...], preferred_element_type=F32))
    al_ref[...] = jnp.dot(mix(4), a1_ref[...], preferred_element_type=F32)
    gl_ref[...] = _sigmoid(jnp.dot(mix(5), g1_ref[...], preferred_element_type=F32))


def _rwkv_mix(x, nw, shift, scale, mu, w_rkv, p, *, rows, tm):
    m, d = x.shape
    n_slab = m // 8
    per = tm // 8
    gmap = lambda i: (rows.group_of_tile(tm)(i), 0, 0)
    row = lambda i: (i, 0)
    full2 = lambda i: (0, 0)
    smalls = [p["w1"], p["a1"], p["g1"]]
    return pl.pallas_call(
        functools.partial(_rwkv_mix_kernel, seq=rows.seq, ctx_len=rows.ctx_len, ml=rows.ml, tm=tm),
        out_shape=[jax.ShapeDtypeStruct((m, d), F32)] * 3
        + [jax.ShapeDtypeStruct((m, w.shape[1]), F32) for w in smalls],
        grid=(m // tm,),
        in_specs=[pl.BlockSpec((tm, d), row),
                  pl.BlockSpec((8, d), lambda i: (jnp.maximum(i * per - 1, 0), 0)),
                  pl.BlockSpec((8, d), lambda i: (jnp.minimum((i + 1) * per, n_slab - 1), 0)),
                  pl.BlockSpec((1, d), full2),
                  pl.BlockSpec((1, 1, d), gmap), pl.BlockSpec((1, 1, d), gmap),
                  pl.BlockSpec((6, d), full2),
                  pl.BlockSpec((3, d, d), lambda i: (0, 0, 0))]
        + [pl.BlockSpec(w.shape, full2) for w in smalls],
        out_specs=[pl.BlockSpec((tm, d), row)] * 3 + [pl.BlockSpec((tm, w.shape[1]), row) for w in smalls],
        compiler_params=_cparams(1),
        name="rwkv_mix",
    )(x, x, x, nw.reshape(1, d), shift, scale, mu, w_rkv, *smalls)


def _rwkv_post_kernel(r_ref, k_ref, v_ref, lwl_ref, al_ref, gl_ref, w2_ref, a2_ref, g2_ref,
                      w0_ref, a0_ref, kk_ref, ka_ref, rk_ref,
                      rt0, kt0, bt0, at0, wc0, rt1, kt1, bt1, at1, wc1, vb_ref, g_ref, bonus_ref, *, tm):
    c = RWKV_CHUNK
    n_pairs = r_ref.shape[1] // LANES
    r, k, v = r_ref[...], k_ref[...], v_ref[...]
    lw_low, a_low = lwl_ref[...], al_ref[...]
    g_ref[...] = _dot(gl_ref[...], g2_ref[...]).astype(g_ref.dtype)

    def put(o_ref, val):
        for p in range(n_pairs):
            o_ref[p] = val[:, p * LANES:(p + 1) * LANES].astype(o_ref.dtype)

    put(vb_ref, v)
    kk = k * kk_ref[...]
    nrm = jnp.sqrt(_group_mean(kk * kk, RWKV_HD) * float(RWKV_HD))
    kk = kk / jnp.maximum(nrm, 1e-12)

    tr = lax.broadcasted_iota(jnp.int32, (c, c), 0)
    tc = lax.broadcasted_iota(jnp.int32, (c, c), 1)
    tri = [(tr >= tc).astype(BF16), (tr <= tc).astype(BF16)]
    lora = w2_ref.shape[1]
    coeff_src = jnp.zeros_like(r)
    outs = [(rt0, kt0, bt0, at0, wc0), (rt1, kt1, bt1, at1, wc1)]
    for z in range(2):
        rt_ref, kt_ref, bt_ref, at_ref, wc_ref = outs[z]
        w_lora = _dot(lw_low[:, z * lora:(z + 1) * lora], w2_ref[z])
        lw = -float(np.exp(-0.5)) * _sigmoid(w0_ref[z:z + 1, :] + w_lora)
        a_z = _sigmoid(a0_ref[z:z + 1, :] + _dot(a_low[:, z * lora:(z + 1) * lora], a2_ref[z]))
        k_dir = k * (1.0 + (a_z - 1.0) * ka_ref[...])
        coeff_src = coeff_src + r * k_dir * rk_ref[...]
        cum = jnp.concatenate([_dot_exact_lhs(tri[z], lw[ci * c:(ci + 1) * c]) for ci in range(tm // c)], axis=0)
        e_pos = jnp.exp(cum)
        e_neg = 1.0 / e_pos
        put(rt_ref, r * e_pos)
        put(kt_ref, k_dir * e_neg)
        put(bt_ref, kk * a_z * e_neg)
        put(at_ref, -kk * jnp.exp(cum - lw))
        for ci in range(tm // c):
            end = ci * c + (c - 1 if z == 0 else 0)
            for p in range(n_pairs):
                wc_ref[p, ci] = e_pos[end:end + 1, p * LANES:(p + 1) * LANES]
    coeff = _group_mean(coeff_src, RWKV_HD) * float(RWKV_HD)
    bonus_ref[...] = coeff * v


def _rwkv_post(r, k, v, lw_low, a_low, g_low, p, *, tm):
    m, d = r.shape
    c = RWKV_CHUNK
    lora = p["w2"].shape[1]
    glora = p["g1"].shape[1]
    small = lambda a: pl.BlockSpec((tm, a.shape[1]), lambda i: (i, 0))
    n_pairs = d // LANES
    pair = jax.ShapeDtypeStruct((n_pairs, m, LANES), BF16)
    chunk = jax.ShapeDtypeStruct((n_pairs, m // c, 1, LANES), F32)
    row = lambda i: (i, 0)
    full2 = lambda i: (0, 0)
    full3 = lambda i: (0, 0, 0)
    tspec = pl.BlockSpec((tm, d), row)
    pspec = pl.BlockSpec((n_pairs, tm, LANES), lambda i: (0, i, 0))
    cspec = pl.BlockSpec((n_pairs, tm // c, 1, LANES), lambda i: (0, i, 0, 0))
    return pl.pallas_call(
        functools.partial(_rwkv_post_kernel, tm=tm),
        out_shape=[pair] * 4 + [chunk] + [pair] * 4 + [chunk]
        + [pair, jax.ShapeDtypeStruct((m, d), BF16), jax.ShapeDtypeStruct((m, d), F32)],
        grid=(m // tm,),
        in_specs=[tspec] * 3 + [small(lw_low), small(a_low), small(g_low)]
        + [pl.BlockSpec((2, lora, d), full3), pl.BlockSpec((2, lora, d), full3), pl.BlockSpec((glora, d), full2),
           pl.BlockSpec((2, d), full2), pl.BlockSpec((2, d), full2)]
        + [pl.BlockSpec((1, d), full2)] * 3,
        out_specs=[pspec] * 4 + [cspec] + [pspec] * 4 + [cspec] + [pspec, tspec, tspec],
        compiler_params=_cparams(1),
        name="rwkv_post",
    )(r, k, v, lw_low, a_low, g_low, p["w2"], p["a2"], p["g2"], p["w0"], p["a0"], p["k_k"], p["k_a"], p["r_k"])


def _bdot(a, b):
    return jnp.einsum("ucd,ude->uce", a.astype(BF16), b.astype(BF16), preferred_element_type=F32)


def _bdot_nt(a, b):
    return jnp.einsum("ucd,usd->ucs", a.astype(BF16), b.astype(BF16), preferred_element_type=F32)


def _bdot_tn(a, b):
    return jnp.einsum("uce,ucd->ued", a.astype(BF16), b.astype(BF16), preferred_element_type=F32)


def _scan_chunks(rt, kt, bt, at, v, wc, s0, n_forward):
    c = RWKV_CHUNK
    n_units = rt.shape[0]
    lane = lax.broadcasted_iota(jnp.int32, (n_units, c, LANES), 2)
    row = lax.broadcasted_iota(jnp.int32, (n_units, c, LANES), 1)
    unit = lax.broadcasted_iota(jnp.int32, (n_units, c, LANES), 0)
    head0 = lane < RWKV_HD
    src = _mod(lane, RWKV_HD)
    ahead = jnp.where(unit < n_forward, row - src, src - row)
    strict = ahead > 0
    incl = ahead >= 0

    def dup(x):
        zero = jnp.zeros_like(x)
        h0 = head0[:, :x.shape[1]]
        return jnp.concatenate([jnp.where(h0, x, zero), jnp.where(h0, zero, x)], axis=1)

    ar = jnp.concatenate([at, rt], axis=1)
    mb = _bdot_nt(ar, dup(bt))
    mk = _bdot_nt(ar, dup(kt))
    p_ab = jnp.where(strict, mb[:, :c], 0.0)
    p_rb = jnp.where(incl, mb[:, c:], 0.0)
    p_ak = jnp.where(strict, mk[:, :c], 0.0)
    p_rk = jnp.where(incl, mk[:, c:], 0.0)
    vd = dup(v)
    rhs = _bdot_nt(at, s0) + _bdot(p_ak, vd)

    power = p_ab.astype(BF16)
    inv = jnp.where(row == src, 1.0, 0.0) + p_ab
    for _ in range(int(np.log2(c)) - 1):
        power = _bdot(power, dup(power)).astype(BF16)
        inv = inv + _bdot(inv, dup(power))
    inv = inv.astype(BF16)
    u = _bdot(inv, dup(rhs))
    resid = (rhs - u) + _bdot(p_ab, dup(u))
    u = u + _bdot(inv, dup(resid))

    y = _bdot_nt(rt, s0) + _bdot(jnp.concatenate([p_rb, p_rk], axis=2), jnp.concatenate([dup(u), vd], axis=1))
    upd = _bdot_tn(jnp.concatenate([u.astype(BF16), v], axis=1), jnp.concatenate([bt, kt], axis=1))
    er = lax.broadcasted_iota(jnp.int32, (1, LANES, LANES), 1) < RWKV_HD
    ec = lax.broadcasted_iota(jnp.int32, (1, LANES, LANES), 2) < RWKV_HD
    s1 = (s0 + jnp.where(er == ec, upd, 0.0)) * wc
    return y, s1


def _rwkv_scan_kernel(rtf, ktf, btf, atf, vf, wcf, rtb, ktb, btb, atb, vb, wcb, yf_ref, yb_ref, s_ref):
    @pl.when(pl.program_id(1) == 0)
    def _():
        s_ref[...] = jnp.zeros_like(s_ref)

    n_pairs = rtf.shape[0]
    both = lambda f, b: jnp.concatenate([f[...], b[...]], axis=0)
    wc = jnp.concatenate([wcf[:, 0], wcb[:, 0]], axis=0)
    y, s1 = _scan_chunks(both(rtf, rtb), both(ktf, ktb), both(btf, btb), both(atf, atb), both(vf, vb), wc,
                         s_ref[...], n_pairs)
    s_ref[...] = s1
    yf_ref[...] = y[:n_pairs]
    yb_ref[...] = y[n_pairs:]


def _rwkv_scan(feats_f, feats_b, v, rows):
    n_pairs, m, _ = v.shape
    c = RWKV_CHUNK
    d = n_pairs * LANES
    nchunks = (rows.seq + rows.ctx_len) // c
    blk_f, _ = rows.chunk_block(c, False)
    blk_b, _ = rows.chunk_block(c, True)

    def specs(blk):
        tok = pl.BlockSpec((n_pairs, c, LANES), lambda bi, t: (0, blk(bi, t), 0))
        return tok, pl.BlockSpec((n_pairs, 1, 1, LANES), lambda bi, t: (0, blk(bi, t), 0, 0))

    tok_f, wc_f = specs(blk_f)
    tok_b, wc_b = specs(blk_b)
    return pl.pallas_call(
        _rwkv_scan_kernel,
        out_shape=[jax.ShapeDtypeStruct((n_pairs, m, LANES), F32)] * 2,
        grid=(rows.batch, nchunks),
        in_specs=[tok_f] * 5 + [wc_f] + [tok_b] * 5 + [wc_b],
        out_specs=[tok_f, tok_b],
        scratch_shapes=[pltpu.VMEM((2 * n_pairs, LANES, LANES), F32)],
        compiler_params=_cparams(2),
        name="rwkv_scan",
    )(*feats_f[:4], v, feats_f[4], *feats_b[:4], v, feats_b[4])


def _rwkv_readout_kernel(yf_ref, yb_ref, bonus_ref, g_ref, lnw_ref, lnb_ref, o_ref):
    y = jnp.concatenate([yf_ref[p] + yb_ref[p] for p in range(yf_ref.shape[0])], axis=1)
    yc = y - _group_mean(y, RWKV_HD)
    yn = yc * lax.rsqrt(_group_mean(yc * yc, RWKV_HD) + RWKV_GN_EPS) * lnw_ref[...] + lnb_ref[...]
    o_ref[...] = ((yn + bonus_ref[...]) * g_ref[...].astype(F32)).astype(o_ref.dtype)


def _rwkv_readout(yf, yb, bonus, g, ln_w, ln_b, rows, tm):
    n_pairs = yf.shape[0]
    d = n_pairs * LANES
    pspec = pl.BlockSpec((n_pairs, tm, LANES), lambda i: (0, i, 0))
    tspec = pl.BlockSpec((tm, d), lambda i: (i, 0))
    vspec = pl.BlockSpec((1, d), lambda i: (0, 0))
    return pl.pallas_call(
        _rwkv_readout_kernel,
        out_shape=jax.ShapeDtypeStruct((rows.ml, d), BF16),
        grid=(rows.ml // tm,),
        in_specs=[pspec, pspec, tspec, tspec, vspec, vspec],
        out_specs=tspec,
        compiler_params=_cparams(1),
        name="rwkv_readout",
    )(yf, yb, bonus, g, ln_w.reshape(1, d).astype(F32), ln_b.reshape(1, d).astype(F32))


def _router_kernel(x_ref, nw_ref, sh_ref, sc_ref, w_ref, o_ref, *, n_experts):
    h = _modnorm(x_ref[...], nw_ref[...], sh_ref[0], sc_ref[0])
    logits = _dot_hi(h, w_ref[...])
    lane = lax.broadcasted_iota(jnp.int32, logits.shape, 1).astype(F32)
    logits = jnp.where(lane < n_experts, logits, -jnp.inf)
    v1 = jnp.max(logits, axis=-1, keepdims=True)
    i1 = jnp.min(jnp.where(logits == v1, lane, float(LANES)), axis=-1, keepdims=True)
    rest = jnp.where(lane == i1, -jnp.inf, logits)
    v2 = jnp.max(rest, axis=-1, keepdims=True)
    i2 = jnp.min(jnp.where(rest == v2, lane, float(LANES)), axis=-1, keepdims=True)
    e2 = jnp.exp(v2 - v1)
    den = 1.0 + e2
    o_ref[...] = jnp.where(lane == i1, 1.0 / den, 0.0) + jnp.where(lane == i2, e2 / den, 0.0)


def _router(x, nw, shift, scale, router, *, tm, group_of_tile):
    m, d = x.shape
    n_experts = router.shape[1]
    w = jnp.zeros((d, LANES), F32).at[:, :n_experts].set(router.astype(F32))
    gmap = lambda i: (group_of_tile(i), 0, 0)
    return pl.pallas_call(
        functools.partial(_router_kernel, n_experts=n_experts),
        out_shape=jax.ShapeDtypeStruct((m, LANES), F32),
        grid=(m // tm,),
        in_specs=[pl.BlockSpec((tm, d), lambda i: (i, 0)), pl.BlockSpec((1, d), lambda i: (0, 0)),
                  pl.BlockSpec((1, 1, d), gmap), pl.BlockSpec((1, 1, d), gmap),
                  pl.BlockSpec((d, LANES), lambda i: (0, 0))],
        out_specs=pl.BlockSpec((tm, LANES), lambda i: (i, 0)),
        compiler_params=_cparams(1),
        name="router",
    )(x, nw.reshape(1, d), shift, scale, w)


MOE_TILE = 512
DMA_UNROLL = 8


def _route_plan(gates, n_experts, tg):
    m = gates.shape[0]
    g = gates[:, :n_experts]
    sel = g > 0.0
    seli = sel.astype(jnp.int32)
    slot = jnp.cumsum(seli, axis=1) - 1
    rank = jnp.cumsum(seli, axis=0) - 1
    counts = jnp.sum(seli, axis=0)
    padded = ((counts + tg - 1) // tg) * tg
    ends = jnp.cumsum(padded)
    dest = (ends - padded)[None, :] + rank
    n_rows = TOP_K * m + n_experts * tg
    n_tiles = n_rows // tg
    first = sel & (slot == 0)
    second = sel & (slot == 1)
    pick = lambda msk, val: jnp.sum(jnp.where(msk, val, 0), axis=1)
    d1 = pick(first, dest)
    has2 = jnp.any(second, axis=1)
    d2 = pick(second, dest)
    w1 = pick(first, g)
    w2 = pick(second, g)
    tile_expert = jnp.minimum(jnp.sum((jnp.arange(n_tiles, dtype=jnp.int32) * tg)[:, None] >= ends[None, :], axis=1),
                              n_experts - 1).astype(jnp.int32)
    n_valid = (ends[-1] // tg).astype(jnp.int32).reshape(1)
    lane = jnp.arange(LANES)[None, :]
    wts = jnp.where(lane == 0, w1[:, None], jnp.where(lane == 1, w2[:, None], 0.0)).astype(F32)
    i32 = lambda a: a.astype(jnp.int32)
    return (n_rows, tile_expert, n_valid, i32(d1), i32(jnp.where(has2, d2, n_rows)), i32(jnp.where(has2, d2, d1)),
            wts)


def _slab_rows(ref3, val):
    for s in range(ref3.shape[1]):
        ref3[:, s, :] = val[:, s * LANES:(s + 1) * LANES].astype(ref3.dtype)


def _unslab_rows(ref3):
    return jnp.concatenate([ref3[:, s, :] for s in range(ref3.shape[1])], axis=1)


def _moe_scatter_kernel(d1_ref, d2_ref, x_ref, nw_ref, sh_ref, sc_ref, a0_hbm, a_hbm, h_ref, sem, *, tm):
    del a0_hbm
    i = pl.program_id(0)
    slot = lax.rem(i, 2)
    _slab_rows(h_ref.at[slot], _modnorm(x_ref[...], nw_ref[...], sh_ref[0], sc_ref[0]))

    def row_copy(s_i, r, dst):
        return pltpu.make_async_copy(h_ref.at[s_i, pl.ds(r, 1)], a_hbm.at[pl.ds(dst, 1)], sem.at[s_i])

    def issue(r, carry):
        row_copy(slot, r, d1_ref[0, 0, r]).start(priority=0)
        row_copy(slot, r, d2_ref[0, 0, r]).start(priority=1)
        return carry

    def drain_slot(s_i):
        def drain(r, carry):
            row_copy(s_i, r, 0).wait()
            row_copy(s_i, r, 0).wait()
            return carry

        lax.fori_loop(0, tm, drain, 0, unroll=DMA_UNROLL)

    lax.fori_loop(0, tm, issue, 0, unroll=DMA_UNROLL)

    @pl.when(i > 0)
    def _():
        drain_slot(1 - slot)

    @pl.when(i == pl.num_programs(0) - 1)
    def _():
        drain_slot(slot)


def _moe_scatter(x, nw, shift, scale, d1, d2, n_rows, *, tm, tg, group_of_tile):
    m, d = x.shape
    slabs = d // LANES
    idx = lambda a: a.reshape(m // tm, 1, tm)
    ispec = pl.BlockSpec((1, 1, tm), lambda i: (i, 0, 0), memory_space=pltpu.SMEM)
    gmap = lambda i: (group_of_tile(i), 0, 0)
    zeros = jnp.zeros((n_rows + tg, slabs, LANES), F32)
    return pl.pallas_call(
        functools.partial(_moe_scatter_kernel, tm=tm),
        out_shape=jax.ShapeDtypeStruct(zeros.shape, F32),
        grid=(m // tm,),
        in_specs=[ispec, ispec, pl.BlockSpec((tm, d), lambda i: (i, 0)), pl.BlockSpec((1, d), lambda i: (0, 0)),
                  pl.BlockSpec((1, 1, d), gmap), pl.BlockSpec((1, 1, d), gmap), pl.BlockSpec(memory_space=pl.ANY)],
        out_specs=pl.BlockSpec(memory_space=pl.ANY),
        scratch_shapes=[pltpu.VMEM((2, tm, slabs, LANES), F32), pltpu.SemaphoreType.DMA((2,))],
        input_output_aliases={6: 0},
        compiler_params=_cparams(1),
        name="moe_scatter",
    )(idx(d1), idx(d2), x, nw.reshape(1, d), shift, scale, zeros)


def _grouped_swiglu_kernel(te_ref, nv_ref, a_ref, wg_ref, wu_ref, o_ref, ab_ref):
    t = pl.program_id(0)

    @pl.when(jnp.logical_and(t < nv_ref[0], pl.program_id(1) == 0))
    def _():
        ab_ref[...] = _unslab_rows(a_ref).astype(BF16)

    @pl.when(t < nv_ref[0])
    def _():
        a = ab_ref[...]
        gate = jnp.dot(a, wg_ref[0], preferred_element_type=F32)
        up = jnp.dot(a, wu_ref[0], preferred_element_type=F32)
        o_ref[...] = (_silu(gate) * up).astype(o_ref.dtype)

    @pl.when(t >= nv_ref[0])
    def _():
        o_ref[...] = jnp.zeros_like(o_ref)


def _grouped_swiglu(a, w13, tile_expert, n_valid, p_rows, *, tg, tn):
    slabs = a.shape[1]
    k = slabs * LANES
    e_hid = w13.shape[2] // 2
    nj = e_hid // tn
    tile = lambda t, nv: jnp.minimum(t, nv[0] - 1)
    return pl.pallas_call(
        _grouped_swiglu_kernel,
        out_shape=jax.ShapeDtypeStruct((p_rows, e_hid), BF16),
        grid_spec=pltpu.PrefetchScalarGridSpec(
            num_scalar_prefetch=2,
            grid=(p_rows // tg, nj),
            in_specs=[pl.BlockSpec((tg, slabs, LANES), lambda t, j, te, nv: (tile(t, nv), 0, 0)),
                      pl.BlockSpec((1, k, tn), lambda t, j, te, nv: (te[tile(t, nv)], 0, j)),
                      pl.BlockSpec((1, k, tn), lambda t, j, te, nv: (te[tile(t, nv)], 0, j + nj))],
            out_specs=pl.BlockSpec((tg, tn), lambda t, j, te, nv: (t, j)),
            scratch_shapes=[pltpu.VMEM((tg, k), BF16)],
        ),
        compiler_params=_cparams(2),
        name="moe_up",
    )(tile_expert, n_valid, a, w13, w13)


def _grouped_mm_kernel(te_ref, nv_ref, a_ref, w_ref, o_ref):
    @pl.when(pl.program_id(0) < nv_ref[0])
    def _():
        _slab_rows(o_ref, jnp.dot(a_ref[...], w_ref[0], preferred_element_type=F32))

    @pl.when(pl.program_id(0) >= nv_ref[0])
    def _():
        o_ref[...] = jnp.zeros_like(o_ref)


def _grouped_mm(a, w, tile_expert, n_valid, *, tg):
    p_rows, k = a.shape
    n = w.shape[2]
    slabs = n // LANES
    tile = lambda t, nv: jnp.minimum(t, nv[0] - 1)
    return pl.pallas_call(
        _grouped_mm_kernel,
        out_shape=jax.ShapeDtypeStruct((p_rows, slabs, LANES), F32),
        grid_spec=pltpu.PrefetchScalarGridSpec(
            num_scalar_prefetch=2,
            grid=(p_rows // tg,),
            in_specs=[pl.BlockSpec((tg, k), lambda t, te, nv: (tile(t, nv), 0)),
                      pl.BlockSpec((1, k, n), lambda t, te, nv: (te[tile(t, nv)], 0, 0))],
            out_specs=pl.BlockSpec((tg, slabs, LANES), lambda t, te, nv: (t, 0, 0)),
        ),
        compiler_params=_cparams(1),
        name="moe_down",
    )(tile_expert, n_valid, a, w)


def _combine_kernel(d1_ref, d2_ref, n1_ref, n2_ref, y_hbm, wts_ref, x_ref, gate_ref, o_ref, buf_ref, sem, *, tm):
    i = pl.program_id(0)
    n_steps = pl.num_programs(0)
    slot = lax.rem(i, 2)

    def row_copy(s_i, which, r, src):
        return pltpu.make_async_copy(y_hbm.at[pl.ds(src, 1)], buf_ref.at[s_i, which, pl.ds(r, 1)], sem.at[s_i])

    def fetch(s_i, first_ref, second_ref):
        def issue(r, carry):
            row_copy(s_i, 0, r, first_ref[0, 0, r]).start(priority=0)
            row_copy(s_i, 1, r, second_ref[0, 0, r]).start(priority=1)
            return carry

        lax.fori_loop(0, tm, issue, 0, unroll=DMA_UNROLL)

    @pl.when(i == 0)
    def _():
        fetch(0, d1_ref, d2_ref)

    @pl.when(i + 1 < n_steps)
    def _():
        fetch(1 - slot, n1_ref, n2_ref)

    def drain(r, carry):
        row_copy(slot, 0, r, 0).wait()
        row_copy(slot, 1, r, 0).wait()
        return carry

    lax.fori_loop(0, tm, drain, 0, unroll=DMA_UNROLL)
    lane = lax.broadcasted_iota(jnp.int32, wts_ref.shape, 1)
    w1 = jnp.sum(jnp.where(lane == 0, wts_ref[...], 0.0), axis=-1, keepdims=True)
    w2 = jnp.sum(jnp.where(lane == 1, wts_ref[...], 0.0), axis=-1, keepdims=True)
    y = w1 * _unslab_rows(buf_ref.at[slot, 0]) + w2 * _unslab_rows(buf_ref.at[slot, 1])
    o_ref[...] = x_ref[...] + gate_ref[0] * y


def _combine(y, d1, d2, wts, x, gate, *, tm, group_of_tile):
    m, d = x.shape
    n_steps = m // tm
    idx = lambda a: a.reshape(n_steps, 1, tm)
    ispec = pl.BlockSpec((1, 1, tm), lambda i: (i, 0, 0), memory_space=pltpu.SMEM)
    nspec = pl.BlockSpec((1, 1, tm), lambda i: (jnp.minimum(i + 1, n_steps - 1), 0, 0), memory_space=pltpu.SMEM)
    return pl.pallas_call(
        functools.partial(_combine_kernel, tm=tm),
        out_shape=jax.ShapeDtypeStruct((m, d), F32),
        grid=(n_steps,),
        in_specs=[ispec, ispec, nspec, nspec, pl.BlockSpec(memory_space=pl.ANY),
                  pl.BlockSpec((tm, LANES), lambda i: (i, 0)), pl.BlockSpec((tm, d), lambda i: (i, 0)),
                  pl.BlockSpec((1, 1, d), lambda i: (group_of_tile(i), 0, 0))],
        out_specs=pl.BlockSpec((tm, d), lambda i: (i, 0)),
        scratch_shapes=[pltpu.VMEM((2, 2, tm, d // LANES, LANES), F32), pltpu.SemaphoreType.DMA((2,))],
        compiler_params=_cparams(1),
        name="moe_combine",
    )(idx(d1), idx(d2), idx(d1), idx(d2), y, wts, x, gate)


def _rope_tables(seq, ctx_len):
    half = RET_HD // 4
    freqs = ROPE_BASE ** (-np.arange(half, dtype=np.float64) / half)
    t = np.arange(seq)
    ang_r = (t // GRID_W)[:, None] * freqs[None, :]
    ang_c = (t % GRID_W)[:, None] * freqs[None, :]
    cos = np.concatenate([np.cos(ang_r)] * 2 + [np.cos(ang_c)] * 2, axis=1)
    sin = np.concatenate([-np.sin(ang_r), np.sin(ang_r), -np.sin(ang_c), np.sin(ang_c)], axis=1)
    cos = np.concatenate([np.ones((ctx_len, RET_HD)), cos], axis=0)
    sin = np.concatenate([np.zeros((ctx_len, RET_HD)), sin], axis=0)
    return jnp.asarray(cos, F32), jnp.asarray(sin, F32)


def _half_or_full(n):
    return n // 2 if (n // 2) % LANES == 0 else n


def kernel(x, c, ctx, c_ctx, ada_w, ada_b, norm_mix_w, norm_ffn_w, ev_w_in, ev_ret_decay_f, ev_ret_decay_b,
           ev_ret_gn_w, ev_na_qn_w, ev_na_kn_w, ev_na_rpb, ev_w_out, ev_ffn_w13, ev_ffn_w2, od_mu, od_w_rkv,
           od_w0, od_w1, od_w2, od_a0, od_a1, od_a2, od_g1, od_g2, od_k_k, od_k_a, od_r_k, od_ln_w, od_ln_b,
           od_w_o, od_router, od_moe_w13, od_moe_w2):
    batch, seq, d = x.shape
    ctx_len = ctx.shape[1]
    rows = _Rows(batch, seq, ctx_len)
    tm = 8 * RWKV_CHUNK
    grp = rows.group_of_tile(tm)
    tm_big = 2 * tm if (rows.mc % (2 * tm) == 0 and seq % (2 * tm) == 0) else tm
    grp_big = rows.group_of_tile(tm_big)
    x_rows, ctx_rows = x.reshape(rows.ml, d), ctx.reshape(rows.mc, d)

    n_mod = ((batch + 1 + 7) // 8) * 8
    c_rows = jnp.zeros((n_mod, d), F32).at[:batch].set(c).at[batch].set(c_ctx)
    mods = _ada(c_rows, ada_w, ada_b)
    mods = mods.reshape(mods.shape[0], n_mod, 6, 1, d).transpose(0, 2, 1, 3, 4)

    mod = mods[0]
    w_in = ev_w_in[0].astype(BF16)
    proj = _modnorm_mm(x_rows, norm_mix_w[0], mod[0], mod[1], w_in, group_of_tile=grp_big, tm=tm_big,
                       tn=_half_or_full(w_in.shape[1]), out_dtype=F32, x_tail=ctx_rows)
    cos, sin = _rope_tables(seq, ctx_len)
    y_f, y_b = _retention(proj, cos, sin, ev_ret_decay_f[0], ev_ret_decay_b[0], rows)
    bias = _na_bias_table(ev_na_rpb[0].astype(F32))
    na_l = _na_latent(proj, ev_na_qn_w[0], ev_na_kn_w[0], bias, rows, 4 * RET_W)
    na_c = _ctx_attention(proj, ev_na_qn_w[0], ev_na_kn_w[0], rows, 4 * RET_W)
    mix = _ret_readout(y_f, y_b, proj, na_l, na_c, ev_ret_gn_w[0], rows, tm)
    w_out = ev_w_out[0].astype(BF16)
    xa = _mm_res(mix, w_out, x_rows, mod[2], tm=tm, group_of_tile=grp, res_tail=ctx_rows)

    w13 = ev_ffn_w13[0].astype(BF16)
    w2 = ev_ffn_w2[0].astype(BF16)
    hid = w2.shape[0]
    hidden = _modnorm_mm(xa, norm_ffn_w[0], mod[3], mod[4], w13, group_of_tile=grp_big, tm=tm_big,
                         tn=_half_or_full(hid), out_dtype=BF16, swiglu=True)
    xa = _mm_res(hidden, w2, xa, mod[5], tm=tm, group_of_tile=grp)

    mod = mods[1]
    glora = od_g1.shape[-1]
    glora_p = ((glora + LANES - 1) // LANES) * LANES
    p = {
        "w1": jnp.concatenate([od_w1[0, 0], od_w1[0, 1]], axis=1).astype(BF16),
        "w2": od_w2[0].astype(BF16),
        "a1": jnp.concatenate([od_a1[0, 0], od_a1[0, 1]], axis=1).astype(BF16),
        "a2": od_a2[0].astype(BF16),
        "g1": jnp.zeros((d, glora_p), BF16).at[:, :glora].set(od_g1[0].astype(BF16)),
        "g2": jnp.zeros((glora_p, d), BF16).at[:glora].set(od_g2[0].astype(BF16)),
        "w0": od_w0[0].astype(F32), "a0": od_a0[0].astype(F32),
        "k_k": od_k_k[0].reshape(1, d).astype(F32), "k_a": od_k_a[0].reshape(1, d).astype(F32),
        "r_k": od_r_k[0].reshape(1, d).astype(F32),
    }
    w_rkv = od_w_rkv[0].astype(BF16)
    r, k, v, lw_low, a_low, g_low = _rwkv_mix(xa, norm_mix_w[1], mod[0], mod[1], od_mu[0].astype(F32), w_rkv, p,
                                              rows=rows, tm=tm)
    f = _rwkv_post(r, k, v, lw_low, a_low, g_low, p, tm=tm // 2)
    y_f, y_b = _rwkv_scan(f[0:5], f[5:10], f[10], rows)
    mixed = _rwkv_readout(y_f, y_b, f[12], f[11], od_ln_w[0], od_ln_b[0], rows, tm)
    x_l = _mm_res(mixed, od_w_o[0].astype(BF16), xa, mod[2], tm=tm, group_of_tile=grp)

    gates = _router(x_l, norm_ffn_w[1], mod[3], mod[4], od_router[0], tm=tm, group_of_tile=grp)
    n_e = od_router.shape[-1]
    n_rows, tile_expert, n_valid, d1, d2_scatter, d2_combine, wts = _route_plan(gates, n_e, MOE_TILE)
    tm_r = tm
    grp_r = rows.group_of_tile(tm_r)
    a_sorted = _moe_scatter(x_l, norm_ffn_w[1], mod[3], mod[4], d1, d2_scatter, n_rows, tm=tm_r, tg=MOE_TILE,
                            group_of_tile=grp_r)
    moe13 = od_moe_w13[0].astype(BF16)
    hid_sorted = _grouped_swiglu(a_sorted, moe13, tile_expert, n_valid, n_rows, tg=MOE_TILE, tn=moe13.shape[2] // 2)
    y_sorted = _grouped_mm(hid_sorted, od_moe_w2[0].astype(BF16), tile_expert, n_valid, tg=MOE_TILE)
    x_l = _combine(y_sorted, d1, d2_combine, wts, x_l, mod[5], tm=tm_r, group_of_tile=grp_r)
    return x_l.reshape(batch, seq, d)
```

```python
import functools

import jax
import jax.numpy as jnp
import numpy as np
from jax import lax
from jax.experimental import pallas as pl
from jax.experimental.pallas import tpu as pltpu

F32 = jnp.float32
BF16 = jnp.bfloat16

LANES = 128
GRID_W = 64
RET_HEADS = 4
RET_HD = 128
RET_W = RET_HEADS * RET_HD
RET_CHUNK = 128
RET_GN_EPS = 1e-5
NA_HEADS = 8
NA_HD = 64
NA_W = NA_HEADS * NA_HD
NA_KR = 8
NA_KC = 16
NA_ROWS_PER_STEP = 8
RWKV_HD = 64
RWKV_GN_EPS = 64e-5
RWKV_CHUNK = 64
TOP_K = 2
ROPE_BASE = 10000.0
NORM_EPS = 1e-6
NEG_BIG = -1e30
VMEM_LIMIT = 56 * 1024 * 1024


def _cparams(n_axes):
    return pltpu.CompilerParams(dimension_semantics=("arbitrary",) * n_axes, vmem_limit_bytes=VMEM_LIMIT)


def _dot(a, b):
    return jnp.dot(a.astype(BF16), b.astype(BF16), preferred_element_type=F32)


def _dot_nt(a, b):
    return lax.dot_general(a.astype(BF16), b.astype(BF16), (((1,), (1,)), ((), ())), preferred_element_type=F32)


def _split2(x):
    hi = x.astype(BF16)
    return hi, (x - hi.astype(F32)).astype(BF16)


def _split3(x):
    hi = x.astype(BF16)
    r1 = x - hi.astype(F32)
    mid = r1.astype(BF16)
    lo = (r1 - mid.astype(F32)).astype(BF16)
    return hi, mid, lo


def _dot_hi(a, b):
    ah, am, al = _split3(a)
    bh, bm, bl = _split3(b)
    d = functools.partial(jnp.dot, preferred_element_type=F32)
    return (d(ah, bh) + (d(ah, bm) + d(am, bh)) + (d(am, bm) + d(ah, bl) + d(al, bh)))


def _dot_exact_rhs(a, b_bf16):
    ah, al = _split2(a)
    d = functools.partial(jnp.dot, preferred_element_type=F32)
    return d(ah, b_bf16) + d(al, b_bf16)


def _dot_exact_lhs(a_bf16, b):
    bh, bl = _split2(b)
    d = functools.partial(jnp.dot, preferred_element_type=F32)
    return d(a_bf16, bh) + d(a_bf16, bl)


def _mod(v, n):
    return (v & (n - 1)) if n & (n - 1) == 0 else v % n


def _sigmoid(x):
    return 1.0 / (1.0 + jnp.exp(-x))


def _silu(x):
    return x * _sigmoid(x)


def _group_mean_mat(width, group):
    r = lax.broadcasted_iota(jnp.int32, (width, width), 0) // group
    c = lax.broadcasted_iota(jnp.int32, (width, width), 1) // group
    return jnp.where(r == c, 1.0 / group, 0.0).astype(BF16)


def _group_mean(x, group):
    g = _group_mean_mat(LANES, group)
    cols = [_dot_exact_rhs(x[:, c:c + LANES], g) for c in range(0, x.shape[1], LANES)]
    return cols[0] if len(cols) == 1 else jnp.concatenate(cols, axis=1)


def _modnorm(x, nw, shift, scale):
    ms = jnp.mean(x * x, axis=-1, keepdims=True)
    y = x * lax.rsqrt(ms + NORM_EPS) * nw
    return y * (1.0 + scale) + shift


class _Rows:
    def __init__(self, batch, seq, ctx_len):
        self.batch, self.seq, self.ctx_len = batch, seq, ctx_len
        self.ml, self.mc = batch * seq, batch * ctx_len
        self.m = self.ml + self.mc

    def group_of_tile(self, tm):
        return lambda i: jnp.minimum((i * tm) // self.seq, self.batch)

    def chunk_block(self, chunk, backward):
        ncc, nlc = self.ctx_len // chunk, self.seq // chunk

        def block(b, t):
            if backward:
                t = jnp.where(t < ncc, ncc - 1 - t, nlc + 2 * ncc - 1 - t)
            return jnp.where(t < ncc, (self.ml + b * self.ctx_len) // chunk + t, (b * self.seq) // chunk + t - ncc)

        def position(t):
            if backward:
                t = jnp.where(t < ncc, ncc - 1 - t, nlc + 2 * ncc - 1 - t)
            return t

        return block, position


def _ada_kernel(c_ref, w_ref, b_ref, o_ref):
    s = _silu(c_ref[...])
    o_ref[0, 0] = _dot_hi(s, w_ref[0]) + b_ref[0]


def _ada(c_rows, ada_w, ada_b):
    depth, d, n = ada_w.shape
    rows = c_rows.shape[0]
    terms = n // d
    return pl.pallas_call(
        _ada_kernel,
        out_shape=jax.ShapeDtypeStruct((depth, terms, rows, d), F32),
        grid=(depth, terms),
        in_specs=[
            pl.BlockSpec((rows, d), lambda l, j: (0, 0)),
            pl.BlockSpec((1, d, d), lambda l, j: (l, 0, j)),
            pl.BlockSpec((1, 1, d), lambda l, j: (l, 0, j)),
        ],
        out_specs=pl.BlockSpec((1, 1, rows, d), lambda l, j: (l, j, 0, 0)),
        compiler_params=_cparams(2),
        name="ada_mod",
    )(c_rows, ada_w, ada_b.reshape(depth, 1, n))


def _modnorm_mm_kernel(*refs, swiglu, head_tiles):
    x_ref, xt_ref, nw_ref, sh_ref, sc_ref = refs[:5]
    w_refs = refs[5:-2]
    o_ref, h_ref = refs[-2:]

    @pl.when(pl.program_id(1) == 0)
    def _():
        x = jnp.where(pl.program_id(0) < head_tiles, x_ref[...], xt_ref[...])
        h_ref[...] = _modnorm(x, nw_ref[...], sh_ref[0], sc_ref[0]).astype(BF16)

    h = h_ref[...]
    acc = jnp.dot(h, w_refs[0][...], preferred_element_type=F32)
    if swiglu:
        acc = _silu(acc) * jnp.dot(h, w_refs[1][...], preferred_element_type=F32)
    o_ref[...] = acc.astype(o_ref.dtype)


def _modnorm_mm(x, nw, shift, scale, w, *, group_of_tile, tm, tn, out_dtype, swiglu=False, x_tail=None):
    k = x.shape[1]
    head_tiles = x.shape[0] // tm
    if x_tail is None:
        x_tail = x
        m = x.shape[0]
    else:
        m = x.shape[0] + x_tail.shape[0]
    n_out = w.shape[1] // 2 if swiglu else w.shape[1]
    nj = n_out // tn
    gmap = lambda i, j: (group_of_tile(i), 0, 0)
    wspecs = [pl.BlockSpec((k, tn), lambda i, j: (0, j))]
    if swiglu:
        wspecs.append(pl.BlockSpec((k, tn), lambda i, j: (0, j + nj)))
    return pl.pallas_call(
        functools.partial(_modnorm_mm_kernel, swiglu=swiglu, head_tiles=head_tiles),
        out_shape=jax.ShapeDtypeStruct((m, n_out), out_dtype),
        grid=(m // tm, nj),
        in_specs=[pl.BlockSpec((tm, k), lambda i, j: (jnp.minimum(i, head_tiles - 1), 0)),
                  pl.BlockSpec((tm, k), lambda i, j: (jnp.maximum(i - head_tiles, 0), 0)),
                  pl.BlockSpec((1, k), lambda i, j: (0, 0)),
                  pl.BlockSpec((1, 1, k), gmap), pl.BlockSpec((1, 1, k), gmap)] + wspecs,
        out_specs=pl.BlockSpec((tm, tn), lambda i, j: (i, j)),
        scratch_shapes=[pltpu.VMEM((tm, k), BF16)],
        compiler_params=_cparams(2),
        name="modnorm_mm",
    )(x, x_tail, nw.reshape(1, k), shift, scale, *([w] * len(wspecs)))


def _mm_res_kernel(a_ref, w_ref, res_ref, rest_ref, gate_ref, o_ref, *, head_tiles):
    res = jnp.where(pl.program_id(0) < head_tiles, res_ref[...], rest_ref[...])
    o_ref[...] = res + gate_ref[0] * jnp.dot(a_ref[...], w_ref[...], preferred_element_type=F32)


def _mm_res(a, w, res, gate, *, tm, group_of_tile, res_tail=None):
    m, k = a.shape
    n = w.shape[1]
    head_tiles = m // tm if res_tail is None else res.shape[0] // tm
    tail = res if res_tail is None else res_tail
    return pl.pallas_call(
        functools.partial(_mm_res_kernel, head_tiles=head_tiles),
        out_shape=jax.ShapeDtypeStruct((m, n), F32),
        grid=(m // tm,),
        in_specs=[pl.BlockSpec((tm, k), lambda i: (i, 0)), pl.BlockSpec((k, n), lambda i: (0, 0)),
                  pl.BlockSpec((tm, n), lambda i: (jnp.minimum(i, head_tiles - 1), 0)),
                  pl.BlockSpec((tm, n), lambda i: (jnp.maximum(i - head_tiles, 0), 0)),
                  pl.BlockSpec((1, 1, n), lambda i: (group_of_tile(i), 0, 0))],
        out_specs=pl.BlockSpec((tm, n), lambda i: (i, 0)),
        compiler_params=_cparams(1),
        name="mm_res",
    )(a, w, res, tail, gate)


def _rope(x, cos, sin_signed):
    lane = lax.broadcasted_iota(jnp.int32, x.shape, 1)
    half = RET_HD // 4
    swapped = jnp.where(_mod(lane, 2 * half) < half, pltpu.roll(x, LANES - half, 1), pltpu.roll(x, half, 1))
    return x * cos + swapped * sin_signed


def _ret_chunks(q, k, v, s0, log_g, n_forward):
    c = RET_CHUNK
    shape = (q.shape[0], c, c)
    row = lax.broadcasted_iota(jnp.int32, shape, 1)
    col = lax.broadcasted_iota(jnp.int32, shape, 2)
    fwd = lax.broadcasted_iota(jnp.int32, shape, 0) < n_forward
    dist = jnp.where(fwd, row - col, col - row).astype(F32)
    q_steps = jnp.where(fwd, row + 1, c - row).astype(F32)
    k_steps = jnp.where(fwd, c - 1 - row, row).astype(F32)
    dmat = jnp.where(dist >= 0, jnp.exp(log_g * jnp.maximum(dist, 0.0)), 0.0)
    q_dec = jnp.exp(log_g * q_steps)
    k_dec = jnp.exp(log_g * k_steps)
    scores = _bdot_nt(q, k) * dmat
    out = _bdot(scores, v) + _bdot(q * q_dec, s0)
    s1 = s0 * jnp.exp(log_g * float(c)) + _bdot_tn(k * k_dec, v)
    return out, s1


def _retention_kernel(qf_ref, kf_ref, vf_ref, cf_ref, sf_ref, qb_ref, kb_ref, vb_ref, cb_ref, sb_ref,
                      decf_ref, decb_ref, of_ref, ob_ref, st_ref):
    @pl.when(pl.program_id(1) == 0)
    def _():
        st_ref[...] = jnp.zeros_like(st_ref)

    h = RET_HEADS
    kscale = RET_HD ** -0.5

    def heads(ref, cos, sin, scale=None):
        out = []
        for hh in range(h):
            xh = ref[:, hh * RET_HD:(hh + 1) * RET_HD]
            if cos is not None:
                xh = _rope(xh, cos, sin)
            out.append(xh if scale is None else xh * scale)
        return out

    cf, sf, cb, sb = cf_ref[...], sf_ref[...], cb_ref[...], sb_ref[...]
    q = jnp.stack(heads(qf_ref, cf, sf) + heads(qb_ref, cb, sb), axis=0)
    k = jnp.stack(heads(kf_ref, cf, sf, kscale) + heads(kb_ref, cb, sb, kscale), axis=0)
    v = jnp.stack(heads(vf_ref, None, None) + heads(vb_ref, None, None), axis=0)
    log_g = -jnp.exp(jnp.concatenate([decf_ref[...], decb_ref[...]], axis=0))
    out, s1 = _ret_chunks(q, k, v, st_ref[...], log_g, h)
    st_ref[...] = s1
    of_ref[...] = jnp.concatenate([out[hh] for hh in range(h)], axis=1)
    ob_ref[...] = jnp.concatenate([out[h + hh] for hh in range(h)], axis=1)


def _retention(proj, cos, sin_signed, dec_f, dec_b, rows):
    c, h = RET_CHUNK, RET_HEADS
    n = (rows.seq + rows.ctx_len) // c
    blk_f, pos_f = rows.chunk_block(c, False)
    blk_b, pos_b = rows.chunk_block(c, True)

    def tok(blk, off):
        return pl.BlockSpec((c, RET_W), lambda bi, t: (blk(bi, t), off))

    def tab(pos):
        return pl.BlockSpec((c, RET_HD), lambda bi, t: (pos(t), 0))

    dec = lambda a: jnp.broadcast_to(a.astype(F32)[:, None, None], (h, 1, LANES))
    dspec = pl.BlockSpec((h, 1, LANES), lambda bi, t: (0, 0, 0))
    return pl.pallas_call(
        _retention_kernel,
        out_shape=[jax.ShapeDtypeStruct((rows.m, RET_W), F32)] * 2,
        grid=(rows.batch, n),
        in_specs=[tok(blk_f, 0), tok(blk_f, 1), tok(blk_f, 2), tab(pos_f), tab(pos_f),
                  tok(blk_b, 0), tok(blk_b, 1), tok(blk_b, 2), tab(pos_b), tab(pos_b), dspec, dspec],
        out_specs=[tok(blk_f, 0), tok(blk_b, 0)],
        scratch_shapes=[pltpu.VMEM((2 * h, RET_HD, RET_HD), F32)],
        compiler_params=_cparams(2),
        name="retention",
    )(proj, proj, proj, cos, sin_signed, proj, proj, proj, cos, sin_signed, dec(dec_f), dec(dec_b))


def _na_qk_norm(x, w):
    ms = _group_mean(x * x, NA_HD)
    return x * lax.rsqrt(ms + NORM_EPS) * w


def _softmax_pv(parts):
    m = functools.reduce(jnp.maximum, [jnp.max(s, axis=-1, keepdims=True) for s, _ in parts])
    ps = [jnp.exp(s - m) for s, _ in parts]
    den = functools.reduce(jnp.add, [jnp.sum(p, axis=-1, keepdims=True) for p in ps])
    num = functools.reduce(jnp.add, [_dot(p, v) for p, (_, v) in zip(ps, parts)])
    return num / den


def _na_kernel(q_ref, k_ref, v_ref, kc_ref, vc_ref, qn_ref, kn_ref, bias_ref, o_ref, qs_ref, ks_ref, vs_ref,
               kcs_ref, *, rows):
    scale = NA_HD ** -0.5
    qs_ref[...] = (_na_qk_norm(q_ref[...], qn_ref[...]) * scale).astype(BF16)
    ks_ref[...] = _na_qk_norm(k_ref[...], kn_ref[...]).astype(BF16)
    vs_ref[...] = v_ref[...].astype(BF16)
    kcs_ref[...] = _na_qk_norm(kc_ref[...], kn_ref[...]).astype(BF16)
    lane = lax.broadcasted_iota(jnp.int32, (GRID_W, LANES), 1)
    first = lane < NA_HD
    vc = vc_ref[...]
    kc = kcs_ref[...]

    nq = 2 * GRID_W

    def body(rb, carry):
        qs, kw, vw, bw = [], [], [], []
        for i in range(NA_ROWS_PER_STEP):
            r = rb * NA_ROWS_PER_STEP + i
            r0 = jnp.clip(r - NA_KR // 2, 0, rows - NA_KR)
            q_r = qs_ref[pl.ds(pl.multiple_of(r * GRID_W, GRID_W), GRID_W), :]
            zero = jnp.zeros_like(q_r)
            qs.append(jnp.concatenate([jnp.where(first, q_r, zero), jnp.where(first, zero, q_r)], axis=0))
            win = pl.ds(pl.multiple_of(r0 * GRID_W, GRID_W), NA_KR * GRID_W)
            kw.append(ks_ref[win, :])
            vw.append(vs_ref[win, :])
            bw.append(bias_ref[0, r0 - r + (NA_KR - 1)])
        q = jnp.stack(qs, axis=0)
        s_w = _bdot_nt(q, jnp.stack(kw, axis=0)) + jnp.stack(bw, axis=0)
        q_flat = q.reshape(NA_ROWS_PER_STEP * nq, LANES)
        s_c = _dot_nt(q_flat, kc).reshape(NA_ROWS_PER_STEP, nq, kc.shape[0])
        m = jnp.maximum(jnp.max(s_w, axis=-1, keepdims=True), jnp.max(s_c, axis=-1, keepdims=True))
        p_w = jnp.exp(s_w - m)
        p_c = jnp.exp(s_c - m)
        den = jnp.sum(p_w, axis=-1, keepdims=True) + jnp.sum(p_c, axis=-1, keepdims=True)
        num = _bdot(p_w, jnp.stack(vw, axis=0)) + _dot(p_c.reshape(NA_ROWS_PER_STEP * nq, kc.shape[0]), vc).reshape(
            NA_ROWS_PER_STEP, nq, LANES)
        out = num / den
        for i in range(NA_ROWS_PER_STEP):
            r = rb * NA_ROWS_PER_STEP + i
            o_ref[pl.ds(pl.multiple_of(r * GRID_W, GRID_W), GRID_W), :] = jnp.where(
                first, out[i, :GRID_W], out[i, GRID_W:])
        return carry

    lax.fori_loop(0, rows // NA_ROWS_PER_STEP, body, 0)


def _na_bias_table(rpb):
    cols = np.arange(GRID_W)
    start = np.clip(cols - NA_KC // 2, 0, GRID_W - NA_KC)
    kcol = np.arange(GRID_W)
    inside = (kcol[None, :] >= start[:, None]) & (kcol[None, :] < start[:, None] + NA_KC)
    col_off = kcol[None, :] - cols[:, None] + (NA_KC - 1)
    pick = np.zeros((2 * NA_KC - 1, GRID_W, GRID_W), np.float32)
    qi, ki = np.nonzero(inside)
    pick[col_off[qi, ki], qi, ki] = 1.0
    rows_qk = jnp.einsum("hro,oqk->hrqk", rpb, jnp.asarray(pick), precision=lax.Precision.HIGHEST)
    n_pairs = rpb.shape[0] // 2
    rows_qk = rows_qk.reshape(n_pairs, 2, rpb.shape[1], GRID_W, GRID_W)
    tab = jnp.stack([jnp.stack([rows_qk[:, :, cls + i] for i in range(NA_KR)], axis=3) for cls in range(NA_KR)],
                    axis=1)
    tab = jnp.where(inside[None, None, None, :, None, :], tab, NEG_BIG)
    return tab.reshape(n_pairs, NA_KR, 2 * GRID_W, NA_KR * GRID_W).astype(F32)


def _na_latent(proj, qn_w, kn_w, bias, rows, col0):
    seq, ctx_len = rows.seq, rows.ctx_len
    grid_rows = seq // GRID_W
    pairs = NA_W // LANES
    cb = col0 // LANES
    cblk = rows.ml // ctx_len
    tile2 = lambda a: jnp.tile(a.astype(F32), 2).reshape(1, LANES)
    lat = lambda off: pl.BlockSpec((seq, LANES), lambda bi, p: (bi, cb + off * pairs + p))
    ctx = lambda off: pl.BlockSpec((ctx_len, LANES), lambda bi, p: (cblk + bi, cb + off * pairs + p))
    return pl.pallas_call(
        functools.partial(_na_kernel, rows=grid_rows),
        out_shape=jax.ShapeDtypeStruct((rows.ml, NA_W), F32),
        grid=(rows.batch, pairs),
        in_specs=[lat(0), lat(1), lat(2), ctx(1), ctx(2),
                  pl.BlockSpec((1, LANES), lambda bi, p: (0, 0)), pl.BlockSpec((1, LANES), lambda bi, p: (0, 0)),
                  pl.BlockSpec((1, NA_KR, 2 * GRID_W, NA_KR * GRID_W), lambda bi, p: (p, 0, 0, 0))],
        out_specs=pl.BlockSpec((seq, LANES), lambda bi, p: (bi, p)),
        scratch_shapes=[pltpu.VMEM((seq, LANES), BF16)] * 3 + [pltpu.VMEM((ctx_len, LANES), BF16)],
        compiler_params=_cparams(2),
        name="na_latent",
    )(proj, proj, proj, proj, proj, tile2(qn_w), tile2(kn_w), bias)


def _ctx_attn_kernel(q_ref, k_ref, v_ref, qn_ref, kn_ref, o_ref):
    scale = NA_HD ** -0.5
    q = (_na_qk_norm(q_ref[...], qn_ref[...]) * scale).astype(BF16)
    k = _na_qk_norm(k_ref[...], kn_ref[...]).astype(BF16)
    v = v_ref[...]
    lane = lax.broadcasted_iota(jnp.int32, q.shape, 1)
    first = lane < NA_HD
    outs = []
    for hh in range(2):
        qh = jnp.where(first if hh == 0 else jnp.logical_not(first), q, jnp.zeros_like(q))
        outs.append(_softmax_pv([(_dot_nt(qh, k), v)]))
    o_ref[...] = jnp.where(first, outs[0], outs[1])


def _ctx_attention(proj, qn_w, kn_w, rows, col0):
    ctx_len = rows.ctx_len
    pairs = NA_W // LANES
    cb = col0 // LANES
    cblk = rows.ml // ctx_len
    tile2 = lambda a: jnp.tile(a.astype(F32), 2).reshape(1, LANES)
    blk = lambda off: pl.BlockSpec((ctx_len, LANES), lambda bi, p: (cblk + bi, cb + off * pairs + p))
    return pl.pallas_call(
        _ctx_attn_kernel,
        out_shape=jax.ShapeDtypeStruct((rows.mc, NA_W), F32),
        grid=(rows.batch, pairs),
        in_specs=[blk(0), blk(1), blk(2),
                  pl.BlockSpec((1, LANES), lambda bi, p: (0, 0)), pl.BlockSpec((1, LANES), lambda bi, p: (0, 0))],
        out_specs=pl.BlockSpec((ctx_len, LANES), lambda bi, p: (bi, p)),
        compiler_params=_cparams(2),
        name="ctx_attention",
    )(proj, proj, proj, tile2(qn_w), tile2(kn_w))


def _ret_readout_kernel(yf_ref, yb_ref, g_ref, nal_ref, nac_ref, gnw_ref, o_ref, *, n_latent_tiles):
    y = yf_ref[...] + yb_ref[...]
    for hh in range(RET_HEADS):
        sl = slice(hh * RET_HD, (hh + 1) * RET_HD)
        yh = y[:, sl]
        yc = yh - jnp.mean(yh, axis=-1, keepdims=True)
        yn = yc * lax.rsqrt(jnp.mean(yc * yc, axis=-1, keepdims=True) + RET_GN_EPS)
        o_ref[:, sl] = (yn * gnw_ref[:, sl] * _silu(g_ref[:, sl])).astype(o_ref.dtype)
    is_latent = pl.program_id(0) < n_latent_tiles
    o_ref[:, RET_W:] = jnp.where(is_latent, nal_ref[...], nac_ref[...]).astype(o_ref.dtype)


def _ret_readout(yf, yb, proj, na_l, na_c, gn_w, rows, tm):
    nl = rows.ml // tm
    row = lambda i: (i, 0)
    return pl.pallas_call(
        functools.partial(_ret_readout_kernel, n_latent_tiles=nl),
        out_shape=jax.ShapeDtypeStruct((rows.m, RET_W + NA_W), BF16),
        grid=(rows.m // tm,),
        in_specs=[pl.BlockSpec((tm, RET_W), row), pl.BlockSpec((tm, RET_W), row),
                  pl.BlockSpec((tm, RET_W), lambda i: (i, 3)),
                  pl.BlockSpec((tm, NA_W), lambda i: (jnp.minimum(i, nl - 1), 0)),
                  pl.BlockSpec((tm, NA_W), lambda i: (jnp.maximum(i - nl, 0), 0)),
                  pl.BlockSpec((1, RET_W), lambda i: (0, 0))],
        out_specs=pl.BlockSpec((tm, RET_W + NA_W), row),
        compiler_params=_cparams(1),
        name="ret_readout",
    )(yf, yb, proj, na_l, na_c, gn_w.reshape(1, RET_W).astype(F32))


def _rwkv_mix_kernel(x_ref, xp_ref, xn_ref, nw_ref, sh_ref, sc_ref, mu_ref, wrkv_ref, w1_ref, a1_ref, g1_ref,
                     r_ref, k_ref, v_ref, lwl_ref, al_ref, gl_ref, *, seq, ctx_len, ml, tm):
    nw, sh, sc = nw_ref[...], sh_ref[0], sc_ref[0]
    h = _modnorm(x_ref[...], nw, sh, sc)
    hp = _modnorm(xp_ref[...], nw, sh, sc)[7:8]
    hn = _modnorm(xn_ref[...], nw, sh, sc)[0:1]
    row = lax.broadcasted_iota(jnp.int32, h.shape, 0)
    g = pl.program_id(0) * tm + row
    latent = g < ml
    pos = jnp.where(latent, _mod(g, seq), _mod(g, ctx_len))
    first = pos == 0
    last = pos == jnp.where(latent, seq - 1, ctx_len - 1)
    up = jnp.where(row == 0, hp, pltpu.roll(h, 1, 0))
    up = jnp.where(first, 0.0, up)
    dn = jnp.where(row == tm - 1, hn, pltpu.roll(h, tm - 1, 0))
    dn = jnp.where(last, 0.0, dn)
    xx = 0.5 * (up + dn) - h
    mix = lambda s_i: (h + xx * mu_ref[s_i:s_i + 1, :]).astype(BF16)
    for s_i, o_ref in enumerate((r_ref, k_ref, v_ref)):
        o_ref[...] = jnp.dot(mix(s_i), wrkv_ref[s_i], preferred_element_type=F32)
    lwl_ref[...] = jnp.tanh(jnp.dot(mix(3), w1_ref[...], preferred_element_type=F32))
    al_ref[...] = jnp.dot(mix(4), a1_ref[...], preferred_element_type=F32)
    gl_ref[...] = _sigmoid(jnp.dot(mix(5), g1_ref[...], preferred_element_type=F32))


def _rwkv_mix(x, nw, shift, scale, mu, w_rkv, p, *, rows, tm):
    m, d = x.shape
    n_slab = m // 8
    per = tm // 8
    gmap = lambda i: (rows.group_of_tile(tm)(i), 0, 0)
    row = lambda i: (i, 0)
    full2 = lambda i: (0, 0)
    smalls = [p["w1"], p["a1"], p["g1"]]
    return pl.pallas_call(
        functools.partial(_rwkv_mix_kernel, seq=rows.seq, ctx_len=rows.ctx_len, ml=rows.ml, tm=tm),
        out_shape=[jax.ShapeDtypeStruct((m, d), F32)] * 3
        + [jax.ShapeDtypeStruct((m, w.shape[1]), F32) for w in smalls],
        grid=(m // tm,),
        in_specs=[pl.BlockSpec((tm, d), row),
                  pl.BlockSpec((8, d), lambda i: (jnp.maximum(i * per - 1, 0), 0)),
                  pl.BlockSpec((8, d), lambda i: (jnp.minimum((i + 1) * per, n_slab - 1), 0)),
                  pl.BlockSpec((1, d), full2),
                  pl.BlockSpec((1, 1, d), gmap), pl.BlockSpec((1, 1, d), gmap),
                  pl.BlockSpec((6, d), full2),
                  pl.BlockSpec((3, d, d), lambda i: (0, 0, 0))]
        + [pl.BlockSpec(w.shape, full2) for w in smalls],
        out_specs=[pl.BlockSpec((tm, d), row)] * 3 + [pl.BlockSpec((tm, w.shape[1]), row) for w in smalls],
        compiler_params=_cparams(1),
        name="rwkv_mix",
    )(x, x, x, nw.reshape(1, d), shift, scale, mu, w_rkv, *smalls)


def _rwkv_post_kernel(r_ref, k_ref, v_ref, lwl_ref, al_ref, gl_ref, w2_ref, a2_ref, g2_ref,
                      w0_ref, a0_ref, kk_ref, ka_ref, rk_ref,
                      rt0, kt0, bt0, at0, wc0, rt1, kt1, bt1, at1, wc1, vb_ref, g_ref, bonus_ref, *, tm):
    c = RWKV_CHUNK
    n_pairs = r_ref.shape[1] // LANES
    r, k, v = r_ref[...], k_ref[...], v_ref[...]
    lw_low, a_low = lwl_ref[...], al_ref[...]
    g_ref[...] = _dot(gl_ref[...], g2_ref[...]).astype(g_ref.dtype)

    def put(o_ref, val):
        for p in range(n_pairs):
            o_ref[p] = val[:, p * LANES:(p + 1) * LANES].astype(o_ref.dtype)

    put(vb_ref, v)
    kk = k * kk_ref[...]
    nrm = jnp.sqrt(_group_mean(kk * kk, RWKV_HD) * float(RWKV_HD))
    kk = kk / jnp.maximum(nrm, 1e-12)

    tr = lax.broadcasted_iota(jnp.int32, (c, c), 0)
    tc = lax.broadcasted_iota(jnp.int32, (c, c), 1)
    tri = [(tr >= tc).astype(BF16), (tr <= tc).astype(BF16)]
    lora = w2_ref.shape[1]
    coeff_src = jnp.zeros_like(r)
    outs = [(rt0, kt0, bt0, at0, wc0), (rt1, kt1, bt1, at1, wc1)]
    for z in range(2):
        rt_ref, kt_ref, bt_ref, at_ref, wc_ref = outs[z]
        w_lora = _dot(lw_low[:, z * lora:(z + 1) * lora], w2_ref[z])
        lw = -float(np.exp(-0.5)) * _sigmoid(w0_ref[z:z + 1, :] + w_lora)
        a_z = _sigmoid(a0_ref[z:z + 1, :] + _dot(a_low[:, z * lora:(z + 1) * lora], a2_ref[z]))
        k_dir = k * (1.0 + (a_z - 1.0) * ka_ref[...])
        coeff_src = coeff_src + r * k_dir * rk_ref[...]
        cum = jnp.concatenate([_dot_exact_lhs(tri[z], lw[ci * c:(ci + 1) * c]) for ci in range(tm // c)], axis=0)
        e_pos = jnp.exp(cum)
        e_neg = 1.0 / e_pos
        put(rt_ref, r * e_pos)
        put(kt_ref, k_dir * e_neg)
        put(bt_ref, kk * a_z * e_neg)
        put(at_ref, -kk * jnp.exp(cum - lw))
        for ci in range(tm // c):
            end = ci * c + (c - 1 if z == 0 else 0)
            for p in range(n_pairs):
                wc_ref[p, ci] = e_pos[end:end + 1, p * LANES:(p + 1) * LANES]
    coeff = _group_mean(coeff_src, RWKV_HD) * float(RWKV_HD)
    bonus_ref[...] = coeff * v


def _rwkv_post(r, k, v, lw_low, a_low, g_low, p, *, tm):
    m, d = r.shape
    c = RWKV_CHUNK
    lora = p["w2"].shape[1]
    glora = p["g1"].shape[1]
    small = lambda a: pl.BlockSpec((tm, a.shape[1]), lambda i: (i, 0))
    n_pairs = d // LANES
    pair = jax.ShapeDtypeStruct((n_pairs, m, LANES), BF16)
    chunk = jax.ShapeDtypeStruct((n_pairs, m // c, 1, LANES), F32)
    row = lambda i: (i, 0)
    full2 = lambda i: (0, 0)
    full3 = lambda i: (0, 0, 0)
    tspec = pl.BlockSpec((tm, d), row)
    pspec = pl.BlockSpec((n_pairs, tm, LANES), lambda i: (0, i, 0))
    cspec = pl.BlockSpec((n_pairs, tm // c, 1, LANES), lambda i: (0, i, 0, 0))
    return pl.pallas_call(
        functools.partial(_rwkv_post_kernel, tm=tm),
        out_shape=[pair] * 4 + [chunk] + [pair] * 4 + [chunk]
        + [pair, jax.ShapeDtypeStruct((m, d), BF16), jax.ShapeDtypeStruct((m, d), F32)],
        grid=(m // tm,),
        in_specs=[tspec] * 3 + [small(lw_low), small(a_low), small(g_low)]
        + [pl.BlockSpec((2, lora, d), full3), pl.BlockSpec((2, lora, d), full3), pl.BlockSpec((glora, d), full2),
           pl.BlockSpec((2, d), full2), pl.BlockSpec((2, d), full2)]
        + [pl.BlockSpec((1, d), full2)] * 3,
        out_specs=[pspec] * 4 + [cspec] + [pspec] * 4 + [cspec] + [pspec, tspec, tspec],
        compiler_params=_cparams(1),
        name="rwkv_post",
    )(r, k, v, lw_low, a_low, g_low, p["w2"], p["a2"], p["g2"], p["w0"], p["a0"], p["k_k"], p["k_a"], p["r_k"])


def _bdot(a, b):
    return jnp.einsum("ucd,ude->uce", a.astype(BF16), b.astype(BF16), preferred_element_type=F32)


def _bdot_nt(a, b):
    return jnp.einsum("ucd,usd->ucs", a.astype(BF16), b.astype(BF16), preferred_element_type=F32)


def _bdot_tn(a, b):
    return jnp.einsum("uce,ucd->ued", a.astype(BF16), b.astype(BF16), preferred_element_type=F32)


def _scan_chunks(rt, kt, bt, at, v, wc, s0, n_forward):
    c = RWKV_CHUNK
    n_units = rt.shape[0]
    lane = lax.broadcasted_iota(jnp.int32, (n_units, c, LANES), 2)
    row = lax.broadcasted_iota(jnp.int32, (n_units, c, LANES), 1)
    unit = lax.broadcasted_iota(jnp.int32, (n_units, c, LANES), 0)
    head0 = lane < RWKV_HD
    src = _mod(lane, RWKV_HD)
    ahead = jnp.where(unit < n_forward, row - src, src - row)
    strict = ahead > 0
    incl = ahead >= 0

    def dup(x):
        zero = jnp.zeros_like(x)
        h0 = head0[:, :x.shape[1]]
        return jnp.concatenate([jnp.where(h0, x, zero), jnp.where(h0, zero, x)], axis=1)

    ar = jnp.concatenate([at, rt], axis=1)
    mb = _bdot_nt(ar, dup(bt))
    mk = _bdot_nt(ar, dup(kt))
    p_ab = jnp.where(strict, mb[:, :c], 0.0)
    p_rb = jnp.where(incl, mb[:, c:], 0.0)
    p_ak = jnp.where(strict, mk[:, :c], 0.0)
    p_rk = jnp.where(incl, mk[:, c:], 0.0)
    vd = dup(v)
    rhs = _bdot_nt(at, s0) + _bdot(p_ak, vd)

    power = p_ab.astype(BF16)
    inv = jnp.where(row == src, 1.0, 0.0) + p_ab
    for _ in range(int(np.log2(c)) - 1):
        power = _bdot(power, dup(power)).astype(BF16)
        inv = inv + _bdot(inv, dup(power))
    inv = inv.astype(BF16)
    u = _bdot(inv, dup(rhs))
    resid = (rhs - u) + _bdot(p_ab, dup(u))
    u = u + _bdot(inv, dup(resid))

    y = _bdot_nt(rt, s0) + _bdot(jnp.concatenate([p_rb, p_rk], axis=2), jnp.concatenate([dup(u), vd], axis=1))
    upd = _bdot_tn(jnp.concatenate([u.astype(BF16), v], axis=1), jnp.concatenate([bt, kt], axis=1))
    er = lax.broadcasted_iota(jnp.int32, (1, LANES, LANES), 1) < RWKV_HD
    ec = lax.broadcasted_iota(jnp.int32, (1, LANES, LANES), 2) < RWKV_HD
    s1 = (s0 + jnp.where(er == ec, upd, 0.0)) * wc
    return y, s1


def _rwkv_scan_kernel(rtf, ktf, btf, atf, vf, wcf, rtb, ktb, btb, atb, vb, wcb, yf_ref, yb_ref, s_ref):
    @pl.when(pl.program_id(1) == 0)
    def _():
        s_ref[...] = jnp.zeros_like(s_ref)

    n_pairs = rtf.shape[0]
    both = lambda f, b: jnp.concatenate([f[...], b[...]], axis=0)
    wc = jnp.concatenate([wcf[:, 0], wcb[:, 0]], axis=0)
    y, s1 = _scan_chunks(both(rtf, rtb), both(ktf, ktb), both(btf, btb), both(atf, atb), both(vf, vb), wc,
                         s_ref[...], n_pairs)
    s_ref[...] = s1
    yf_ref[...] = y[:n_pairs]
    yb_ref[...] = y[n_pairs:]


def _rwkv_scan(feats_f, feats_b, v, rows):
    n_pairs, m, _ = v.shape
    c = RWKV_CHUNK
    d = n_pairs * LANES
    nchunks = (rows.seq + rows.ctx_len) // c
    blk_f, _ = rows.chunk_block(c, False)
    blk_b, _ = rows.chunk_block(c, True)

    def specs(blk):
        tok = pl.BlockSpec((n_pairs, c, LANES), lambda bi, t: (0, blk(bi, t), 0))
        return tok, pl.BlockSpec((n_pairs, 1, 1, LANES), lambda bi, t: (0, blk(bi, t), 0, 0))

    tok_f, wc_f = specs(blk_f)
    tok_b, wc_b = specs(blk_b)
    return pl.pallas_call(
        _rwkv_scan_kernel,
        out_shape=[jax.ShapeDtypeStruct((n_pairs, m, LANES), F32)] * 2,
        grid=(rows.batch, nchunks),
        in_specs=[tok_f] * 5 + [wc_f] + [tok_b] * 5 + [wc_b],
        out_specs=[tok_f, tok_b],
        scratch_shapes=[pltpu.VMEM((2 * n_pairs, LANES, LANES), F32)],
        compiler_params=_cparams(2),
        name="rwkv_scan",
    )(*feats_f[:4], v, feats_f[4], *feats_b[:4], v, feats_b[4])


def _rwkv_readout_kernel(yf_ref, yb_ref, bonus_ref, g_ref, lnw_ref, lnb_ref, o_ref):
    y = jnp.concatenate([yf_ref[p] + yb_ref[p] for p in range(yf_ref.shape[0])], axis=1)
    yc = y - _group_mean(y, RWKV_HD)
    yn = yc * lax.rsqrt(_group_mean(yc * yc, RWKV_HD) + RWKV_GN_EPS) * lnw_ref[...] + lnb_ref[...]
    o_ref[...] = ((yn + bonus_ref[...]) * g_ref[...].astype(F32)).astype(o_ref.dtype)


def _rwkv_readout(yf, yb, bonus, g, ln_w, ln_b, rows, tm):
    n_pairs = yf.shape[0]
    d = n_pairs * LANES
    pspec = pl.BlockSpec((n_pairs, tm, LANES), lambda i: (0, i, 0))
    tspec = pl.BlockSpec((tm, d), lambda i: (i, 0))
    vspec = pl.BlockSpec((1, d), lambda i: (0, 0))
    return pl.pallas_call(
        _rwkv_readout_kernel,
        out_shape=jax.ShapeDtypeStruct((rows.ml, d), BF16),
        grid=(rows.ml // tm,),
        in_specs=[pspec, pspec, tspec, tspec, vspec, vspec],
        out_specs=tspec,
        compiler_params=_cparams(1),
        name="rwkv_readout",
    )(yf, yb, bonus, g, ln_w.reshape(1, d).astype(F32), ln_b.reshape(1, d).astype(F32))


def _router_kernel(x_ref, nw_ref, sh_ref, sc_ref, w_ref, o_ref, *, n_experts):
    h = _modnorm(x_ref[...], nw_ref[...], sh_ref[0], sc_ref[0])
    logits = _dot_hi(h, w_ref[...])
    lane = lax.broadcasted_iota(jnp.int32, logits.shape, 1).astype(F32)
    logits = jnp.where(lane < n_experts, logits, -jnp.inf)
    v1 = jnp.max(logits, axis=-1, keepdims=True)
    i1 = jnp.min(jnp.where(logits == v1, lane, float(LANES)), axis=-1, keepdims=True)
    rest = jnp.where(lane == i1, -jnp.inf, logits)
    v2 = jnp.max(rest, axis=-1, keepdims=True)
    i2 = jnp.min(jnp.where(rest == v2, lane, float(LANES)), axis=-1, keepdims=True)
    e2 = jnp.exp(v2 - v1)
    den = 1.0 + e2
    o_ref[...] = jnp.where(lane == i1, 1.0 / den, 0.0) + jnp.where(lane == i2, e2 / den, 0.0)


def _router(x, nw, shift, scale, router, *, tm, group_of_tile):
    m, d = x.shape
    n_experts = router.shape[1]
    w = jnp.zeros((d, LANES), F32).at[:, :n_experts].set(router.astype(F32))
    gmap = lambda i: (group_of_tile(i), 0, 0)
    return pl.pallas_call(
        functools.partial(_router_kernel, n_experts=n_experts),
        out_shape=jax.ShapeDtypeStruct((m, LANES), F32),
        grid=(m // tm,),
        in_specs=[pl.BlockSpec((tm, d), lambda i: (i, 0)), pl.BlockSpec((1, d), lambda i: (0, 0)),
                  pl.BlockSpec((1, 1, d), gmap), pl.BlockSpec((1, 1, d), gmap),
                  pl.BlockSpec((d, LANES), lambda i: (0, 0))],
        out_specs=pl.BlockSpec((tm, LANES), lambda i: (i, 0)),
        compiler_params=_cparams(1),
        name="router",
    )(x, nw.reshape(1, d), shift, scale, w)


MOE_TILE = 512
DMA_UNROLL = 8


def _route_plan(gates, n_experts, tg):
    m = gates.shape[0]
    g = gates[:, :n_experts]
    sel = g > 0.0
    seli = sel.astype(jnp.int32)
    slot = jnp.cumsum(seli, axis=1) - 1
    rank = jnp.cumsum(seli, axis=0) - 1
    counts = jnp.sum(seli, axis=0)
    padded = ((counts + tg - 1) // tg) * tg
    ends = jnp.cumsum(padded)
    dest = (ends - padded)[None, :] + rank
    n_rows = TOP_K * m + n_experts * tg
    n_tiles = n_rows // tg
    first = sel & (slot == 0)
    second = sel & (slot == 1)
    pick = lambda msk, val: jnp.sum(jnp.where(msk, val, 0), axis=1)
    d1 = pick(first, dest)
    has2 = jnp.any(second, axis=1)
    d2 = pick(second, dest)
    w1 = pick(first, g)
    w2 = pick(second, g)
    tile_expert = jnp.minimum(jnp.sum((jnp.arange(n_tiles, dtype=jnp.int32) * tg)[:, None] >= ends[None, :], axis=1),
                              n_experts - 1).astype(jnp.int32)
    n_valid = (ends[-1] // tg).astype(jnp.int32).reshape(1)
    lane = jnp.arange(LANES)[None, :]
    wts = jnp.where(lane == 0, w1[:, None], jnp.where(lane == 1, w2[:, None], 0.0)).astype(F32)
    i32 = lambda a: a.astype(jnp.int32)
    return (n_rows, tile_expert, n_valid, i32(d1), i32(jnp.where(has2, d2, n_rows)), i32(jnp.where(has2, d2, d1)),
            wts)


def _slab_rows(ref3, val):
    for s in range(ref3.shape[1]):
        ref3[:, s, :] = val[:, s * LANES:(s + 1) * LANES].astype(ref3.dtype)


def _unslab_rows(ref3):
    return jnp.concatenate([ref3[:, s, :] for s in range(ref3.shape[1])], axis=1)


def _moe_scatter_kernel(d1_ref, d2_ref, x_ref, nw_ref, sh_ref, sc_ref, a0_hbm, a_hbm, h_ref, sem, *, tm):
    del a0_hbm
    i = pl.program_id(0)
    slot = lax.rem(i, 2)
    _slab_rows(h_ref.at[slot], _modnorm(x_ref[...], nw_ref[...], sh_ref[0], sc_ref[0]))

    def row_copy(s_i, r, dst):
        return pltpu.make_async_copy(h_ref.at[s_i, pl.ds(r, 1)], a_hbm.at[pl.ds(dst, 1)], sem.at[s_i])

    def issue(r, carry):
        row_copy(slot, r, d1_ref[0, 0, r]).start(priority=0)
        row_copy(slot, r, d2_ref[0, 0, r]).start(priority=1)
        return carry

    def drain_slot(s_i):
        def drain(r, carry):
            row_copy(s_i, r, 0).wait()
            row_copy(s_i, r, 0).wait()
            return carry

        lax.fori_loop(0, tm, drain, 0, unroll=DMA_UNROLL)

    lax.fori_loop(0, tm, issue, 0, unroll=DMA_UNROLL)

    @pl.when(i > 0)
    def _():
        drain_slot(1 - slot)

    @pl.when(i == pl.num_programs(0) - 1)
    def _():
        drain_slot(slot)


def _moe_scatter(x, nw, shift, scale, d1, d2, n_rows, *, tm, tg, group_of_tile):
    m, d = x.shape
    slabs = d // LANES
    idx = lambda a: a.reshape(m // tm, 1, tm)
    ispec = pl.BlockSpec((1, 1, tm), lambda i: (i, 0, 0), memory_space=pltpu.SMEM)
    gmap = lambda i: (group_of_tile(i), 0, 0)
    zeros = jnp.zeros((n_rows + tg, slabs, LANES), F32)
    return pl.pallas_call(
        functools.partial(_moe_scatter_kernel, tm=tm),
        out_shape=jax.ShapeDtypeStruct(zeros.shape, F32),
        grid=(m // tm,),
        in_specs=[ispec, ispec, pl.BlockSpec((tm, d), lambda i: (i, 0)), pl.BlockSpec((1, d), lambda i: (0, 0)),
                  pl.BlockSpec((1, 1, d), gmap), pl.BlockSpec((1, 1, d), gmap), pl.BlockSpec(memory_space=pl.ANY)],
        out_specs=pl.BlockSpec(memory_space=pl.ANY),
        scratch_shapes=[pltpu.VMEM((2, tm, slabs, LANES), F32), pltpu.SemaphoreType.DMA((2,))],
        input_output_aliases={6: 0},
        compiler_params=_cparams(1),
        name="moe_scatter",
    )(idx(d1), idx(d2), x, nw.reshape(1, d), shift, scale, zeros)


def _grouped_swiglu_kernel(te_ref, nv_ref, a_ref, wg_ref, wu_ref, o_ref, ab_ref):
    t = pl.program_id(0)

    @pl.when(jnp.logical_and(t < nv_ref[0], pl.program_id(1) == 0))
    def _():
        ab_ref[...] = _unslab_rows(a_ref).astype(BF16)

    @pl.when(t < nv_ref[0])
    def _():
        a = ab_ref[...]
        gate = jnp.dot(a, wg_ref[0], preferred_element_type=F32)
        up = jnp.dot(a, wu_ref[0], preferred_element_type=F32)
        o_ref[...] = (_silu(gate) * up).astype(o_ref.dtype)

    @pl.when(t >= nv_ref[0])
    def _():
        o_ref[...] = jnp.zeros_like(o_ref)


def _grouped_swiglu(a, w13, tile_expert, n_valid, p_rows, *, tg, tn):
    slabs = a.shape[1]
    k = slabs * LANES
    e_hid = w13.shape[2] // 2
    nj = e_hid // tn
    tile = lambda t, nv: jnp.minimum(t, nv[0] - 1)
    return pl.pallas_call(
        _grouped_swiglu_kernel,
        out_shape=jax.ShapeDtypeStruct((p_rows, e_hid), BF16),
        grid_spec=pltpu.PrefetchScalarGridSpec(
            num_scalar_prefetch=2,
            grid=(p_rows // tg, nj),
            in_specs=[pl.BlockSpec((tg, slabs, LANES), lambda t, j, te, nv: (tile(t, nv), 0, 0)),
                      pl.BlockSpec((1, k, tn), lambda t, j, te, nv: (te[tile(t, nv)], 0, j)),
                      pl.BlockSpec((1, k, tn), lambda t, j, te, nv: (te[tile(t, nv)], 0, j + nj))],
            out_specs=pl.BlockSpec((tg, tn), lambda t, j, te, nv: (t, j)),
            scratch_shapes=[pltpu.VMEM((tg, k), BF16)],
        ),
        compiler_params=_cparams(2),
        name="moe_up",
    )(tile_expert, n_valid, a, w13, w13)


def _grouped_mm_kernel(te_ref, nv_ref, a_ref, w_ref, o_ref, wb_ref):
    t = pl.program_id(0)
    valid = t < nv_ref[0]
    new_expert = jnp.logical_or(t == 0, te_ref[t] != te_ref[jnp.maximum(t - 1, 0)])

    @pl.when(jnp.logical_and(valid, new_expert))
    def _():
        wb_ref[...] = w_ref[0].astype(BF16)

    @pl.when(valid)
    def _():
        _slab_rows(o_ref, jnp.dot(a_ref[...], wb_ref[...], preferred_element_type=F32))

    @pl.when(jnp.logical_not(valid))
    def _():
        o_ref[...] = jnp.zeros_like(o_ref)


def _grouped_mm(a, w, tile_expert, n_valid, *, tg):
    p_rows, k = a.shape
    n = w.shape[2]
    slabs = n // LANES
    tile = lambda t, nv: jnp.minimum(t, nv[0] - 1)
    return pl.pallas_call(
        _grouped_mm_kernel,
        out_shape=jax.ShapeDtypeStruct((p_rows, slabs, LANES), F32),
        grid_spec=pltpu.PrefetchScalarGridSpec(
            num_scalar_prefetch=2,
            grid=(p_rows // tg,),
            in_specs=[pl.BlockSpec((tg, k), lambda t, te, nv: (tile(t, nv), 0)),
                      pl.BlockSpec((1, k, n), lambda t, te, nv: (te[tile(t, nv)], 0, 0))],
            out_specs=pl.BlockSpec((tg, slabs, LANES), lambda t, te, nv: (t, 0, 0)),
            scratch_shapes=[pltpu.VMEM((k, n), BF16)],
        ),
        compiler_params=_cparams(1),
        name="moe_down",
    )(tile_expert, n_valid, a, w)


def _combine_kernel(d1_ref, d2_ref, n1_ref, n2_ref, y_hbm, wts_ref, x_ref, gate_ref, o_ref, buf_ref, sem, *, tm):
    i = pl.program_id(0)
    n_steps = pl.num_programs(0)
    slot = lax.rem(i, 2)

    def row_copy(s_i, which, r, src):
        return pltpu.make_async_copy(y_hbm.at[pl.ds(src, 1)], buf_ref.at[s_i, which, pl.ds(r, 1)], sem.at[s_i])

    def fetch(s_i, first_ref, second_ref):
        def issue(r, carry):
            row_copy(s_i, 0, r, first_ref[0, 0, r]).start(priority=0)
            row_copy(s_i, 1, r, second_ref[0, 0, r]).start(priority=1)
            return carry

        lax.fori_loop(0, tm, issue, 0, unroll=DMA_UNROLL)

    @pl.when(i == 0)
    def _():
        fetch(0, d1_ref, d2_ref)

    @pl.when(i + 1 < n_steps)
    def _():
        fetch(1 - slot, n1_ref, n2_ref)

    def drain(r, carry):
        row_copy(slot, 0, r, 0).wait()
        row_copy(slot, 1, r, 0).wait()
        return carry

    lax.fori_loop(0, tm, drain, 0, unroll=DMA_UNROLL)
    lane = lax.broadcasted_iota(jnp.int32, wts_ref.shape, 1)
    w1 = jnp.sum(jnp.where(lane == 0, wts_ref[...], 0.0), axis=-1, keepdims=True)
    w2 = jnp.sum(jnp.where(lane == 1, wts_ref[...], 0.0), axis=-1, keepdims=True)
    y = w1 * _unslab_rows(buf_ref.at[slot, 0]) + w2 * _unslab_rows(buf_ref.at[slot, 1])
    o_ref[...] = x_ref[...] + gate_ref[0] * y


def _combine(y, d1, d2, wts, x, gate, *, tm, group_of_tile):
    m, d = x.shape
    n_steps = m // tm
    idx = lambda a: a.reshape(n_steps, 1, tm)
    ispec = pl.BlockSpec((1, 1, tm), lambda i: (i, 0, 0), memory_space=pltpu.SMEM)
    nspec = pl.BlockSpec((1, 1, tm), lambda i: (jnp.minimum(i + 1, n_steps - 1), 0, 0), memory_space=pltpu.SMEM)
    return pl.pallas_call(
        functools.partial(_combine_kernel, tm=tm),
        out_shape=jax.ShapeDtypeStruct((m, d), F32),
        grid=(n_steps,),
        in_specs=[ispec, ispec, nspec, nspec, pl.BlockSpec(memory_space=pl.ANY),
                  pl.BlockSpec((tm, LANES), lambda i: (i, 0)), pl.BlockSpec((tm, d), lambda i: (i, 0)),
                  pl.BlockSpec((1, 1, d), lambda i: (group_of_tile(i), 0, 0))],
        out_specs=pl.BlockSpec((tm, d), lambda i: (i, 0)),
        scratch_shapes=[pltpu.VMEM((2, 2, tm, d // LANES, LANES), F32), pltpu.SemaphoreType.DMA((2,))],
        compiler_params=_cparams(1),
        name="moe_combine",
    )(idx(d1), idx(d2), idx(d1), idx(d2), y, wts, x, gate)


def _rope_tables(seq, ctx_len):
    half = RET_HD // 4
    freqs = ROPE_BASE ** (-np.arange(half, dtype=np.float64) / half)
    t = np.arange(seq)
    ang_r = (t // GRID_W)[:, None] * freqs[None, :]
    ang_c = (t % GRID_W)[:, None] * freqs[None, :]
    cos = np.concatenate([np.cos(ang_r)] * 2 + [np.cos(ang_c)] * 2, axis=1)
    sin = np.concatenate([-np.sin(ang_r), np.sin(ang_r), -np.sin(ang_c), np.sin(ang_c)], axis=1)
    cos = np.concatenate([np.ones((ctx_len, RET_HD)), cos], axis=0)
    sin = np.concatenate([np.zeros((ctx_len, RET_HD)), sin], axis=0)
    return jnp.asarray(cos, F32), jnp.asarray(sin, F32)


def _half_or_full(n):
    return n // 2 if (n // 2) % LANES == 0 else n


def kernel(x, c, ctx, c_ctx, ada_w, ada_b, norm_mix_w, norm_ffn_w, ev_w_in, ev_ret_decay_f, ev_ret_decay_b,
           ev_ret_gn_w, ev_na_qn_w, ev_na_kn_w, ev_na_rpb, ev_w_out, ev_ffn_w13, ev_ffn_w2, od_mu, od_w_rkv,
           od_w0, od_w1, od_w2, od_a0, od_a1, od_a2, od_g1, od_g2, od_k_k, od_k_a, od_r_k, od_ln_w, od_ln_b,
           od_w_o, od_router, od_moe_w13, od_moe_w2):
    batch, seq, d = x.shape
    ctx_len = ctx.shape[1]
    rows = _Rows(batch, seq, ctx_len)
    tm = 8 * RWKV_CHUNK
    grp = rows.group_of_tile(tm)
    tm_big = 2 * tm if (rows.mc % (2 * tm) == 0 and seq % (2 * tm) == 0) else tm
    grp_big = rows.group_of_tile(tm_big)
    x_rows, ctx_rows = x.reshape(rows.ml, d), ctx.reshape(rows.mc, d)

    n_mod = ((batch + 1 + 7) // 8) * 8
    c_rows = jnp.zeros((n_mod, d), F32).at[:batch].set(c).at[batch].set(c_ctx)
    mods = _ada(c_rows, ada_w, ada_b)
    mods = mods.reshape(mods.shape[0], 6, n_mod, 1, d)

    mod = mods[0]
    w_in = ev_w_in[0].astype(BF16)
    proj = _modnorm_mm(x_rows, norm_mix_w[0], mod[0], mod[1], w_in, group_of_tile=grp_big, tm=tm_big,
                       tn=_half_or_full(w_in.shape[1]), out_dtype=F32, x_tail=ctx_rows)
    cos, sin = _rope_tables(seq, ctx_len)
    y_f, y_b = _retention(proj, cos, sin, ev_ret_decay_f[0], ev_ret_decay_b[0], rows)
    bias = _na_bias_table(ev_na_rpb[0].astype(F32))
    na_l = _na_latent(proj, ev_na_qn_w[0], ev_na_kn_w[0], bias, rows, 4 * RET_W)
    na_c = _ctx_attention(proj, ev_na_qn_w[0], ev_na_kn_w[0], rows, 4 * RET_W)
    mix = _ret_readout(y_f, y_b, proj, na_l, na_c, ev_ret_gn_w[0], rows, tm)
    w_out = ev_w_out[0].astype(BF16)
    xa = _mm_res(mix, w_out, x_rows, mod[2], tm=tm, group_of_tile=grp, res_tail=ctx_rows)

    w13 = ev_ffn_w13[0].astype(BF16)
    w2 = ev_ffn_w2[0].astype(BF16)
    hid = w2.shape[0]
    hidden = _modnorm_mm(xa, norm_ffn_w[0], mod[3], mod[4], w13, group_of_tile=grp_big, tm=tm_big,
                         tn=_half_or_full(hid), out_dtype=BF16, swiglu=True)
    xa = _mm_res(hidden, w2, xa, mod[5], tm=tm, group_of_tile=grp)

    mod = mods[1]
    glora = od_g1.shape[-1]
    glora_p = ((glora + LANES - 1) // LANES) * LANES
    p = {
        "w1": jnp.concatenate([od_w1[0, 0], od_w1[0, 1]], axis=1).astype(BF16),
        "w2": od_w2[0].astype(BF16),
        "a1": jnp.concatenate([od_a1[0, 0], od_a1[0, 1]], axis=1).astype(BF16),
        "a2": od_a2[0].astype(BF16),
        "g1": jnp.zeros((d, glora_p), BF16).at[:, :glora].set(od_g1[0].astype(BF16)),
        "g2": jnp.zeros((glora_p, d), BF16).at[:glora].set(od_g2[0].astype(BF16)),
        "w0": od_w0[0].astype(F32), "a0": od_a0[0].astype(F32),
        "k_k": od_k_k[0].reshape(1, d).astype(F32), "k_a": od_k_a[0].reshape(1, d).astype(F32),
        "r_k": od_r_k[0].reshape(1, d).astype(F32),
    }
    w_rkv = od_w_rkv[0].astype(BF16)
    r, k, v, lw_low, a_low, g_low = _rwkv_mix(xa, norm_mix_w[1], mod[0], mod[1], od_mu[0].astype(F32), w_rkv, p,
                                              rows=rows, tm=tm)
    f = _rwkv_post(r, k, v, lw_low, a_low, g_low, p, tm=tm // 2)
    y_f, y_b = _rwkv_scan(f[0:5], f[5:10], f[10], rows)
    mixed = _rwkv_readout(y_f, y_b, f[12], f[11], od_ln_w[0], od_ln_b[0], rows, tm)
    x_l = _mm_res(mixed, od_w_o[0].astype(BF16), xa, mod[2], tm=tm, group_of_tile=grp)

    gates = _router(x_l, norm_ffn_w[1], mod[3], mod[4], od_router[0], tm=tm, group_of_tile=grp)
    n_e = od_router.shape[-1]
    n_rows, tile_expert, n_valid, d1, d2_scatter, d2_combine, wts = _route_plan(gates, n_e, MOE_TILE)
    tm_r = tm
    grp_r = rows.group_of_tile(tm_r)
    a_sorted = _moe_scatter(x_l, norm_ffn_w[1], mod[3], mod[4], d1, d2_scatter, n_rows, tm=tm_r, tg=MOE_TILE,
                            group_of_tile=grp_r)
    moe13 = od_moe_w13[0].astype(BF16)
    hid_sorted = _grouped_swiglu(a_sorted, moe13, tile_expert, n_valid, n_rows, tg=MOE_TILE, tn=moe13.shape[2] // 2)
    y_sorted = _grouped_mm(hid_sorted, od_moe_w2[0].astype(F32), tile_expert, n_valid, tg=MOE_TILE)
    x_l = _combine(y_sorted, d1, d2_combine, wts, x_l, mod[5], tm=tm_r, group_of_tile=grp_r)
    return x_l.reshape(batch, seq, d)
```

```python
import functools

import jax
import jax.numpy as jnp
import numpy as np
from jax import lax
from jax.experimental import pallas as pl
from jax.experimental.pallas import tpu as pltpu

F32 = jnp.float32
BF16 = jnp.bfloat16

LANES = 128
GRID_W = 64
RET_HEADS = 4
RET_HD = 128
RET_W = RET_HEADS * RET_HD
RET_CHUNK = 128
RET_GN_EPS = 1e-5
NA_HEADS = 8
NA_HD = 64
NA_W = NA_HEADS * NA_HD
NA_KR = 8
NA_KC = 16
NA_ROWS_PER_STEP = 8
RWKV_HD = 64
RWKV_GN_EPS = 64e-5
RWKV_CHUNK = 64
TOP_K = 2
ROPE_BASE = 10000.0
NORM_EPS = 1e-6
NEG_BIG = -1e30
VMEM_LIMIT = 56 * 1024 * 1024


def _cparams(n_axes):
    return pltpu.CompilerParams(dimension_semantics=("arbitrary",) * n_axes, vmem_limit_bytes=VMEM_LIMIT)


def _dot(a, b):
    return jnp.dot(a.astype(BF16), b.astype(BF16), preferred_element_type=F32)


def _dot_nt(a, b):
    return lax.dot_general(a.astype(BF16), b.astype(BF16), (((1,), (1,)), ((), ())), preferred_element_type=F32)


def _split2(x):
    hi = x.astype(BF16)
    return hi, (x - hi.astype(F32)).astype(BF16)


def _split3(x):
    hi = x.astype(BF16)
    r1 = x - hi.astype(F32)
    mid = r1.astype(BF16)
    lo = (r1 - mid.astype(F32)).astype(BF16)
    return hi, mid, lo


def _dot_hi(a, b):
    ah, am, al = _split3(a)
    bh, bm, bl = _split3(b)
    d = functools.partial(jnp.dot, preferred_element_type=F32)
    return (d(ah, bh) + (d(ah, bm) + d(am, bh)) + (d(am, bm) + d(ah, bl) + d(al, bh)))


def _dot_exact_rhs(a, b_bf16):
    ah, al = _split2(a)
    d = functools.partial(jnp.dot, preferred_element_type=F32)
    return d(ah, b_bf16) + d(al, b_bf16)


def _dot_exact_lhs(a_bf16, b):
    bh, bl = _split2(b)
    d = functools.partial(jnp.dot, preferred_element_type=F32)
    return d(a_bf16, bh) + d(a_bf16, bl)


def _mod(v, n):
    return (v & (n - 1)) if n & (n - 1) == 0 else v % n


def _sigmoid(x):
    return 1.0 / (1.0 + jnp.exp(-x))


def _silu(x):
    return x * _sigmoid(x)


def _group_mean_mat(width, group):
    r = lax.broadcasted_iota(jnp.int32, (width, width), 0) // group
    c = lax.broadcasted_iota(jnp.int32, (width, width), 1) // group
    return jnp.where(r == c, 1.0 / group, 0.0).astype(BF16)


def _group_mean(x, group):
    g = _group_mean_mat(LANES, group)
    cols = [_dot_exact_rhs(x[:, c:c + LANES], g) for c in range(0, x.shape[1], LANES)]
    return cols[0] if len(cols) == 1 else jnp.concatenate(cols, axis=1)


def _modnorm(x, nw, shift, scale):
    ms = jnp.mean(x * x, axis=-1, keepdims=True)
    y = x * lax.rsqrt(ms + NORM_EPS) * nw
    return y * (1.0 + scale) + shift


class _Rows:
    def __init__(self, batch, seq, ctx_len):
        self.batch, self.seq, self.ctx_len = batch, seq, ctx_len
        self.ml, self.mc = batch * seq, batch * ctx_len
        self.m = self.ml + self.mc

    def group_of_tile(self, tm):
        return lambda i: jnp.minimum((i * tm) // self.seq, self.batch)

    def chunk_block(self, chunk, backward):
        ncc, nlc = self.ctx_len // chunk, self.seq // chunk

        def block(b, t):
            if backward:
                t = jnp.where(t < ncc, ncc - 1 - t, nlc + 2 * ncc - 1 - t)
            return jnp.where(t < ncc, (self.ml + b * self.ctx_len) // chunk + t, (b * self.seq) // chunk + t - ncc)

        def position(t):
            if backward:
                t = jnp.where(t < ncc, ncc - 1 - t, nlc + 2 * ncc - 1 - t)
            return t

        return block, position


def _ada_kernel(c_ref, w_ref, b_ref, o_ref):
    s = _silu(c_ref[...])
    o_ref[0, 0] = _dot_hi(s, w_ref[0]) + b_ref[0]


def _ada(c_rows, ada_w, ada_b):
    depth, d, n = ada_w.shape
    rows = c_rows.shape[0]
    terms = n // d
    return pl.pallas_call(
        _ada_kernel,
        out_shape=jax.ShapeDtypeStruct((depth, terms, rows, d), F32),
        grid=(depth, terms),
        in_specs=[
            pl.BlockSpec((rows, d), lambda l, j: (0, 0)),
            pl.BlockSpec((1, d, d), lambda l, j: (l, 0, j)),
            pl.BlockSpec((1, 1, d), lambda l, j: (l, 0, j)),
        ],
        out_specs=pl.BlockSpec((1, 1, rows, d), lambda l, j: (l, j, 0, 0)),
        compiler_params=_cparams(2),
        name="ada_mod",
    )(c_rows, ada_w, ada_b.reshape(depth, 1, n))


def _modnorm_mm_kernel(*refs, swiglu, head_tiles):
    x_ref, xt_ref, nw_ref, sh_ref, sc_ref = refs[:5]
    w_refs = refs[5:-2]
    o_ref, h_ref = refs[-2:]

    @pl.when(pl.program_id(1) == 0)
    def _():
        x = jnp.where(pl.program_id(0) < head_tiles, x_ref[...], xt_ref[...])
        h_ref[...] = _modnorm(x, nw_ref[...], sh_ref[0], sc_ref[0]).astype(BF16)

    h = h_ref[...]
    acc = jnp.dot(h, w_refs[0][...], preferred_element_type=F32)
    if swiglu:
        acc = _silu(acc) * jnp.dot(h, w_refs[1][...], preferred_element_type=F32)
    o_ref[...] = acc.astype(o_ref.dtype)


def _modnorm_mm(x, nw, shift, scale, w, *, group_of_tile, tm, tn, out_dtype, swiglu=False, x_tail=None):
    k = x.shape[1]
    head_tiles = x.shape[0] // tm
    if x_tail is None:
        x_tail = x
        m = x.shape[0]
    else:
        m = x.shape[0] + x_tail.shape[0]
    n_out = w.shape[1] // 2 if swiglu else w.shape[1]
    nj = n_out // tn
    gmap = lambda i, j: (group_of_tile(i), 0, 0)
    wspecs = [pl.BlockSpec((k, tn), lambda i, j: (0, j))]
    if swiglu:
        wspecs.append(pl.BlockSpec((k, tn), lambda i, j: (0, j + nj)))
    return pl.pallas_call(
        functools.partial(_modnorm_mm_kernel, swiglu=swiglu, head_tiles=head_tiles),
        out_shape=jax.ShapeDtypeStruct((m, n_out), out_dtype),
        grid=(m // tm, nj),
        in_specs=[pl.BlockSpec((tm, k), lambda i, j: (jnp.minimum(i, head_tiles - 1), 0)),
                  pl.BlockSpec((tm, k), lambda i, j: (jnp.maximum(i - head_tiles, 0), 0)),
                  pl.BlockSpec((1, k), lambda i, j: (0, 0)),
                  pl.BlockSpec((1, 1, k), gmap), pl.BlockSpec((1, 1, k), gmap)] + wspecs,
        out_specs=pl.BlockSpec((tm, tn), lambda i, j: (i, j)),
        scratch_shapes=[pltpu.VMEM((tm, k), BF16)],
        compiler_params=_cparams(2),
        name="modnorm_mm",
    )(x, x_tail, nw.reshape(1, k), shift, scale, *([w] * len(wspecs)))


def _mm_res_kernel(a_ref, w_ref, res_ref, rest_ref, gate_ref, o_ref, *, head_tiles):
    res = jnp.where(pl.program_id(0) < head_tiles, res_ref[...], rest_ref[...])
    o_ref[...] = res + gate_ref[0] * jnp.dot(a_ref[...], w_ref[...], preferred_element_type=F32)


def _mm_res(a, w, res, gate, *, tm, group_of_tile, res_tail=None):
    m, k = a.shape
    n = w.shape[1]
    head_tiles = m // tm if res_tail is None else res.shape[0] // tm
    tail = res if res_tail is None else res_tail
    return pl.pallas_call(
        functools.partial(_mm_res_kernel, head_tiles=head_tiles),
        out_shape=jax.ShapeDtypeStruct((m, n), F32),
        grid=(m // tm,),
        in_specs=[pl.BlockSpec((tm, k), lambda i: (i, 0)), pl.BlockSpec((k, n), lambda i: (0, 0)),
                  pl.BlockSpec((tm, n), lambda i: (jnp.minimum(i, head_tiles - 1), 0)),
                  pl.BlockSpec((tm, n), lambda i: (jnp.maximum(i - head_tiles, 0), 0)),
                  pl.BlockSpec((1, 1, n), lambda i: (group_of_tile(i), 0, 0))],
        out_specs=pl.BlockSpec((tm, n), lambda i: (i, 0)),
        compiler_params=_cparams(1),
        name="mm_res",
    )(a, w, res, tail, gate)


def _rope(x, cos, sin_signed):
    lane = lax.broadcasted_iota(jnp.int32, x.shape, 1)
    half = RET_HD // 4
    swapped = jnp.where(_mod(lane, 2 * half) < half, pltpu.roll(x, LANES - half, 1), pltpu.roll(x, half, 1))
    return x * cos + swapped * sin_signed


def _ret_chunks(q, k, v, s0, log_g, n_forward):
    c = RET_CHUNK
    shape = (q.shape[0], c, c)
    row = lax.broadcasted_iota(jnp.int32, shape, 1)
    col = lax.broadcasted_iota(jnp.int32, shape, 2)
    fwd = lax.broadcasted_iota(jnp.int32, shape, 0) < n_forward
    dist = jnp.where(fwd, row - col, col - row).astype(F32)
    q_steps = jnp.where(fwd, row + 1, c - row).astype(F32)
    k_steps = jnp.where(fwd, c - 1 - row, row).astype(F32)
    dmat = jnp.where(dist >= 0, jnp.exp(log_g * jnp.maximum(dist, 0.0)), 0.0)
    q_dec = jnp.exp(log_g * q_steps)
    k_dec = jnp.exp(log_g * k_steps)
    scores = _bdot_nt(q, k) * dmat
    out = _bdot(scores, v) + _bdot(q * q_dec, s0)
    s1 = s0 * jnp.exp(log_g * float(c)) + _bdot_tn(k * k_dec, v)
    return out, s1


def _retention_kernel(qf_ref, kf_ref, vf_ref, cf_ref, sf_ref, qb_ref, kb_ref, vb_ref, cb_ref, sb_ref,
                      decf_ref, decb_ref, of_ref, ob_ref, st_ref):
    @pl.when(pl.program_id(1) == 0)
    def _():
        st_ref[...] = jnp.zeros_like(st_ref)

    h = RET_HEADS
    kscale = RET_HD ** -0.5

    def heads(ref, cos, sin, scale=None):
        out = []
        for hh in range(h):
            xh = ref[:, hh * RET_HD:(hh + 1) * RET_HD]
            if cos is not None:
                xh = _rope(xh, cos, sin)
            out.append(xh if scale is None else xh * scale)
        return out

    cf, sf, cb, sb = cf_ref[...], sf_ref[...], cb_ref[...], sb_ref[...]
    q = jnp.stack(heads(qf_ref, cf, sf) + heads(qb_ref, cb, sb), axis=0)
    k = jnp.stack(heads(kf_ref, cf, sf, kscale) + heads(kb_ref, cb, sb, kscale), axis=0)
    v = jnp.stack(heads(vf_ref, None, None) + heads(vb_ref, None, None), axis=0)
    log_g = -jnp.exp(jnp.concatenate([decf_ref[...], decb_ref[...]], axis=0))
    out, s1 = _ret_chunks(q, k, v, st_ref[...], log_g, h)
    st_ref[...] = s1
    of_ref[...] = jnp.concatenate([out[hh] for hh in range(h)], axis=1)
    ob_ref[...] = jnp.concatenate([out[h + hh] for hh in range(h)], axis=1)


def _retention(proj, cos, sin_signed, dec_f, dec_b, rows):
    c, h = RET_CHUNK, RET_HEADS
    n = (rows.seq + rows.ctx_len) // c
    blk_f, pos_f = rows.chunk_block(c, False)
    blk_b, pos_b = rows.chunk_block(c, True)

    def tok(blk, off):
        return pl.BlockSpec((c, RET_W), lambda bi, t: (blk(bi, t), off))

    def tab(pos):
        return pl.BlockSpec((c, RET_HD), lambda bi, t: (pos(t), 0))

    dec = lambda a: jnp.broadcast_to(a.astype(F32)[:, None, None], (h, 1, LANES))
    dspec = pl.BlockSpec((h, 1, LANES), lambda bi, t: (0, 0, 0))
    return pl.pallas_call(
        _retention_kernel,
        out_shape=[jax.ShapeDtypeStruct((rows.m, RET_W), F32)] * 2,
        grid=(rows.batch, n),
        in_specs=[tok(blk_f, 0), tok(blk_f, 1), tok(blk_f, 2), tab(pos_f), tab(pos_f),
                  tok(blk_b, 0), tok(blk_b, 1), tok(blk_b, 2), tab(pos_b), tab(pos_b), dspec, dspec],
        out_specs=[tok(blk_f, 0), tok(blk_b, 0)],
        scratch_shapes=[pltpu.VMEM((2 * h, RET_HD, RET_HD), F32)],
        compiler_params=_cparams(2),
        name="retention",
    )(proj, proj, proj, cos, sin_signed, proj, proj, proj, cos, sin_signed, dec(dec_f), dec(dec_b))


def _na_qk_norm(x, w):
    ms = _group_mean(x * x, NA_HD)
    return x * lax.rsqrt(ms + NORM_EPS) * w


def _softmax_pv(parts):
    m = functools.reduce(jnp.maximum, [jnp.max(s, axis=-1, keepdims=True) for s, _ in parts])
    ps = [jnp.exp(s - m) for s, _ in parts]
    den = functools.reduce(jnp.add, [jnp.sum(p, axis=-1, keepdims=True) for p in ps])
    num = functools.reduce(jnp.add, [_dot(p, v) for p, (_, v) in zip(ps, parts)])
    return num / den


def _na_kernel(q_ref, k_ref, v_ref, kc_ref, vc_ref, qn_ref, kn_ref, bias_ref, o_ref, qs_ref, ks_ref, vs_ref,
               kcs_ref, *, rows):
    scale = NA_HD ** -0.5
    qs_ref[...] = (_na_qk_norm(q_ref[...], qn_ref[...]) * scale).astype(BF16)
    ks_ref[...] = _na_qk_norm(k_ref[...], kn_ref[...]).astype(BF16)
    vs_ref[...] = v_ref[...].astype(BF16)
    kcs_ref[...] = _na_qk_norm(kc_ref[...], kn_ref[...]).astype(BF16)
    lane = lax.broadcasted_iota(jnp.int32, (GRID_W, LANES), 1)
    first = lane < NA_HD
    vc = vc_ref[...]
    kc = kcs_ref[...]

    nq = 2 * GRID_W

    def body(rb, carry):
        qs, kw, vw, bw = [], [], [], []
        for i in range(NA_ROWS_PER_STEP):
            r = rb * NA_ROWS_PER_STEP + i
            r0 = jnp.clip(r - NA_KR // 2, 0, rows - NA_KR)
            q_r = qs_ref[pl.ds(pl.multiple_of(r * GRID_W, GRID_W), GRID_W), :]
            zero = jnp.zeros_like(q_r)
            qs.append(jnp.concatenate([jnp.where(first, q_r, zero), jnp.where(first, zero, q_r)], axis=0))
            win = pl.ds(pl.multiple_of(r0 * GRID_W, GRID_W), NA_KR * GRID_W)
            kw.append(ks_ref[win, :])
            vw.append(vs_ref[win, :])
            bw.append(bias_ref[0, r0 - r + (NA_KR - 1)])
        q = jnp.stack(qs, axis=0)
        s_w = _bdot_nt(q, jnp.stack(kw, axis=0)) + jnp.stack(bw, axis=0)
        q_flat = q.reshape(NA_ROWS_PER_STEP * nq, LANES)
        s_c = _dot_nt(q_flat, kc).reshape(NA_ROWS_PER_STEP, nq, kc.shape[0])
        m = jnp.maximum(jnp.max(s_w, axis=-1, keepdims=True), jnp.max(s_c, axis=-1, keepdims=True))
        p_w = jnp.exp(s_w - m)
        p_c = jnp.exp(s_c - m)
        den = jnp.sum(p_w, axis=-1, keepdims=True) + jnp.sum(p_c, axis=-1, keepdims=True)
        num = _bdot(p_w, jnp.stack(vw, axis=0)) + _dot(p_c.reshape(NA_ROWS_PER_STEP * nq, kc.shape[0]), vc).reshape(
            NA_ROWS_PER_STEP, nq, LANES)
        out = num / den
        for i in range(NA_ROWS_PER_STEP):
            r = rb * NA_ROWS_PER_STEP + i
            o_ref[pl.ds(pl.multiple_of(r * GRID_W, GRID_W), GRID_W), :] = jnp.where(
                first, out[i, :GRID_W], out[i, GRID_W:])
        return carry

    lax.fori_loop(0, rows // NA_ROWS_PER_STEP, body, 0)


def _na_bias_table(rpb):
    cols = np.arange(GRID_W)
    start = np.clip(cols - NA_KC // 2, 0, GRID_W - NA_KC)
    kcol = np.arange(GRID_W)
    inside = (kcol[None, :] >= start[:, None]) & (kcol[None, :] < start[:, None] + NA_KC)
    col_off = kcol[None, :] - cols[:, None] + (NA_KC - 1)
    pick = np.zeros((2 * NA_KC - 1, GRID_W, GRID_W), np.float32)
    qi, ki = np.nonzero(inside)
    pick[col_off[qi, ki], qi, ki] = 1.0
    rows_qk = jnp.einsum("hro,oqk->hrqk", rpb, jnp.asarray(pick), precision=lax.Precision.HIGHEST)
    n_pairs = rpb.shape[0] // 2
    rows_qk = rows_qk.reshape(n_pairs, 2, rpb.shape[1], GRID_W, GRID_W)
    tab = jnp.stack([jnp.stack([rows_qk[:, :, cls + i] for i in range(NA_KR)], axis=3) for cls in range(NA_KR)],
                    axis=1)
    tab = jnp.where(inside[None, None, None, :, None, :], tab, NEG_BIG)
    return tab.reshape(n_pairs, NA_KR, 2 * GRID_W, NA_KR * GRID_W).astype(F32)


def _na_latent(proj, qn_w, kn_w, bias, rows, col0):
    seq, ctx_len = rows.seq, rows.ctx_len
    grid_rows = seq // GRID_W
    pairs = NA_W // LANES
    cb = col0 // LANES
    cblk = rows.ml // ctx_len
    tile2 = lambda a: jnp.tile(a.astype(F32), 2).reshape(1, LANES)
    lat = lambda off: pl.BlockSpec((seq, LANES), lambda bi, p: (bi, cb + off * pairs + p))
    ctx = lambda off: pl.BlockSpec((ctx_len, LANES), lambda bi, p: (cblk + bi, cb + off * pairs + p))
    return pl.pallas_call(
        functools.partial(_na_kernel, rows=grid_rows),
        out_shape=jax.ShapeDtypeStruct((rows.ml, NA_W), F32),
        grid=(rows.batch, pairs),
        in_specs=[lat(0), lat(1), lat(2), ctx(1), ctx(2),
                  pl.BlockSpec((1, LANES), lambda bi, p: (0, 0)), pl.BlockSpec((1, LANES), lambda bi, p: (0, 0)),
                  pl.BlockSpec((1, NA_KR, 2 * GRID_W, NA_KR * GRID_W), lambda bi, p: (p, 0, 0, 0))],
        out_specs=pl.BlockSpec((seq, LANES), lambda bi, p: (bi, p)),
        scratch_shapes=[pltpu.VMEM((seq, LANES), BF16)] * 3 + [pltpu.VMEM((ctx_len, LANES), BF16)],
        compiler_params=_cparams(2),
        name="na_latent",
    )(proj, proj, proj, proj, proj, tile2(qn_w), tile2(kn_w), bias)


def _ctx_attn_kernel(q_ref, k_ref, v_ref, qn_ref, kn_ref, o_ref):
    scale = NA_HD ** -0.5
    q = (_na_qk_norm(q_ref[...], qn_ref[...]) * scale).astype(BF16)
    k = _na_qk_norm(k_ref[...], kn_ref[...]).astype(BF16)
    v = v_ref[...]
    lane = lax.broadcasted_iota(jnp.int32, q.shape, 1)
    first = lane < NA_HD
    outs = []
    for hh in range(2):
        qh = jnp.where(first if hh == 0 else jnp.logical_not(first), q, jnp.zeros_like(q))
        outs.append(_softmax_pv([(_dot_nt(qh, k), v)]))
    o_ref[...] = jnp.where(first, outs[0], outs[1])


def _ctx_attention(proj, qn_w, kn_w, rows, col0):
    ctx_len = rows.ctx_len
    pairs = NA_W // LANES
    cb = col0 // LANES
    cblk = rows.ml // ctx_len
    tile2 = lambda a: jnp.tile(a.astype(F32), 2).reshape(1, LANES)
    blk = lambda off: pl.BlockSpec((ctx_len, LANES), lambda bi, p: (cblk + bi, cb + off * pairs + p))
    return pl.pallas_call(
        _ctx_attn_kernel,
        out_shape=jax.ShapeDtypeStruct((rows.mc, NA_W), F32),
        grid=(rows.batch, pairs),
        in_specs=[blk(0), blk(1), blk(2),
                  pl.BlockSpec((1, LANES), lambda bi, p: (0, 0)), pl.BlockSpec((1, LANES), lambda bi, p: (0, 0))],
        out_specs=pl.BlockSpec((ctx_len, LANES), lambda bi, p: (bi, p)),
        compiler_params=_cparams(2),
        name="ctx_attention",
    )(proj, proj, proj, tile2(qn_w), tile2(kn_w))


def _ret_readout_kernel(yf_ref, yb_ref, g_ref, nal_ref, nac_ref, gnw_ref, o_ref, *, n_latent_tiles):
    y = yf_ref[...] + yb_ref[...]
    for hh in range(RET_HEADS):
        sl = slice(hh * RET_HD, (hh + 1) * RET_HD)
        yh = y[:, sl]
        yc = yh - jnp.mean(yh, axis=-1, keepdims=True)
        yn = yc * lax.rsqrt(jnp.mean(yc * yc, axis=-1, keepdims=True) + RET_GN_EPS)
        o_ref[:, sl] = (yn * gnw_ref[:, sl] * _silu(g_ref[:, sl])).astype(o_ref.dtype)
    is_latent = pl.program_id(0) < n_latent_tiles
    o_ref[:, RET_W:] = jnp.where(is_latent, nal_ref[...], nac_ref[...]).astype(o_ref.dtype)


def _ret_readout(yf, yb, proj, na_l, na_c, gn_w, rows, tm):
    nl = rows.ml // tm
    row = lambda i: (i, 0)
    return pl.pallas_call(
        functools.partial(_ret_readout_kernel, n_latent_tiles=nl),
        out_shape=jax.ShapeDtypeStruct((rows.m, RET_W + NA_W), BF16),
        grid=(rows.m // tm,),
        in_specs=[pl.BlockSpec((tm, RET_W), row), pl.BlockSpec((tm, RET_W), row),
                  pl.BlockSpec((tm, RET_W), lambda i: (i, 3)),
                  pl.BlockSpec((tm, NA_W), lambda i: (jnp.minimum(i, nl - 1), 0)),
                  pl.BlockSpec((tm, NA_W), lambda i: (jnp.maximum(i - nl, 0), 0)),
                  pl.BlockSpec((1, RET_W), lambda i: (0, 0))],
        out_specs=pl.BlockSpec((tm, RET_W + NA_W), row),
        compiler_params=_cparams(1),
        name="ret_readout",
    )(yf, yb, proj, na_l, na_c, gn_w.reshape(1, RET_W).astype(F32))


def _rwkv_mix_kernel(x_ref, xp_ref, xn_ref, nw_ref, sh_ref, sc_ref, mu_ref, wrkv_ref, w1_ref, a1_ref, g1_ref,
                     r_ref, k_ref, v_ref, lwl_ref, al_ref, gl_ref, *, seq, ctx_len, ml, tm):
    nw, sh, sc = nw_ref[...], sh_ref[0], sc_ref[0]
    h = _modnorm(x_ref[...], nw, sh, sc)
    hp = _modnorm(xp_ref[...], nw, sh, sc)[7:8]
    hn = _modnorm(xn_ref[...], nw, sh, sc)[0:1]
    row = lax.broadcasted_iota(jnp.int32, h.shape, 0)
    g = pl.program_id(0) * tm + row
    latent = g < ml
    pos = jnp.where(latent, _mod(g, seq), _mod(g, ctx_len))
    first = pos == 0
    last = pos == jnp.where(latent, seq - 1, ctx_len - 1)
    up = jnp.where(row == 0, hp, pltpu.roll(h, 1, 0))
    up = jnp.where(first, 0.0, up)
    dn = jnp.where(row == tm - 1, hn, pltpu.roll(h, tm - 1, 0))
    dn = jnp.where(last, 0.0, dn)
    xx = 0.5 * (up + dn) - h
    mix = lambda s_i: (h + xx * mu_ref[s_i:s_i + 1, :]).astype(BF16)
    for s_i, o_ref in enumerate((r_ref, k_ref, v_ref)):
        o_ref[...] = jnp.dot(mix(s_i), wrkv_ref[s_i], preferred_element_type=F32)
    lwl_ref[...] = jnp.tanh(jnp.dot(mix(3), w1_ref[...], preferred_element_type=F32))
    al_ref[...] = jnp.dot(mix(4), a1_ref[...], preferred_element_type=F32)
    gl_ref[...] = _sigmoid(jnp.dot(mix(5), g1_ref[...], preferred_element_type=F32))


def _rwkv_mix(x, nw, shift, scale, mu, w_rkv, p, *, rows, tm):
    m, d = x.shape
    n_slab = m // 8
    per = tm // 8
    gmap = lambda i: (rows.group_of_tile(tm)(i), 0, 0)
    row = lambda i: (i, 0)
    full2 = lambda i: (0, 0)
    smalls = [p["w1"], p["a1"], p["g1"]]
    return pl.pallas_call(
        functools.partial(_rwkv_mix_kernel, seq=rows.seq, ctx_len=rows.ctx_len, ml=rows.ml, tm=tm),
        out_shape=[jax.ShapeDtypeStruct((m, d), F32)] * 3
        + [jax.ShapeDtypeStruct((m, w.shape[1]), F32) for w in smalls],
        grid=(m // tm,),
        in_specs=[pl.BlockSpec((tm, d), row),
                  pl.BlockSpec((8, d), lambda i: (jnp.maximum(i * per - 1, 0), 0)),
                  pl.BlockSpec((8, d), lambda i: (jnp.minimum((i + 1) * per, n_slab - 1), 0)),
                  pl.BlockSpec((1, d), full2),
                  pl.BlockSpec((1, 1, d), gmap), pl.BlockSpec((1, 1, d), gmap),
                  pl.BlockSpec((6, d), full2),
                  pl.BlockSpec((3, d, d), lambda i: (0, 0, 0))]
        + [pl.BlockSpec(w.shape, full2) for w in smalls],
        out_specs=[pl.BlockSpec((tm, d), row)] * 3 + [pl.BlockSpec((tm, w.shape[1]), row) for w in smalls],
        compiler_params=_cparams(1),
        name="rwkv_mix",
    )(x, x, x, nw.reshape(1, d), shift, scale, mu, w_rkv, *smalls)


def _rwkv_post_kernel(r_ref, k_ref, v_ref, lwl_ref, al_ref, gl_ref, w2_ref, a2_ref, g2_ref,
                      w0_ref, a0_ref, kk_ref, ka_ref, rk_ref,
                      rt0, kt0, bt0, at0, wc0, rt1, kt1, bt1, at1, wc1, vb_ref, g_ref, bonus_ref, *, tm):
    c = RWKV_CHUNK
    n_pairs = r_ref.shape[1] // LANES
    r, k, v = r_ref[...], k_ref[...], v_ref[...]
    lw_low, a_low = lwl_ref[...], al_ref[...]
    g_ref[...] = _dot(gl_ref[...], g2_ref[...]).astype(g_ref.dtype)

    def put(o_ref, val):
        for p in range(n_pairs):
            o_ref[p] = val[:, p * LANES:(p + 1) * LANES].astype(o_ref.dtype)

    put(vb_ref, v)
    kk = k * kk_ref[...]
    nrm = jnp.sqrt(_group_mean(kk * kk, RWKV_HD) * float(RWKV_HD))
    kk = kk / jnp.maximum(nrm, 1e-12)

    tr = lax.broadcasted_iota(jnp.int32, (c, c), 0)
    tc = lax.broadcasted_iota(jnp.int32, (c, c), 1)
    tri = [(tr >= tc).astype(BF16), (tr <= tc).astype(BF16)]
    lora = w2_ref.shape[1]
    coeff_src = jnp.zeros_like(r)
    outs = [(rt0, kt0, bt0, at0, wc0), (rt1, kt1, bt1, at1, wc1)]
    for z in range(2):
        rt_ref, kt_ref, bt_ref, at_ref, wc_ref = outs[z]
        w_lora = _dot(lw_low[:, z * lora:(z + 1) * lora], w2_ref[z])
        lw = -float(np.exp(-0.5)) * _sigmoid(w0_ref[z:z + 1, :] + w_lora)
        a_z = _sigmoid(a0_ref[z:z + 1, :] + _dot(a_low[:, z * lora:(z + 1) * lora], a2_ref[z]))
        k_dir = k * (1.0 + (a_z - 1.0) * ka_ref[...])
        coeff_src = coeff_src + r * k_dir * rk_ref[...]
        cum = jnp.concatenate([_dot_exact_lhs(tri[z], lw[ci * c:(ci + 1) * c]) for ci in range(tm // c)], axis=0)
        e_pos = jnp.exp(cum)
        e_neg = 1.0 / e_pos
        put(rt_ref, r * e_pos)
        put(kt_ref, k_dir * e_neg)
        put(bt_ref, kk * a_z * e_neg)
        put(at_ref, -kk * jnp.exp(cum - lw))
        for ci in range(tm // c):
            end = ci * c + (c - 1 if z == 0 else 0)
            for p in range(n_pairs):
                wc_ref[p, ci] = e_pos[end:end + 1, p * LANES:(p + 1) * LANES]
    coeff = _group_mean(coeff_src, RWKV_HD) * float(RWKV_HD)
    bonus_ref[...] = coeff * v


def _rwkv_post(r, k, v, lw_low, a_low, g_low, p, *, tm):
    m, d = r.shape
    c = RWKV_CHUNK
    lora = p["w2"].shape[1]
    glora = p["g1"].shape[1]
    small = lambda a: pl.BlockSpec((tm, a.shape[1]), lambda i: (i, 0))
    n_pairs = d // LANES
    pair = jax.ShapeDtypeStruct((n_pairs, m, LANES), BF16)
    chunk = jax.ShapeDtypeStruct((n_pairs, m // c, 1, LANES), F32)
    row = lambda i: (i, 0)
    full2 = lambda i: (0, 0)
    full3 = lambda i: (0, 0, 0)
    tspec = pl.BlockSpec((tm, d), row)
    pspec = pl.BlockSpec((n_pairs, tm, LANES), lambda i: (0, i, 0))
    cspec = pl.BlockSpec((n_pairs, tm // c, 1, LANES), lambda i: (0, i, 0, 0))
    return pl.pallas_call(
        functools.partial(_rwkv_post_kernel, tm=tm),
        out_shape=[pair] * 4 + [chunk] + [pair] * 4 + [chunk]
        + [pair, jax.ShapeDtypeStruct((m, d), BF16), jax.ShapeDtypeStruct((m, d), F32)],
        grid=(m // tm,),
        in_specs=[tspec] * 3 + [small(lw_low), small(a_low), small(g_low)]
        + [pl.BlockSpec((2, lora, d), full3), pl.BlockSpec((2, lora, d), full3), pl.BlockSpec((glora, d), full2),
           pl.BlockSpec((2, d), full2), pl.BlockSpec((2, d), full2)]
        + [pl.BlockSpec((1, d), full2)] * 3,
        out_specs=[pspec] * 4 + [cspec] + [pspec] * 4 + [cspec] + [pspec, tspec, tspec],
        compiler_params=_cparams(1),
        name="rwkv_post",
    )(r, k, v, lw_low, a_low, g_low, p["w2"], p["a2"], p["g2"], p["w0"], p["a0"], p["k_k"], p["k_a"], p["r_k"])


def _bdot(a, b):
    return jnp.einsum("ucd,ude->uce", a.astype(BF16), b.astype(BF16), preferred_element_type=F32)


def _bdot_nt(a, b):
    return jnp.einsum("ucd,usd->ucs", a.astype(BF16), b.astype(BF16), preferred_element_type=F32)


def _bdot_tn(a, b):
    return jnp.einsum("uce,ucd->ued", a.astype(BF16), b.astype(BF16), preferred_element_type=F32)


def _scan_chunks(rt, kt, bt, at, v, wc, s0, n_forward):
    c = RWKV_CHUNK
    n_units = rt.shape[0]
    lane = lax.broadcasted_iota(jnp.int32, (n_units, c, LANES), 2)
    row = lax.broadcasted_iota(jnp.int32, (n_units, c, LANES), 1)
    unit = lax.broadcasted_iota(jnp.int32, (n_units, c, LANES), 0)
    head0 = lane < RWKV_HD
    src = _mod(lane, RWKV_HD)
    ahead = jnp.where(unit < n_forward, row - src, src - row)
    strict = ahead > 0
    incl = ahead >= 0

    def dup(x):
        zero = jnp.zeros_like(x)
        h0 = head0[:, :x.shape[1]]
        return jnp.concatenate([jnp.where(h0, x, zero), jnp.where(h0, zero, x)], axis=1)

    ar = jnp.concatenate([at, rt], axis=1)
    mb = _bdot_nt(ar, dup(bt))
    mk = _bdot_nt(ar, dup(kt))
    p_ab = jnp.where(strict, mb[:, :c], 0.0)
    p_rb = jnp.where(incl, mb[:, c:], 0.0)
    p_ak = jnp.where(strict, mk[:, :c], 0.0)
    p_rk = jnp.where(incl, mk[:, c:], 0.0)
    vd = dup(v)
    rhs = _bdot_nt(at, s0) + _bdot(p_ak, vd)

    power = p_ab.astype(BF16)
    inv = jnp.where(row == src, 1.0, 0.0) + p_ab
    for _ in range(int(np.log2(c)) - 1):
        power = _bdot(power, dup(power)).astype(BF16)
        inv = inv + _bdot(inv, dup(power))
    inv = inv.astype(BF16)
    u = _bdot(inv, dup(rhs))
    resid = (rhs - u) + _bdot(p_ab, dup(u))
    u = u + _bdot(inv, dup(resid))

    y = _bdot_nt(rt, s0) + _bdot(jnp.concatenate([p_rb, p_rk], axis=2), jnp.concatenate([dup(u), vd], axis=1))
    upd = _bdot_tn(jnp.concatenate([u.astype(BF16), v], axis=1), jnp.concatenate([bt, kt], axis=1))
    er = lax.broadcasted_iota(jnp.int32, (1, LANES, LANES), 1) < RWKV_HD
    ec = lax.broadcasted_iota(jnp.int32, (1, LANES, LANES), 2) < RWKV_HD
    s1 = (s0 + jnp.where(er == ec, upd, 0.0)) * wc
    return y, s1


def _rwkv_scan_kernel(rtf, ktf, btf, atf, vf, wcf, rtb, ktb, btb, atb, vb, wcb, yf_ref, yb_ref, s_ref):
    @pl.when(pl.program_id(1) == 0)
    def _():
        s_ref[...] = jnp.zeros_like(s_ref)

    n_pairs = rtf.shape[0]
    both = lambda f, b: jnp.concatenate([f[...], b[...]], axis=0)
    wc = jnp.concatenate([wcf[:, 0], wcb[:, 0]], axis=0)
    y, s1 = _scan_chunks(both(rtf, rtb), both(ktf, ktb), both(btf, btb), both(atf, atb), both(vf, vb), wc,
                         s_ref[...], n_pairs)
    s_ref[...] = s1
    yf_ref[...] = y[:n_pairs]
    yb_ref[...] = y[n_pairs:]


def _rwkv_scan(feats_f, feats_b, v, rows):
    n_pairs, m, _ = v.shape
    c = RWKV_CHUNK
    d = n_pairs * LANES
    nchunks = (rows.seq + rows.ctx_len) // c
    blk_f, _ = rows.chunk_block(c, False)
    blk_b, _ = rows.chunk_block(c, True)

    def specs(blk):
        tok = pl.BlockSpec((n_pairs, c, LANES), lambda bi, t: (0, blk(bi, t), 0))
        return tok, pl.BlockSpec((n_pairs, 1, 1, LANES), lambda bi, t: (0, blk(bi, t), 0, 0))

    tok_f, wc_f = specs(blk_f)
    tok_b, wc_b = specs(blk_b)
    return pl.pallas_call(
        _rwkv_scan_kernel,
        out_shape=[jax.ShapeDtypeStruct((n_pairs, m, LANES), F32)] * 2,
        grid=(rows.batch, nchunks),
        in_specs=[tok_f] * 5 + [wc_f] + [tok_b] * 5 + [wc_b],
        out_specs=[tok_f, tok_b],
        scratch_shapes=[pltpu.VMEM((2 * n_pairs, LANES, LANES), F32)],
        compiler_params=_cparams(2),
        name="rwkv_scan",
    )(*feats_f[:4], v, feats_f[4], *feats_b[:4], v, feats_b[4])


def _rwkv_readout_kernel(yf_ref, yb_ref, bonus_ref, g_ref, lnw_ref, lnb_ref, o_ref):
    y = jnp.concatenate([yf_ref[p] + yb_ref[p] for p in range(yf_ref.shape[0])], axis=1)
    yc = y - _group_mean(y, RWKV_HD)
    yn = yc * lax.rsqrt(_group_mean(yc * yc, RWKV_HD) + RWKV_GN_EPS) * lnw_ref[...] + lnb_ref[...]
    o_ref[...] = ((yn + bonus_ref[...]) * g_ref[...].astype(F32)).astype(o_ref.dtype)


def _rwkv_readout(yf, yb, bonus, g, ln_w, ln_b, rows, tm):
    n_pairs = yf.shape[0]
    d = n_pairs * LANES
    pspec = pl.BlockSpec((n_pairs, tm, LANES), lambda i: (0, i, 0))
    tspec = pl.BlockSpec((tm, d), lambda i: (i, 0))
    vspec = pl.BlockSpec((1, d), lambda i: (0, 0))
    return pl.pallas_call(
        _rwkv_readout_kernel,
        out_shape=jax.ShapeDtypeStruct((rows.ml, d), BF16),
        grid=(rows.ml // tm,),
        in_specs=[pspec, pspec, tspec, tspec, vspec, vspec],
        out_specs=tspec,
        compiler_params=_cparams(1),
        name="rwkv_readout",
    )(yf, yb, bonus, g, ln_w.reshape(1, d).astype(F32), ln_b.reshape(1, d).astype(F32))


def _router_kernel(x_ref, nw_ref, sh_ref, sc_ref, w_ref, o_ref, *, n_experts):
    h = _modnorm(x_ref[...], nw_ref[...], sh_ref[0], sc_ref[0])
    logits = _dot_hi(h, w_ref[...])
    lane = lax.broadcasted_iota(jnp.int32, logits.shape, 1).astype(F32)
    logits = jnp.where(lane < n_experts, logits, -jnp.inf)
    v1 = jnp.max(logits, axis=-1, keepdims=True)
    i1 = jnp.min(jnp.where(logits == v1, lane, float(LANES)), axis=-1, keepdims=True)
    rest = jnp.where(lane == i1, -jnp.inf, logits)
    v2 = jnp.max(rest, axis=-1, keepdims=True)
    i2 = jnp.min(jnp.where(rest == v2, lane, float(LANES)), axis=-1, keepdims=True)
    e2 = jnp.exp(v2 - v1)
    den = 1.0 + e2
    o_ref[...] = jnp.where(lane == i1, 1.0 / den, 0.0) + jnp.where(lane == i2, e2 / den, 0.0)


def _router(x, nw, shift, scale, router, *, tm, group_of_tile):
    m, d = x.shape
    n_experts = router.shape[1]
    w = jnp.zeros((d, LANES), F32).at[:, :n_experts].set(router.astype(F32))
    gmap = lambda i: (group_of_tile(i), 0, 0)
    return pl.pallas_call(
        functools.partial(_router_kernel, n_experts=n_experts),
        out_shape=jax.ShapeDtypeStruct((m, LANES), F32),
        grid=(m // tm,),
        in_specs=[pl.BlockSpec((tm, d), lambda i: (i, 0)), pl.BlockSpec((1, d), lambda i: (0, 0)),
                  pl.BlockSpec((1, 1, d), gmap), pl.BlockSpec((1, 1, d), gmap),
                  pl.BlockSpec((d, LANES), lambda i: (0, 0))],
        out_specs=pl.BlockSpec((tm, LANES), lambda i: (i, 0)),
        compiler_params=_cparams(1),
        name="router",
    )(x, nw.reshape(1, d), shift, scale, w)


MOE_TILE = 512
DMA_UNROLL = 16


def _route_plan(gates, n_experts, tg):
    m = gates.shape[0]
    g = gates[:, :n_experts]
    sel = g > 0.0
    seli = sel.astype(jnp.int32)
    slot = jnp.cumsum(seli, axis=1) - 1
    rank = jnp.cumsum(seli, axis=0) - 1
    counts = jnp.sum(seli, axis=0)
    padded = ((counts + tg - 1) // tg) * tg
    ends = jnp.cumsum(padded)
    dest = (ends - padded)[None, :] + rank
    n_rows = TOP_K * m + n_experts * tg
    n_tiles = n_rows // tg
    first = sel & (slot == 0)
    second = sel & (slot == 1)
    pick = lambda msk, val: jnp.sum(jnp.where(msk, val, 0), axis=1)
    d1 = pick(first, dest)
    has2 = jnp.any(second, axis=1)
    d2 = pick(second, dest)
    w1 = pick(first, g)
    w2 = pick(second, g)
    tile_expert = jnp.minimum(jnp.sum((jnp.arange(n_tiles, dtype=jnp.int32) * tg)[:, None] >= ends[None, :], axis=1),
                              n_experts - 1).astype(jnp.int32)
    n_valid = (ends[-1] // tg).astype(jnp.int32).reshape(1)
    lane = jnp.arange(LANES)[None, :]
    wts = jnp.where(lane == 0, w1[:, None], jnp.where(lane == 1, w2[:, None], 0.0)).astype(F32)
    i32 = lambda a: a.astype(jnp.int32)
    return (n_rows, tile_expert, n_valid, i32(d1), i32(jnp.where(has2, d2, n_rows)), i32(jnp.where(has2, d2, d1)),
            wts)


def _slab_rows(ref3, val):
    for s in range(ref3.shape[1]):
        ref3[:, s, :] = val[:, s * LANES:(s + 1) * LANES].astype(ref3.dtype)


def _unslab_rows(ref3):
    return jnp.concatenate([ref3[:, s, :] for s in range(ref3.shape[1])], axis=1)


def _moe_scatter_kernel(d1_ref, d2_ref, x_ref, nw_ref, sh_ref, sc_ref, a0_hbm, a_hbm, h_ref, sem, *, tm):
    del a0_hbm
    i = pl.program_id(0)
    slot = lax.rem(i, 2)
    _slab_rows(h_ref.at[slot], _modnorm(x_ref[...], nw_ref[...], sh_ref[0], sc_ref[0]))

    def row_copy(s_i, r, dst):
        return pltpu.make_async_copy(h_ref.at[s_i, pl.ds(r, 1)], a_hbm.at[pl.ds(dst, 1)], sem.at[s_i])

    def issue(r, carry):
        row_copy(slot, r, d1_ref[0, 0, r]).start(priority=0)
        row_copy(slot, r, d2_ref[0, 0, r]).start(priority=1)
        return carry

    def drain_slot(s_i):
        def drain(r, carry):
            row_copy(s_i, r, 0).wait()
            row_copy(s_i, r, 0).wait()
            return carry

        lax.fori_loop(0, tm, drain, 0, unroll=DMA_UNROLL)

    lax.fori_loop(0, tm, issue, 0, unroll=DMA_UNROLL)

    @pl.when(i > 0)
    def _():
        drain_slot(1 - slot)

    @pl.when(i == pl.num_programs(0) - 1)
    def _():
        drain_slot(slot)


def _moe_scatter(x, nw, shift, scale, d1, d2, n_rows, *, tm, tg, group_of_tile):
    m, d = x.shape
    slabs = d // LANES
    idx = lambda a: a.reshape(m // tm, 1, tm)
    ispec = pl.BlockSpec((1, 1, tm), lambda i: (i, 0, 0), memory_space=pltpu.SMEM)
    gmap = lambda i: (group_of_tile(i), 0, 0)
    zeros = jnp.zeros((n_rows + tg, slabs, LANES), F32)
    return pl.pallas_call(
        functools.partial(_moe_scatter_kernel, tm=tm),
        out_shape=jax.ShapeDtypeStruct(zeros.shape, F32),
        grid=(m // tm,),
        in_specs=[ispec, ispec, pl.BlockSpec((tm, d), lambda i: (i, 0)), pl.BlockSpec((1, d), lambda i: (0, 0)),
                  pl.BlockSpec((1, 1, d), gmap), pl.BlockSpec((1, 1, d), gmap), pl.BlockSpec(memory_space=pl.ANY)],
        out_specs=pl.BlockSpec(memory_space=pl.ANY),
        scratch_shapes=[pltpu.VMEM((2, tm, slabs, LANES), F32), pltpu.SemaphoreType.DMA((2,))],
        input_output_aliases={6: 0},
        compiler_params=_cparams(1),
        name="moe_scatter",
    )(idx(d1), idx(d2), x, nw.reshape(1, d), shift, scale, zeros)


def _grouped_swiglu_kernel(te_ref, nv_ref, a_ref, wg_ref, wu_ref, o_ref, ab_ref):
    t = pl.program_id(0)

    @pl.when(jnp.logical_and(t < nv_ref[0], pl.program_id(1) == 0))
    def _():
        ab_ref[...] = _unslab_rows(a_ref).astype(BF16)

    @pl.when(t < nv_ref[0])
    def _():
        a = ab_ref[...]
        gate = jnp.dot(a, wg_ref[0], preferred_element_type=F32)
        up = jnp.dot(a, wu_ref[0], preferred_element_type=F32)
        o_ref[...] = (_silu(gate) * up).astype(o_ref.dtype)

    @pl.when(t >= nv_ref[0])
    def _():
        o_ref[...] = jnp.zeros_like(o_ref)


def _grouped_swiglu(a, w13, tile_expert, n_valid, p_rows, *, tg, tn):
    slabs = a.shape[1]
    k = slabs * LANES
    e_hid = w13.shape[2] // 2
    nj = e_hid // tn
    tile = lambda t, nv: jnp.minimum(t, nv[0] - 1)
    return pl.pallas_call(
        _grouped_swiglu_kernel,
        out_shape=jax.ShapeDtypeStruct((p_rows, e_hid), BF16),
        grid_spec=pltpu.PrefetchScalarGridSpec(
            num_scalar_prefetch=2,
            grid=(p_rows // tg, nj),
            in_specs=[pl.BlockSpec((tg, slabs, LANES), lambda t, j, te, nv: (tile(t, nv), 0, 0)),
                      pl.BlockSpec((1, k, tn), lambda t, j, te, nv: (te[tile(t, nv)], 0, j)),
                      pl.BlockSpec((1, k, tn), lambda t, j, te, nv: (te[tile(t, nv)], 0, j + nj))],
            out_specs=pl.BlockSpec((tg, tn), lambda t, j, te, nv: (t, j)),
            scratch_shapes=[pltpu.VMEM((tg, k), BF16)],
        ),
        compiler_params=_cparams(2),
        name="moe_up",
    )(tile_expert, n_valid, a, w13, w13)


def _grouped_mm_kernel(te_ref, nv_ref, a_ref, w_ref, o_ref, wb_ref):
    t = pl.program_id(0)
    valid = t < nv_ref[0]
    new_expert = jnp.logical_or(t == 0, te_ref[t] != te_ref[jnp.maximum(t - 1, 0)])

    @pl.when(jnp.logical_and(valid, new_expert))
    def _():
        wb_ref[...] = w_ref[0].astype(BF16)

    @pl.when(valid)
    def _():
        _slab_rows(o_ref, jnp.dot(a_ref[...], wb_ref[...], preferred_element_type=F32))

    @pl.when(jnp.logical_not(valid))
    def _():
        o_ref[...] = jnp.zeros_like(o_ref)


def _grouped_mm(a, w, tile_expert, n_valid, *, tg):
    p_rows, k = a.shape
    n = w.shape[2]
    slabs = n // LANES
    tile = lambda t, nv: jnp.minimum(t, nv[0] - 1)
    return pl.pallas_call(
        _grouped_mm_kernel,
        out_shape=jax.ShapeDtypeStruct((p_rows, slabs, LANES), F32),
        grid_spec=pltpu.PrefetchScalarGridSpec(
            num_scalar_prefetch=2,
            grid=(p_rows // tg,),
            in_specs=[pl.BlockSpec((tg, k), lambda t, te, nv: (tile(t, nv), 0)),
                      pl.BlockSpec((1, k, n), lambda t, te, nv: (te[tile(t, nv)], 0, 0))],
            out_specs=pl.BlockSpec((tg, slabs, LANES), lambda t, te, nv: (t, 0, 0)),
            scratch_shapes=[pltpu.VMEM((k, n), BF16)],
        ),
        compiler_params=_cparams(1),
        name="moe_down",
    )(tile_expert, n_valid, a, w)


def _combine_kernel(d1_ref, d2_ref, n1_ref, n2_ref, y_hbm, wts_ref, x_ref, gate_ref, o_ref, buf_ref, sem, *, tm):
    i = pl.program_id(0)
    n_steps = pl.num_programs(0)
    slot = lax.rem(i, 2)

    def row_copy(s_i, which, r, src):
        return pltpu.make_async_copy(y_hbm.at[pl.ds(src, 1)], buf_ref.at[s_i, which, pl.ds(r, 1)], sem.at[s_i])

    def fetch(s_i, first_ref, second_ref):
        def issue(r, carry):
            row_copy(s_i, 0, r, first_ref[0, 0, r]).start(priority=0)
            row_copy(s_i, 1, r, second_ref[0, 0, r]).start(priority=1)
            return carry

        lax.fori_loop(0, tm, issue, 0, unroll=DMA_UNROLL)

    @pl.when(i == 0)
    def _():
        fetch(0, d1_ref, d2_ref)

    @pl.when(i + 1 < n_steps)
    def _():
        fetch(1 - slot, n1_ref, n2_ref)

    def drain(r, carry):
        row_copy(slot, 0, r, 0).wait()
        row_copy(slot, 1, r, 0).wait()
        return carry

    lax.fori_loop(0, tm, drain, 0, unroll=DMA_UNROLL)
    lane = lax.broadcasted_iota(jnp.int32, wts_ref.shape, 1)
    w1 = jnp.sum(jnp.where(lane == 0, wts_ref[...], 0.0), axis=-1, keepdims=True)
    w2 = jnp.sum(jnp.where(lane == 1, wts_ref[...], 0.0), axis=-1, keepdims=True)
    y = w1 * _unslab_rows(buf_ref.at[slot, 0]) + w2 * _unslab_rows(buf_ref.at[slot, 1])
    o_ref[...] = x_ref[...] + gate_ref[0] * y


def _combine(y, d1, d2, wts, x, gate, *, tm, group_of_tile):
    m, d = x.shape
    n_steps = m // tm
    idx = lambda a: a.reshape(n_steps, 1, tm)
    ispec = pl.BlockSpec((1, 1, tm), lambda i: (i, 0, 0), memory_space=pltpu.SMEM)
    nspec = pl.BlockSpec((1, 1, tm), lambda i: (jnp.minimum(i + 1, n_steps - 1), 0, 0), memory_space=pltpu.SMEM)
    return pl.pallas_call(
        functools.partial(_combine_kernel, tm=tm),
        out_shape=jax.ShapeDtypeStruct((m, d), F32),
        grid=(n_steps,),
        in_specs=[ispec, ispec, nspec, nspec, pl.BlockSpec(memory_space=pl.ANY),
                  pl.BlockSpec((tm, LANES), lambda i: (i, 0)), pl.BlockSpec((tm, d), lambda i: (i, 0)),
                  pl.BlockSpec((1, 1, d), lambda i: (group_of_tile(i), 0, 0))],
        out_specs=pl.BlockSpec((tm, d), lambda i: (i, 0)),
        scratch_shapes=[pltpu.VMEM((2, 2, tm, d // LANES, LANES), F32), pltpu.SemaphoreType.DMA((2,))],
        compiler_params=_cparams(1),
        name="moe_combine",
    )(idx(d1), idx(d2), idx(d1), idx(d2), y, wts, x, gate)


def _rope_tables(seq, ctx_len):
    half = RET_HD // 4
    freqs = ROPE_BASE ** (-np.arange(half, dtype=np.float64) / half)
    t = np.arange(seq)
    ang_r = (t // GRID_W)[:, None] * freqs[None, :]
    ang_c = (t % GRID_W)[:, None] * freqs[None, :]
    cos = np.concatenate([np.cos(ang_r)] * 2 + [np.cos(ang_c)] * 2, axis=1)
    sin = np.concatenate([-np.sin(ang_r), np.sin(ang_r), -np.sin(ang_c), np.sin(ang_c)], axis=1)
    cos = np.concatenate([np.ones((ctx_len, RET_HD)), cos], axis=0)
    sin = np.concatenate([np.zeros((ctx_len, RET_HD)), sin], axis=0)
    return jnp.asarray(cos, F32), jnp.asarray(sin, F32)


def _half_or_full(n):
    return n // 2 if (n // 2) % LANES == 0 else n


def kernel(x, c, ctx, c_ctx, ada_w, ada_b, norm_mix_w, norm_ffn_w, ev_w_in, ev_ret_decay_f, ev_ret_decay_b,
           ev_ret_gn_w, ev_na_qn_w, ev_na_kn_w, ev_na_rpb, ev_w_out, ev_ffn_w13, ev_ffn_w2, od_mu, od_w_rkv,
           od_w0, od_w1, od_w2, od_a0, od_a1, od_a2, od_g1, od_g2, od_k_k, od_k_a, od_r_k, od_ln_w, od_ln_b,
           od_w_o, od_router, od_moe_w13, od_moe_w2):
    batch, seq, d = x.shape
    ctx_len = ctx.shape[1]
    rows = _Rows(batch, seq, ctx_len)
    tm = 8 * RWKV_CHUNK
    grp = rows.group_of_tile(tm)
    tm_big = 2 * tm if (rows.mc % (2 * tm) == 0 and seq % (2 * tm) == 0) else tm
    grp_big = rows.group_of_tile(tm_big)
    x_rows, ctx_rows = x.reshape(rows.ml, d), ctx.reshape(rows.mc, d)

    n_mod = ((batch + 1 + 7) // 8) * 8
    c_rows = jnp.zeros((n_mod, d), F32).at[:batch].set(c).at[batch].set(c_ctx)
    mods = _ada(c_rows, ada_w, ada_b)
    mods = mods.reshape(mods.shape[0], 6, n_mod, 1, d)

    mod = mods[0]
    w_in = ev_w_in[0].astype(BF16)
    proj = _modnorm_mm(x_rows, norm_mix_w[0], mod[0], mod[1], w_in, group_of_tile=grp_big, tm=tm_big,
                       tn=_half_or_full(w_in.shape[1]), out_dtype=F32, x_tail=ctx_rows)
    cos, sin = _rope_tables(seq, ctx_len)
    y_f, y_b = _retention(proj, cos, sin, ev_ret_decay_f[0], ev_ret_decay_b[0], rows)
    bias = _na_bias_table(ev_na_rpb[0].astype(F32))
    na_l = _na_latent(proj, ev_na_qn_w[0], ev_na_kn_w[0], bias, rows, 4 * RET_W)
    na_c = _ctx_attention(proj, ev_na_qn_w[0], ev_na_kn_w[0], rows, 4 * RET_W)
    mix = _ret_readout(y_f, y_b, proj, na_l, na_c, ev_ret_gn_w[0], rows, tm)
    w_out = ev_w_out[0].astype(BF16)
    xa = _mm_res(mix, w_out, x_rows, mod[2], tm=tm, group_of_tile=grp, res_tail=ctx_rows)

    w13 = ev_ffn_w13[0].astype(BF16)
    w2 = ev_ffn_w2[0].astype(BF16)
    hid = w2.shape[0]
    hidden = _modnorm_mm(xa, norm_ffn_w[0], mod[3], mod[4], w13, group_of_tile=grp_big, tm=tm_big,
                         tn=_half_or_full(hid), out_dtype=BF16, swiglu=True)
    xa = _mm_res(hidden, w2, xa, mod[5], tm=tm, group_of_tile=grp)

    mod = mods[1]
    glora = od_g1.shape[-1]
    glora_p = ((glora + LANES - 1) // LANES) * LANES
    p = {
        "w1": jnp.concatenate([od_w1[0, 0], od_w1[0, 1]], axis=1).astype(BF16),
        "w2": od_w2[0].astype(BF16),
        "a1": jnp.concatenate([od_a1[0, 0], od_a1[0, 1]], axis=1).astype(BF16),
        "a2": od_a2[0].astype(BF16),
        "g1": jnp.zeros((d, glora_p), BF16).at[:, :glora].set(od_g1[0].astype(BF16)),
        "g2": jnp.zeros((glora_p, d), BF16).at[:glora].set(od_g2[0].astype(BF16)),
        "w0": od_w0[0].astype(F32), "a0": od_a0[0].astype(F32),
        "k_k": od_k_k[0].reshape(1, d).astype(F32), "k_a": od_k_a[0].reshape(1, d).astype(F32),
        "r_k": od_r_k[0].reshape(1, d).astype(F32),
    }
    w_rkv = od_w_rkv[0].astype(BF16)
    r, k, v, lw_low, a_low, g_low = _rwkv_mix(xa, norm_mix_w[1], mod[0], mod[1], od_mu[0].astype(F32), w_rkv, p,
                                              rows=rows, tm=tm)
    f = _rwkv_post(r, k, v, lw_low, a_low, g_low, p, tm=tm // 2)
    y_f, y_b = _rwkv_scan(f[0:5], f[5:10], f[10], rows)
    mixed = _rwkv_readout(y_f, y_b, f[12], f[11], od_ln_w[0], od_ln_b[0], rows, tm)
    x_l = _mm_res(mixed, od_w_o[0].astype(BF16), xa, mod[2], tm=tm, group_of_tile=grp)

    gates = _router(x_l, norm_ffn_w[1], mod[3], mod[4], od_router[0], tm=tm, group_of_tile=grp)
    n_e = od_router.shape[-1]
    n_rows, tile_expert, n_valid, d1, d2_scatter, d2_combine, wts = _route_plan(gates, n_e, MOE_TILE)
    tm_r = tm
    grp_r = rows.group_of_tile(tm_r)
    a_sorted = _moe_scatter(x_l, norm_ffn_w[1], mod[3], mod[4], d1, d2_scatter, n_rows, tm=tm_r, tg=MOE_TILE,
                            group_of_tile=grp_r)
    moe13 = od_moe_w13[0].astype(BF16)
    hid_sorted = _grouped_swiglu(a_sorted, moe13, tile_expert, n_valid, n_rows, tg=MOE_TILE, tn=moe13.shape[2] // 2)
    y_sorted = _grouped_mm(hid_sorted, od_moe_w2[0].astype(F32), tile_expert, n_valid, tg=MOE_TILE)
    x_l = _combine(y_sorted, d1, d2_combine, wts, x_l, mod[5], tm=tm_r, group_of_tile=grp_r)
    return x_l.reshape(batch, seq, d)
```
